```python
import math
import jax, jax.numpy as jnp
from jax import lax
import numpy as np

D_MODEL = 1024
BATCH = 32
SEQ = 2048
DEPTH = 2

CHUNK = 64
Q_BLOCK = 128
EPS = 1e-6
MASK_VALUE = -1e30
TINY = 1e-30
H_A = 4
DK_A = D_MODEL // 16
DV_A = 2 * DK_A
ROT_DIM = DK_A // 4
ROPE_THETA = 500000.0
H_B = 4
DK_B = D_MODEL // 16
DV_B = D_MODEL // 16
C_WIDTH = D_MODEL // 4
CONV_WIDTH = 3
D_MIX = H_A * DV_A + H_B * DV_B + C_WIDTH
SPLIT_SIZES = (H_A * 2 * DK_A, H_A * 2 * DK_A, H_A * DV_A,
               H_B * DK_B, H_B * DK_B, H_B * DV_B, H_B * DV_B,
               C_WIDTH, C_WIDTH, C_WIDTH)
D_IN = H_A * (4 * DK_A + DV_A) + H_B * (2 * DK_B + 2 * DV_B) + 3 * C_WIDTH
N_GROUPS = 4
EXPERTS_PER_GROUP = 8
N_EXPERTS = N_GROUPS * EXPERTS_PER_GROUP
TOP_K_INNER = 2
D_EXPERT = D_MODEL // 2
MOE_BLOCK = 256

kernel_name = "hymba_style_hybrid_diffattn_hgrn2_shortconv_hmoe"


def rms_norm(x, g):
    xf = x.astype(jnp.float32)
    y = xf * lax.rsqrt(jnp.mean(xf * xf, axis=-1, keepdims=True) + EPS)
    return (y * g.astype(jnp.float32)).astype(x.dtype)


def partial_rope(t, cos, sin):
    half = ROT_DIM // 2
    tf = t.astype(jnp.float32)
    t1, t2, rest = tf[..., :half], tf[..., half:ROT_DIM], tf[..., ROT_DIM:]
    out = jnp.concatenate([t1 * cos - t2 * sin, t1 * sin + t2 * cos, rest], axis=-1)
    return out.astype(t.dtype)


def differential_attention(q_in, k_in, v_in, positions, q_g, k_g, lq1, lk1, lq2, lk2, subln_g, layer_idx):
    b, s = q_in.shape[0], q_in.shape[1]
    q = rms_norm(q_in.reshape(b, s, H_A, 2, DK_A), q_g)
    k = rms_norm(k_in.reshape(b, s, H_A, 2, DK_A), k_g)
    v = v_in.reshape(b, s, H_A, DV_A)
    inv_freq = ROPE_THETA ** (-jnp.arange(0, ROT_DIM, 2, dtype=jnp.float32) / ROT_DIM)
    ang = positions.astype(jnp.float32)[..., None] * inv_freq
    cos = jnp.cos(ang)[:, :, None, None, :]
    sin = jnp.sin(ang)[:, :, None, None, :]
    q = partial_rope(q, cos, sin)
    k = partial_rope(k, cos, sin)
    lam_init = 0.8 - 0.6 * math.exp(-0.3 * layer_idx)
    lam = (jnp.exp(jnp.sum(lq1.astype(jnp.float32) * lk1.astype(jnp.float32)))
           - jnp.exp(jnp.sum(lq2.astype(jnp.float32) * lk2.astype(jnp.float32))) + lam_init)
    scale = DK_A ** -0.5
    pos_chunk = jnp.arange(s) // CHUNK
    outs = []
    for blk in range(s // Q_BLOCK):
        q_lo, q_hi = blk * Q_BLOCK, (blk + 1) * Q_BLOCK
        scores = jnp.einsum('bqhcd,bkhcd->bhcqk', q[:, q_lo:q_hi], k[:, :q_hi]).astype(jnp.float32) * scale
        mask = pos_chunk[None, :q_hi] <= pos_chunk[q_lo:q_hi, None]
        probs = jax.nn.softmax(jnp.where(mask, scores, MASK_VALUE), axis=-1)
        weights = probs[:, :, 0] - lam * probs[:, :, 1]
        outs.append(jnp.einsum('bhqk,bkhv->bqhv', weights.astype(v.dtype), v[:, :q_hi]))
    o = jnp.concatenate(outs, axis=1)
    o = rms_norm(o, subln_g) * (1.0 - lam_init)
    return o.reshape(b, s, H_A * DV_A)


def hgrn2(q_in, f_in, i_in, g_in, lb, norm_g):
    b, s = q_in.shape[0], q_in.shape[1]
    nc = s // CHUNK
    zf = f_in.reshape(b, s, H_B, DK_B).astype(jnp.float32)
    lbh = lb.reshape(H_B, DK_B).astype(jnp.float32)
    f = lbh + (1.0 - lbh) * jax.nn.sigmoid(zf)
    log_f = jnp.log(jnp.maximum(f, TINY))
    key = (1.0 - lbh) * jax.nn.sigmoid(-zf)
    q = jax.nn.silu(q_in.reshape(b, s, H_B, DK_B).astype(jnp.float32))
    v = i_in.reshape(b, s, H_B, DV_B).astype(jnp.float32)

    def to_chunks(t):
        return t.reshape(b, nc, CHUNK, H_B, t.shape[-1]).transpose(1, 0, 3, 2, 4)

    tri = (jnp.arange(CHUNK)[:, None] >= jnp.arange(CHUNK)[None, :])[:, :, None]

    def step(state, inp):
        qc, kc, vc, lfc = inp
        g = jnp.cumsum(lfc, axis=2)
        diff = g[:, :, :, None, :] - g[:, :, None, :, :]
        decay = jnp.where(tri, jnp.exp(jnp.where(tri, diff, 0.0)), 0.0)
        scores = jnp.einsum('bhtd,bhsd,bhtsd->bhts', qc, kc, decay)
        o = (jnp.einsum('bhts,bhsv->bhtv', scores, vc)
             + jnp.einsum('bhtd,bhdv->bhtv', qc * jnp.exp(g), state))
        g_last = g[:, :, -1:, :]
        new_state = (jnp.exp(g_last[:, :, 0, :])[..., None] * state
                     + jnp.einsum('bhsd,bhsv->bhdv', kc * jnp.exp(g_last - g), vc))
        return new_state, o

    state0 = jnp.zeros((b, H_B, DK_B, DV_B), jnp.float32)
    _, o = lax.scan(step, state0, (to_chunks(q), to_chunks(key), to_chunks(v), to_chunks(log_f)))
    o = o.transpose(1, 0, 3, 2, 4).reshape(b, s, H_B, DV_B)
    o = rms_norm(o, norm_g).reshape(b, s, H_B * DV_B) * jax.nn.silu(g_in.astype(jnp.float32))
    return o.astype(g_in.dtype)


def short_conv(b_gate, c_gate, x_in, conv_w):
    h = c_gate * x_in
    s = h.shape[1]
    hp = jnp.pad(h, ((0, 0), (CONV_WIDTH - 1, 0), (0, 0)))
    y = conv_w[0] * hp[:, 0:s]
    for j in range(1, CONV_WIDTH):
        y = y + conv_w[j] * hp[:, j:j + s]
    return b_gate * y


def hier_moe(h, wg, bg, we, be, w_gate, w_up, w_down):
    t, d = h.shape
    hf = h.astype(jnp.float32)
    g_logits = hf @ wg.astype(jnp.float32) + bg.astype(jnp.float32)
    p_group = jax.nn.softmax(g_logits, axis=-1)
    g_idx = jnp.argmax(g_logits, axis=-1)
    p_g = jnp.take_along_axis(p_group, g_idx[:, None], axis=-1)
    e_logits = (hf @ we.astype(jnp.float32) + be.astype(jnp.float32)).reshape(t, N_GROUPS, EXPERTS_PER_GROUP)
    e_in_group = jnp.take_along_axis(e_logits, g_idx[:, None, None], axis=1)[:, 0]
    top_val, top_idx = lax.top_k(e_in_group, TOP_K_INNER)
    gates = p_g * jax.nn.softmax(top_val, axis=-1)
    expert_id = (g_idx[:, None] * EXPERTS_PER_GROUP + top_idx).reshape(-1)
    token_id = jnp.repeat(jnp.arange(t), TOP_K_INNER)
    n_assign = t * TOP_K_INNER
    order = jnp.argsort(expert_id)
    sorted_e = expert_id[order]
    counts = jnp.bincount(expert_id, length=N_EXPERTS)
    padded = (counts + MOE_BLOCK - 1) // MOE_BLOCK * MOE_BLOCK
    pad_end = jnp.cumsum(padded)
    pad_start = pad_end - padded
    raw_start = jnp.cumsum(counts) - counts
    dest = pad_start[sorted_e] + jnp.arange(n_assign) - raw_start[sorted_e]
    cap = -(-n_assign // MOE_BLOCK) * MOE_BLOCK + N_EXPERTS * MOE_BLOCK
    n_blocks = cap // MOE_BLOCK
    buf = jnp.zeros((cap, d), h.dtype).at[dest].set(h[token_id[order]])
    blk_e = jnp.minimum(jnp.searchsorted(pad_end, jnp.arange(n_blocks) * MOE_BLOCK, side='right'), N_EXPERTS - 1)

    def expert_block(args):
        xb, e = args
        return (jax.nn.silu(xb @ w_gate[e]) * (xb @ w_up[e])) @ w_down[e]

    y_buf = lax.map(expert_block, (buf.reshape(n_blocks, MOE_BLOCK, d), blk_e)).reshape(cap, d)
    y_assign = jnp.zeros((n_assign, d), y_buf.dtype).at[order].set(y_buf[dest])
    return jnp.einsum('tk,tkd->td', gates.astype(y_assign.dtype), y_assign.reshape(t, TOP_K_INNER, d))


def setup_inputs(seed: int = 0) -> dict:
    key = jax.random.key(seed)
    ks = jax.random.split(key, 24)
    f32 = jnp.float32

    def nrm(k, shape, scale):
        return jax.random.normal(k, shape, f32) * scale

    def gain(k, shape):
        return 1.0 + 0.02 * jax.random.normal(k, shape, f32)

    x = jax.random.normal(ks[0], (BATCH, SEQ, D_MODEL), f32)
    offsets = jax.random.randint(ks[1], (BATCH, 1), 0, 64, dtype=jnp.int32) * CHUNK
    positions = (offsets + jnp.arange(SEQ, dtype=jnp.int32)[None, :]).astype(jnp.int32)
    return {
        "x": x,
        "positions": positions,
        "norm_mix_g": gain(ks[2], (DEPTH, D_MODEL)),
        "w_in": nrm(ks[3], (DEPTH, D_MODEL, D_IN), D_MODEL ** -0.5),
        "attn_q_norm_g": gain(ks[4], (DEPTH, DK_A)),
        "attn_k_norm_g": gain(ks[5], (DEPTH, DK_A)),
        "lambda_q1": nrm(ks[6], (DEPTH, DK_A), 0.1),
        "lambda_k1": nrm(ks[7], (DEPTH, DK_A), 0.1),
        "lambda_q2": nrm(ks[8], (DEPTH, DK_A), 0.1),
        "lambda_k2": nrm(ks[9], (DEPTH, DK_A), 0.1),
        "attn_subln_g": gain(ks[10], (DEPTH, DV_A)),
        "hgrn_lower_bounds": nrm(ks[11], (DEPTH, H_B * DK_B), 0.5),
        "hgrn_norm_g": gain(ks[12], (DEPTH, DV_B)),
        "conv_w": nrm(ks[13], (DEPTH, CONV_WIDTH, C_WIDTH), CONV_WIDTH ** -0.5),
        "w_out": nrm(ks[14], (DEPTH, D_MIX, D_MODEL), D_MIX ** -0.5),
        "norm_ffn_g": gain(ks[15], (DEPTH, D_MODEL)),
        "router_group_w": nrm(ks[16], (DEPTH, D_MODEL, N_GROUPS), D_MODEL ** -0.5),
        "router_group_b": nrm(ks[17], (DEPTH, N_GROUPS), 0.01),
        "router_expert_w": nrm(ks[18], (DEPTH, D_MODEL, N_EXPERTS), D_MODEL ** -0.5),
        "router_expert_b": nrm(ks[19], (DEPTH, N_EXPERTS), 0.01),
        "expert_w_gate": nrm(ks[20], (DEPTH, N_EXPERTS, D_MODEL, D_EXPERT), D_MODEL ** -0.5),
        "expert_w_up": nrm(ks[21], (DEPTH, N_EXPERTS, D_MODEL, D_EXPERT), D_MODEL ** -0.5),
        "expert_w_down": nrm(ks[22], (DEPTH, N_EXPERTS, D_EXPERT, D_MODEL), D_EXPERT ** -0.5),
    }


def reference(x, positions, norm_mix_g, w_in, attn_q_norm_g, attn_k_norm_g, lambda_q1, lambda_k1,
              lambda_q2, lambda_k2, attn_subln_g, hgrn_lower_bounds, hgrn_norm_g, conv_w, w_out,
              norm_ffn_g, router_group_w, router_group_b, router_expert_w, router_expert_b,
              expert_w_gate, expert_w_up, expert_w_down):
    b, s, d = x.shape
    lb_probs = jax.nn.softmax(hgrn_lower_bounds.astype(jnp.float32), axis=0)
    lower_bounds = jnp.cumsum(lb_probs, axis=0) - lb_probs[0:1]
    split_points = [int(v) for v in np.cumsum(SPLIT_SIZES)[:-1]]
    for l in range(DEPTH):
        u = rms_norm(x, norm_mix_g[l]) @ w_in[l]
        aq, ak, av, bq, bf, bi, bg, cb, cc, cx = jnp.split(u, split_points, axis=-1)
        o_a = differential_attention(aq, ak, av, positions, attn_q_norm_g[l], attn_k_norm_g[l],
                                     lambda_q1[l], lambda_k1[l], lambda_q2[l], lambda_k2[l],
                                     attn_subln_g[l], l)
        o_b = hgrn2(bq, bf, bi, bg, lower_bounds[l], hgrn_norm_g[l])
        o_c = short_conv(cb, cc, cx, conv_w[l])
        mix = jnp.concatenate([o_a.astype(x.dtype), o_b.astype(x.dtype), o_c.astype(x.dtype)], axis=-1)
        x = x + mix @ w_out[l]
        h = rms_norm(x, norm_ffn_g[l]).reshape(b * s, d)
        y = hier_moe(h, router_group_w[l], router_group_b[l], router_expert_w[l], router_expert_b[l],
                     expert_w_gate[l], expert_w_up[l], expert_w_down[l])
        x = x + y.reshape(b, s, d).astype(x.dtype)
    return x
```

```python
import functools
import math

import jax
import jax.numpy as jnp
from jax import lax
from jax.experimental import pallas as pl
from jax.experimental.pallas import tpu as pltpu

F32 = jnp.float32
BF16 = jnp.bfloat16

LANES = 128
SUBLANES = 8

D_MODEL = 1024
CHUNK = 64
EPS = 1e-6
MASK_VALUE = -1e30
TINY = 1e-30
H_A = 4
DK_A = D_MODEL // 16
DV_A = 2 * DK_A
ROT_DIM = DK_A // 4
ROPE_THETA = 500000.0
H_B = 4
DK_B = D_MODEL // 16
DV_B = D_MODEL // 16
C_WIDTH = D_MODEL // 4
CONV_WIDTH = 3
N_GROUPS = 4
EXPERTS_PER_GROUP = 8
N_EXPERTS = N_GROUPS * EXPERTS_PER_GROUP
D_EXPERT = D_MODEL // 2

W_A = H_A * DV_A
W_B = H_B * DK_B
ROUTE_LANES = LANES
N_LOGITS = N_GROUPS + N_EXPERTS

TM_PROJ = 512
TQ = 256
TK = 256
TM_ROUTE = 512
MOE_BLOCK = 512
TM_SCATTER = 512
TM_COMBINE = 256
VMEM_LIMIT = 56 * 1024 * 1024


def _cparams(sem, vmem=VMEM_LIMIT):
    return pltpu.CompilerParams(dimension_semantics=sem, vmem_limit_bytes=vmem)


def _inproj_kernel(x_ref, g_ref, wqk_ref, wv_ref, whb_ref, wf_ref, wc_ref, gsum_ref, gqk_ref,
                   cos_ref, s1_ref, s2_ref, qk_out, v_out, hb_out, f_out, c_out):
    x = x_ref[...]
    ms = jnp.mean(x * x, axis=-1, keepdims=True)
    xn = (x * lax.rsqrt(ms + EPS) * g_ref[...]).astype(BF16)

    v_out[...] = jnp.dot(xn, wv_ref[...], preferred_element_type=F32).astype(BF16)
    hb_out[...] = jnp.dot(xn, whb_ref[...], preferred_element_type=F32).astype(BF16)
    f_out[...] = jnp.dot(xn, wf_ref[...], preferred_element_type=F32)
    c_out[...] = jnp.dot(xn, wc_ref[...], preferred_element_type=F32).astype(BF16)

    cos = cos_ref[...]
    s1 = s1_ref[...]
    s2 = s2_ref[...]
    gsum = gsum_ref[...]
    for part in range(2):
        lo = part * W_A
        t = jnp.dot(xn, wqk_ref[:, lo:lo + W_A], preferred_element_type=F32)
        msq = jnp.dot((t * t).astype(BF16), gsum, preferred_element_type=F32)
        t = t * lax.rsqrt(msq + EPS) * gqk_ref[:, lo:lo + W_A]
        for h in range(H_A):
            slab = t[:, h * LANES:(h + 1) * LANES]
            up = pltpu.roll(slab, LANES - ROT_DIM // 2, axis=1)
            dn = pltpu.roll(slab, ROT_DIM // 2, axis=1)
            qk_out[:, lo + h * LANES:lo + (h + 1) * LANES] = (slab * cos + up * s1 + dn * s2).astype(BF16)


def _inproj(x2, g, wqk, wv, whb, wf, wc, gsum, gqk, cos_t, s1_t, s2_t):
    t_tok = x2.shape[0]
    tm = min(TM_PROJ, t_tok)
    grid = (t_tok // tm,)
    row = lambda i: (i, 0)
    fixed = lambda i: (0, 0)
    return pl.pallas_call(
        _inproj_kernel,
        grid=grid,
        in_specs=[
            pl.BlockSpec((tm, D_MODEL), row),
            pl.BlockSpec((1, D_MODEL), fixed),
            pl.BlockSpec(wqk.shape, fixed),
            pl.BlockSpec(wv.shape, fixed),
            pl.BlockSpec(whb.shape, fixed),
            pl.BlockSpec(wf.shape, fixed),
            pl.BlockSpec(wc.shape, fixed),
            pl.BlockSpec(gsum.shape, fixed),
            pl.BlockSpec(gqk.shape, fixed),
            pl.BlockSpec((tm, LANES), row),
            pl.BlockSpec((tm, LANES), row),
            pl.BlockSpec((tm, LANES), row),
        ],
        out_specs=[
            pl.BlockSpec((tm, 2 * W_A), row),
            pl.BlockSpec((tm, W_A), row),
            pl.BlockSpec((tm, 3 * W_B), row),
            pl.BlockSpec((tm, W_B), row),
            pl.BlockSpec((tm, 3 * C_WIDTH), row),
        ],
        out_shape=[
            jax.ShapeDtypeStruct((t_tok, 2 * W_A), BF16),
            jax.ShapeDtypeStruct((t_tok, W_A), BF16),
            jax.ShapeDtypeStruct((t_tok, 3 * W_B), BF16),
            jax.ShapeDtypeStruct((t_tok, W_B), F32),
            jax.ShapeDtypeStruct((t_tok, 3 * C_WIDTH), BF16),
        ],
        compiler_params=_cparams(("parallel",)),
        name="inproj",
    )(x2, g, wqk, wv, whb, wf, wc, gsum, gqk, cos_t, s1_t, s2_t)


def _attn_kernel(lam_ref, q_ref, k_ref, v_ref, g_ref, o_ref, acc_ref, m_ref, l_ref, *, lam_init):
    qi = pl.program_id(2)
    q = q_ref[...]
    lane = lax.broadcasted_iota(jnp.int32, q.shape, 1)
    zero = jnp.zeros_like(q)
    qc = (jnp.where(lane < DK_A, q, zero), jnp.where(lane >= DK_A, q, zero))

    acc_ref[...] = jnp.zeros_like(acc_ref)
    m_ref[...] = jnp.full_like(m_ref, -jnp.inf)
    l_ref[...] = jnp.zeros_like(l_ref)

    kpos = lax.broadcasted_iota(jnp.int32, (TK, TQ), 0) // CHUNK
    qpos = lax.broadcasted_iota(jnp.int32, (TK, TQ), 1) // CHUNK
    diag_mask = kpos <= qpos

    def block(j, masked):
        start = pl.multiple_of(j * TK, TK)
        kb = k_ref[pl.ds(start, TK), :]
        vb = v_ref[pl.ds(start, TK), :]
        for c in range(2):
            s = lax.dot_general(kb, qc[c], (((1,), (1,)), ((), ())), preferred_element_type=F32)
            if masked:
                s = jnp.where(diag_mask, s, MASK_VALUE)
            m_old = m_ref[c]
            m_new = jnp.maximum(m_old, jnp.max(s, axis=0, keepdims=True))
            alpha = jnp.exp(m_old - m_new)
            p = jnp.exp(s - m_new)
            l_ref[c] = alpha * l_ref[c] + jnp.sum(p, axis=0, keepdims=True)
            pv = lax.dot_general(vb, p.astype(BF16), (((0,), (0,)), ((), ())), preferred_element_type=F32)
            acc_ref[c] = alpha * acc_ref[c] + pv
            m_ref[c] = m_new

    def body(j, carry):
        block(j, False)
        return carry

    lax.fori_loop(0, qi, body, 0)
    block(qi, True)

    lam = lam_ref[0]
    o = acc_ref[0] / l_ref[0] - lam * (acc_ref[1] / l_ref[1])
    ms = jnp.mean(o * o, axis=0, keepdims=True)
    y = o * lax.rsqrt(ms + EPS) * g_ref[...] * (1.0 - lam_init)
    o_ref[...] = y.T.astype(BF16)


def _attention(qk, v, lam, subln_col, batch, seq, lam_init):
    assert TQ == TK and TQ % CHUNK == 0 and seq % TQ == 0
    nq = seq // TQ
    t_tok = batch * seq
    kernel = functools.partial(_attn_kernel, lam_init=lam_init)
    return pl.pallas_call(
        kernel,
        grid=(batch, H_A, nq),
        in_specs=[
            pl.BlockSpec(memory_space=pltpu.SMEM),
            pl.BlockSpec((TQ, DV_A), lambda b, h, i: (b * nq + i, h)),
            pl.BlockSpec((seq, DV_A), lambda b, h, i: (b, H_A + h)),
            pl.BlockSpec((seq, DV_A), lambda b, h, i: (b, h)),
            pl.BlockSpec((DV_A, 1), lambda b, h, i: (0, 0)),
        ],
        out_specs=pl.BlockSpec((TQ, DV_A), lambda b, h, i: (b * nq + i, h)),
        out_shape=jax.ShapeDtypeStruct((t_tok, W_A), BF16),
        scratch_shapes=[
            pltpu.VMEM((2, DV_A, TQ), F32),
            pltpu.VMEM((2, 1, TQ), F32),
            pltpu.VMEM((2, 1, TQ), F32),
        ],
        compiler_params=_cparams(("parallel", "parallel", "arbitrary")),
        name="diff_attn",
    )(lam, qk, qk, v, subln_col)


_LEVELS = (32, 16, 8, 4, 2, 1)


def _ref_rows(g, h):
    n = g.shape[1]
    if 2 * h >= SUBLANES:
        pieces = []
        for m in range(CHUNK // (2 * h)):
            r = 2 * h * m + h - 1
            pieces.append(jnp.broadcast_to(g[r:r + 1, :], (2 * h, n)))
        return pieces[0] if len(pieces) == 1 else jnp.concatenate(pieces, axis=0)
    g3 = g.reshape(CHUNK // SUBLANES, SUBLANES, n)
    sub = lax.broadcasted_iota(jnp.int32, g3.shape, 1)
    out = None
    for m in reversed(range(SUBLANES // (2 * h))):
        r = 2 * h * m + h - 1
        piece = jnp.broadcast_to(g3[:, r:r + 1, :], g3.shape)
        out = piece if out is None else jnp.where(sub < 2 * h * (m + 1), piece, out)
    return out.reshape(CHUNK, n)


def _stack_heads(a, lane_head):
    zero = jnp.zeros_like(a)
    return jnp.concatenate([jnp.where(lane_head == hh, a, zero) for hh in range(H_B)], axis=0)


def _hgrn_kernel(hb_ref, f_ref, c_ref, lb_ref, ng_ref, cw_ref, gavg_ref, o_ref, state_ref, hbuf_ref):
    seq = hb_ref.shape[0]
    n_chunks = seq // CHUNK
    lb = lb_ref[...]
    ng = ng_ref[...]
    gavg = gavg_ref[...]

    row = lax.broadcasted_iota(jnp.int32, (CHUNK, W_B), 0)
    lane = lax.broadcasted_iota(jnp.int32, (CHUNK, W_B), 1)
    lane_head = lane // DK_B
    lane_s = lane % CHUNK
    tri = (lax.broadcasted_iota(jnp.int32, (CHUNK, CHUNK), 1)
           <= lax.broadcasted_iota(jnp.int32, (CHUNK, CHUNK), 0)).astype(BF16)
    lvl_mask = [(row // (2 * h)) == (lane_s // (2 * h)) for h in _LEVELS]
    eye_mask = row == lane_s
    r4 = lax.broadcasted_iota(jnp.int32, (H_B * DK_B, W_B), 0) // DK_B
    c4 = lax.broadcasted_iota(jnp.int32, (H_B * DK_B, W_B), 1) // DK_B
    bd_mask = r4 == c4

    state_ref[...] = jnp.zeros_like(state_ref)

    def chunk(i, carry):
        r0 = pl.multiple_of(i * CHUNK, CHUNK)
        hb = hb_ref[pl.ds(r0, CHUNK), :].astype(F32)
        zq = hb[:, 0:W_B]
        v = hb[:, W_B:2 * W_B]
        zg = hb[:, 2 * W_B:3 * W_B]
        zf = f_ref[pl.ds(r0, CHUNK), :]

        f = lb + (1.0 - lb) * jax.nn.sigmoid(zf)
        logf = jnp.log(jnp.maximum(f, TINY))
        key = (1.0 - lb) * jax.nn.sigmoid(-zf)
        q = zq * jax.nn.sigmoid(zq)
        vb16 = v.astype(BF16)

        hi = logf.astype(BF16)
        r1 = logf - hi.astype(F32)
        mid = r1.astype(BF16)
        lo = (r1 - mid.astype(F32)).astype(BF16)
        g = (jnp.dot(tri, hi, preferred_element_type=F32)
             + jnp.dot(tri, mid, preferred_element_type=F32)
             + jnp.dot(tri, lo, preferred_element_type=F32))

        a_all = jnp.where(eye_mask,
                          lax.dot_general(q.astype(BF16), _stack_heads(key.astype(BF16), lane_head[0:CHUNK]),
                                          (((1,), (1,)), ((), ())), preferred_element_type=F32), 0.0)
        for h, mask in zip(_LEVELS, lvl_mask):
            w = jnp.exp(-jnp.abs(g - _ref_rows(g, h)))
            second = (row % (2 * h)) >= h
            qh = jnp.where(second, q * w, 0.0).astype(BF16)
            kh = jnp.where(second, 0.0, key * w).astype(BF16)
            ah = lax.dot_general(qh, _stack_heads(kh, lane_head), (((1,), (1,)), ((), ())),
                                 preferred_element_type=F32)
            a_all = a_all + jnp.where(mask, ah, 0.0)
        o = jnp.dot(a_all.astype(BF16), _stack_heads(vb16, lane_head), preferred_element_type=F32)

        st = state_ref[...]
        qg = (q * jnp.exp(g)).astype(BF16)
        o = o + lax.dot_general(qg, st.astype(BF16), (((1,), (1,)), ((), ())), preferred_element_type=F32)
        g_last = g[CHUNK - 1:CHUNK, :]
        kd = (key * jnp.exp(g_last - g)).astype(BF16)
        upd = lax.dot_general(vb16, kd, (((0,), (0,)), ((), ())), preferred_element_type=F32)
        state_ref[...] = jnp.exp(g_last) * st + jnp.where(bd_mask, upd, 0.0)

        ms = jnp.dot((o * o).astype(BF16), gavg, preferred_element_type=F32)
        y = o * lax.rsqrt(ms + EPS) * ng * (zg * jax.nn.sigmoid(zg))
        o_ref[pl.ds(r0, CHUNK), 0:W_B] = y.astype(BF16)
        return carry

    lax.fori_loop(0, n_chunks, chunk, 0)

    piece = min(512, seq)
    hbuf_ref[0:SUBLANES, :] = jnp.zeros((SUBLANES, C_WIDTH), F32)
    for p in range(seq // piece):
        cblk = c_ref[p * piece:(p + 1) * piece, :].astype(F32)
        hbuf_ref[SUBLANES + p * piece:SUBLANES + (p + 1) * piece, :] = cblk[:, C_WIDTH:2 * C_WIDTH] * cblk[:, 2 * C_WIDTH:]
    for p in range(seq // piece):
        base = SUBLANES + p * piece
        y = (cw_ref[0:1, :] * hbuf_ref[base - 2:base - 2 + piece, :]
             + cw_ref[1:2, :] * hbuf_ref[base - 1:base - 1 + piece, :]
             + cw_ref[2:3, :] * hbuf_ref[base:base + piece, :])
        bgate = c_ref[p * piece:(p + 1) * piece, 0:C_WIDTH].astype(F32)
        o_ref[p * piece:(p + 1) * piece, W_B:W_B + C_WIDTH] = (bgate * y).astype(BF16)


def _hgrn_conv(hb, f, cv, lb, ng, cw, gavg, batch, seq):
    t_tok = batch * seq
    fixed = lambda b: (0, 0)
    return pl.pallas_call(
        _hgrn_kernel,
        grid=(batch,),
        in_specs=[
            pl.BlockSpec((seq, 3 * W_B), lambda b: (b, 0)),
            pl.BlockSpec((seq, W_B), lambda b: (b, 0)),
            pl.BlockSpec((seq, 3 * C_WIDTH), lambda b: (b, 0)),
            pl.BlockSpec((1, W_B), fixed),
            pl.BlockSpec((1, W_B), fixed),
            pl.BlockSpec((CONV_WIDTH, C_WIDTH), fixed),
            pl.BlockSpec((W_B, W_B), fixed),
        ],
        out_specs=pl.BlockSpec((seq, W_B + C_WIDTH), lambda b: (b, 0)),
        out_shape=jax.ShapeDtypeStruct((t_tok, W_B + C_WIDTH), BF16),
        scratch_shapes=[
            pltpu.VMEM((H_B * DV_B, W_B), F32),
            pltpu.VMEM((seq + SUBLANES, C_WIDTH), F32),
        ],
        compiler_params=_cparams(("parallel",)),
        name="hgrn_conv",
    )(hb, f, cv, lb, ng, cw, gavg)


def _outproj_router_kernel(oa_ref, obc_ref, x_ref, wo_ref, g_ref, wr_ref, br_ref,
                           xo_ref, route_ref, cnt_ref, carry_ref):
    i = pl.program_id(0)

    @pl.when(i == 0)
    def _():
        carry_ref[...] = jnp.zeros_like(carry_ref)

    mixed = (jnp.dot(oa_ref[...], wo_ref[0:W_A, :], preferred_element_type=F32)
             + jnp.dot(obc_ref[...], wo_ref[W_A:, :], preferred_element_type=F32))
    xn = x_ref[...] + mixed
    xo_ref[...] = xn

    ms = jnp.mean(xn * xn, axis=-1, keepdims=True)
    hn = (xn * lax.rsqrt(ms + EPS) * g_ref[...]).astype(BF16)
    logits = jnp.dot(hn, wr_ref[...], preferred_element_type=F32) + br_ref[...]

    tm = logits.shape[0]
    lane = lax.broadcasted_iota(jnp.int32, (tm, ROUTE_LANES), 1)
    neg = -jnp.inf
    big = ROUTE_LANES

    def top(mask):
        val = jnp.max(jnp.where(mask, logits, neg), axis=-1, keepdims=True)
        idx = jnp.min(jnp.where(mask & (logits == val), lane, big), axis=-1, keepdims=True)
        return val, idx

    gmask = lane < N_GROUPS
    gmax, gidx = top(gmask)
    p_g = 1.0 / jnp.sum(jnp.where(gmask, jnp.exp(logits - gmax), 0.0), axis=-1, keepdims=True)
    e_lo = N_GROUPS + EXPERTS_PER_GROUP * gidx
    emask = (lane >= e_lo) & (lane < e_lo + EXPERTS_PER_GROUP)
    v1, i1 = top(emask)
    v2, i2 = top(emask & (lane != i1))
    t2 = jnp.exp(v2 - v1)
    w1 = p_g / (1.0 + t2)
    w2 = p_g * t2 / (1.0 + t2)

    sel1 = lane == i1
    sel2 = lane == i2
    onehot = jnp.where(sel1 | sel2, 1.0, 0.0)
    strict = (lax.broadcasted_iota(jnp.int32, (tm, tm), 1)
              < lax.broadcasted_iota(jnp.int32, (tm, tm), 0)).astype(BF16)
    before = jnp.dot(strict, onehot.astype(BF16), preferred_element_type=F32) + carry_ref[0:1, :]
    rank1 = jnp.sum(jnp.where(sel1, before, 0.0), axis=-1, keepdims=True)
    rank2 = jnp.sum(jnp.where(sel2, before, 0.0), axis=-1, keepdims=True)
    carry_ref[0:1, :] = carry_ref[0:1, :] + jnp.sum(onehot, axis=0, keepdims=True)

    rec = jnp.where(lane == 0, (i1 - N_GROUPS).astype(F32),
          jnp.where(lane == 1, (i2 - N_GROUPS).astype(F32),
          jnp.where(lane == 2, w1,
          jnp.where(lane == 3, w2,
          jnp.where(lane == 4, rank1,
          jnp.where(lane == 5, rank2, 0.0))))))
    route_ref[...] = rec
    cnt_ref[...] = carry_ref[...]


def _outproj_router(o_a, o_bc, x2, wo, g, wr, br):
    t_tok = x2.shape[0]
    tm = min(TM_ROUTE, t_tok)
    row = lambda i: (i, 0)
    fixed = lambda i: (0, 0)
    return pl.pallas_call(
        _outproj_router_kernel,
        grid=(t_tok // tm,),
        in_specs=[
            pl.BlockSpec((tm, W_A), row),
            pl.BlockSpec((tm, W_B + C_WIDTH), row),
            pl.BlockSpec((tm, D_MODEL), row),
            pl.BlockSpec(wo.shape, fixed),
            pl.BlockSpec((1, D_MODEL), fixed),
            pl.BlockSpec(wr.shape, fixed),
            pl.BlockSpec((1, ROUTE_LANES), fixed),
        ],
        out_specs=[
            pl.BlockSpec((tm, D_MODEL), row),
            pl.BlockSpec((tm, ROUTE_LANES), row),
            pl.BlockSpec((SUBLANES, ROUTE_LANES), fixed),
        ],
        out_shape=[
            jax.ShapeDtypeStruct((t_tok, D_MODEL), F32),
            jax.ShapeDtypeStruct((t_tok, ROUTE_LANES), F32),
            jax.ShapeDtypeStruct((SUBLANES, ROUTE_LANES), F32),
        ],
        scratch_shapes=[pltpu.VMEM((SUBLANES, ROUTE_LANES), F32)],
        compiler_params=_cparams(("arbitrary",)),
        name="outproj_router",
    )(o_a, o_bc, x2, wo, g, wr, br)


def _scatter_kernel(dest_ref, x_ref, buf_in_ref, buf_ref, sem):
    del buf_in_ref
    tm = x_ref.shape[0]

    def copy(t, k):
        d = dest_ref[0, 0, 2 * t + k]
        return pltpu.make_async_copy(x_ref.at[pl.ds(t, 1), :], buf_ref.at[pl.ds(d, 1), :], sem)

    def start(t, carry):
        copy(t, 0).start()
        copy(t, 1).start()
        return carry

    def wait(t, carry):
        copy(t, 0).wait()
        copy(t, 1).wait()
        return carry

    lax.fori_loop(0, tm, start, 0)
    lax.fori_loop(0, tm, wait, 0)


def _dispatch(dest, x2, buf_init):
    t_tok = x2.shape[0]
    tm = min(TM_SCATTER, t_tok)
    n_tiles = t_tok // tm
    dest3 = dest.reshape(n_tiles, 1, 2 * tm)
    return pl.pallas_call(
        _scatter_kernel,
        grid=(n_tiles,),
        in_specs=[
            pl.BlockSpec((1, 1, 2 * tm), lambda i: (i, 0, 0), memory_space=pltpu.SMEM),
            pl.BlockSpec((tm, D_MODEL), lambda i: (i, 0)),
            pl.BlockSpec(memory_space=pl.ANY),
        ],
        out_specs=pl.BlockSpec(memory_space=pl.ANY),
        out_shape=jax.ShapeDtypeStruct(buf_init.shape, F32),
        scratch_shapes=[pltpu.SemaphoreType.DMA(())],
        input_output_aliases={2: 0},
        compiler_params=_cparams(("arbitrary",)),
        name="moe_dispatch",
    )(dest3, x2, buf_init)


def _experts_kernel(blk_e_ref, n_used_ref, x_ref, g_ref, wg_ref, wu_ref, wd_ref, y_ref):
    del blk_e_ref
    i = pl.program_id(0)

    @pl.when(i < n_used_ref[0])
    def _():
        x = x_ref[...]
        ms = jnp.mean(x * x, axis=-1, keepdims=True)
        hn = (x * lax.rsqrt(ms + EPS) * g_ref[...]).astype(BF16)
        a = jnp.dot(hn, wg_ref[0], preferred_element_type=F32)
        u = jnp.dot(hn, wu_ref[0], preferred_element_type=F32)
        act = (a * jax.nn.sigmoid(a) * u).astype(BF16)
        y_ref[...] = jnp.dot(act, wd_ref[0], preferred_element_type=F32)

    @pl.when(i >= n_used_ref[0])
    def _():
        y_ref[...] = jnp.zeros_like(y_ref)


def _experts(blk_e, n_used, buf, g, wg, wu, wd):
    n_blocks = buf.shape[0] // MOE_BLOCK
    grid_spec = pltpu.PrefetchScalarGridSpec(
        num_scalar_prefetch=2,
        grid=(n_blocks,),
        in_specs=[
            pl.BlockSpec((MOE_BLOCK, D_MODEL), lambda i, be, nu: (i, 0)),
            pl.BlockSpec((1, D_MODEL), lambda i, be, nu: (0, 0)),
            pl.BlockSpec((1, D_MODEL, D_EXPERT), lambda i, be, nu: (be[i], 0, 0)),
            pl.BlockSpec((1, D_MODEL, D_EXPERT), lambda i, be, nu: (be[i], 0, 0)),
            pl.BlockSpec((1, D_EXPERT, D_MODEL), lambda i, be, nu: (be[i], 0, 0)),
        ],
        out_specs=pl.BlockSpec((MOE_BLOCK, D_MODEL), lambda i, be, nu: (i, 0)),
    )
    return pl.pallas_call(
        _experts_kernel,
        grid_spec=grid_spec,
        out_shape=jax.ShapeDtypeStruct(buf.shape, F32),
        compiler_params=_cparams(("arbitrary",)),
        name="moe_experts",
    )(blk_e, n_used, buf, g, wg, wu, wd)


def _combine_kernel(dest_ref, x_ref, route_ref, y_ref, o_ref, g_ref, sem):
    tm = x_ref.shape[0]

    def copy(t, k):
        d = dest_ref[0, 0, 2 * t + k]
        return pltpu.make_async_copy(y_ref.at[pl.ds(d, 1), :], g_ref.at[k, pl.ds(t, 1), :], sem)

    def start(t, carry):
        copy(t, 0).start()
        copy(t, 1).start()
        return carry

    def wait(t, carry):
        copy(t, 0).wait()
        copy(t, 1).wait()
        return carry

    lax.fori_loop(0, tm, start, 0)
    lax.fori_loop(0, tm, wait, 0)
    route = route_ref[...]
    o_ref[...] = x_ref[...] + route[:, 2:3] * g_ref[0] + route[:, 3:4] * g_ref[1]


def _combine(dest, x2, route, y_buf):
    t_tok = x2.shape[0]
    tm = min(TM_COMBINE, t_tok)
    n_tiles = t_tok // tm
    dest3 = dest.reshape(n_tiles, 1, 2 * tm)
    return pl.pallas_call(
        _combine_kernel,
        grid=(n_tiles,),
        in_specs=[
            pl.BlockSpec((1, 1, 2 * tm), lambda i: (i, 0, 0), memory_space=pltpu.SMEM),
            pl.BlockSpec((tm, D_MODEL), lambda i: (i, 0)),
            pl.BlockSpec((tm, ROUTE_LANES), lambda i: (i, 0)),
            pl.BlockSpec(memory_space=pl.ANY),
        ],
        out_specs=pl.BlockSpec((tm, D_MODEL), lambda i: (i, 0)),
        out_shape=jax.ShapeDtypeStruct((t_tok, D_MODEL), F32),
        scratch_shapes=[pltpu.VMEM((2, tm, D_MODEL), F32), pltpu.SemaphoreType.DMA(())],
        compiler_params=_cparams(("arbitrary",)),
        name="moe_combine",
    )(dest3, x2, route, y_buf)


def _rope_tables(positions):
    half = ROT_DIM // 2
    inv_freq = ROPE_THETA ** (-jnp.arange(0, ROT_DIM, 2, dtype=F32) / ROT_DIM)
    ang = positions.reshape(-1).astype(F32)[:, None] * inv_freq
    cos, sin = jnp.cos(ang), jnp.sin(ang)
    t_tok = ang.shape[0]
    rest = DK_A - ROT_DIM
    c64 = jnp.concatenate([cos, cos, jnp.ones((t_tok, rest), F32)], axis=1)
    s1 = jnp.concatenate([-sin, jnp.zeros((t_tok, DK_A - half), F32)], axis=1)
    s2 = jnp.concatenate([jnp.zeros((t_tok, half), F32), sin, jnp.zeros((t_tok, rest), F32)], axis=1)
    rep = lambda a: jnp.concatenate([a, a], axis=1)
    return rep(c64), rep(s1), rep(s2)


def _block_avg(width, group):
    idx = jnp.arange(width) // group
    return jnp.where(idx[:, None] == idx[None, :], 1.0 / group, 0.0).astype(BF16)


def kernel(x, positions, norm_mix_g, w_in, attn_q_norm_g, attn_k_norm_g, lambda_q1, lambda_k1, lambda_q2,
           lambda_k2, attn_subln_g, hgrn_lower_bounds, hgrn_norm_g, conv_w, w_out, norm_ffn_g, router_group_w,
           router_group_b, router_expert_w, router_expert_b, expert_w_gate, expert_w_up, expert_w_down):
    batch, seq, d = x.shape
    assert d == D_MODEL
    depth = w_in.shape[0]
    t_tok = batch * seq
    n_assign = 2 * t_tok
    n_blocks = n_assign // MOE_BLOCK + N_EXPERTS
    cap = n_blocks * MOE_BLOCK

    lb_probs = jax.nn.softmax(hgrn_lower_bounds.astype(F32), axis=0)
    lower_bounds = jnp.cumsum(lb_probs, axis=0) - lb_probs[0:1]
    cos_t, s1_t, s2_t = _rope_tables(positions)
    gsum_a = _block_avg(W_A, DK_A)
    gavg_b = _block_avg(W_B, DV_B)

    o_q, o_k, o_v = 0, W_A, 2 * W_A
    o_bq = 3 * W_A
    o_bf, o_bi, o_bg = o_bq + W_B, o_bq + 2 * W_B, o_bq + 3 * W_B
    o_c = o_bq + 4 * W_B

    x2 = x.reshape(t_tok, d)
    for l in range(depth):
        w = w_in[l]
        wqk = w[:, o_q:o_v].astype(BF16)
        wv = w[:, o_v:o_bq].astype(BF16)
        whb = jnp.concatenate([w[:, o_bq:o_bf], w[:, o_bi:o_c]], axis=1).astype(BF16)
        wf = w[:, o_bf:o_bi].astype(BF16)
        wc = w[:, o_c:].astype(BF16)
        gqk = jnp.concatenate([jnp.tile(attn_q_norm_g[l], 2 * H_A) * (DK_A ** -0.5),
                               jnp.tile(attn_k_norm_g[l], 2 * H_A)]).reshape(1, 2 * W_A).astype(F32)
        qk, v, hb, f, cv = _inproj(x2, norm_mix_g[l].reshape(1, d), wqk, wv, whb, wf, wc, gsum_a, gqk,
                                   cos_t, s1_t, s2_t)

        lam_init = 0.8 - 0.6 * math.exp(-0.3 * l)
        lam = (jnp.exp(jnp.sum(lambda_q1[l].astype(F32) * lambda_k1[l].astype(F32)))
               - jnp.exp(jnp.sum(lambda_q2[l].astype(F32) * lambda_k2[l].astype(F32))) + lam_init).reshape(1)
        o_a = _attention(qk, v, lam, attn_subln_g[l].reshape(DV_A, 1).astype(F32), batch, seq, lam_init)

        o_bc = _hgrn_conv(hb, f, cv, lower_bounds[l].reshape(1, W_B), jnp.tile(hgrn_norm_g[l], H_B).reshape(1, W_B),
                          conv_w[l], gavg_b, batch, seq)

        wr = jnp.concatenate([router_group_w[l], router_expert_w[l],
                              jnp.zeros((d, ROUTE_LANES - N_LOGITS), F32)], axis=1).astype(BF16)
        br = jnp.concatenate([router_group_b[l], router_expert_b[l],
                              jnp.zeros((ROUTE_LANES - N_LOGITS,), F32)]).reshape(1, ROUTE_LANES)
        x2, route, cnt = _outproj_router(o_a, o_bc, x2, w_out[l].astype(BF16), norm_ffn_g[l].reshape(1, d), wr, br)

        counts = cnt[0, N_GROUPS:N_GROUPS + N_EXPERTS].astype(jnp.int32)
        padded = (counts + MOE_BLOCK - 1) // MOE_BLOCK * MOE_BLOCK
        pad_end = jnp.cumsum(padded)
        pad_start = pad_end - padded
        e_idx = route[:, 0:2].astype(jnp.int32)
        rank = route[:, 4:6].astype(jnp.int32)
        start_of = jnp.sum(jnp.where(e_idx[..., None] == jnp.arange(N_EXPERTS), pad_start, 0), axis=-1)
        dest = (start_of + rank).reshape(-1)
        blk_e = jnp.minimum(jnp.searchsorted(pad_end, jnp.arange(n_blocks) * MOE_BLOCK, side='right'),
                            N_EXPERTS - 1).astype(jnp.int32)
        n_used = (pad_end[-1] // MOE_BLOCK).astype(jnp.int32).reshape(1)

        buf = _dispatch(dest, x2, jnp.zeros((cap, d), F32))
        y_buf = _experts(blk_e, n_used, buf, norm_ffn_g[l].reshape(1, d), expert_w_gate[l].astype(BF16),
                         expert_w_up[l].astype(BF16), expert_w_down[l].astype(BF16))
        x2 = _combine(dest, x2, route, y_buf)
    return x2.reshape(batch, seq, d)
```

```python
import functools
import math

import jax
import jax.numpy as jnp
from jax import lax
from jax.experimental import pallas as pl
from jax.experimental.pallas import tpu as pltpu

F32 = jnp.float32
BF16 = jnp.bfloat16

LANES = 128
SUBLANES = 8

D_MODEL = 1024
CHUNK = 64
EPS = 1e-6
MASK_VALUE = -1e30
TINY = 1e-30
H_A = 4
DK_A = D_MODEL // 16
DV_A = 2 * DK_A
ROT_DIM = DK_A // 4
ROPE_THETA = 500000.0
H_B = 4
DK_B = D_MODEL // 16
DV_B = D_MODEL // 16
C_WIDTH = D_MODEL // 4
CONV_WIDTH = 3
N_GROUPS = 4
EXPERTS_PER_GROUP = 8
N_EXPERTS = N_GROUPS * EXPERTS_PER_GROUP
D_EXPERT = D_MODEL // 2

W_A = H_A * DV_A
W_B = H_B * DK_B
ROUTE_LANES = LANES
N_LOGITS = N_GROUPS + N_EXPERTS

TM_PROJ = 512
TQ = 256
TK = 256
TM_ROUTE = 512
MOE_BLOCK = 512
TM_SCATTER = 512
TM_COMBINE = 256
VMEM_LIMIT = 56 * 1024 * 1024


def _cparams(sem, vmem=VMEM_LIMIT):
    return pltpu.CompilerParams(dimension_semantics=sem, vmem_limit_bytes=vmem)


def _inproj_kernel(x_ref, g_ref, wqk_ref, wv_ref, whb_ref, wf_ref, wc_ref, gsum_ref, gqk_ref,
                   cos_ref, s1_ref, s2_ref, qk_out, v_out, hb_out, f_out, c_out):
    x = x_ref[...]
    ms = jnp.mean(x * x, axis=-1, keepdims=True)
    xn = (x * lax.rsqrt(ms + EPS) * g_ref[...]).astype(BF16)

    v_out[...] = jnp.dot(xn, wv_ref[...], preferred_element_type=F32).astype(BF16)
    hb_out[...] = jnp.dot(xn, whb_ref[...], preferred_element_type=F32).astype(BF16)
    f_out[...] = jnp.dot(xn, wf_ref[...], preferred_element_type=F32)
    c_out[...] = jnp.dot(xn, wc_ref[...], preferred_element_type=F32).astype(BF16)

    cos = cos_ref[...]
    s1 = s1_ref[...]
    s2 = s2_ref[...]
    gsum = gsum_ref[...]
    for part in range(2):
        lo = part * W_A
        t = jnp.dot(xn, wqk_ref[:, lo:lo + W_A], preferred_element_type=F32)
        msq = jnp.dot((t * t).astype(BF16), gsum, preferred_element_type=F32)
        t = t * lax.rsqrt(msq + EPS) * gqk_ref[:, lo:lo + W_A]
        for h in range(H_A):
            slab = t[:, h * LANES:(h + 1) * LANES]
            up = pltpu.roll(slab, LANES - ROT_DIM // 2, axis=1)
            dn = pltpu.roll(slab, ROT_DIM // 2, axis=1)
            qk_out[:, lo + h * LANES:lo + (h + 1) * LANES] = (slab * cos + up * s1 + dn * s2).astype(BF16)


def _inproj(x2, g, wqk, wv, whb, wf, wc, gsum, gqk, cos_t, s1_t, s2_t):
    t_tok = x2.shape[0]
    tm = min(TM_PROJ, t_tok)
    grid = (t_tok // tm,)
    row = lambda i: (i, 0)
    fixed = lambda i: (0, 0)
    return pl.pallas_call(
        _inproj_kernel,
        grid=grid,
        in_specs=[
            pl.BlockSpec((tm, D_MODEL), row),
            pl.BlockSpec((1, D_MODEL), fixed),
            pl.BlockSpec(wqk.shape, fixed),
            pl.BlockSpec(wv.shape, fixed),
            pl.BlockSpec(whb.shape, fixed),
            pl.BlockSpec(wf.shape, fixed),
            pl.BlockSpec(wc.shape, fixed),
            pl.BlockSpec(gsum.shape, fixed),
            pl.BlockSpec(gqk.shape, fixed),
            pl.BlockSpec((tm, LANES), row),
            pl.BlockSpec((tm, LANES), row),
            pl.BlockSpec((tm, LANES), row),
        ],
        out_specs=[
            pl.BlockSpec((tm, 2 * W_A), row),
            pl.BlockSpec((tm, W_A), row),
            pl.BlockSpec((tm, 3 * W_B), row),
            pl.BlockSpec((tm, W_B), row),
            pl.BlockSpec((tm, 3 * C_WIDTH), row),
        ],
        out_shape=[
            jax.ShapeDtypeStruct((t_tok, 2 * W_A), BF16),
            jax.ShapeDtypeStruct((t_tok, W_A), BF16),
            jax.ShapeDtypeStruct((t_tok, 3 * W_B), BF16),
            jax.ShapeDtypeStruct((t_tok, W_B), F32),
            jax.ShapeDtypeStruct((t_tok, 3 * C_WIDTH), BF16),
        ],
        compiler_params=_cparams(("parallel",)),
        name="inproj",
    )(x2, g, wqk, wv, whb, wf, wc, gsum, gqk, cos_t, s1_t, s2_t)


def _attn_kernel(lam_ref, q_ref, k_ref, v_ref, g_ref, o_ref, qc_ref, sa_ref, sb_ref, acc_ref, m_ref, l_ref, *,
                 lam_init):
    qi = pl.program_id(1)
    q = q_ref[...]
    lane = lax.broadcasted_iota(jnp.int32, (TQ, DV_A), 1)
    zero = jnp.zeros((TQ, DV_A), BF16)
    for h in range(H_A):
        qh = q[:, h * DV_A:(h + 1) * DV_A]
        qc_ref[h, 0] = jnp.where(lane < DK_A, qh, zero)
        qc_ref[h, 1] = jnp.where(lane >= DK_A, qh, zero)

    acc_ref[...] = jnp.zeros_like(acc_ref)
    m_ref[...] = jnp.full_like(m_ref, -jnp.inf)
    l_ref[...] = jnp.zeros_like(l_ref)

    kpos = lax.broadcasted_iota(jnp.int32, (TK, TQ), 0) // CHUNK
    qpos = lax.broadcasted_iota(jnp.int32, (TK, TQ), 1) // CHUNK
    diag_mask = kpos <= qpos

    def scores(j, s_ref):
        start = pl.multiple_of(j * TK, TK)
        kb = k_ref[pl.ds(start, TK), :]
        for h in range(H_A):
            kh = kb[:, h * DV_A:(h + 1) * DV_A]
            for c in range(2):
                s_ref[h, c] = lax.dot_general(kh, qc_ref[h, c], (((1,), (1,)), ((), ())),
                                              preferred_element_type=F32)

    def consume(j, s_ref, masked):
        start = pl.multiple_of(j * TK, TK)
        vb = v_ref[pl.ds(start, TK), :]
        for h in range(H_A):
            vh = vb[:, h * DV_A:(h + 1) * DV_A]
            for c in range(2):
                s = s_ref[h, c]
                if masked:
                    s = jnp.where(diag_mask, s, MASK_VALUE)
                m_old = m_ref[h, c]
                m_new = jnp.maximum(m_old, jnp.max(s, axis=0, keepdims=True))
                alpha = jnp.exp(m_old - m_new)
                p = jnp.exp(s - m_new)
                l_ref[h, c] = alpha * l_ref[h, c] + jnp.sum(p, axis=0, keepdims=True)
                pv = lax.dot_general(vh, p.astype(BF16), (((0,), (0,)), ((), ())),
                                     preferred_element_type=F32)
                acc_ref[h, c] = alpha * acc_ref[h, c] + pv
                m_ref[h, c] = m_new

    scores(0, sa_ref)

    def body(j, carry):
        @pl.when(j % 2 == 0)
        def _():
            scores(j + 1, sb_ref)
            consume(j, sa_ref, False)

        @pl.when(j % 2 == 1)
        def _():
            scores(j + 1, sa_ref)
            consume(j, sb_ref, False)

        return carry

    lax.fori_loop(0, qi, body, 0)

    @pl.when(qi % 2 == 0)
    def _():
        consume(qi, sa_ref, True)

    @pl.when(qi % 2 == 1)
    def _():
        consume(qi, sb_ref, True)

    lam = lam_ref[0]
    for h in range(H_A):
        o = acc_ref[h, 0] / l_ref[h, 0] - lam * (acc_ref[h, 1] / l_ref[h, 1])
        ms = jnp.mean(o * o, axis=0, keepdims=True)
        y = o * lax.rsqrt(ms + EPS) * g_ref[...] * (1.0 - lam_init)
        o_ref[:, h * DV_A:(h + 1) * DV_A] = y.T.astype(BF16)


def _attention(qk, v, lam, subln_col, batch, seq, lam_init):
    assert TQ == TK and TQ % CHUNK == 0 and seq % TQ == 0
    nq = seq // TQ
    t_tok = batch * seq
    kernel = functools.partial(_attn_kernel, lam_init=lam_init)
    return pl.pallas_call(
        kernel,
        grid=(batch, nq),
        in_specs=[
            pl.BlockSpec(memory_space=pltpu.SMEM),
            pl.BlockSpec((TQ, W_A), lambda b, i: (b * nq + i, 0)),
            pl.BlockSpec((seq, W_A), lambda b, i: (b, 1)),
            pl.BlockSpec((seq, W_A), lambda b, i: (b, 0)),
            pl.BlockSpec((DV_A, 1), lambda b, i: (0, 0)),
        ],
        out_specs=pl.BlockSpec((TQ, W_A), lambda b, i: (b * nq + i, 0)),
        out_shape=jax.ShapeDtypeStruct((t_tok, W_A), BF16),
        scratch_shapes=[
            pltpu.VMEM((H_A, 2, TQ, DV_A), BF16),
            pltpu.VMEM((H_A, 2, TK, TQ), F32),
            pltpu.VMEM((H_A, 2, TK, TQ), F32),
            pltpu.VMEM((H_A, 2, DV_A, TQ), F32),
            pltpu.VMEM((H_A, 2, 1, TQ), F32),
            pltpu.VMEM((H_A, 2, 1, TQ), F32),
        ],
        compiler_params=_cparams(("parallel", "arbitrary")),
        name="diff_attn",
    )(lam, qk, qk, v, subln_col)


_LEVELS = (32, 16, 8, 4, 2, 1)


def _ref_rows(g, h):
    n = g.shape[1]
    if 2 * h >= SUBLANES:
        pieces = []
        for m in range(CHUNK // (2 * h)):
            r = 2 * h * m + h - 1
            pieces.append(jnp.broadcast_to(g[r:r + 1, :], (2 * h, n)))
        return pieces[0] if len(pieces) == 1 else jnp.concatenate(pieces, axis=0)
    g3 = g.reshape(CHUNK // SUBLANES, SUBLANES, n)
    sub = lax.broadcasted_iota(jnp.int32, g3.shape, 1)
    out = None
    for m in reversed(range(SUBLANES // (2 * h))):
        r = 2 * h * m + h - 1
        piece = jnp.broadcast_to(g3[:, r:r + 1, :], g3.shape)
        out = piece if out is None else jnp.where(sub < 2 * h * (m + 1), piece, out)
    return out.reshape(CHUNK, n)


def _stack_heads(a, lane_head):
    zero = jnp.zeros_like(a)
    return jnp.concatenate([jnp.where(lane_head == hh, a, zero) for hh in range(H_B)], axis=0)


def _hgrn_kernel(hb_ref, f_ref, c_ref, lb_ref, ng_ref, cw_ref, gavg_ref, o_ref, state_ref, hbuf_ref):
    seq = hb_ref.shape[0]
    n_chunks = seq // CHUNK
    lb = lb_ref[...]
    ng = ng_ref[...]
    gavg = gavg_ref[...]

    row = lax.broadcasted_iota(jnp.int32, (CHUNK, W_B), 0)
    lane = lax.broadcasted_iota(jnp.int32, (CHUNK, W_B), 1)
    lane_head = lane // DK_B
    lane_s = lane % CHUNK
    tri = (lax.broadcasted_iota(jnp.int32, (CHUNK, CHUNK), 1)
           <= lax.broadcasted_iota(jnp.int32, (CHUNK, CHUNK), 0)).astype(BF16)
    lvl_mask = [(row // (2 * h)) == (lane_s // (2 * h)) for h in _LEVELS]
    eye_mask = row == lane_s
    r4 = lax.broadcasted_iota(jnp.int32, (H_B * DK_B, W_B), 0) // DK_B
    c4 = lax.broadcasted_iota(jnp.int32, (H_B * DK_B, W_B), 1) // DK_B
    bd_mask = r4 == c4

    state_ref[...] = jnp.zeros_like(state_ref)

    def chunk(i, carry):
        r0 = pl.multiple_of(i * CHUNK, CHUNK)
        hb = hb_ref[pl.ds(r0, CHUNK), :].astype(F32)
        zq = hb[:, 0:W_B]
        v = hb[:, W_B:2 * W_B]
        zg = hb[:, 2 * W_B:3 * W_B]
        zf = f_ref[pl.ds(r0, CHUNK), :]

        f = lb + (1.0 - lb) * jax.nn.sigmoid(zf)
        logf = jnp.log(jnp.maximum(f, TINY))
        key = (1.0 - lb) * jax.nn.sigmoid(-zf)
        q = zq * jax.nn.sigmoid(zq)
        vb16 = v.astype(BF16)

        hi = logf.astype(BF16)
        r1 = logf - hi.astype(F32)
        mid = r1.astype(BF16)
        lo = (r1 - mid.astype(F32)).astype(BF16)
        g = (jnp.dot(tri, hi, preferred_element_type=F32)
             + jnp.dot(tri, mid, preferred_element_type=F32)
             + jnp.dot(tri, lo, preferred_element_type=F32))

        a_all = jnp.where(eye_mask,
                          lax.dot_general(q.astype(BF16), _stack_heads(key.astype(BF16), lane_head[0:CHUNK]),
                                          (((1,), (1,)), ((), ())), preferred_element_type=F32), 0.0)
        for h, mask in zip(_LEVELS, lvl_mask):
            w = jnp.exp(-jnp.abs(g - _ref_rows(g, h)))
            second = (row % (2 * h)) >= h
            qh = jnp.where(second, q * w, 0.0).astype(BF16)
            kh = jnp.where(second, 0.0, key * w).astype(BF16)
            ah = lax.dot_general(qh, _stack_heads(kh, lane_head), (((1,), (1,)), ((), ())),
                                 preferred_element_type=F32)
            a_all = a_all + jnp.where(mask, ah, 0.0)
        o = jnp.dot(a_all.astype(BF16), _stack_heads(vb16, lane_head), preferred_element_type=F32)

        st = state_ref[...]
        qg = (q * jnp.exp(g)).astype(BF16)
        o = o + lax.dot_general(qg, st.astype(BF16), (((1,), (1,)), ((), ())), preferred_element_type=F32)
        g_last = g[CHUNK - 1:CHUNK, :]
        kd = (key * jnp.exp(g_last - g)).astype(BF16)
        upd = lax.dot_general(vb16, kd, (((0,), (0,)), ((), ())), preferred_element_type=F32)
        state_ref[...] = jnp.exp(g_last) * st + jnp.where(bd_mask, upd, 0.0)

        ms = jnp.dot((o * o).astype(BF16), gavg, preferred_element_type=F32)
        y = o * lax.rsqrt(ms + EPS) * ng * (zg * jax.nn.sigmoid(zg))
        o_ref[pl.ds(r0, CHUNK), 0:W_B] = y.astype(BF16)
        return carry

    lax.fori_loop(0, n_chunks, chunk, 0)

    piece = min(512, seq)
    hbuf_ref[0:SUBLANES, :] = jnp.zeros((SUBLANES, C_WIDTH), F32)
    for p in range(seq // piece):
        cblk = c_ref[p * piece:(p + 1) * piece, :].astype(F32)
        hbuf_ref[SUBLANES + p * piece:SUBLANES + (p + 1) * piece, :] = cblk[:, C_WIDTH:2 * C_WIDTH] * cblk[:, 2 * C_WIDTH:]
    for p in range(seq // piece):
        base = SUBLANES + p * piece
        y = (cw_ref[0:1, :] * hbuf_ref[base - 2:base - 2 + piece, :]
             + cw_ref[1:2, :] * hbuf_ref[base - 1:base - 1 + piece, :]
             + cw_ref[2:3, :] * hbuf_ref[base:base + piece, :])
        bgate = c_ref[p * piece:(p + 1) * piece, 0:C_WIDTH].astype(F32)
        o_ref[p * piece:(p + 1) * piece, W_B:W_B + C_WIDTH] = (bgate * y).astype(BF16)


def _hgrn_conv(hb, f, cv, lb, ng, cw, gavg, batch, seq):
    t_tok = batch * seq
    fixed = lambda b: (0, 0)
    return pl.pallas_call(
        _hgrn_kernel,
        grid=(batch,),
        in_specs=[
            pl.BlockSpec((seq, 3 * W_B), lambda b: (b, 0)),
            pl.BlockSpec((seq, W_B), lambda b: (b, 0)),
            pl.BlockSpec((seq, 3 * C_WIDTH), lambda b: (b, 0)),
            pl.BlockSpec((1, W_B), fixed),
            pl.BlockSpec((1, W_B), fixed),
            pl.BlockSpec((CONV_WIDTH, C_WIDTH), fixed),
            pl.BlockSpec((W_B, W_B), fixed),
        ],
        out_specs=pl.BlockSpec((seq, W_B + C_WIDTH), lambda b: (b, 0)),
        out_shape=jax.ShapeDtypeStruct((t_tok, W_B + C_WIDTH), BF16),
        scratch_shapes=[
            pltpu.VMEM((H_B * DV_B, W_B), F32),
            pltpu.VMEM((seq + SUBLANES, C_WIDTH), F32),
        ],
        compiler_params=_cparams(("parallel",)),
        name="hgrn_conv",
    )(hb, f, cv, lb, ng, cw, gavg)


def _outproj_router_kernel(oa_ref, obc_ref, x_ref, wo_ref, g_ref, wr_ref, br_ref,
                           xo_ref, route_ref, cnt_ref, carry_ref):
    i = pl.program_id(0)

    @pl.when(i == 0)
    def _():
        carry_ref[...] = jnp.zeros_like(carry_ref)

    mixed = (jnp.dot(oa_ref[...], wo_ref[0:W_A, :], preferred_element_type=F32)
             + jnp.dot(obc_ref[...], wo_ref[W_A:, :], preferred_element_type=F32))
    xn = x_ref[...] + mixed
    xo_ref[...] = xn

    ms = jnp.mean(xn * xn, axis=-1, keepdims=True)
    hn = (xn * lax.rsqrt(ms + EPS) * g_ref[...]).astype(BF16)
    logits = jnp.dot(hn, wr_ref[...], preferred_element_type=F32) + br_ref[...]

    tm = logits.shape[0]
    lane = lax.broadcasted_iota(jnp.int32, (tm, ROUTE_LANES), 1)
    neg = -jnp.inf
    big = ROUTE_LANES

    def top(mask):
        val = jnp.max(jnp.where(mask, logits, neg), axis=-1, keepdims=True)
        idx = jnp.min(jnp.where(mask & (logits == val), lane, big), axis=-1, keepdims=True)
        return val, idx

    gmask = lane < N_GROUPS
    gmax, gidx = top(gmask)
    p_g = 1.0 / jnp.sum(jnp.where(gmask, jnp.exp(logits - gmax), 0.0), axis=-1, keepdims=True)
    e_lo = N_GROUPS + EXPERTS_PER_GROUP * gidx
    emask = (lane >= e_lo) & (lane < e_lo + EXPERTS_PER_GROUP)
    v1, i1 = top(emask)
    v2, i2 = top(emask & (lane != i1))
    t2 = jnp.exp(v2 - v1)
    w1 = p_g / (1.0 + t2)
    w2 = p_g * t2 / (1.0 + t2)

    sel1 = lane == i1
    sel2 = lane == i2
    onehot = jnp.where(sel1 | sel2, 1.0, 0.0)
    strict = (lax.broadcasted_iota(jnp.int32, (tm, tm), 1)
              < lax.broadcasted_iota(jnp.int32, (tm, tm), 0)).astype(BF16)
    before = jnp.dot(strict, onehot.astype(BF16), preferred_element_type=F32) + carry_ref[0:1, :]
    rank1 = jnp.sum(jnp.where(sel1, before, 0.0), axis=-1, keepdims=True)
    rank2 = jnp.sum(jnp.where(sel2, before, 0.0), axis=-1, keepdims=True)
    carry_ref[0:1, :] = carry_ref[0:1, :] + jnp.sum(onehot, axis=0, keepdims=True)

    rec = jnp.where(lane == 0, (i1 - N_GROUPS).astype(F32),
          jnp.where(lane == 1, (i2 - N_GROUPS).astype(F32),
          jnp.where(lane == 2, w1,
          jnp.where(lane == 3, w2,
          jnp.where(lane == 4, rank1,
          jnp.where(lane == 5, rank2, 0.0))))))
    route_ref[...] = rec
    cnt_ref[...] = carry_ref[...]


def _outproj_router(o_a, o_bc, x2, wo, g, wr, br):
    t_tok = x2.shape[0]
    tm = min(TM_ROUTE, t_tok)
    row = lambda i: (i, 0)
    fixed = lambda i: (0, 0)
    return pl.pallas_call(
        _outproj_router_kernel,
        grid=(t_tok // tm,),
        in_specs=[
            pl.BlockSpec((tm, W_A), row),
            pl.BlockSpec((tm, W_B + C_WIDTH), row),
            pl.BlockSpec((tm, D_MODEL), row),
            pl.BlockSpec(wo.shape, fixed),
            pl.BlockSpec((1, D_MODEL), fixed),
            pl.BlockSpec(wr.shape, fixed),
            pl.BlockSpec((1, ROUTE_LANES), fixed),
        ],
        out_specs=[
            pl.BlockSpec((tm, D_MODEL), row),
            pl.BlockSpec((tm, ROUTE_LANES), row),
            pl.BlockSpec((SUBLANES, ROUTE_LANES), fixed),
        ],
        out_shape=[
            jax.ShapeDtypeStruct((t_tok, D_MODEL), F32),
            jax.ShapeDtypeStruct((t_tok, ROUTE_LANES), F32),
            jax.ShapeDtypeStruct((SUBLANES, ROUTE_LANES), F32),
        ],
        scratch_shapes=[pltpu.VMEM((SUBLANES, ROUTE_LANES), F32)],
        compiler_params=_cparams(("arbitrary",)),
        name="outproj_router",
    )(o_a, o_bc, x2, wo, g, wr, br)


def _scatter_kernel(dest_ref, x_ref, buf_in_ref, buf_ref, sem):
    del buf_in_ref
    tm = x_ref.shape[0]

    def copy(t, k):
        d = dest_ref[0, 0, 2 * t + k]
        return pltpu.make_async_copy(x_ref.at[pl.ds(t, 1), :], buf_ref.at[pl.ds(d, 1), :], sem)

    def start(t, carry):
        copy(t, 0).start()
        copy(t, 1).start()
        return carry

    def wait(t, carry):
        copy(t, 0).wait()
        copy(t, 1).wait()
        return carry

    lax.fori_loop(0, tm, start, 0)
    lax.fori_loop(0, tm, wait, 0)


def _dispatch(dest, x2, buf_init):
    t_tok = x2.shape[0]
    tm = min(TM_SCATTER, t_tok)
    n_tiles = t_tok // tm
    dest3 = dest.reshape(n_tiles, 1, 2 * tm)
    return pl.pallas_call(
        _scatter_kernel,
        grid=(n_tiles,),
        in_specs=[
            pl.BlockSpec((1, 1, 2 * tm), lambda i: (i, 0, 0), memory_space=pltpu.SMEM),
            pl.BlockSpec((tm, D_MODEL), lambda i: (i, 0)),
            pl.BlockSpec(memory_space=pl.ANY),
        ],
        out_specs=pl.BlockSpec(memory_space=pl.ANY),
        out_shape=jax.ShapeDtypeStruct(buf_init.shape, F32),
        scratch_shapes=[pltpu.SemaphoreType.DMA(())],
        input_output_aliases={2: 0},
        compiler_params=_cparams(("arbitrary",)),
        name="moe_dispatch",
    )(dest3, x2, buf_init)


def _experts_kernel(blk_e_ref, n_used_ref, x_ref, g_ref, wg_ref, wu_ref, wd_ref, y_ref):
    del blk_e_ref
    i = pl.program_id(0)

    @pl.when(i < n_used_ref[0])
    def _():
        x = x_ref[...]
        ms = jnp.mean(x * x, axis=-1, keepdims=True)
        hn = (x * lax.rsqrt(ms + EPS) * g_ref[...]).astype(BF16)
        a = jnp.dot(hn, wg_ref[0], preferred_element_type=F32)
        u = jnp.dot(hn, wu_ref[0], preferred_element_type=F32)
        act = (a * jax.nn.sigmoid(a) * u).astype(BF16)
        y_ref[...] = jnp.dot(act, wd_ref[0], preferred_element_type=F32)

    @pl.when(i >= n_used_ref[0])
    def _():
        y_ref[...] = jnp.zeros_like(y_ref)


def _experts(blk_e, n_used, buf, g, wg, wu, wd):
    n_blocks = buf.shape[0] // MOE_BLOCK
    grid_spec = pltpu.PrefetchScalarGridSpec(
        num_scalar_prefetch=2,
        grid=(n_blocks,),
        in_specs=[
            pl.BlockSpec((MOE_BLOCK, D_MODEL), lambda i, be, nu: (i, 0)),
            pl.BlockSpec((1, D_MODEL), lambda i, be, nu: (0, 0)),
            pl.BlockSpec((1, D_MODEL, D_EXPERT), lambda i, be, nu: (be[i], 0, 0)),
            pl.BlockSpec((1, D_MODEL, D_EXPERT), lambda i, be, nu: (be[i], 0, 0)),
            pl.BlockSpec((1, D_EXPERT, D_MODEL), lambda i, be, nu: (be[i], 0, 0)),
        ],
        out_specs=pl.BlockSpec((MOE_BLOCK, D_MODEL), lambda i, be, nu: (i, 0)),
    )
    return pl.pallas_call(
        _experts_kernel,
        grid_spec=grid_spec,
        out_shape=jax.ShapeDtypeStruct(buf.shape, F32),
        compiler_params=_cparams(("arbitrary",)),
        name="moe_experts",
    )(blk_e, n_used, buf, g, wg, wu, wd)


def _combine_kernel(dest_ref, x_ref, route_ref, y_ref, o_ref, g_ref, sem):
    tm = x_ref.shape[0]

    def copy(t, k):
        d = dest_ref[0, 0, 2 * t + k]
        return pltpu.make_async_copy(y_ref.at[pl.ds(d, 1), :], g_ref.at[k, pl.ds(t, 1), :], sem)

    def start(t, carry):
        copy(t, 0).start()
        copy(t, 1).start()
        return carry

    def wait(t, carry):
        copy(t, 0).wait()
        copy(t, 1).wait()
        return carry

    lax.fori_loop(0, tm, start, 0)
    lax.fori_loop(0, tm, wait, 0)
    route = route_ref[...]
    o_ref[...] = x_ref[...] + route[:, 2:3] * g_ref[0] + route[:, 3:4] * g_ref[1]


def _combine(dest, x2, route, y_buf):
    t_tok = x2.shape[0]
    tm = min(TM_COMBINE, t_tok)
    n_tiles = t_tok // tm
    dest3 = dest.reshape(n_tiles, 1, 2 * tm)
    return pl.pallas_call(
        _combine_kernel,
        grid=(n_tiles,),
        in_specs=[
            pl.BlockSpec((1, 1, 2 * tm), lambda i: (i, 0, 0), memory_space=pltpu.SMEM),
            pl.BlockSpec((tm, D_MODEL), lambda i: (i, 0)),
            pl.BlockSpec((tm, ROUTE_LANES), lambda i: (i, 0)),
            pl.BlockSpec(memory_space=pl.ANY),
        ],
        out_specs=pl.BlockSpec((tm, D_MODEL), lambda i: (i, 0)),
        out_shape=jax.ShapeDtypeStruct((t_tok, D_MODEL), F32),
        scratch_shapes=[pltpu.VMEM((2, tm, D_MODEL), F32), pltpu.SemaphoreType.DMA(())],
        compiler_params=_cparams(("arbitrary",)),
        name="moe_combine",
    )(dest3, x2, route, y_buf)


def _rope_tables(positions):
    half = ROT_DIM // 2
    inv_freq = ROPE_THETA ** (-jnp.arange(0, ROT_DIM, 2, dtype=F32) / ROT_DIM)
    ang = positions.reshape(-1).astype(F32)[:, None] * inv_freq
    cos, sin = jnp.cos(ang), jnp.sin(ang)
    t_tok = ang.shape[0]
    rest = DK_A - ROT_DIM
    c64 = jnp.concatenate([cos, cos, jnp.ones((t_tok, rest), F32)], axis=1)
    s1 = jnp.concatenate([-sin, jnp.zeros((t_tok, DK_A - half), F32)], axis=1)
    s2 = jnp.concatenate([jnp.zeros((t_tok, half), F32), sin, jnp.zeros((t_tok, rest), F32)], axis=1)
    rep = lambda a: jnp.concatenate([a, a], axis=1)
    return rep(c64), rep(s1), rep(s2)


def _block_avg(width, group):
    idx = jnp.arange(width) // group
    return jnp.where(idx[:, None] == idx[None, :], 1.0 / group, 0.0).astype(BF16)


def kernel(x, positions, norm_mix_g, w_in, attn_q_norm_g, attn_k_norm_g, lambda_q1, lambda_k1, lambda_q2,
           lambda_k2, attn_subln_g, hgrn_lower_bounds, hgrn_norm_g, conv_w, w_out, norm_ffn_g, router_group_w,
           router_group_b, router_expert_w, router_expert_b, expert_w_gate, expert_w_up, expert_w_down):
    batch, seq, d = x.shape
    assert d == D_MODEL
    depth = w_in.shape[0]
    t_tok = batch * seq
    n_assign = 2 * t_tok
    n_blocks = n_assign // MOE_BLOCK + N_EXPERTS
    cap = n_blocks * MOE_BLOCK

    lb_probs = jax.nn.softmax(hgrn_lower_bounds.astype(F32), axis=0)
    lower_bounds = jnp.cumsum(lb_probs, axis=0) - lb_probs[0:1]
    cos_t, s1_t, s2_t = _rope_tables(positions)
    gsum_a = _block_avg(W_A, DK_A)
    gavg_b = _block_avg(W_B, DV_B)

    o_q, o_k, o_v = 0, W_A, 2 * W_A
    o_bq = 3 * W_A
    o_bf, o_bi, o_bg = o_bq + W_B, o_bq + 2 * W_B, o_bq + 3 * W_B
    o_c = o_bq + 4 * W_B

    x2 = x.reshape(t_tok, d)
    for l in range(depth):
        w = w_in[l]
        wqk = w[:, o_q:o_v].astype(BF16)
        wv = w[:, o_v:o_bq].astype(BF16)
        whb = jnp.concatenate([w[:, o_bq:o_bf], w[:, o_bi:o_c]], axis=1).astype(BF16)
        wf = w[:, o_bf:o_bi].astype(BF16)
        wc = w[:, o_c:].astype(BF16)
        gqk = jnp.concatenate([jnp.tile(attn_q_norm_g[l], 2 * H_A) * (DK_A ** -0.5),
                               jnp.tile(attn_k_norm_g[l], 2 * H_A)]).reshape(1, 2 * W_A).astype(F32)
        qk, v, hb, f, cv = _inproj(x2, norm_mix_g[l].reshape(1, d), wqk, wv, whb, wf, wc, gsum_a, gqk,
                                   cos_t, s1_t, s2_t)

        lam_init = 0.8 - 0.6 * math.exp(-0.3 * l)
        lam = (jnp.exp(jnp.sum(lambda_q1[l].astype(F32) * lambda_k1[l].astype(F32)))
               - jnp.exp(jnp.sum(lambda_q2[l].astype(F32) * lambda_k2[l].astype(F32))) + lam_init).reshape(1)
        o_a = _attention(qk, v, lam, attn_subln_g[l].reshape(DV_A, 1).astype(F32), batch, seq, lam_init)

        o_bc = _hgrn_conv(hb, f, cv, lower_bounds[l].reshape(1, W_B), jnp.tile(hgrn_norm_g[l], H_B).reshape(1, W_B),
                          conv_w[l], gavg_b, batch, seq)

        wr = jnp.concatenate([router_group_w[l], router_expert_w[l],
                              jnp.zeros((d, ROUTE_LANES - N_LOGITS), F32)], axis=1).astype(BF16)
        br = jnp.concatenate([router_group_b[l], router_expert_b[l],
                              jnp.zeros((ROUTE_LANES - N_LOGITS,), F32)]).reshape(1, ROUTE_LANES)
        x2, route, cnt = _outproj_router(o_a, o_bc, x2, w_out[l].astype(BF16), norm_ffn_g[l].reshape(1, d), wr, br)

        counts = cnt[0, N_GROUPS:N_GROUPS + N_EXPERTS].astype(jnp.int32)
        padded = (counts + MOE_BLOCK - 1) // MOE_BLOCK * MOE_BLOCK
        pad_end = jnp.cumsum(padded)
        pad_start = pad_end - padded
        e_idx = route[:, 0:2].astype(jnp.int32)
        rank = route[:, 4:6].astype(jnp.int32)
        start_of = jnp.sum(jnp.where(e_idx[..., None] == jnp.arange(N_EXPERTS), pad_start, 0), axis=-1)
        dest = (start_of + rank).reshape(-1)
        blk_pos = jnp.arange(n_blocks, dtype=jnp.int32) * MOE_BLOCK
        blk_e = jnp.minimum(jnp.sum((pad_end[None, :] <= blk_pos[:, None]).astype(jnp.int32), axis=1),
                            N_EXPERTS - 1)
        n_used = (pad_end[-1] // MOE_BLOCK).astype(jnp.int32).reshape(1)

        buf = _dispatch(dest, x2, jnp.zeros((cap, d), F32))
        y_buf = _experts(blk_e, n_used, buf, norm_ffn_g[l].reshape(1, d), expert_w_gate[l].astype(BF16),
                         expert_w_up[l].astype(BF16), expert_w_down[l].astype(BF16))
        x2 = _combine(dest, x2, route, y_buf)
    return x2.reshape(batch, seq, d)
```

```python
import functools
import math

import jax
import jax.numpy as jnp
from jax import lax
from jax.experimental import pallas as pl
from jax.experimental.pallas import tpu as pltpu

F32 = jnp.float32
BF16 = jnp.bfloat16

LANES = 128
SUBLANES = 8

D_MODEL = 1024
CHUNK = 64
EPS = 1e-6
MASK_VALUE = -1e30
TINY = 1e-30
H_A = 4
DK_A = D_MODEL // 16
DV_A = 2 * DK_A
ROT_DIM = DK_A // 4
ROPE_THETA = 500000.0
H_B = 4
DK_B = D_MODEL // 16
DV_B = D_MODEL // 16
C_WIDTH = D_MODEL // 4
CONV_WIDTH = 3
N_GROUPS = 4
EXPERTS_PER_GROUP = 8
N_EXPERTS = N_GROUPS * EXPERTS_PER_GROUP
D_EXPERT = D_MODEL // 2

W_A = H_A * DV_A
W_B = H_B * DK_B
ROUTE_LANES = LANES
N_LOGITS = N_GROUPS + N_EXPERTS

TM_PROJ = 512
TQ = 256
TK = 256
TM_ROUTE = 512
MOE_BLOCK = 512
MOE_WINDOW = 16
MAX_WINDOWS = 2 * TM_ROUTE // MOE_WINDOW + N_EXPERTS
VMEM_LIMIT = 56 * 1024 * 1024


def _staging_rows(tm):
    return 2 * tm + N_EXPERTS * MOE_WINDOW


def _cparams(sem, vmem=VMEM_LIMIT):
    return pltpu.CompilerParams(dimension_semantics=sem, vmem_limit_bytes=vmem)


def _inproj_kernel(x_ref, g_ref, wqk_ref, wv_ref, whb_ref, wf_ref, wc_ref, gsum_ref, gqk_ref,
                   cos_ref, s1_ref, s2_ref, qk_out, v_out, hb_out, f_out, c_out):
    x = x_ref[...]
    ms = jnp.mean(x * x, axis=-1, keepdims=True)
    xn = (x * lax.rsqrt(ms + EPS) * g_ref[...]).astype(BF16)

    v_out[...] = jnp.dot(xn, wv_ref[...], preferred_element_type=F32).astype(BF16)
    hb_out[...] = jnp.dot(xn, whb_ref[...], preferred_element_type=F32).astype(BF16)
    f_out[...] = jnp.dot(xn, wf_ref[...], preferred_element_type=F32)
    c_out[...] = jnp.dot(xn, wc_ref[...], preferred_element_type=F32).astype(BF16)

    cos = cos_ref[...]
    s1 = s1_ref[...]
    s2 = s2_ref[...]
    gsum = gsum_ref[...]
    for part in range(2):
        lo = part * W_A
        t = jnp.dot(xn, wqk_ref[:, lo:lo + W_A], preferred_element_type=F32)
        msq = jnp.dot((t * t).astype(BF16), gsum, preferred_element_type=F32)
        t = t * lax.rsqrt(msq + EPS) * gqk_ref[:, lo:lo + W_A]
        for h in range(H_A):
            slab = t[:, h * LANES:(h + 1) * LANES]
            up = pltpu.roll(slab, LANES - ROT_DIM // 2, axis=1)
            dn = pltpu.roll(slab, ROT_DIM // 2, axis=1)
            qk_out[:, lo + h * LANES:lo + (h + 1) * LANES] = (slab * cos + up * s1 + dn * s2).astype(BF16)


def _inproj(x2, g, wqk, wv, whb, wf, wc, gsum, gqk, cos_t, s1_t, s2_t):
    t_tok = x2.shape[0]
    tm = min(TM_PROJ, t_tok)
    grid = (t_tok // tm,)
    row = lambda i: (i, 0)
    fixed = lambda i: (0, 0)
    return pl.pallas_call(
        _inproj_kernel,
        grid=grid,
        in_specs=[
            pl.BlockSpec((tm, D_MODEL), row),
            pl.BlockSpec((1, D_MODEL), fixed),
            pl.BlockSpec(wqk.shape, fixed),
            pl.BlockSpec(wv.shape, fixed),
            pl.BlockSpec(whb.shape, fixed),
            pl.BlockSpec(wf.shape, fixed),
            pl.BlockSpec(wc.shape, fixed),
            pl.BlockSpec(gsum.shape, fixed),
            pl.BlockSpec(gqk.shape, fixed),
            pl.BlockSpec((tm, LANES), row),
            pl.BlockSpec((tm, LANES), row),
            pl.BlockSpec((tm, LANES), row),
        ],
        out_specs=[
            pl.BlockSpec((tm, 2 * W_A), row),
            pl.BlockSpec((tm, W_A), row),
            pl.BlockSpec((tm, 3 * W_B), row),
            pl.BlockSpec((tm, W_B), row),
            pl.BlockSpec((tm, 3 * C_WIDTH), row),
        ],
        out_shape=[
            jax.ShapeDtypeStruct((t_tok, 2 * W_A), BF16),
            jax.ShapeDtypeStruct((t_tok, W_A), BF16),
            jax.ShapeDtypeStruct((t_tok, 3 * W_B), BF16),
            jax.ShapeDtypeStruct((t_tok, W_B), F32),
            jax.ShapeDtypeStruct((t_tok, 3 * C_WIDTH), BF16),
        ],
        compiler_params=_cparams(("parallel",)),
        name="inproj",
    )(x2, g, wqk, wv, whb, wf, wc, gsum, gqk, cos_t, s1_t, s2_t)


def _attn_kernel(lam_ref, q_ref, k_ref, v_ref, g_ref, o_ref, qc_ref, sa_ref, sb_ref, acc_ref, m_ref, l_ref, *,
                 lam_init):
    qi = pl.program_id(1)
    q = q_ref[...]
    lane = lax.broadcasted_iota(jnp.int32, (TQ, DV_A), 1)
    zero = jnp.zeros((TQ, DV_A), BF16)
    for h in range(H_A):
        qh = q[:, h * DV_A:(h + 1) * DV_A]
        qc_ref[h, 0] = jnp.where(lane < DK_A, qh, zero)
        qc_ref[h, 1] = jnp.where(lane >= DK_A, qh, zero)

    acc_ref[...] = jnp.zeros_like(acc_ref)
    m_ref[...] = jnp.full_like(m_ref, -jnp.inf)
    l_ref[...] = jnp.zeros_like(l_ref)

    kpos = lax.broadcasted_iota(jnp.int32, (TK, TQ), 0) // CHUNK
    qpos = lax.broadcasted_iota(jnp.int32, (TK, TQ), 1) // CHUNK
    diag_mask = kpos <= qpos

    def scores(j, s_ref):
        start = pl.multiple_of(j * TK, TK)
        kb = k_ref[pl.ds(start, TK), :]
        for h in range(H_A):
            kh = kb[:, h * DV_A:(h + 1) * DV_A]
            for c in range(2):
                s_ref[h, c] = lax.dot_general(kh, qc_ref[h, c], (((1,), (1,)), ((), ())),
                                              preferred_element_type=F32)

    def consume(j, s_ref, masked):
        start = pl.multiple_of(j * TK, TK)
        vb = v_ref[pl.ds(start, TK), :]
        for h in range(H_A):
            vh = vb[:, h * DV_A:(h + 1) * DV_A]
            for c in range(2):
                s = s_ref[h, c]
                if masked:
                    s = jnp.where(diag_mask, s, MASK_VALUE)
                m_old = m_ref[h, c]
                m_new = jnp.maximum(m_old, jnp.max(s, axis=0, keepdims=True))
                alpha = jnp.exp(m_old - m_new)
                p = jnp.exp(s - m_new)
                l_ref[h, c] = alpha * l_ref[h, c] + jnp.sum(p, axis=0, keepdims=True)
                pv = lax.dot_general(vh, p.astype(BF16), (((0,), (0,)), ((), ())),
                                     preferred_element_type=F32)
                acc_ref[h, c] = alpha * acc_ref[h, c] + pv
                m_ref[h, c] = m_new

    scores(0, sa_ref)

    def body(j, carry):
        @pl.when(j % 2 == 0)
        def _():
            scores(j + 1, sb_ref)
            consume(j, sa_ref, False)

        @pl.when(j % 2 == 1)
        def _():
            scores(j + 1, sa_ref)
            consume(j, sb_ref, False)

        return carry

    lax.fori_loop(0, qi, body, 0)

    @pl.when(qi % 2 == 0)
    def _():
        consume(qi, sa_ref, True)

    @pl.when(qi % 2 == 1)
    def _():
        consume(qi, sb_ref, True)

    lam = lam_ref[0]
    for h in range(H_A):
        o = acc_ref[h, 0] / l_ref[h, 0] - lam * (acc_ref[h, 1] / l_ref[h, 1])
        ms = jnp.mean(o * o, axis=0, keepdims=True)
        y = o * lax.rsqrt(ms + EPS) * g_ref[...] * (1.0 - lam_init)
        o_ref[:, h * DV_A:(h + 1) * DV_A] = y.T.astype(BF16)


def _attention(qk, v, lam, subln_col, batch, seq, lam_init):
    assert TQ == TK and TQ % CHUNK == 0 and seq % TQ == 0
    nq = seq // TQ
    t_tok = batch * seq
    kernel = functools.partial(_attn_kernel, lam_init=lam_init)
    return pl.pallas_call(
        kernel,
        grid=(batch, nq),
        in_specs=[
            pl.BlockSpec(memory_space=pltpu.SMEM),
            pl.BlockSpec((TQ, W_A), lambda b, i: (b * nq + i, 0)),
            pl.BlockSpec((seq, W_A), lambda b, i: (b, 1)),
            pl.BlockSpec((seq, W_A), lambda b, i: (b, 0)),
            pl.BlockSpec((DV_A, 1), lambda b, i: (0, 0)),
        ],
        out_specs=pl.BlockSpec((TQ, W_A), lambda b, i: (b * nq + i, 0)),
        out_shape=jax.ShapeDtypeStruct((t_tok, W_A), BF16),
        scratch_shapes=[
            pltpu.VMEM((H_A, 2, TQ, DV_A), BF16),
            pltpu.VMEM((H_A, 2, TK, TQ), F32),
            pltpu.VMEM((H_A, 2, TK, TQ), F32),
            pltpu.VMEM((H_A, 2, DV_A, TQ), F32),
            pltpu.VMEM((H_A, 2, 1, TQ), F32),
            pltpu.VMEM((H_A, 2, 1, TQ), F32),
        ],
        compiler_params=_cparams(("parallel", "arbitrary")),
        name="diff_attn",
    )(lam, qk, qk, v, subln_col)


_LEVELS = (32, 16, 8, 4, 2, 1)


def _ref_rows(g, h):
    n = g.shape[1]
    if 2 * h >= SUBLANES:
        pieces = []
        for m in range(CHUNK // (2 * h)):
            r = 2 * h * m + h - 1
            pieces.append(jnp.broadcast_to(g[r:r + 1, :], (2 * h, n)))
        return pieces[0] if len(pieces) == 1 else jnp.concatenate(pieces, axis=0)
    g3 = g.reshape(CHUNK // SUBLANES, SUBLANES, n)
    sub = lax.broadcasted_iota(jnp.int32, g3.shape, 1)
    out = None
    for m in reversed(range(SUBLANES // (2 * h))):
        r = 2 * h * m + h - 1
        piece = jnp.broadcast_to(g3[:, r:r + 1, :], g3.shape)
        out = piece if out is None else jnp.where(sub < 2 * h * (m + 1), piece, out)
    return out.reshape(CHUNK, n)


def _stack_heads(a, lane_head):
    zero = jnp.zeros_like(a)
    return jnp.concatenate([jnp.where(lane_head == hh, a, zero) for hh in range(H_B)], axis=0)


def _hgrn_kernel(hb_ref, f_ref, c_ref, lb_ref, ng_ref, cw_ref, gavg_ref, o_ref, state_ref, hbuf_ref):
    seq = hb_ref.shape[0]
    n_chunks = seq // CHUNK
    lb = lb_ref[...]
    ng = ng_ref[...]
    gavg = gavg_ref[...]

    row = lax.broadcasted_iota(jnp.int32, (CHUNK, W_B), 0)
    lane = lax.broadcasted_iota(jnp.int32, (CHUNK, W_B), 1)
    lane_head = lane // DK_B
    lane_s = lane % CHUNK
    tri = (lax.broadcasted_iota(jnp.int32, (CHUNK, CHUNK), 1)
           <= lax.broadcasted_iota(jnp.int32, (CHUNK, CHUNK), 0)).astype(BF16)
    lvl_mask = [(row // (2 * h)) == (lane_s // (2 * h)) for h in _LEVELS]
    eye_mask = row == lane_s
    r4 = lax.broadcasted_iota(jnp.int32, (H_B * DK_B, W_B), 0) // DK_B
    c4 = lax.broadcasted_iota(jnp.int32, (H_B * DK_B, W_B), 1) // DK_B
    bd_mask = r4 == c4

    state_ref[...] = jnp.zeros_like(state_ref)

    def chunk(i, carry):
        r0 = pl.multiple_of(i * CHUNK, CHUNK)
        hb = hb_ref[pl.ds(r0, CHUNK), :].astype(F32)
        zq = hb[:, 0:W_B]
        v = hb[:, W_B:2 * W_B]
        zg = hb[:, 2 * W_B:3 * W_B]
        zf = f_ref[pl.ds(r0, CHUNK), :]

        f = lb + (1.0 - lb) * jax.nn.sigmoid(zf)
        logf = jnp.log(jnp.maximum(f, TINY))
        key = (1.0 - lb) * jax.nn.sigmoid(-zf)
        q = zq * jax.nn.sigmoid(zq)
        vb16 = v.astype(BF16)

        hi = logf.astype(BF16)
        r1 = logf - hi.astype(F32)
        mid = r1.astype(BF16)
        lo = (r1 - mid.astype(F32)).astype(BF16)
        g = (jnp.dot(tri, hi, preferred_element_type=F32)
             + jnp.dot(tri, mid, preferred_element_type=F32)
             + jnp.dot(tri, lo, preferred_element_type=F32))

        a_all = jnp.where(eye_mask,
                          lax.dot_general(q.astype(BF16), _stack_heads(key.astype(BF16), lane_head[0:CHUNK]),
                                          (((1,), (1,)), ((), ())), preferred_element_type=F32), 0.0)
        for h, mask in zip(_LEVELS, lvl_mask):
            w = jnp.exp(-jnp.abs(g - _ref_rows(g, h)))
            second = (row % (2 * h)) >= h
            qh = jnp.where(second, q * w, 0.0).astype(BF16)
            kh = jnp.where(second, 0.0, key * w).astype(BF16)
            ah = lax.dot_general(qh, _stack_heads(kh, lane_head), (((1,), (1,)), ((), ())),
                                 preferred_element_type=F32)
            a_all = a_all + jnp.where(mask, ah, 0.0)
        o = jnp.dot(a_all.astype(BF16), _stack_heads(vb16, lane_head), preferred_element_type=F32)

        st = state_ref[...]
        qg = (q * jnp.exp(g)).astype(BF16)
        o = o + lax.dot_general(qg, st.astype(BF16), (((1,), (1,)), ((), ())), preferred_element_type=F32)
        g_last = g[CHUNK - 1:CHUNK, :]
        kd = (key * jnp.exp(g_last - g)).astype(BF16)
        upd = lax.dot_general(vb16, kd, (((0,), (0,)), ((), ())), preferred_element_type=F32)
        state_ref[...] = jnp.exp(g_last) * st + jnp.where(bd_mask, upd, 0.0)

        ms = jnp.dot((o * o).astype(BF16), gavg, preferred_element_type=F32)
        y = o * lax.rsqrt(ms + EPS) * ng * (zg * jax.nn.sigmoid(zg))
        o_ref[pl.ds(r0, CHUNK), 0:W_B] = y.astype(BF16)
        return carry

    lax.fori_loop(0, n_chunks, chunk, 0)

    piece = min(512, seq)
    hbuf_ref[0:SUBLANES, :] = jnp.zeros((SUBLANES, C_WIDTH), F32)
    for p in range(seq // piece):
        cblk = c_ref[p * piece:(p + 1) * piece, :].astype(F32)
        hbuf_ref[SUBLANES + p * piece:SUBLANES + (p + 1) * piece, :] = cblk[:, C_WIDTH:2 * C_WIDTH] * cblk[:, 2 * C_WIDTH:]
    for p in range(seq // piece):
        base = SUBLANES + p * piece
        y = (cw_ref[0:1, :] * hbuf_ref[base - 2:base - 2 + piece, :]
             + cw_ref[1:2, :] * hbuf_ref[base - 1:base - 1 + piece, :]
             + cw_ref[2:3, :] * hbuf_ref[base:base + piece, :])
        bgate = c_ref[p * piece:(p + 1) * piece, 0:C_WIDTH].astype(F32)
        o_ref[p * piece:(p + 1) * piece, W_B:W_B + C_WIDTH] = (bgate * y).astype(BF16)


def _hgrn_conv(hb, f, cv, lb, ng, cw, gavg, batch, seq):
    t_tok = batch * seq
    fixed = lambda b: (0, 0)
    return pl.pallas_call(
        _hgrn_kernel,
        grid=(batch,),
        in_specs=[
            pl.BlockSpec((seq, 3 * W_B), lambda b: (b, 0)),
            pl.BlockSpec((seq, W_B), lambda b: (b, 0)),
            pl.BlockSpec((seq, 3 * C_WIDTH), lambda b: (b, 0)),
            pl.BlockSpec((1, W_B), fixed),
            pl.BlockSpec((1, W_B), fixed),
            pl.BlockSpec((CONV_WIDTH, C_WIDTH), fixed),
            pl.BlockSpec((W_B, W_B), fixed),
        ],
        out_specs=pl.BlockSpec((seq, W_B + C_WIDTH), lambda b: (b, 0)),
        out_shape=jax.ShapeDtypeStruct((t_tok, W_B + C_WIDTH), BF16),
        scratch_shapes=[
            pltpu.VMEM((H_B * DV_B, W_B), F32),
            pltpu.VMEM((seq + SUBLANES, C_WIDTH), F32),
        ],
        compiler_params=_cparams(("parallel",)),
        name="hgrn_conv",
    )(hb, f, cv, lb, ng, cw, gavg)


def _outproj_router_kernel(oa_ref, obc_ref, x_ref, wo_ref, g_ref, wr_ref, br_ref,
                           xo_ref, route_ref, tab_ref, cnt_ref, carry_ref):
    i = pl.program_id(0)

    @pl.when(i == 0)
    def _():
        carry_ref[...] = jnp.zeros_like(carry_ref)

    mixed = (jnp.dot(oa_ref[...], wo_ref[0:W_A, :], preferred_element_type=F32)
             + jnp.dot(obc_ref[...], wo_ref[W_A:, :], preferred_element_type=F32))
    xn = x_ref[...] + mixed
    xo_ref[...] = xn

    ms = jnp.mean(xn * xn, axis=-1, keepdims=True)
    hn = (xn * lax.rsqrt(ms + EPS) * g_ref[...]).astype(BF16)
    logits = jnp.dot(hn, wr_ref[...], preferred_element_type=F32) + br_ref[...]

    tm = logits.shape[0]
    lane = lax.broadcasted_iota(jnp.int32, (tm, ROUTE_LANES), 1)
    neg = -jnp.inf
    big = ROUTE_LANES

    def top(mask):
        val = jnp.max(jnp.where(mask, logits, neg), axis=-1, keepdims=True)
        idx = jnp.min(jnp.where(mask & (logits == val), lane, big), axis=-1, keepdims=True)
        return val, idx

    gmask = lane < N_GROUPS
    gmax, gidx = top(gmask)
    p_g = 1.0 / jnp.sum(jnp.where(gmask, jnp.exp(logits - gmax), 0.0), axis=-1, keepdims=True)
    e_lo = N_GROUPS + EXPERTS_PER_GROUP * gidx
    emask = (lane >= e_lo) & (lane < e_lo + EXPERTS_PER_GROUP)
    v1, i1 = top(emask)
    v2, i2 = top(emask & (lane != i1))
    t2 = jnp.exp(v2 - v1)
    w1 = p_g / (1.0 + t2)
    w2 = p_g * t2 / (1.0 + t2)

    sel1 = lane == i1
    sel2 = lane == i2
    onehot = jnp.where(sel1, 1.0, jnp.where(sel2, 1.0, 0.0))
    strict = (lax.broadcasted_iota(jnp.int32, (tm, tm), 1)
              < lax.broadcasted_iota(jnp.int32, (tm, tm), 0)).astype(BF16)
    local = jnp.dot(strict, onehot.astype(BF16), preferred_element_type=F32)
    cnt = jnp.sum(onehot, axis=0, keepdims=True)
    wrows = jnp.floor((cnt + (MOE_WINDOW - 1)) * (1.0 / MOE_WINDOW)) * MOE_WINDOW
    upper = (lax.broadcasted_iota(jnp.int32, (ROUTE_LANES, ROUTE_LANES), 0)
             < lax.broadcasted_iota(jnp.int32, (ROUTE_LANES, ROUTE_LANES), 1)).astype(BF16)
    soff = jnp.dot(jnp.broadcast_to(wrows, (SUBLANES, ROUTE_LANES)).astype(BF16), upper,
                   preferred_element_type=F32)[0:1, :]
    slot = local + soff
    row1 = jnp.sum(jnp.where(sel1, slot, 0.0), axis=-1, keepdims=True)
    row2 = jnp.sum(jnp.where(sel2, slot, 0.0), axis=-1, keepdims=True)

    rec = jnp.where(lane == 0, (i1 - N_GROUPS).astype(F32),
          jnp.where(lane == 1, (i2 - N_GROUPS).astype(F32),
          jnp.where(lane == 2, w1,
          jnp.where(lane == 3, w2,
          jnp.where(lane == 4, row1,
          jnp.where(lane == 5, row2, 0.0))))))
    route_ref[...] = rec
    tab_ref[0] = jnp.zeros((SUBLANES, ROUTE_LANES), F32)
    tab_ref[0, 0:1, :] = cnt
    tab_ref[0, 1:2, :] = carry_ref[0:1, :]
    carry_ref[0:1, :] = carry_ref[0:1, :] + jnp.floor((cnt + (SUBLANES - 1)) * (1.0 / SUBLANES)) * SUBLANES
    cnt_ref[...] = carry_ref[...]


def _outproj_router(o_a, o_bc, x2, wo, g, wr, br):
    t_tok = x2.shape[0]
    tm = min(TM_ROUTE, t_tok)
    row = lambda i: (i, 0)
    fixed = lambda i: (0, 0)
    return pl.pallas_call(
        _outproj_router_kernel,
        grid=(t_tok // tm,),
        in_specs=[
            pl.BlockSpec((tm, W_A), row),
            pl.BlockSpec((tm, W_B + C_WIDTH), row),
            pl.BlockSpec((tm, D_MODEL), row),
            pl.BlockSpec(wo.shape, fixed),
            pl.BlockSpec((1, D_MODEL), fixed),
            pl.BlockSpec(wr.shape, fixed),
            pl.BlockSpec((1, ROUTE_LANES), fixed),
        ],
        out_specs=[
            pl.BlockSpec((tm, D_MODEL), row),
            pl.BlockSpec((tm, ROUTE_LANES), row),
            pl.BlockSpec((1, SUBLANES, ROUTE_LANES), lambda i: (i, 0, 0)),
            pl.BlockSpec((SUBLANES, ROUTE_LANES), fixed),
        ],
        out_shape=[
            jax.ShapeDtypeStruct((t_tok, D_MODEL), F32),
            jax.ShapeDtypeStruct((t_tok, ROUTE_LANES), F32),
            jax.ShapeDtypeStruct((t_tok // tm, SUBLANES, ROUTE_LANES), F32),
            jax.ShapeDtypeStruct((SUBLANES, ROUTE_LANES), F32),
        ],
        scratch_shapes=[pltpu.VMEM((SUBLANES, ROUTE_LANES), F32)],
        compiler_params=_cparams(("arbitrary",)),
        name="outproj_router",
    )(o_a, o_bc, x2, wo, g, wr, br)


def _window_copy_loops(n, make_copy):
    def start(k, carry):
        make_copy(k).start()
        return carry

    def wait(k, carry):
        make_copy(k).wait()
        return carry

    lax.fori_loop(0, n, start, 0)
    lax.fori_loop(0, n, wait, 0)


def _dispatch_kernel(tab_ref, route_ref, x_ref, buf_in_ref, buf_ref, xp_ref, sem):
    del buf_in_ref
    tm = x_ref.shape[0]
    r_st = xp_ref.shape[0]
    route = route_ref[...]
    r0 = route[:, 4:5].astype(jnp.int32)
    r1 = route[:, 5:6].astype(jnp.int32)
    col = lax.broadcasted_iota(jnp.int32, (tm, r_st), 1)
    sel = jnp.where(col == r0, 1.0, jnp.where(col == r1, 1.0, 0.0)).astype(BF16)
    xp_ref[...] = lax.dot_general(sel, x_ref[...].astype(BF16), (((0,), (0,)), ((), ())),
                                  preferred_element_type=F32)

    def make_copy(k):
        src = pl.multiple_of(k * MOE_WINDOW, MOE_WINDOW)
        return pltpu.make_async_copy(xp_ref.at[pl.ds(src, MOE_WINDOW), :],
                                     buf_ref.at[pl.ds(pl.multiple_of(tab_ref[0, 0, k], SUBLANES), MOE_WINDOW), :],
                                     sem)

    _window_copy_loops(tab_ref[0, 0, MAX_WINDOWS], make_copy)


def _dispatch(tabs, route, x2, buf_init):
    t_tok = x2.shape[0]
    tm = min(TM_ROUTE, t_tok)
    n_tiles = t_tok // tm
    return pl.pallas_call(
        _dispatch_kernel,
        grid=(n_tiles,),
        in_specs=[
            pl.BlockSpec((1, 1, ROUTE_LANES), lambda i: (i, 0, 0), memory_space=pltpu.SMEM),
            pl.BlockSpec((tm, ROUTE_LANES), lambda i: (i, 0)),
            pl.BlockSpec((tm, D_MODEL), lambda i: (i, 0)),
            pl.BlockSpec(memory_space=pl.ANY),
        ],
        out_specs=pl.BlockSpec(memory_space=pl.ANY),
        out_shape=jax.ShapeDtypeStruct(buf_init.shape, F32),
        scratch_shapes=[pltpu.VMEM((_staging_rows(tm), D_MODEL), F32), pltpu.SemaphoreType.DMA(())],
        input_output_aliases={3: 0},
        compiler_params=_cparams(("arbitrary",)),
        name="moe_dispatch",
    )(tabs, route, x2, buf_init)


def _experts_kernel(blk_e_ref, n_used_ref, x_ref, g_ref, wg_ref, wu_ref, wd_ref, y_ref):
    del blk_e_ref
    i = pl.program_id(0)

    @pl.when(i < n_used_ref[0])
    def _():
        x = x_ref[...]
        ms = jnp.mean(x * x, axis=-1, keepdims=True)
        hn = (x * lax.rsqrt(ms + EPS) * g_ref[...]).astype(BF16)
        a = jnp.dot(hn, wg_ref[0], preferred_element_type=F32)
        u = jnp.dot(hn, wu_ref[0], preferred_element_type=F32)
        act = (a * jax.nn.sigmoid(a) * u).astype(BF16)
        y_ref[...] = jnp.dot(act, wd_ref[0], preferred_element_type=F32)

    @pl.when(i >= n_used_ref[0])
    def _():
        y_ref[...] = jnp.zeros_like(y_ref)


def _experts(blk_e, n_used, buf, g, wg, wu, wd):
    n_blocks = buf.shape[0] // MOE_BLOCK
    grid_spec = pltpu.PrefetchScalarGridSpec(
        num_scalar_prefetch=2,
        grid=(n_blocks,),
        in_specs=[
            pl.BlockSpec((MOE_BLOCK, D_MODEL), lambda i, be, nu: (i, 0)),
            pl.BlockSpec((1, D_MODEL), lambda i, be, nu: (0, 0)),
            pl.BlockSpec((1, D_MODEL, D_EXPERT), lambda i, be, nu: (be[i], 0, 0)),
            pl.BlockSpec((1, D_MODEL, D_EXPERT), lambda i, be, nu: (be[i], 0, 0)),
            pl.BlockSpec((1, D_EXPERT, D_MODEL), lambda i, be, nu: (be[i], 0, 0)),
        ],
        out_specs=pl.BlockSpec((MOE_BLOCK, D_MODEL), lambda i, be, nu: (i, 0)),
    )
    return pl.pallas_call(
        _experts_kernel,
        grid_spec=grid_spec,
        out_shape=jax.ShapeDtypeStruct(buf.shape, F32),
        compiler_params=_cparams(("arbitrary",)),
        name="moe_experts",
    )(blk_e, n_used, buf, g, wg, wu, wd)


def _combine_kernel(tab_ref, route_ref, x_ref, y_ref, o_ref, yst_ref, sem):
    tm = x_ref.shape[0]
    r_st = yst_ref.shape[0]

    @pl.when(pl.program_id(0) == 0)
    def _():
        yst_ref[...] = jnp.zeros_like(yst_ref)

    def make_copy(k):
        dst = pl.multiple_of(k * MOE_WINDOW, MOE_WINDOW)
        return pltpu.make_async_copy(y_ref.at[pl.ds(pl.multiple_of(tab_ref[0, 0, k], SUBLANES), MOE_WINDOW), :],
                                     yst_ref.at[pl.ds(dst, MOE_WINDOW), :], sem)

    _window_copy_loops(tab_ref[0, 0, MAX_WINDOWS], make_copy)

    route = route_ref[...]
    r0 = route[:, 4:5].astype(jnp.int32)
    r1 = route[:, 5:6].astype(jnp.int32)
    col = lax.broadcasted_iota(jnp.int32, (tm, r_st), 1)
    gate = jnp.where(col == r0, route[:, 2:3], jnp.where(col == r1, route[:, 3:4], 0.0)).astype(BF16)
    o_ref[...] = x_ref[...] + jnp.dot(gate, yst_ref[...].astype(BF16), preferred_element_type=F32)


def _combine(tabs, route, x2, y_buf):
    t_tok = x2.shape[0]
    tm = min(TM_ROUTE, t_tok)
    n_tiles = t_tok // tm
    return pl.pallas_call(
        _combine_kernel,
        grid=(n_tiles,),
        in_specs=[
            pl.BlockSpec((1, 1, ROUTE_LANES), lambda i: (i, 0, 0), memory_space=pltpu.SMEM),
            pl.BlockSpec((tm, ROUTE_LANES), lambda i: (i, 0)),
            pl.BlockSpec((tm, D_MODEL), lambda i: (i, 0)),
            pl.BlockSpec(memory_space=pl.ANY),
        ],
        out_specs=pl.BlockSpec((tm, D_MODEL), lambda i: (i, 0)),
        out_shape=jax.ShapeDtypeStruct((t_tok, D_MODEL), F32),
        scratch_shapes=[pltpu.VMEM((_staging_rows(tm), D_MODEL), F32), pltpu.SemaphoreType.DMA(())],
        compiler_params=_cparams(("arbitrary",)),
        name="moe_combine",
    )(tabs, route, x2, y_buf)


def _rope_tables(positions):
    half = ROT_DIM // 2
    inv_freq = ROPE_THETA ** (-jnp.arange(0, ROT_DIM, 2, dtype=F32) / ROT_DIM)
    ang = positions.reshape(-1).astype(F32)[:, None] * inv_freq
    cos, sin = jnp.cos(ang), jnp.sin(ang)
    t_tok = ang.shape[0]
    rest = DK_A - ROT_DIM
    c64 = jnp.concatenate([cos, cos, jnp.ones((t_tok, rest), F32)], axis=1)
    s1 = jnp.concatenate([-sin, jnp.zeros((t_tok, DK_A - half), F32)], axis=1)
    s2 = jnp.concatenate([jnp.zeros((t_tok, half), F32), sin, jnp.zeros((t_tok, rest), F32)], axis=1)
    rep = lambda a: jnp.concatenate([a, a], axis=1)
    return rep(c64), rep(s1), rep(s2)


def _block_avg(width, group):
    idx = jnp.arange(width) // group
    return jnp.where(idx[:, None] == idx[None, :], 1.0 / group, 0.0).astype(BF16)


def kernel(x, positions, norm_mix_g, w_in, attn_q_norm_g, attn_k_norm_g, lambda_q1, lambda_k1, lambda_q2,
           lambda_k2, attn_subln_g, hgrn_lower_bounds, hgrn_norm_g, conv_w, w_out, norm_ffn_g, router_group_w,
           router_group_b, router_expert_w, router_expert_b, expert_w_gate, expert_w_up, expert_w_down):
    batch, seq, d = x.shape
    assert d == D_MODEL
    depth = w_in.shape[0]
    t_tok = batch * seq
    n_assign = 2 * t_tok
    n_route_tiles = t_tok // min(TM_ROUTE, t_tok)
    n_blocks = (n_assign + (SUBLANES - 1) * n_route_tiles * N_EXPERTS) // MOE_BLOCK + 2 * N_EXPERTS + 1
    cap = n_blocks * MOE_BLOCK

    lb_probs = jax.nn.softmax(hgrn_lower_bounds.astype(F32), axis=0)
    lower_bounds = jnp.cumsum(lb_probs, axis=0) - lb_probs[0:1]
    cos_t, s1_t, s2_t = _rope_tables(positions)
    gsum_a = _block_avg(W_A, DK_A)
    gavg_b = _block_avg(W_B, DV_B)

    o_q, o_k, o_v = 0, W_A, 2 * W_A
    o_bq = 3 * W_A
    o_bf, o_bi, o_bg = o_bq + W_B, o_bq + 2 * W_B, o_bq + 3 * W_B
    o_c = o_bq + 4 * W_B

    x2 = x.reshape(t_tok, d)
    for l in range(depth):
        w = w_in[l]
        wqk = w[:, o_q:o_v].astype(BF16)
        wv = w[:, o_v:o_bq].astype(BF16)
        whb = jnp.concatenate([w[:, o_bq:o_bf], w[:, o_bi:o_c]], axis=1).astype(BF16)
        wf = w[:, o_bf:o_bi].astype(BF16)
        wc = w[:, o_c:].astype(BF16)
        gqk = jnp.concatenate([jnp.tile(attn_q_norm_g[l], 2 * H_A) * (DK_A ** -0.5),
                               jnp.tile(attn_k_norm_g[l], 2 * H_A)]).reshape(1, 2 * W_A).astype(F32)
        qk, v, hb, f, cv = _inproj(x2, norm_mix_g[l].reshape(1, d), wqk, wv, whb, wf, wc, gsum_a, gqk,
                                   cos_t, s1_t, s2_t)

        lam_init = 0.8 - 0.6 * math.exp(-0.3 * l)
        lam = (jnp.exp(jnp.sum(lambda_q1[l].astype(F32) * lambda_k1[l].astype(F32)))
               - jnp.exp(jnp.sum(lambda_q2[l].astype(F32) * lambda_k2[l].astype(F32))) + lam_init).reshape(1)
        o_a = _attention(qk, v, lam, attn_subln_g[l].reshape(DV_A, 1).astype(F32), batch, seq, lam_init)

        o_bc = _hgrn_conv(hb, f, cv, lower_bounds[l].reshape(1, W_B), jnp.tile(hgrn_norm_g[l], H_B).reshape(1, W_B),
                          conv_w[l], gavg_b, batch, seq)

        wr = jnp.concatenate([router_group_w[l], router_expert_w[l],
                              jnp.zeros((d, ROUTE_LANES - N_LOGITS), F32)], axis=1).astype(BF16)
        br = jnp.concatenate([router_group_b[l], router_expert_b[l],
                              jnp.zeros((ROUTE_LANES - N_LOGITS,), F32)]).reshape(1, ROUTE_LANES)
        x2, route, tab, cnt = _outproj_router(o_a, o_bc, x2, w_out[l].astype(BF16), norm_ffn_g[l].reshape(1, d), wr, br)

        counts = cnt[0, N_GROUPS:N_GROUPS + N_EXPERTS].astype(jnp.int32)
        padded = (counts + MOE_WINDOW + MOE_BLOCK - 1) // MOE_BLOCK * MOE_BLOCK
        pad_end = jnp.cumsum(padded)
        pad_start = pad_end - padded
        blk_pos = jnp.arange(n_blocks, dtype=jnp.int32) * MOE_BLOCK
        blk_e = jnp.minimum(jnp.sum((pad_end[None, :] <= blk_pos[:, None]).astype(jnp.int32), axis=1),
                            N_EXPERTS - 1)
        n_used = (pad_end[-1] // MOE_BLOCK).astype(jnp.int32).reshape(1)
        tile_cnt = tab[:, 0, N_GROUPS:N_GROUPS + N_EXPERTS].astype(jnp.int32)
        tile_base = tab[:, 1, N_GROUPS:N_GROUPS + N_EXPERTS].astype(jnp.int32) + pad_start[None, :]
        n_win = (tile_cnt + MOE_WINDOW - 1) // MOE_WINDOW
        w_end = jnp.cumsum(n_win, axis=1)
        w_start = w_end - n_win
        k_idx = jnp.arange(MAX_WINDOWS, dtype=jnp.int32)[None, :, None]
        owner = (w_start[:, None, :] <= k_idx) & (k_idx < w_end[:, None, :])
        win_dst = jnp.sum(jnp.where(owner, tile_base[:, None, :] + (k_idx - w_start[:, None, :]) * MOE_WINDOW, 0),
                          axis=-1)
        tabs = jnp.concatenate([win_dst, w_end[:, -1:],
                                jnp.zeros((win_dst.shape[0], ROUTE_LANES - MAX_WINDOWS - 1), jnp.int32)],
                               axis=1).reshape(-1, 1, ROUTE_LANES)

        buf = _dispatch(tabs, route, x2, jnp.zeros((cap, d), F32))
        y_buf = _experts(blk_e, n_used, buf, norm_ffn_g[l].reshape(1, d), expert_w_gate[l].astype(BF16),
                         expert_w_up[l].astype(BF16), expert_w_down[l].astype(BF16))
        x2 = _combine(tabs, route, x2, y_buf)
    return x2.reshape(batch, seq, d)
```

```python
import functools
import math

import jax
import jax.numpy as jnp
from jax import lax
from jax.experimental import pallas as pl
from jax.experimental.pallas import tpu as pltpu

F32 = jnp.float32
BF16 = jnp.bfloat16

LANES = 128
SUBLANES = 8

D_MODEL = 1024
CHUNK = 64
EPS = 1e-6
MASK_VALUE = -1e30
TINY = 1e-30
H_A = 4
DK_A = D_MODEL // 16
DV_A = 2 * DK_A
ROT_DIM = DK_A // 4
ROPE_THETA = 500000.0
H_B = 4
DK_B = D_MODEL // 16
DV_B = D_MODEL // 16
C_WIDTH = D_MODEL // 4
CONV_WIDTH = 3
N_GROUPS = 4
EXPERTS_PER_GROUP = 8
N_EXPERTS = N_GROUPS * EXPERTS_PER_GROUP
D_EXPERT = D_MODEL // 2

W_A = H_A * DV_A
W_B = H_B * DK_B
ROUTE_LANES = LANES
N_LOGITS = N_GROUPS + N_EXPERTS

TM_PROJ = 512
TQ = 256
TK = 256
TM_ROUTE = 512
MOE_BLOCK = 512
HGRN_BATCH = 2
MOE_WINDOW = 16
MAX_WINDOWS = 2 * TM_ROUTE // MOE_WINDOW + N_EXPERTS
VMEM_LIMIT = 56 * 1024 * 1024


def _staging_rows(tm):
    return 2 * tm + N_EXPERTS * MOE_WINDOW


def _cparams(sem, vmem=VMEM_LIMIT):
    return pltpu.CompilerParams(dimension_semantics=sem, vmem_limit_bytes=vmem)


def _inproj_kernel(x_ref, g_ref, wqk_ref, wv_ref, whb_ref, wf_ref, wc_ref, gsum_ref, gqk_ref,
                   cos_ref, s1_ref, s2_ref, qk_out, v_out, hb_out, f_out, c_out):
    x = x_ref[...]
    ms = jnp.mean(x * x, axis=-1, keepdims=True)
    xn = (x * lax.rsqrt(ms + EPS) * g_ref[...]).astype(BF16)

    v_out[...] = jnp.dot(xn, wv_ref[...], preferred_element_type=F32).astype(BF16)
    hb_out[...] = jnp.dot(xn, whb_ref[...], preferred_element_type=F32).astype(BF16)
    f_out[...] = jnp.dot(xn, wf_ref[...], preferred_element_type=F32)
    c_out[...] = jnp.dot(xn, wc_ref[...], preferred_element_type=F32).astype(BF16)

    cos = cos_ref[...]
    s1 = s1_ref[...]
    s2 = s2_ref[...]
    gsum = gsum_ref[...]
    for part in range(2):
        lo = part * W_A
        t = jnp.dot(xn, wqk_ref[:, lo:lo + W_A], preferred_element_type=F32)
        msq = jnp.dot((t * t).astype(BF16), gsum, preferred_element_type=F32)
        t = t * lax.rsqrt(msq + EPS) * gqk_ref[:, lo:lo + W_A]
        for h in range(H_A):
            slab = t[:, h * LANES:(h + 1) * LANES]
            up = pltpu.roll(slab, LANES - ROT_DIM // 2, axis=1)
            dn = pltpu.roll(slab, ROT_DIM // 2, axis=1)
            qk_out[:, lo + h * LANES:lo + (h + 1) * LANES] = (slab * cos + up * s1 + dn * s2).astype(BF16)


def _inproj(x2, g, wqk, wv, whb, wf, wc, gsum, gqk, cos_t, s1_t, s2_t):
    t_tok = x2.shape[0]
    tm = min(TM_PROJ, t_tok)
    grid = (t_tok // tm,)
    row = lambda i: (i, 0)
    fixed = lambda i: (0, 0)
    return pl.pallas_call(
        _inproj_kernel,
        grid=grid,
        in_specs=[
            pl.BlockSpec((tm, D_MODEL), row),
            pl.BlockSpec((1, D_MODEL), fixed),
            pl.BlockSpec(wqk.shape, fixed),
            pl.BlockSpec(wv.shape, fixed),
            pl.BlockSpec(whb.shape, fixed),
            pl.BlockSpec(wf.shape, fixed),
            pl.BlockSpec(wc.shape, fixed),
            pl.BlockSpec(gsum.shape, fixed),
            pl.BlockSpec(gqk.shape, fixed),
            pl.BlockSpec((tm, LANES), row),
            pl.BlockSpec((tm, LANES), row),
            pl.BlockSpec((tm, LANES), row),
        ],
        out_specs=[
            pl.BlockSpec((tm, 2 * W_A), row),
            pl.BlockSpec((tm, W_A), row),
            pl.BlockSpec((tm, 3 * W_B), row),
            pl.BlockSpec((tm, W_B), row),
            pl.BlockSpec((tm, 3 * C_WIDTH), row),
        ],
        out_shape=[
            jax.ShapeDtypeStruct((t_tok, 2 * W_A), BF16),
            jax.ShapeDtypeStruct((t_tok, W_A), BF16),
            jax.ShapeDtypeStruct((t_tok, 3 * W_B), BF16),
            jax.ShapeDtypeStruct((t_tok, W_B), F32),
            jax.ShapeDtypeStruct((t_tok, 3 * C_WIDTH), BF16),
        ],
        compiler_params=_cparams(("parallel",)),
        name="inproj",
    )(x2, g, wqk, wv, whb, wf, wc, gsum, gqk, cos_t, s1_t, s2_t)


def _attn_kernel(lam_ref, q_ref, k_ref, v_ref, g_ref, o_ref, qc_ref, sa_ref, sb_ref, acc_ref, m_ref, l_ref, *,
                 lam_init):
    qi = pl.program_id(1)
    q = q_ref[...]
    lane = lax.broadcasted_iota(jnp.int32, (TQ, DV_A), 1)
    zero = jnp.zeros((TQ, DV_A), BF16)
    for h in range(H_A):
        qh = q[:, h * DV_A:(h + 1) * DV_A]
        qc_ref[h, 0] = jnp.where(lane < DK_A, qh, zero)
        qc_ref[h, 1] = jnp.where(lane >= DK_A, qh, zero)

    acc_ref[...] = jnp.zeros_like(acc_ref)
    m_ref[...] = jnp.full_like(m_ref, -jnp.inf)
    l_ref[...] = jnp.zeros_like(l_ref)

    kpos = lax.broadcasted_iota(jnp.int32, (TK, TQ), 0) // CHUNK
    qpos = lax.broadcasted_iota(jnp.int32, (TK, TQ), 1) // CHUNK
    diag_mask = kpos <= qpos

    def scores(j, s_ref):
        start = pl.multiple_of(j * TK, TK)
        kb = k_ref[pl.ds(start, TK), :]
        for h in range(H_A):
            kh = kb[:, h * DV_A:(h + 1) * DV_A]
            for c in range(2):
                s_ref[h, c] = lax.dot_general(kh, qc_ref[h, c], (((1,), (1,)), ((), ())),
                                              preferred_element_type=F32)

    def consume(j, s_ref, masked):
        start = pl.multiple_of(j * TK, TK)
        vb = v_ref[pl.ds(start, TK), :]
        for h in range(H_A):
            vh = vb[:, h * DV_A:(h + 1) * DV_A]
            for c in range(2):
                s = s_ref[h, c]
                if masked:
                    s = jnp.where(diag_mask, s, MASK_VALUE)
                m_old = m_ref[h, c]
                m_new = jnp.maximum(m_old, jnp.max(s, axis=0, keepdims=True))
                alpha = jnp.exp(m_old - m_new)
                p = jnp.exp(s - m_new)
                l_ref[h, c] = alpha * l_ref[h, c] + jnp.sum(p, axis=0, keepdims=True)
                pv = lax.dot_general(vh, p.astype(BF16), (((0,), (0,)), ((), ())),
                                     preferred_element_type=F32)
                acc_ref[h, c] = alpha * acc_ref[h, c] + pv
                m_ref[h, c] = m_new

    scores(0, sa_ref)

    def body(j, carry):
        @pl.when(j % 2 == 0)
        def _():
            scores(j + 1, sb_ref)
            consume(j, sa_ref, False)

        @pl.when(j % 2 == 1)
        def _():
            scores(j + 1, sa_ref)
            consume(j, sb_ref, False)

        return carry

    lax.fori_loop(0, qi, body, 0)

    @pl.when(qi % 2 == 0)
    def _():
        consume(qi, sa_ref, True)

    @pl.when(qi % 2 == 1)
    def _():
        consume(qi, sb_ref, True)

    lam = lam_ref[0]
    for h in range(H_A):
        o = acc_ref[h, 0] / l_ref[h, 0] - lam * (acc_ref[h, 1] / l_ref[h, 1])
        ms = jnp.mean(o * o, axis=0, keepdims=True)
        y = o * lax.rsqrt(ms + EPS) * g_ref[...] * (1.0 - lam_init)
        o_ref[:, h * DV_A:(h + 1) * DV_A] = y.T.astype(BF16)


def _attention(qk, v, lam, subln_col, batch, seq, lam_init):
    assert TQ == TK and TQ % CHUNK == 0 and seq % TQ == 0
    nq = seq // TQ
    t_tok = batch * seq
    kernel = functools.partial(_attn_kernel, lam_init=lam_init)
    return pl.pallas_call(
        kernel,
        grid=(batch, nq),
        in_specs=[
            pl.BlockSpec(memory_space=pltpu.SMEM),
            pl.BlockSpec((TQ, W_A), lambda b, i: (b * nq + i, 0)),
            pl.BlockSpec((seq, W_A), lambda b, i: (b, 1)),
            pl.BlockSpec((seq, W_A), lambda b, i: (b, 0)),
            pl.BlockSpec((DV_A, 1), lambda b, i: (0, 0)),
        ],
        out_specs=pl.BlockSpec((TQ, W_A), lambda b, i: (b * nq + i, 0)),
        out_shape=jax.ShapeDtypeStruct((t_tok, W_A), BF16),
        scratch_shapes=[
            pltpu.VMEM((H_A, 2, TQ, DV_A), BF16),
            pltpu.VMEM((H_A, 2, TK, TQ), F32),
            pltpu.VMEM((H_A, 2, TK, TQ), F32),
            pltpu.VMEM((H_A, 2, DV_A, TQ), F32),
            pltpu.VMEM((H_A, 2, 1, TQ), F32),
            pltpu.VMEM((H_A, 2, 1, TQ), F32),
        ],
        compiler_params=_cparams(("parallel", "arbitrary")),
        name="diff_attn",
    )(lam, qk, qk, v, subln_col)


_LEVELS = (32, 16, 8, 4, 2, 1)


def _ref_rows(g, h):
    n = g.shape[1]
    if 2 * h >= SUBLANES:
        pieces = []
        for m in range(CHUNK // (2 * h)):
            r = 2 * h * m + h - 1
            pieces.append(jnp.broadcast_to(g[r:r + 1, :], (2 * h, n)))
        return pieces[0] if len(pieces) == 1 else jnp.concatenate(pieces, axis=0)
    g3 = g.reshape(CHUNK // SUBLANES, SUBLANES, n)
    sub = lax.broadcasted_iota(jnp.int32, g3.shape, 1)
    out = None
    for m in reversed(range(SUBLANES // (2 * h))):
        r = 2 * h * m + h - 1
        piece = jnp.broadcast_to(g3[:, r:r + 1, :], g3.shape)
        out = piece if out is None else jnp.where(sub < 2 * h * (m + 1), piece, out)
    return out.reshape(CHUNK, n)


def _stack_heads(a, lane_head):
    zero = jnp.zeros_like(a)
    return jnp.concatenate([jnp.where(lane_head == hh, a, zero) for hh in range(H_B)], axis=0)


def _hgrn_kernel(hb_ref, f_ref, c_ref, lb_ref, ng_ref, cw_ref, gavg_ref, o_ref, state_ref, hbuf_ref, *, seq):
    n_chunks = seq // CHUNK
    nb = hb_ref.shape[0] // seq
    lb = lb_ref[...]
    ng = ng_ref[...]
    gavg = gavg_ref[...]

    row = lax.broadcasted_iota(jnp.int32, (CHUNK, W_B), 0)
    lane = lax.broadcasted_iota(jnp.int32, (CHUNK, W_B), 1)
    lane_head = lane // DK_B
    lane_s = lane % CHUNK
    tri = (lax.broadcasted_iota(jnp.int32, (CHUNK, CHUNK), 1)
           <= lax.broadcasted_iota(jnp.int32, (CHUNK, CHUNK), 0)).astype(BF16)
    lvl_mask = [((row // (2 * h)) == (lane_s // (2 * h))) & ((row % (2 * h)) >= h) & ((lane_s % (2 * h)) < h)
                for h in _LEVELS]
    eye_mask = row == lane_s
    r4 = lax.broadcasted_iota(jnp.int32, (H_B * DK_B, W_B), 0) // DK_B
    c4 = lax.broadcasted_iota(jnp.int32, (H_B * DK_B, W_B), 1) // DK_B
    bd_mask = r4 == c4

    state_ref[...] = jnp.zeros_like(state_ref)
    nt = (((1,), (1,)), ((), ()))
    tn = (((0,), (0,)), ((), ()))
    each = range(nb)

    def chunk(i, carry):
        rows = [pl.ds(pl.multiple_of(j * seq + i * CHUNK, CHUNK), CHUNK) for j in each]
        hb = [hb_ref[rows[j], :].astype(F32) for j in each]
        zq = [t[:, 0:W_B] for t in hb]
        v16 = [t[:, W_B:2 * W_B].astype(BF16) for t in hb]
        zg = [t[:, 2 * W_B:3 * W_B] for t in hb]
        zf = [f_ref[rows[j], :] for j in each]

        logf = [jnp.log(jnp.maximum(lb + (1.0 - lb) * jax.nn.sigmoid(z), TINY)) for z in zf]
        key = [(1.0 - lb) * jax.nn.sigmoid(-z) for z in zf]
        q = [z * jax.nn.sigmoid(z) for z in zq]

        g = []
        for x in logf:
            hi = x.astype(BF16)
            r1 = x - hi.astype(F32)
            mid = r1.astype(BF16)
            lo = (r1 - mid.astype(F32)).astype(BF16)
            g.append(jnp.dot(tri, hi, preferred_element_type=F32) + jnp.dot(tri, mid, preferred_element_type=F32)
                     + jnp.dot(tri, lo, preferred_element_type=F32))

        a_all = [jnp.where(eye_mask, lax.dot_general(q[j].astype(BF16), _stack_heads(key[j].astype(BF16), lane_head),
                                                     nt, preferred_element_type=F32), 0.0) for j in each]
        for h, mask in zip(_LEVELS, lvl_mask):
            w = [jnp.exp(-jnp.abs(g[j] - _ref_rows(g[j], h))) for j in each]
            ah = [lax.dot_general((q[j] * w[j]).astype(BF16), _stack_heads((key[j] * w[j]).astype(BF16), lane_head),
                                  nt, preferred_element_type=F32) for j in each]
            a_all = [a_all[j] + jnp.where(mask, ah[j], 0.0) for j in each]
        o = [jnp.dot(a_all[j].astype(BF16), _stack_heads(v16[j], lane_head), preferred_element_type=F32) for j in each]

        st = [state_ref[j] for j in each]
        o = [o[j] + lax.dot_general((q[j] * jnp.exp(g[j])).astype(BF16), st[j].astype(BF16), nt,
                                    preferred_element_type=F32) for j in each]
        g_last = [t[CHUNK - 1:CHUNK, :] for t in g]
        upd = [lax.dot_general(v16[j], (key[j] * jnp.exp(g_last[j] - g[j])).astype(BF16), tn,
                               preferred_element_type=F32) for j in each]
        for j in each:
            state_ref[j] = jnp.exp(g_last[j]) * st[j] + jnp.where(bd_mask, upd[j], 0.0)

        ms = [jnp.dot((t * t).astype(BF16), gavg, preferred_element_type=F32) for t in o]
        for j in each:
            y = o[j] * lax.rsqrt(ms[j] + EPS) * ng * (zg[j] * jax.nn.sigmoid(zg[j]))
            o_ref[rows[j], 0:W_B] = y.astype(BF16)
        return carry

    lax.fori_loop(0, n_chunks, chunk, 0)

    piece = min(512, seq)
    hbuf_ref[0:SUBLANES, :] = jnp.zeros((SUBLANES, C_WIDTH), F32)
    for j in each:
        for p in range(seq // piece):
            cblk = c_ref[j * seq + p * piece:j * seq + (p + 1) * piece, :].astype(F32)
            hbuf_ref[SUBLANES + p * piece:SUBLANES + (p + 1) * piece, :] = (cblk[:, C_WIDTH:2 * C_WIDTH]
                                                                             * cblk[:, 2 * C_WIDTH:])
        for p in range(seq // piece):
            base = SUBLANES + p * piece
            y = (cw_ref[0:1, :] * hbuf_ref[base - 2:base - 2 + piece, :]
                 + cw_ref[1:2, :] * hbuf_ref[base - 1:base - 1 + piece, :]
                 + cw_ref[2:3, :] * hbuf_ref[base:base + piece, :])
            lo_r = j * seq + p * piece
            bgate = c_ref[lo_r:lo_r + piece, 0:C_WIDTH].astype(F32)
            o_ref[lo_r:lo_r + piece, W_B:W_B + C_WIDTH] = (bgate * y).astype(BF16)


def _hgrn_conv(hb, f, cv, lb, ng, cw, gavg, batch, seq):
    t_tok = batch * seq
    nb = HGRN_BATCH if batch % HGRN_BATCH == 0 else 1
    fixed = lambda b: (0, 0)
    return pl.pallas_call(
        functools.partial(_hgrn_kernel, seq=seq),
        grid=(batch // nb,),
        in_specs=[
            pl.BlockSpec((nb * seq, 3 * W_B), lambda b: (b, 0)),
            pl.BlockSpec((nb * seq, W_B), lambda b: (b, 0)),
            pl.BlockSpec((nb * seq, 3 * C_WIDTH), lambda b: (b, 0)),
            pl.BlockSpec((1, W_B), fixed),
            pl.BlockSpec((1, W_B), fixed),
            pl.BlockSpec((CONV_WIDTH, C_WIDTH), fixed),
            pl.BlockSpec((W_B, W_B), fixed),
        ],
        out_specs=pl.BlockSpec((nb * seq, W_B + C_WIDTH), lambda b: (b, 0)),
        out_shape=jax.ShapeDtypeStruct((t_tok, W_B + C_WIDTH), BF16),
        scratch_shapes=[
            pltpu.VMEM((nb, H_B * DV_B, W_B), F32),
            pltpu.VMEM((seq + SUBLANES, C_WIDTH), F32),
        ],
        compiler_params=_cparams(("parallel",)),
        name="hgrn_conv",
    )(hb, f, cv, lb, ng, cw, gavg)


def _outproj_router_kernel(oa_ref, obc_ref, x_ref, wo_ref, g_ref, wr_ref, br_ref,
                           xo_ref, route_ref, tab_ref, cnt_ref, carry_ref):
    i = pl.program_id(0)

    @pl.when(i == 0)
    def _():
        carry_ref[...] = jnp.zeros_like(carry_ref)

    mixed = (jnp.dot(oa_ref[...], wo_ref[0:W_A, :], preferred_element_type=F32)
             + jnp.dot(obc_ref[...], wo_ref[W_A:, :], preferred_element_type=F32))
    xn = x_ref[...] + mixed
    xo_ref[...] = xn

    ms = jnp.mean(xn * xn, axis=-1, keepdims=True)
    hn = (xn * lax.rsqrt(ms + EPS) * g_ref[...]).astype(BF16)
    logits = jnp.dot(hn, wr_ref[...], preferred_element_type=F32) + br_ref[...]

    tm = logits.shape[0]
    lane = lax.broadcasted_iota(jnp.int32, (tm, ROUTE_LANES), 1)
    neg = -jnp.inf
    big = ROUTE_LANES

    def top(mask):
        val = jnp.max(jnp.where(mask, logits, neg), axis=-1, keepdims=True)
        idx = jnp.min(jnp.where(mask & (logits == val), lane, big), axis=-1, keepdims=True)
        return val, idx

    gmask = lane < N_GROUPS
    gmax, gidx = top(gmask)
    p_g = 1.0 / jnp.sum(jnp.where(gmask, jnp.exp(logits - gmax), 0.0), axis=-1, keepdims=True)
    e_lo = N_GROUPS + EXPERTS_PER_GROUP * gidx
    emask = (lane >= e_lo) & (lane < e_lo + EXPERTS_PER_GROUP)
    v1, i1 = top(emask)
    v2, i2 = top(emask & (lane != i1))
    t2 = jnp.exp(v2 - v1)
    w1 = p_g / (1.0 + t2)
    w2 = p_g * t2 / (1.0 + t2)

    sel1 = lane == i1
    sel2 = lane == i2
    onehot = jnp.where(sel1, 1.0, jnp.where(sel2, 1.0, 0.0))
    strict = (lax.broadcasted_iota(jnp.int32, (tm, tm), 1)
              < lax.broadcasted_iota(jnp.int32, (tm, tm), 0)).astype(BF16)
    local = jnp.dot(strict, onehot.astype(BF16), preferred_element_type=F32)
    cnt = jnp.sum(onehot, axis=0, keepdims=True)
    wrows = jnp.floor((cnt + (MOE_WINDOW - 1)) * (1.0 / MOE_WINDOW)) * MOE_WINDOW
    upper = (lax.broadcasted_iota(jnp.int32, (ROUTE_LANES, ROUTE_LANES), 0)
             < lax.broadcasted_iota(jnp.int32, (ROUTE_LANES, ROUTE_LANES), 1)).astype(BF16)
    soff = jnp.dot(jnp.broadcast_to(wrows, (SUBLANES, ROUTE_LANES)).astype(BF16), upper,
                   preferred_element_type=F32)[0:1, :]
    slot = local + soff
    row1 = jnp.sum(jnp.where(sel1, slot, 0.0), axis=-1, keepdims=True)
    row2 = jnp.sum(jnp.where(sel2, slot, 0.0), axis=-1, keepdims=True)

    rec = jnp.where(lane == 0, (i1 - N_GROUPS).astype(F32),
          jnp.where(lane == 1, (i2 - N_GROUPS).astype(F32),
          jnp.where(lane == 2, w1,
          jnp.where(lane == 3, w2,
          jnp.where(lane == 4, row1,
          jnp.where(lane == 5, row2, 0.0))))))
    route_ref[...] = rec
    tab_ref[0] = jnp.zeros((SUBLANES, ROUTE_LANES), F32)
    tab_ref[0, 0:1, :] = cnt
    tab_ref[0, 1:2, :] = carry_ref[0:1, :]
    carry_ref[0:1, :] = carry_ref[0:1, :] + jnp.floor((cnt + (SUBLANES - 1)) * (1.0 / SUBLANES)) * SUBLANES
    cnt_ref[...] = carry_ref[...]


def _outproj_router(o_a, o_bc, x2, wo, g, wr, br):
    t_tok = x2.shape[0]
    tm = min(TM_ROUTE, t_tok)
    row = lambda i: (i, 0)
    fixed = lambda i: (0, 0)
    return pl.pallas_call(
        _outproj_router_kernel,
        grid=(t_tok // tm,),
        in_specs=[
            pl.BlockSpec((tm, W_A), row),
            pl.BlockSpec((tm, W_B + C_WIDTH), row),
            pl.BlockSpec((tm, D_MODEL), row),
            pl.BlockSpec(wo.shape, fixed),
            pl.BlockSpec((1, D_MODEL), fixed),
            pl.BlockSpec(wr.shape, fixed),
            pl.BlockSpec((1, ROUTE_LANES), fixed),
        ],
        out_specs=[
            pl.BlockSpec((tm, D_MODEL), row),
            pl.BlockSpec((tm, ROUTE_LANES), row),
            pl.BlockSpec((1, SUBLANES, ROUTE_LANES), lambda i: (i, 0, 0)),
            pl.BlockSpec((SUBLANES, ROUTE_LANES), fixed),
        ],
        out_shape=[
            jax.ShapeDtypeStruct((t_tok, D_MODEL), F32),
            jax.ShapeDtypeStruct((t_tok, ROUTE_LANES), F32),
            jax.ShapeDtypeStruct((t_tok // tm, SUBLANES, ROUTE_LANES), F32),
            jax.ShapeDtypeStruct((SUBLANES, ROUTE_LANES), F32),
        ],
        scratch_shapes=[pltpu.VMEM((SUBLANES, ROUTE_LANES), F32)],
        compiler_params=_cparams(("arbitrary",)),
        name="outproj_router",
    )(o_a, o_bc, x2, wo, g, wr, br)


def _window_copy_loops(n, make_copy):
    def start(k, carry):
        make_copy(k).start()
        return carry

    def wait(k, carry):
        make_copy(k).wait()
        return carry

    lax.fori_loop(0, n, start, 0)
    lax.fori_loop(0, n, wait, 0)


def _dispatch_kernel(pend_ref, tab_ref, route_ref, x_ref, buf_ref, xp_ref, sem):
    tm = x_ref.shape[0]
    r_st = xp_ref.shape[0]

    @pl.when(pl.program_id(0) == 0)
    def _():
        xp_ref[...] = jnp.zeros_like(xp_ref)
        copies = []
        for e in range(N_EXPERTS):
            end = pend_ref[e]
            copies.append((None, pltpu.make_async_copy(
                xp_ref.at[0:MOE_BLOCK, :],
                buf_ref.at[pl.ds(pl.multiple_of(end - MOE_BLOCK, SUBLANES), MOE_BLOCK), :], sem)))
            copies.append((end >= MOE_BLOCK + MOE_WINDOW, pltpu.make_async_copy(
                xp_ref.at[0:MOE_WINDOW, :],
                buf_ref.at[pl.ds(pl.multiple_of(jnp.maximum(end - MOE_BLOCK - MOE_WINDOW, 0), SUBLANES), MOE_WINDOW), :],
                sem)))
        for cond, cp in copies:
            if cond is None:
                cp.start()
            else:
                pl.when(cond)(cp.start)
        for cond, cp in copies:
            if cond is None:
                cp.wait()
            else:
                pl.when(cond)(cp.wait)

        def tail_copy(k):
            return pltpu.make_async_copy(
                xp_ref.at[0:MOE_BLOCK, :],
                buf_ref.at[pl.ds(pl.multiple_of(k * MOE_BLOCK, MOE_BLOCK), MOE_BLOCK), :], sem)

        first_unused = pend_ref[N_EXPERTS - 1] // MOE_BLOCK
        n_unused = buf_ref.shape[0] // MOE_BLOCK - first_unused
        _window_copy_loops(n_unused, lambda k: tail_copy(first_unused + k))

    route = route_ref[...]
    r0 = route[:, 4:5].astype(jnp.int32)
    r1 = route[:, 5:6].astype(jnp.int32)
    col = lax.broadcasted_iota(jnp.int32, (tm, r_st), 1)
    sel = jnp.where(col == r0, 1.0, jnp.where(col == r1, 1.0, 0.0)).astype(BF16)
    xp_ref[...] = lax.dot_general(sel, x_ref[...].astype(BF16), (((0,), (0,)), ((), ())),
                                  preferred_element_type=F32)

    def make_copy(k):
        src = pl.multiple_of(k * MOE_WINDOW, MOE_WINDOW)
        return pltpu.make_async_copy(xp_ref.at[pl.ds(src, MOE_WINDOW), :],
                                     buf_ref.at[pl.ds(pl.multiple_of(tab_ref[0, 0, k], SUBLANES), MOE_WINDOW), :],
                                     sem)

    _window_copy_loops(tab_ref[0, 0, MAX_WINDOWS], make_copy)


def _dispatch(pad_end, tabs, route, x2, cap):
    t_tok = x2.shape[0]
    tm = min(TM_ROUTE, t_tok)
    n_tiles = t_tok // tm
    return pl.pallas_call(
        _dispatch_kernel,
        grid=(n_tiles,),
        in_specs=[
            pl.BlockSpec(memory_space=pltpu.SMEM),
            pl.BlockSpec((1, 1, ROUTE_LANES), lambda i: (i, 0, 0), memory_space=pltpu.SMEM),
            pl.BlockSpec((tm, ROUTE_LANES), lambda i: (i, 0)),
            pl.BlockSpec((tm, D_MODEL), lambda i: (i, 0)),
        ],
        out_specs=pl.BlockSpec(memory_space=pl.ANY),
        out_shape=jax.ShapeDtypeStruct((cap, D_MODEL), F32),
        scratch_shapes=[pltpu.VMEM((_staging_rows(tm), D_MODEL), F32), pltpu.SemaphoreType.DMA(())],
        compiler_params=_cparams(("arbitrary",)),
        name="moe_dispatch",
    )(pad_end, tabs, route, x2)


def _experts_kernel(blk_e_ref, n_used_ref, x_ref, g_ref, wg_ref, wu_ref, wd_ref, y_ref):
    del blk_e_ref
    i = pl.program_id(0)

    @pl.when(i < n_used_ref[0])
    def _():
        x = x_ref[...]
        ms = jnp.mean(x * x, axis=-1, keepdims=True)
        hn = (x * lax.rsqrt(ms + EPS) * g_ref[...]).astype(BF16)
        a = jnp.dot(hn, wg_ref[0].astype(BF16), preferred_element_type=F32)
        u = jnp.dot(hn, wu_ref[0].astype(BF16), preferred_element_type=F32)
        act = (a * jax.nn.sigmoid(a) * u).astype(BF16)
        y_ref[...] = jnp.dot(act, wd_ref[0].astype(BF16), preferred_element_type=F32)

    @pl.when(i >= n_used_ref[0])
    def _():
        y_ref[...] = jnp.zeros_like(y_ref)


def _experts(blk_e, n_used, buf, g, wg, wu, wd):
    n_blocks = buf.shape[0] // MOE_BLOCK
    grid_spec = pltpu.PrefetchScalarGridSpec(
        num_scalar_prefetch=2,
        grid=(n_blocks,),
        in_specs=[
            pl.BlockSpec((MOE_BLOCK, D_MODEL), lambda i, be, nu: (i, 0)),
            pl.BlockSpec((1, D_MODEL), lambda i, be, nu: (0, 0)),
            pl.BlockSpec((1, D_MODEL, D_EXPERT), lambda i, be, nu: (be[i], 0, 0)),
            pl.BlockSpec((1, D_MODEL, D_EXPERT), lambda i, be, nu: (be[i], 0, 0)),
            pl.BlockSpec((1, D_EXPERT, D_MODEL), lambda i, be, nu: (be[i], 0, 0)),
        ],
        out_specs=pl.BlockSpec((MOE_BLOCK, D_MODEL), lambda i, be, nu: (i, 0)),
    )
    return pl.pallas_call(
        _experts_kernel,
        grid_spec=grid_spec,
        out_shape=jax.ShapeDtypeStruct(buf.shape, F32),
        compiler_params=_cparams(("arbitrary",)),
        name="moe_experts",
    )(blk_e, n_used, buf, g, wg, wu, wd)


def _combine_kernel(tab_ref, route_ref, x_ref, y_ref, o_ref, yst_ref, sem):
    tm = x_ref.shape[0]
    r_st = yst_ref.shape[0]

    @pl.when(pl.program_id(0) == 0)
    def _():
        yst_ref[...] = jnp.zeros_like(yst_ref)

    def make_copy(k):
        dst = pl.multiple_of(k * MOE_WINDOW, MOE_WINDOW)
        return pltpu.make_async_copy(y_ref.at[pl.ds(pl.multiple_of(tab_ref[0, 0, k], SUBLANES), MOE_WINDOW), :],
                                     yst_ref.at[pl.ds(dst, MOE_WINDOW), :], sem)

    _window_copy_loops(tab_ref[0, 0, MAX_WINDOWS], make_copy)

    route = route_ref[...]
    r0 = route[:, 4:5].astype(jnp.int32)
    r1 = route[:, 5:6].astype(jnp.int32)
    col = lax.broadcasted_iota(jnp.int32, (tm, r_st), 1)
    gate = jnp.where(col == r0, route[:, 2:3], jnp.where(col == r1, route[:, 3:4], 0.0)).astype(BF16)
    o_ref[...] = x_ref[...] + jnp.dot(gate, yst_ref[...].astype(BF16), preferred_element_type=F32)


def _combine(tabs, route, x2, y_buf):
    t_tok = x2.shape[0]
    tm = min(TM_ROUTE, t_tok)
    n_tiles = t_tok // tm
    return pl.pallas_call(
        _combine_kernel,
        grid=(n_tiles,),
        in_specs=[
            pl.BlockSpec((1, 1, ROUTE_LANES), lambda i: (i, 0, 0), memory_space=pltpu.SMEM),
            pl.BlockSpec((tm, ROUTE_LANES), lambda i: (i, 0)),
            pl.BlockSpec((tm, D_MODEL), lambda i: (i, 0)),
            pl.BlockSpec(memory_space=pl.ANY),
        ],
        out_specs=pl.BlockSpec((tm, D_MODEL), lambda i: (i, 0)),
        out_shape=jax.ShapeDtypeStruct((t_tok, D_MODEL), F32),
        scratch_shapes=[pltpu.VMEM((_staging_rows(tm), D_MODEL), F32), pltpu.SemaphoreType.DMA(())],
        compiler_params=_cparams(("arbitrary",)),
        name="moe_combine",
    )(tabs, route, x2, y_buf)


def _rope_tables(positions):
    half = ROT_DIM // 2
    inv_freq = ROPE_THETA ** (-jnp.arange(0, ROT_DIM, 2, dtype=F32) / ROT_DIM)
    d = jnp.arange(LANES) % DK_A
    freq_lane = jnp.where(d < ROT_DIM, inv_freq[d % half], 0.0)
    ang = positions.reshape(-1, 1).astype(F32) * freq_lane[None, :]
    cos, sin = jnp.cos(ang), jnp.sin(ang)
    s1 = jnp.where(d < half, -sin, 0.0)
    s2 = jnp.where((d >= half) & (d < ROT_DIM), sin, 0.0)
    return cos, s1, s2


def _block_avg(width, group):
    idx = jnp.arange(width) // group
    return jnp.where(idx[:, None] == idx[None, :], 1.0 / group, 0.0).astype(BF16)


def kernel(x, positions, norm_mix_g, w_in, attn_q_norm_g, attn_k_norm_g, lambda_q1, lambda_k1, lambda_q2,
           lambda_k2, attn_subln_g, hgrn_lower_bounds, hgrn_norm_g, conv_w, w_out, norm_ffn_g, router_group_w,
           router_group_b, router_expert_w, router_expert_b, expert_w_gate, expert_w_up, expert_w_down):
    batch, seq, d = x.shape
    assert d == D_MODEL
    depth = w_in.shape[0]
    t_tok = batch * seq
    n_assign = 2 * t_tok
    n_route_tiles = t_tok // min(TM_ROUTE, t_tok)
    n_blocks = (n_assign + (SUBLANES - 1) * n_route_tiles * N_EXPERTS) // MOE_BLOCK + 2 * N_EXPERTS + 1
    cap = n_blocks * MOE_BLOCK

    lb_probs = jax.nn.softmax(hgrn_lower_bounds.astype(F32), axis=0)
    lower_bounds = jnp.cumsum(lb_probs, axis=0) - lb_probs[0:1]
    cos_t, s1_t, s2_t = _rope_tables(positions)
    gsum_a = _block_avg(W_A, DK_A)
    gavg_b = _block_avg(W_B, DV_B)

    o_q, o_k, o_v = 0, W_A, 2 * W_A
    o_bq = 3 * W_A
    o_bf, o_bi, o_bg = o_bq + W_B, o_bq + 2 * W_B, o_bq + 3 * W_B
    o_c = o_bq + 4 * W_B

    x2 = x.reshape(t_tok, d)
    for l in range(depth):
        w = w_in[l]
        wqk = w[:, o_q:o_v].astype(BF16)
        wv = w[:, o_v:o_bq].astype(BF16)
        whb = jnp.concatenate([w[:, o_bq:o_bf], w[:, o_bi:o_c]], axis=1).astype(BF16)
        wf = w[:, o_bf:o_bi].astype(BF16)
        wc = w[:, o_c:].astype(BF16)
        gqk = jnp.concatenate([jnp.tile(attn_q_norm_g[l], 2 * H_A) * (DK_A ** -0.5),
                               jnp.tile(attn_k_norm_g[l], 2 * H_A)]).reshape(1, 2 * W_A).astype(F32)
        qk, v, hb, f, cv = _inproj(x2, norm_mix_g[l].reshape(1, d), wqk, wv, whb, wf, wc, gsum_a, gqk,
                                   cos_t, s1_t, s2_t)

        lam_init = 0.8 - 0.6 * math.exp(-0.3 * l)
        lam = (jnp.exp(jnp.sum(lambda_q1[l].astype(F32) * lambda_k1[l].astype(F32)))
               - jnp.exp(jnp.sum(lambda_q2[l].astype(F32) * lambda_k2[l].astype(F32))) + lam_init).reshape(1)
        o_a = _attention(qk, v, lam, attn_subln_g[l].reshape(DV_A, 1).astype(F32), batch, seq, lam_init)

        o_bc = _hgrn_conv(hb, f, cv, lower_bounds[l].reshape(1, W_B), jnp.tile(hgrn_norm_g[l], H_B).reshape(1, W_B),
                          conv_w[l], gavg_b, batch, seq)

        wr = jnp.concatenate([router_group_w[l], router_expert_w[l],
                              jnp.zeros((d, ROUTE_LANES - N_LOGITS), F32)], axis=1).astype(BF16)
        br = jnp.concatenate([router_group_b[l], router_expert_b[l],
                              jnp.zeros((ROUTE_LANES - N_LOGITS,), F32)]).reshape(1, ROUTE_LANES)
        x2, route, tab, cnt = _outproj_router(o_a, o_bc, x2, w_out[l].astype(BF16), norm_ffn_g[l].reshape(1, d), wr, br)

        counts = cnt[0, N_GROUPS:N_GROUPS + N_EXPERTS].astype(jnp.int32)
        padded = (counts + MOE_WINDOW + MOE_BLOCK - 1) // MOE_BLOCK * MOE_BLOCK
        pad_end = jnp.cumsum(padded)
        pad_start = pad_end - padded
        blk_pos = jnp.arange(n_blocks, dtype=jnp.int32) * MOE_BLOCK
        blk_e = jnp.minimum(jnp.sum((pad_end[None, :] <= blk_pos[:, None]).astype(jnp.int32), axis=1),
                            N_EXPERTS - 1)
        n_used = (pad_end[-1] // MOE_BLOCK).astype(jnp.int32).reshape(1)
        tile_cnt = tab[:, 0, N_GROUPS:N_GROUPS + N_EXPERTS].astype(jnp.int32)
        tile_base = tab[:, 1, N_GROUPS:N_GROUPS + N_EXPERTS].astype(jnp.int32) + pad_start[None, :]
        n_win = (tile_cnt + MOE_WINDOW - 1) // MOE_WINDOW
        w_end = jnp.cumsum(n_win, axis=1)
        w_start = w_end - n_win
        k_idx = jnp.arange(MAX_WINDOWS, dtype=jnp.int32)[None, :, None]
        owner = (w_start[:, None, :] <= k_idx) & (k_idx < w_end[:, None, :])
        win_dst = jnp.sum(jnp.where(owner, tile_base[:, None, :] + (k_idx - w_start[:, None, :]) * MOE_WINDOW, 0),
                          axis=-1)
        tabs = jnp.concatenate([win_dst, w_end[:, -1:],
                                jnp.zeros((win_dst.shape[0], ROUTE_LANES - MAX_WINDOWS - 1), jnp.int32)],
                               axis=1).reshape(-1, 1, ROUTE_LANES)

        buf = _dispatch(pad_end, tabs, route, x2, cap)
        y_buf = _experts(blk_e, n_used, buf, norm_ffn_g[l].reshape(1, d), expert_w_gate[l], expert_w_up[l],
                         expert_w_down[l])
        x2 = _combine(tabs, route, x2, y_buf)
    return x2.reshape(batch, seq, d)
```

```python
import functools
import math

import jax
import jax.numpy as jnp
from jax import lax
from jax.experimental import pallas as pl
from jax.experimental.pallas import tpu as pltpu

F32 = jnp.float32
BF16 = jnp.bfloat16

LANES = 128
SUBLANES = 8

D_MODEL = 1024
CHUNK = 64
EPS = 1e-6
MASK_VALUE = -1e30
TINY = 1e-30
H_A = 4
DK_A = D_MODEL // 16
DV_A = 2 * DK_A
ROT_DIM = DK_A // 4
ROPE_THETA = 500000.0
H_B = 4
DK_B = D_MODEL // 16
DV_B = D_MODEL // 16
C_WIDTH = D_MODEL // 4
CONV_WIDTH = 3
N_GROUPS = 4
EXPERTS_PER_GROUP = 8
N_EXPERTS = N_GROUPS * EXPERTS_PER_GROUP
D_EXPERT = D_MODEL // 2

W_A = H_A * DV_A
W_B = H_B * DK_B
ROUTE_LANES = LANES
N_LOGITS = N_GROUPS + N_EXPERTS

TM_PROJ = 512
TQ = 256
TK = 256
TM_ROUTE = 512
MOE_BLOCK = 512
HGRN_BATCH = 2
MOE_WINDOW = 16
MAX_WINDOWS = 2 * TM_ROUTE // MOE_WINDOW + N_EXPERTS
VMEM_LIMIT = 56 * 1024 * 1024


def _staging_rows(tm):
    return 2 * tm + N_EXPERTS * MOE_WINDOW


def _cparams(sem, vmem=VMEM_LIMIT):
    return pltpu.CompilerParams(dimension_semantics=sem, vmem_limit_bytes=vmem)


def _inproj_kernel(x_ref, g_ref, wqk_ref, wv_ref, whb_ref, wf_ref, wc_ref, gsum_ref, gqk_ref,
                   cos_ref, s1_ref, s2_ref, qk_out, v_out, hb_out, f_out, c_out):
    x = x_ref[...]
    ms = jnp.mean(x * x, axis=-1, keepdims=True)
    xn = (x * lax.rsqrt(ms + EPS) * g_ref[...]).astype(BF16)

    v_out[...] = jnp.dot(xn, wv_ref[...], preferred_element_type=F32).astype(BF16)
    hb_out[...] = jnp.dot(xn, whb_ref[...], preferred_element_type=F32).astype(BF16)
    f_out[...] = jnp.dot(xn, wf_ref[...], preferred_element_type=F32)
    c_out[...] = jnp.dot(xn, wc_ref[...], preferred_element_type=F32).astype(BF16)

    cos = cos_ref[...]
    s1 = s1_ref[...]
    s2 = s2_ref[...]
    gsum = gsum_ref[...]
    for part in range(2):
        lo = part * W_A
        t = jnp.dot(xn, wqk_ref[:, lo:lo + W_A], preferred_element_type=F32)
        msq = jnp.dot((t * t).astype(BF16), gsum, preferred_element_type=F32)
        t = t * lax.rsqrt(msq + EPS) * gqk_ref[:, lo:lo + W_A]
        for h in range(H_A):
            slab = t[:, h * LANES:(h + 1) * LANES]
            up = pltpu.roll(slab, LANES - ROT_DIM // 2, axis=1)
            dn = pltpu.roll(slab, ROT_DIM // 2, axis=1)
            qk_out[:, lo + h * LANES:lo + (h + 1) * LANES] = (slab * cos + up * s1 + dn * s2).astype(BF16)


def _inproj(x2, g, wqk, wv, whb, wf, wc, gsum, gqk, cos_t, s1_t, s2_t):
    t_tok = x2.shape[0]
    tm = min(TM_PROJ, t_tok)
    grid = (t_tok // tm,)
    row = lambda i: (i, 0)
    fixed = lambda i: (0, 0)
    return pl.pallas_call(
        _inproj_kernel,
        grid=grid,
        in_specs=[
            pl.BlockSpec((tm, D_MODEL), row),
            pl.BlockSpec((1, D_MODEL), fixed),
            pl.BlockSpec(wqk.shape, fixed),
            pl.BlockSpec(wv.shape, fixed),
            pl.BlockSpec(whb.shape, fixed),
            pl.BlockSpec(wf.shape, fixed),
            pl.BlockSpec(wc.shape, fixed),
            pl.BlockSpec(gsum.shape, fixed),
            pl.BlockSpec(gqk.shape, fixed),
            pl.BlockSpec((tm, LANES), row),
            pl.BlockSpec((tm, LANES), row),
            pl.BlockSpec((tm, LANES), row),
        ],
        out_specs=[
            pl.BlockSpec((tm, 2 * W_A), row),
            pl.BlockSpec((tm, W_A), row),
            pl.BlockSpec((tm, 3 * W_B), row),
            pl.BlockSpec((tm, W_B), row),
            pl.BlockSpec((tm, 3 * C_WIDTH), row),
        ],
        out_shape=[
            jax.ShapeDtypeStruct((t_tok, 2 * W_A), BF16),
            jax.ShapeDtypeStruct((t_tok, W_A), BF16),
            jax.ShapeDtypeStruct((t_tok, 3 * W_B), BF16),
            jax.ShapeDtypeStruct((t_tok, W_B), F32),
            jax.ShapeDtypeStruct((t_tok, 3 * C_WIDTH), BF16),
        ],
        compiler_params=_cparams(("parallel",)),
        name="inproj",
    )(x2, g, wqk, wv, whb, wf, wc, gsum, gqk, cos_t, s1_t, s2_t)


def _attn_kernel(lam_ref, q_ref, k_ref, v_ref, g_ref, o_ref, qc_ref, sa_ref, sb_ref, acc_ref, m_ref, l_ref, *,
                 lam_init):
    qi = pl.program_id(1)
    q = q_ref[...]
    lane = lax.broadcasted_iota(jnp.int32, (TQ, DV_A), 1)
    zero = jnp.zeros((TQ, DV_A), BF16)
    for h in range(H_A):
        qh = q[:, h * DV_A:(h + 1) * DV_A]
        qc_ref[h, 0] = jnp.where(lane < DK_A, qh, zero)
        qc_ref[h, 1] = jnp.where(lane >= DK_A, qh, zero)

    acc_ref[...] = jnp.zeros_like(acc_ref)
    m_ref[...] = jnp.full_like(m_ref, -jnp.inf)
    l_ref[...] = jnp.zeros_like(l_ref)

    kpos = lax.broadcasted_iota(jnp.int32, (TK, TQ), 0) // CHUNK
    qpos = lax.broadcasted_iota(jnp.int32, (TK, TQ), 1) // CHUNK
    diag_mask = kpos <= qpos

    def scores(j, s_ref):
        start = pl.multiple_of(j * TK, TK)
        kb = k_ref[pl.ds(start, TK), :]
        for h in range(H_A):
            kh = kb[:, h * DV_A:(h + 1) * DV_A]
            for c in range(2):
                s_ref[h, c] = lax.dot_general(kh, qc_ref[h, c], (((1,), (1,)), ((), ())),
                                              preferred_element_type=F32)

    def consume(j, s_ref, masked):
        start = pl.multiple_of(j * TK, TK)
        vb = v_ref[pl.ds(start, TK), :]
        for h in range(H_A):
            vh = vb[:, h * DV_A:(h + 1) * DV_A]
            for c in range(2):
                s = s_ref[h, c]
                if masked:
                    s = jnp.where(diag_mask, s, MASK_VALUE)
                m_old = m_ref[h, c]
                m_new = jnp.maximum(m_old, jnp.max(s, axis=0, keepdims=True))
                alpha = jnp.exp2(m_old - m_new)
                p = jnp.exp2(s - m_new)
                l_ref[h, c] = alpha * l_ref[h, c] + jnp.sum(p, axis=0, keepdims=True)
                pv = lax.dot_general(vh, p.astype(BF16), (((0,), (0,)), ((), ())),
                                     preferred_element_type=F32)
                acc_ref[h, c] = alpha * acc_ref[h, c] + pv
                m_ref[h, c] = m_new

    scores(0, sa_ref)

    def body(j, carry):
        @pl.when(j % 2 == 0)
        def _():
            scores(j + 1, sb_ref)
            consume(j, sa_ref, False)

        @pl.when(j % 2 == 1)
        def _():
            scores(j + 1, sa_ref)
            consume(j, sb_ref, False)

        return carry

    lax.fori_loop(0, qi, body, 0)

    @pl.when(qi % 2 == 0)
    def _():
        consume(qi, sa_ref, True)

    @pl.when(qi % 2 == 1)
    def _():
        consume(qi, sb_ref, True)

    lam = lam_ref[0]
    for h in range(H_A):
        o = acc_ref[h, 0] / l_ref[h, 0] - lam * (acc_ref[h, 1] / l_ref[h, 1])
        ms = jnp.mean(o * o, axis=0, keepdims=True)
        y = o * lax.rsqrt(ms + EPS) * g_ref[...] * (1.0 - lam_init)
        o_ref[:, h * DV_A:(h + 1) * DV_A] = y.T.astype(BF16)


def _attention(qk, v, lam, subln_col, batch, seq, lam_init):
    assert TQ == TK and TQ % CHUNK == 0 and seq % TQ == 0
    nq = seq // TQ
    t_tok = batch * seq
    kernel = functools.partial(_attn_kernel, lam_init=lam_init)
    return pl.pallas_call(
        kernel,
        grid=(batch, nq),
        in_specs=[
            pl.BlockSpec(memory_space=pltpu.SMEM),
            pl.BlockSpec((TQ, W_A), lambda b, i: (b * nq + i, 0)),
            pl.BlockSpec((seq, W_A), lambda b, i: (b, 1)),
            pl.BlockSpec((seq, W_A), lambda b, i: (b, 0)),
            pl.BlockSpec((DV_A, 1), lambda b, i: (0, 0)),
        ],
        out_specs=pl.BlockSpec((TQ, W_A), lambda b, i: (b * nq + i, 0)),
        out_shape=jax.ShapeDtypeStruct((t_tok, W_A), BF16),
        scratch_shapes=[
            pltpu.VMEM((H_A, 2, TQ, DV_A), BF16),
            pltpu.VMEM((H_A, 2, TK, TQ), F32),
            pltpu.VMEM((H_A, 2, TK, TQ), F32),
            pltpu.VMEM((H_A, 2, DV_A, TQ), F32),
            pltpu.VMEM((H_A, 2, 1, TQ), F32),
            pltpu.VMEM((H_A, 2, 1, TQ), F32),
        ],
        compiler_params=_cparams(("parallel", "arbitrary")),
        name="diff_attn",
    )(lam, qk, qk, v, subln_col)


_LEVELS = (32, 16, 8, 4, 2, 1)


def _ref_rows(g, h):
    n = g.shape[1]
    if 2 * h >= SUBLANES:
        pieces = []
        for m in range(CHUNK // (2 * h)):
            r = 2 * h * m + h - 1
            pieces.append(jnp.broadcast_to(g[r:r + 1, :], (2 * h, n)))
        return pieces[0] if len(pieces) == 1 else jnp.concatenate(pieces, axis=0)
    g3 = g.reshape(CHUNK // SUBLANES, SUBLANES, n)
    sub = lax.broadcasted_iota(jnp.int32, g3.shape, 1)
    out = None
    for m in reversed(range(SUBLANES // (2 * h))):
        r = 2 * h * m + h - 1
        piece = jnp.broadcast_to(g3[:, r:r + 1, :], g3.shape)
        out = piece if out is None else jnp.where(sub < 2 * h * (m + 1), piece, out)
    return out.reshape(CHUNK, n)


def _stack_heads(a, lane_head):
    zero = jnp.zeros_like(a)
    return jnp.concatenate([jnp.where(lane_head == hh, a, zero) for hh in range(H_B)], axis=0)


def _hgrn_kernel(hb_ref, f_ref, c_ref, lb_ref, ng_ref, cw_ref, gavg_ref, o_ref, state_ref, hbuf_ref, *, seq):
    n_chunks = seq // CHUNK
    nb = hb_ref.shape[0] // seq
    lb = lb_ref[...]
    ng = ng_ref[...]
    gavg = gavg_ref[...]

    row = lax.broadcasted_iota(jnp.int32, (CHUNK, W_B), 0)
    lane = lax.broadcasted_iota(jnp.int32, (CHUNK, W_B), 1)
    lane_head = lane // DK_B
    lane_s = lane % CHUNK
    tri = (lax.broadcasted_iota(jnp.int32, (CHUNK, CHUNK), 1)
           <= lax.broadcasted_iota(jnp.int32, (CHUNK, CHUNK), 0)).astype(BF16)
    lvl_mask = [((row // (2 * h)) == (lane_s // (2 * h))) & ((row % (2 * h)) >= h) & ((lane_s % (2 * h)) < h)
                for h in _LEVELS]
    eye_mask = row == lane_s
    r4 = lax.broadcasted_iota(jnp.int32, (H_B * DK_B, W_B), 0) // DK_B
    c4 = lax.broadcasted_iota(jnp.int32, (H_B * DK_B, W_B), 1) // DK_B
    bd_mask = r4 == c4

    state_ref[...] = jnp.zeros_like(state_ref)
    nt = (((1,), (1,)), ((), ()))
    tn = (((0,), (0,)), ((), ()))
    each = range(nb)

    def chunk(i, carry):
        rows = [pl.ds(pl.multiple_of(j * seq + i * CHUNK, CHUNK), CHUNK) for j in each]
        hb = [hb_ref[rows[j], :].astype(F32) for j in each]
        zq = [t[:, 0:W_B] for t in hb]
        v16 = [t[:, W_B:2 * W_B].astype(BF16) for t in hb]
        zg = [t[:, 2 * W_B:3 * W_B] for t in hb]
        zf = [f_ref[rows[j], :] for j in each]

        logf = [jnp.log(jnp.maximum(lb + (1.0 - lb) * jax.nn.sigmoid(z), TINY)) for z in zf]
        key = [(1.0 - lb) * jax.nn.sigmoid(-z) for z in zf]
        q = [z * jax.nn.sigmoid(z) for z in zq]

        g = []
        for x in logf:
            hi = x.astype(BF16)
            r1 = x - hi.astype(F32)
            mid = r1.astype(BF16)
            lo = (r1 - mid.astype(F32)).astype(BF16)
            g.append(jnp.dot(tri, hi, preferred_element_type=F32) + jnp.dot(tri, mid, preferred_element_type=F32)
                     + jnp.dot(tri, lo, preferred_element_type=F32))

        a_all = [jnp.where(eye_mask, lax.dot_general(q[j].astype(BF16), _stack_heads(key[j].astype(BF16), lane_head),
                                                     nt, preferred_element_type=F32), 0.0) for j in each]
        for h, mask in zip(_LEVELS, lvl_mask):
            w = [jnp.exp(-jnp.abs(g[j] - _ref_rows(g[j], h))) for j in each]
            ah = [lax.dot_general((q[j] * w[j]).astype(BF16), _stack_heads((key[j] * w[j]).astype(BF16), lane_head),
                                  nt, preferred_element_type=F32) for j in each]
            a_all = [a_all[j] + jnp.where(mask, ah[j], 0.0) for j in each]
        o = [jnp.dot(a_all[j].astype(BF16), _stack_heads(v16[j], lane_head), preferred_element_type=F32) for j in each]

        st = [state_ref[j] for j in each]
        o = [o[j] + lax.dot_general((q[j] * jnp.exp(g[j])).astype(BF16), st[j].astype(BF16), nt,
                                    preferred_element_type=F32) for j in each]
        g_last = [t[CHUNK - 1:CHUNK, :] for t in g]
        upd = [lax.dot_general(v16[j], (key[j] * jnp.exp(g_last[j] - g[j])).astype(BF16), tn,
                               preferred_element_type=F32) for j in each]
        for j in each:
            state_ref[j] = jnp.exp(g_last[j]) * st[j] + jnp.where(bd_mask, upd[j], 0.0)

        ms = [jnp.dot((t * t).astype(BF16), gavg, preferred_element_type=F32) for t in o]
        for j in each:
            y = o[j] * lax.rsqrt(ms[j] + EPS) * ng * (zg[j] * jax.nn.sigmoid(zg[j]))
            o_ref[rows[j], 0:W_B] = y.astype(BF16)
        return carry

    lax.fori_loop(0, n_chunks, chunk, 0)

    piece = min(512, seq)
    hbuf_ref[0:SUBLANES, :] = jnp.zeros((SUBLANES, C_WIDTH), F32)
    for j in each:
        for p in range(seq // piece):
            cblk = c_ref[j * seq + p * piece:j * seq + (p + 1) * piece, :].astype(F32)
            hbuf_ref[SUBLANES + p * piece:SUBLANES + (p + 1) * piece, :] = (cblk[:, C_WIDTH:2 * C_WIDTH]
                                                                             * cblk[:, 2 * C_WIDTH:])
        for p in range(seq // piece):
            base = SUBLANES + p * piece
            y = (cw_ref[0:1, :] * hbuf_ref[base - 2:base - 2 + piece, :]
                 + cw_ref[1:2, :] * hbuf_ref[base - 1:base - 1 + piece, :]
                 + cw_ref[2:3, :] * hbuf_ref[base:base + piece, :])
            lo_r = j * seq + p * piece
            bgate = c_ref[lo_r:lo_r + piece, 0:C_WIDTH].astype(F32)
            o_ref[lo_r:lo_r + piece, W_B:W_B + C_WIDTH] = (bgate * y).astype(BF16)


def _hgrn_conv(hb, f, cv, lb, ng, cw, gavg, batch, seq):
    t_tok = batch * seq
    nb = HGRN_BATCH if batch % HGRN_BATCH == 0 else 1
    fixed = lambda b: (0, 0)
    return pl.pallas_call(
        functools.partial(_hgrn_kernel, seq=seq),
        grid=(batch // nb,),
        in_specs=[
            pl.BlockSpec((nb * seq, 3 * W_B), lambda b: (b, 0)),
            pl.BlockSpec((nb * seq, W_B), lambda b: (b, 0)),
            pl.BlockSpec((nb * seq, 3 * C_WIDTH), lambda b: (b, 0)),
            pl.BlockSpec((1, W_B), fixed),
            pl.BlockSpec((1, W_B), fixed),
            pl.BlockSpec((CONV_WIDTH, C_WIDTH), fixed),
            pl.BlockSpec((W_B, W_B), fixed),
        ],
        out_specs=pl.BlockSpec((nb * seq, W_B + C_WIDTH), lambda b: (b, 0)),
        out_shape=jax.ShapeDtypeStruct((t_tok, W_B + C_WIDTH), BF16),
        scratch_shapes=[
            pltpu.VMEM((nb, H_B * DV_B, W_B), F32),
            pltpu.VMEM((seq + SUBLANES, C_WIDTH), F32),
        ],
        compiler_params=_cparams(("parallel",)),
        name="hgrn_conv",
    )(hb, f, cv, lb, ng, cw, gavg)


def _outproj_router_kernel(oa_ref, obc_ref, x_ref, wo_ref, g_ref, wr_ref, br_ref,
                           xo_ref, route_ref, tab_ref, cnt_ref, carry_ref):
    i = pl.program_id(0)

    @pl.when(i == 0)
    def _():
        carry_ref[...] = jnp.zeros_like(carry_ref)

    mixed = (jnp.dot(oa_ref[...], wo_ref[0:W_A, :], preferred_element_type=F32)
             + jnp.dot(obc_ref[...], wo_ref[W_A:, :], preferred_element_type=F32))
    xn = x_ref[...] + mixed
    xo_ref[...] = xn

    ms = jnp.mean(xn * xn, axis=-1, keepdims=True)
    hn = (xn * lax.rsqrt(ms + EPS) * g_ref[...]).astype(BF16)
    logits = jnp.dot(hn, wr_ref[...], preferred_element_type=F32) + br_ref[...]

    tm = logits.shape[0]
    lane = lax.broadcasted_iota(jnp.int32, (tm, ROUTE_LANES), 1)
    neg = -jnp.inf
    big = ROUTE_LANES

    def top(mask):
        val = jnp.max(jnp.where(mask, logits, neg), axis=-1, keepdims=True)
        idx = jnp.min(jnp.where(mask & (logits == val), lane, big), axis=-1, keepdims=True)
        return val, idx

    gmask = lane < N_GROUPS
    gmax, gidx = top(gmask)
    p_g = 1.0 / jnp.sum(jnp.where(gmask, jnp.exp(logits - gmax), 0.0), axis=-1, keepdims=True)
    e_lo = N_GROUPS + EXPERTS_PER_GROUP * gidx
    emask = (lane >= e_lo) & (lane < e_lo + EXPERTS_PER_GROUP)
    v1, i1 = top(emask)
    v2, i2 = top(emask & (lane != i1))
    t2 = jnp.exp(v2 - v1)
    w1 = p_g / (1.0 + t2)
    w2 = p_g * t2 / (1.0 + t2)

    sel1 = lane == i1
    sel2 = lane == i2
    onehot = jnp.where(sel1, 1.0, jnp.where(sel2, 1.0, 0.0))
    strict = (lax.broadcasted_iota(jnp.int32, (tm, tm), 1)
              < lax.broadcasted_iota(jnp.int32, (tm, tm), 0)).astype(BF16)
    local = jnp.dot(strict, onehot.astype(BF16), preferred_element_type=F32)
    cnt = jnp.sum(onehot, axis=0, keepdims=True)
    wrows = jnp.floor((cnt + (MOE_WINDOW - 1)) * (1.0 / MOE_WINDOW)) * MOE_WINDOW
    upper = (lax.broadcasted_iota(jnp.int32, (ROUTE_LANES, ROUTE_LANES), 0)
             < lax.broadcasted_iota(jnp.int32, (ROUTE_LANES, ROUTE_LANES), 1)).astype(BF16)
    soff = jnp.dot(jnp.broadcast_to(wrows, (SUBLANES, ROUTE_LANES)).astype(BF16), upper,
                   preferred_element_type=F32)[0:1, :]
    slot = local + soff
    row1 = jnp.sum(jnp.where(sel1, slot, 0.0), axis=-1, keepdims=True)
    row2 = jnp.sum(jnp.where(sel2, slot, 0.0), axis=-1, keepdims=True)

    rec = jnp.where(lane == 0, (i1 - N_GROUPS).astype(F32),
          jnp.where(lane == 1, (i2 - N_GROUPS).astype(F32),
          jnp.where(lane == 2, w1,
          jnp.where(lane == 3, w2,
          jnp.where(lane == 4, row1,
          jnp.where(lane == 5, row2, 0.0))))))
    route_ref[...] = rec
    tab_ref[0] = jnp.zeros((SUBLANES, ROUTE_LANES), F32)
    tab_ref[0, 0:1, :] = cnt
    tab_ref[0, 1:2, :] = carry_ref[0:1, :]
    carry_ref[0:1, :] = carry_ref[0:1, :] + jnp.floor((cnt + (SUBLANES - 1)) * (1.0 / SUBLANES)) * SUBLANES
    cnt_ref[...] = carry_ref[...]


def _outproj_router(o_a, o_bc, x2, wo, g, wr, br):
    t_tok = x2.shape[0]
    tm = min(TM_ROUTE, t_tok)
    row = lambda i: (i, 0)
    fixed = lambda i: (0, 0)
    return pl.pallas_call(
        _outproj_router_kernel,
        grid=(t_tok // tm,),
        in_specs=[
            pl.BlockSpec((tm, W_A), row),
            pl.BlockSpec((tm, W_B + C_WIDTH), row),
            pl.BlockSpec((tm, D_MODEL), row),
            pl.BlockSpec(wo.shape, fixed),
            pl.BlockSpec((1, D_MODEL), fixed),
            pl.BlockSpec(wr.shape, fixed),
            pl.BlockSpec((1, ROUTE_LANES), fixed),
        ],
        out_specs=[
            pl.BlockSpec((tm, D_MODEL), row),
            pl.BlockSpec((tm, ROUTE_LANES), row),
            pl.BlockSpec((1, SUBLANES, ROUTE_LANES), lambda i: (i, 0, 0)),
            pl.BlockSpec((SUBLANES, ROUTE_LANES), fixed),
        ],
        out_shape=[
            jax.ShapeDtypeStruct((t_tok, D_MODEL), F32),
            jax.ShapeDtypeStruct((t_tok, ROUTE_LANES), F32),
            jax.ShapeDtypeStruct((t_tok // tm, SUBLANES, ROUTE_LANES), F32),
            jax.ShapeDtypeStruct((SUBLANES, ROUTE_LANES), F32),
        ],
        scratch_shapes=[pltpu.VMEM((SUBLANES, ROUTE_LANES), F32)],
        compiler_params=_cparams(("arbitrary",)),
        name="outproj_router",
    )(o_a, o_bc, x2, wo, g, wr, br)


def _start_copies(n, make_copy):
    def start(k, carry):
        make_copy(k).start()
        return carry

    lax.fori_loop(0, n, start, 0)


def _wait_copies(n, make_copy):
    def wait(k, carry):
        make_copy(k).wait()
        return carry

    lax.fori_loop(0, n, wait, 0)


def _window_copy_loops(n, make_copy):
    _start_copies(n, make_copy)
    _wait_copies(n, make_copy)


def _dispatch_kernel(pend_ref, tab_ref, tabp_ref, route_ref, x_ref, buf_ref, xp_ref, xq_ref, sem, semq):
    tm = x_ref.shape[0]
    r_st = xp_ref.shape[0]
    step = pl.program_id(0)

    @pl.when(pl.program_id(0) == 0)
    def _():
        xp_ref[...] = jnp.zeros_like(xp_ref)
        copies = []
        for e in range(N_EXPERTS):
            end = pend_ref[e]
            copies.append((None, pltpu.make_async_copy(
                xp_ref.at[0:MOE_BLOCK, :],
                buf_ref.at[pl.ds(pl.multiple_of(end - MOE_BLOCK, SUBLANES), MOE_BLOCK), :], sem)))
            copies.append((end >= MOE_BLOCK + MOE_WINDOW, pltpu.make_async_copy(
                xp_ref.at[0:MOE_WINDOW, :],
                buf_ref.at[pl.ds(pl.multiple_of(jnp.maximum(end - MOE_BLOCK - MOE_WINDOW, 0), SUBLANES), MOE_WINDOW), :],
                sem)))
        for cond, cp in copies:
            if cond is None:
                cp.start()
            else:
                pl.when(cond)(cp.start)
        for cond, cp in copies:
            if cond is None:
                cp.wait()
            else:
                pl.when(cond)(cp.wait)

        def tail_copy(k):
            return pltpu.make_async_copy(
                xp_ref.at[0:MOE_BLOCK, :],
                buf_ref.at[pl.ds(pl.multiple_of(k * MOE_BLOCK, MOE_BLOCK), MOE_BLOCK), :], sem)

        first_unused = pend_ref[N_EXPERTS - 1] // MOE_BLOCK
        n_unused = buf_ref.shape[0] // MOE_BLOCK - first_unused
        _window_copy_loops(n_unused, lambda k: tail_copy(first_unused + k))

    route = route_ref[...]
    r0 = route[:, 4:5].astype(jnp.int32)
    r1 = route[:, 5:6].astype(jnp.int32)
    col = lax.broadcasted_iota(jnp.int32, (tm, r_st), 1)
    sel = jnp.where(col == r0, 1.0, jnp.where(col == r1, 1.0, 0.0)).astype(BF16)

    def window(tabs, stage_ref, s):
        def make_copy(k):
            src = pl.multiple_of(k * MOE_WINDOW, MOE_WINDOW)
            return pltpu.make_async_copy(
                stage_ref.at[pl.ds(src, MOE_WINDOW), :],
                buf_ref.at[pl.ds(pl.multiple_of(tabs[0, 0, k], SUBLANES), MOE_WINDOW), :], s)
        return tabs[0, 0, MAX_WINDOWS], make_copy

    def run(cur_ref, cur_sem, prev_ref, prev_sem):
        cur_ref[...] = lax.dot_general(sel, x_ref[...].astype(BF16), (((0,), (0,)), ((), ())),
                                       preferred_element_type=F32)

        @pl.when(step > 0)
        def _():
            _wait_copies(*window(tabp_ref, prev_ref, prev_sem))

        _start_copies(*window(tab_ref, cur_ref, cur_sem))

        @pl.when(step == pl.num_programs(0) - 1)
        def _():
            _wait_copies(*window(tab_ref, cur_ref, cur_sem))

    @pl.when(step % 2 == 0)
    def _():
        run(xp_ref, sem, xq_ref, semq)

    @pl.when(step % 2 == 1)
    def _():
        run(xq_ref, semq, xp_ref, sem)


def _dispatch(pad_end, tabs, route, x2, cap):
    t_tok = x2.shape[0]
    tm = min(TM_ROUTE, t_tok)
    n_tiles = t_tok // tm
    return pl.pallas_call(
        _dispatch_kernel,
        grid=(n_tiles,),
        in_specs=[
            pl.BlockSpec(memory_space=pltpu.SMEM),
            pl.BlockSpec((1, 1, ROUTE_LANES), lambda i: (i, 0, 0), memory_space=pltpu.SMEM),
            pl.BlockSpec((1, 1, ROUTE_LANES), lambda i: (jnp.maximum(i - 1, 0), 0, 0), memory_space=pltpu.SMEM),
            pl.BlockSpec((tm, ROUTE_LANES), lambda i: (i, 0)),
            pl.BlockSpec((tm, D_MODEL), lambda i: (i, 0)),
        ],
        out_specs=pl.BlockSpec(memory_space=pl.ANY),
        out_shape=jax.ShapeDtypeStruct((cap, D_MODEL), F32),
        scratch_shapes=[pltpu.VMEM((_staging_rows(tm), D_MODEL), F32), pltpu.VMEM((_staging_rows(tm), D_MODEL), F32),
                        pltpu.SemaphoreType.DMA(()), pltpu.SemaphoreType.DMA(())],
        compiler_params=_cparams(("arbitrary",)),
        name="moe_dispatch",
    )(pad_end, tabs, tabs, route, x2)


def _experts_kernel(blk_e_ref, n_used_ref, x_ref, g_ref, wg_ref, wu_ref, wd_ref, y_ref):
    del blk_e_ref
    i = pl.program_id(0)

    @pl.when(i < n_used_ref[0])
    def _():
        x = x_ref[...]
        ms = jnp.mean(x * x, axis=-1, keepdims=True)
        hn = (x * lax.rsqrt(ms + EPS) * g_ref[...]).astype(BF16)
        a = jnp.dot(hn, wg_ref[0].astype(BF16), preferred_element_type=F32)
        u = jnp.dot(hn, wu_ref[0].astype(BF16), preferred_element_type=F32)
        act = (a * jax.nn.sigmoid(a) * u).astype(BF16)
        y_ref[...] = jnp.dot(act, wd_ref[0].astype(BF16), preferred_element_type=F32)

    @pl.when(i >= n_used_ref[0])
    def _():
        y_ref[...] = jnp.zeros_like(y_ref)


def _experts(blk_e, n_used, buf, g, wg, wu, wd):
    n_blocks = buf.shape[0] // MOE_BLOCK
    grid_spec = pltpu.PrefetchScalarGridSpec(
        num_scalar_prefetch=2,
        grid=(n_blocks,),
        in_specs=[
            pl.BlockSpec((MOE_BLOCK, D_MODEL), lambda i, be, nu: (i, 0)),
            pl.BlockSpec((1, D_MODEL), lambda i, be, nu: (0, 0)),
            pl.BlockSpec((1, D_MODEL, D_EXPERT), lambda i, be, nu: (be[i], 0, 0)),
            pl.BlockSpec((1, D_MODEL, D_EXPERT), lambda i, be, nu: (be[i], 0, 0)),
            pl.BlockSpec((1, D_EXPERT, D_MODEL), lambda i, be, nu: (be[i], 0, 0)),
        ],
        out_specs=pl.BlockSpec((MOE_BLOCK, D_MODEL), lambda i, be, nu: (i, 0)),
    )
    return pl.pallas_call(
        _experts_kernel,
        grid_spec=grid_spec,
        out_shape=jax.ShapeDtypeStruct(buf.shape, F32),
        compiler_params=_cparams(("arbitrary",)),
        name="moe_experts",
    )(blk_e, n_used, buf, g, wg, wu, wd)


def _combine_kernel(tab_ref, tabn_ref, route_ref, x_ref, y_ref, o_ref, ya_ref, yb_ref, sema, semb):
    tm = x_ref.shape[0]
    r_st = ya_ref.shape[0]
    step = pl.program_id(0)

    def window(tabs, stage_ref, s):
        def make_copy(k):
            dst = pl.multiple_of(k * MOE_WINDOW, MOE_WINDOW)
            return pltpu.make_async_copy(y_ref.at[pl.ds(pl.multiple_of(tabs[0, 0, k], SUBLANES), MOE_WINDOW), :],
                                         stage_ref.at[pl.ds(dst, MOE_WINDOW), :], s)
        return tabs[0, 0, MAX_WINDOWS], make_copy

    @pl.when(step == 0)
    def _():
        ya_ref[...] = jnp.zeros_like(ya_ref)
        yb_ref[...] = jnp.zeros_like(yb_ref)
        _start_copies(*window(tab_ref, ya_ref, sema))

    route = route_ref[...]
    r0 = route[:, 4:5].astype(jnp.int32)
    r1 = route[:, 5:6].astype(jnp.int32)
    col = lax.broadcasted_iota(jnp.int32, (tm, r_st), 1)
    gate = jnp.where(col == r0, route[:, 2:3], jnp.where(col == r1, route[:, 3:4], 0.0)).astype(BF16)

    def run(cur_ref, cur_sem, nxt_ref, nxt_sem):
        @pl.when(step + 1 < pl.num_programs(0))
        def _():
            _start_copies(*window(tabn_ref, nxt_ref, nxt_sem))

        _wait_copies(*window(tab_ref, cur_ref, cur_sem))
        o_ref[...] = x_ref[...] + jnp.dot(gate, cur_ref[...].astype(BF16), preferred_element_type=F32)

    @pl.when(step % 2 == 0)
    def _():
        run(ya_ref, sema, yb_ref, semb)

    @pl.when(step % 2 == 1)
    def _():
        run(yb_ref, semb, ya_ref, sema)


def _combine(tabs, route, x2, y_buf):
    t_tok = x2.shape[0]
    tm = min(TM_ROUTE, t_tok)
    n_tiles = t_tok // tm
    return pl.pallas_call(
        _combine_kernel,
        grid=(n_tiles,),
        in_specs=[
            pl.BlockSpec((1, 1, ROUTE_LANES), lambda i: (i, 0, 0), memory_space=pltpu.SMEM),
            pl.BlockSpec((1, 1, ROUTE_LANES), lambda i: (jnp.minimum(i + 1, n_tiles - 1), 0, 0),
                         memory_space=pltpu.SMEM),
            pl.BlockSpec((tm, ROUTE_LANES), lambda i: (i, 0)),
            pl.BlockSpec((tm, D_MODEL), lambda i: (i, 0)),
            pl.BlockSpec(memory_space=pl.ANY),
        ],
        out_specs=pl.BlockSpec((tm, D_MODEL), lambda i: (i, 0)),
        out_shape=jax.ShapeDtypeStruct((t_tok, D_MODEL), F32),
        scratch_shapes=[pltpu.VMEM((_staging_rows(tm), D_MODEL), F32), pltpu.VMEM((_staging_rows(tm), D_MODEL), F32),
                        pltpu.SemaphoreType.DMA(()), pltpu.SemaphoreType.DMA(())],
        compiler_params=_cparams(("arbitrary",)),
        name="moe_combine",
    )(tabs, tabs, route, x2, y_buf)


def _rope_tables(positions):
    half = ROT_DIM // 2
    inv_freq = ROPE_THETA ** (-jnp.arange(0, ROT_DIM, 2, dtype=F32) / ROT_DIM)
    d = jnp.arange(LANES) % DK_A
    freq_lane = jnp.where(d < ROT_DIM, inv_freq[d % half], 0.0)
    ang = positions.reshape(-1, 1).astype(F32) * freq_lane[None, :]
    cos, sin = jnp.cos(ang), jnp.sin(ang)
    s1 = jnp.where(d < half, -sin, 0.0)
    s2 = jnp.where((d >= half) & (d < ROT_DIM), sin, 0.0)
    return cos, s1, s2


def _block_avg(width, group):
    idx = jnp.arange(width) // group
    return jnp.where(idx[:, None] == idx[None, :], 1.0 / group, 0.0).astype(BF16)


def kernel(x, positions, norm_mix_g, w_in, attn_q_norm_g, attn_k_norm_g, lambda_q1, lambda_k1, lambda_q2,
           lambda_k2, attn_subln_g, hgrn_lower_bounds, hgrn_norm_g, conv_w, w_out, norm_ffn_g, router_group_w,
           router_group_b, router_expert_w, router_expert_b, expert_w_gate, expert_w_up, expert_w_down):
    batch, seq, d = x.shape
    assert d == D_MODEL
    depth = w_in.shape[0]
    t_tok = batch * seq
    n_assign = 2 * t_tok
    n_route_tiles = t_tok // min(TM_ROUTE, t_tok)
    n_blocks = (n_assign + (SUBLANES - 1) * n_route_tiles * N_EXPERTS) // MOE_BLOCK + 2 * N_EXPERTS + 1
    cap = n_blocks * MOE_BLOCK

    lb_probs = jax.nn.softmax(hgrn_lower_bounds.astype(F32), axis=0)
    lower_bounds = jnp.cumsum(lb_probs, axis=0) - lb_probs[0:1]
    cos_t, s1_t, s2_t = _rope_tables(positions)
    gsum_a = _block_avg(W_A, DK_A)
    gavg_b = _block_avg(W_B, DV_B)

    o_q, o_k, o_v = 0, W_A, 2 * W_A
    o_bq = 3 * W_A
    o_bf, o_bi, o_bg = o_bq + W_B, o_bq + 2 * W_B, o_bq + 3 * W_B
    o_c = o_bq + 4 * W_B

    x2 = x.reshape(t_tok, d)
    for l in range(depth):
        w = w_in[l]
        wqk = w[:, o_q:o_v].astype(BF16)
        wv = w[:, o_v:o_bq].astype(BF16)
        whb = jnp.concatenate([w[:, o_bq:o_bf], w[:, o_bi:o_c]], axis=1).astype(BF16)
        wf = w[:, o_bf:o_bi].astype(BF16)
        wc = w[:, o_c:].astype(BF16)
        gqk = jnp.concatenate([jnp.tile(attn_q_norm_g[l], 2 * H_A) * (DK_A ** -0.5 * math.log2(math.e)),
                               jnp.tile(attn_k_norm_g[l], 2 * H_A)]).reshape(1, 2 * W_A).astype(F32)
        qk, v, hb, f, cv = _inproj(x2, norm_mix_g[l].reshape(1, d), wqk, wv, whb, wf, wc, gsum_a, gqk,
                                   cos_t, s1_t, s2_t)

        lam_init = 0.8 - 0.6 * math.exp(-0.3 * l)
        lam = (jnp.exp(jnp.sum(lambda_q1[l].astype(F32) * lambda_k1[l].astype(F32)))
               - jnp.exp(jnp.sum(lambda_q2[l].astype(F32) * lambda_k2[l].astype(F32))) + lam_init).reshape(1)
        o_a = _attention(qk, v, lam, attn_subln_g[l].reshape(DV_A, 1).astype(F32), batch, seq, lam_init)

        o_bc = _hgrn_conv(hb, f, cv, lower_bounds[l].reshape(1, W_B), jnp.tile(hgrn_norm_g[l], H_B).reshape(1, W_B),
                          conv_w[l], gavg_b, batch, seq)

        wr = jnp.concatenate([router_group_w[l], router_expert_w[l],
                              jnp.zeros((d, ROUTE_LANES - N_LOGITS), F32)], axis=1).astype(BF16)
        br = jnp.concatenate([router_group_b[l], router_expert_b[l],
                              jnp.zeros((ROUTE_LANES - N_LOGITS,), F32)]).reshape(1, ROUTE_LANES)
        x2, route, tab, cnt = _outproj_router(o_a, o_bc, x2, w_out[l].astype(BF16), norm_ffn_g[l].reshape(1, d), wr, br)

        counts = cnt[0, N_GROUPS:N_GROUPS + N_EXPERTS].astype(jnp.int32)
        padded = (counts + MOE_WINDOW + MOE_BLOCK - 1) // MOE_BLOCK * MOE_BLOCK
        pad_end = jnp.cumsum(padded)
        pad_start = pad_end - padded
        blk_pos = jnp.arange(n_blocks, dtype=jnp.int32) * MOE_BLOCK
        blk_e = jnp.minimum(jnp.sum((pad_end[None, :] <= blk_pos[:, None]).astype(jnp.int32), axis=1),
                            N_EXPERTS - 1)
        n_used = (pad_end[-1] // MOE_BLOCK).astype(jnp.int32).reshape(1)
        tile_cnt = tab[:, 0, N_GROUPS:N_GROUPS + N_EXPERTS].astype(jnp.int32)
        tile_base = tab[:, 1, N_GROUPS:N_GROUPS + N_EXPERTS].astype(jnp.int32) + pad_start[None, :]
        n_win = (tile_cnt + MOE_WINDOW - 1) // MOE_WINDOW
        w_end = jnp.cumsum(n_win, axis=1)
        w_start = w_end - n_win
        k_idx = jnp.arange(MAX_WINDOWS, dtype=jnp.int32)[None, :, None]
        owner = (w_start[:, None, :] <= k_idx) & (k_idx < w_end[:, None, :])
        win_dst = jnp.sum(jnp.where(owner, tile_base[:, None, :] + (k_idx - w_start[:, None, :]) * MOE_WINDOW, 0),
                          axis=-1)
        tabs = jnp.concatenate([win_dst, w_end[:, -1:],
                                jnp.zeros((win_dst.shape[0], ROUTE_LANES - MAX_WINDOWS - 1), jnp.int32)],
                               axis=1).reshape(-1, 1, ROUTE_LANES)

        buf = _dispatch(pad_end, tabs, route, x2, cap)
        y_buf = _experts(blk_e, n_used, buf, norm_ffn_g[l].reshape(1, d), expert_w_gate[l], expert_w_up[l],
                         expert_w_down[l])
        x2 = _combine(tabs, route, x2, y_buf)
    return x2.reshape(batch, seq, d)
```

```python
import functools
import math

import jax
import jax.numpy as jnp
from jax import lax
from jax.experimental import pallas as pl
from jax.experimental.pallas import tpu as pltpu

F32 = jnp.float32
BF16 = jnp.bfloat16

LANES = 128
SUBLANES = 8

D_MODEL = 1024
CHUNK = 64
EPS = 1e-6
MASK_VALUE = -1e30
TINY = 1e-30
H_A = 4
DK_A = D_MODEL // 16
DV_A = 2 * DK_A
ROT_DIM = DK_A // 4
ROPE_THETA = 500000.0
H_B = 4
DK_B = D_MODEL // 16
DV_B = D_MODEL // 16
C_WIDTH = D_MODEL // 4
CONV_WIDTH = 3
N_GROUPS = 4
EXPERTS_PER_GROUP = 8
N_EXPERTS = N_GROUPS * EXPERTS_PER_GROUP
D_EXPERT = D_MODEL // 2
D_PACKED = D_MODEL // 2

W_A = H_A * DV_A
W_B = H_B * DK_B
ROUTE_LANES = LANES
N_LOGITS = N_GROUPS + N_EXPERTS

TM_PROJ = 512
TQ = 256
TK = 256
TM_ROUTE = 512
MOE_BLOCK = 512
HGRN_BATCH = 2
MOE_WINDOW = 16
MAX_WINDOWS = 2 * TM_ROUTE // MOE_WINDOW + N_EXPERTS
VMEM_LIMIT = 56 * 1024 * 1024


def _staging_rows(tm):
    return 2 * tm + N_EXPERTS * MOE_WINDOW


def _row_parts(rows, n=2):
    step = rows // n
    return [slice(r * step, (r + 1) * step) for r in range(n)]


def _pack_rows(x):
    half = x.shape[1] // 2
    lo = lax.bitcast_convert_type(x[:, :half], jnp.int32)
    hi = lax.bitcast_convert_type(x[:, half:], jnp.int32)
    return hi | lax.shift_right_logical(lo, 16)


def _unpack_rows(p):
    lo = lax.bitcast_convert_type(lax.shift_left(p, 16), F32)
    hi = lax.bitcast_convert_type(p & jnp.int32(-65536), F32)
    return jnp.concatenate([lo, hi], axis=1)


def _cparams(sem, vmem=VMEM_LIMIT):
    return pltpu.CompilerParams(dimension_semantics=sem, vmem_limit_bytes=vmem)


def _inproj_kernel(x_ref, g_ref, wqk_ref, wv_ref, whb_ref, wf_ref, wc_ref, gsum_ref, gqk_ref,
                   cos_ref, s1_ref, s2_ref, qk_out, v_out, hb_out, f_out, c_out):
    parts = _row_parts(x_ref.shape[0])
    xn = []
    for rows in parts:
        x = x_ref[rows, :]
        ms = jnp.mean(x * x, axis=-1, keepdims=True)
        xn.append((x * lax.rsqrt(ms + EPS) * g_ref[...]).astype(BF16))

    gsum = gsum_ref[...]
    for part in range(2):
        lo = part * W_A
        t = [jnp.dot(xr, wqk_ref[:, lo:lo + W_A], preferred_element_type=F32) for xr in xn]
        msq = [jnp.dot((tr * tr).astype(BF16), gsum, preferred_element_type=F32) for tr in t]
        for r, rows in enumerate(parts):
            tn = t[r] * lax.rsqrt(msq[r] + EPS) * gqk_ref[:, lo:lo + W_A]
            cos = cos_ref[rows, :]
            s1 = s1_ref[rows, :]
            s2 = s2_ref[rows, :]
            for h in range(H_A):
                slab = tn[:, h * LANES:(h + 1) * LANES]
                up = pltpu.roll(slab, LANES - ROT_DIM // 2, axis=1)
                dn = pltpu.roll(slab, ROT_DIM // 2, axis=1)
                qk_out[rows, lo + h * LANES:lo + (h + 1) * LANES] = (slab * cos + up * s1 + dn * s2).astype(BF16)

    for r, rows in enumerate(parts):
        v_out[rows, :] = jnp.dot(xn[r], wv_ref[...], preferred_element_type=F32).astype(BF16)
    for r, rows in enumerate(parts):
        hb_out[rows, :] = jnp.dot(xn[r], whb_ref[...], preferred_element_type=F32).astype(BF16)
    for r, rows in enumerate(parts):
        f_out[rows, :] = jnp.dot(xn[r], wf_ref[...], preferred_element_type=F32)
    for r, rows in enumerate(parts):
        c_out[rows, :] = jnp.dot(xn[r], wc_ref[...], preferred_element_type=F32).astype(BF16)


def _inproj(x2, g, wqk, wv, whb, wf, wc, gsum, gqk, cos_t, s1_t, s2_t):
    t_tok = x2.shape[0]
    tm = min(TM_PROJ, t_tok)
    grid = (t_tok // tm,)
    row = lambda i: (i, 0)
    fixed = lambda i: (0, 0)
    return pl.pallas_call(
        _inproj_kernel,
        grid=grid,
        in_specs=[
            pl.BlockSpec((tm, D_MODEL), row),
            pl.BlockSpec((1, D_MODEL), fixed),
            pl.BlockSpec(wqk.shape, fixed),
            pl.BlockSpec(wv.shape, fixed),
            pl.BlockSpec(whb.shape, fixed),
            pl.BlockSpec(wf.shape, fixed),
            pl.BlockSpec(wc.shape, fixed),
            pl.BlockSpec(gsum.shape, fixed),
            pl.BlockSpec(gqk.shape, fixed),
            pl.BlockSpec((tm, LANES), row),
            pl.BlockSpec((tm, LANES), row),
            pl.BlockSpec((tm, LANES), row),
        ],
        out_specs=[
            pl.BlockSpec((tm, 2 * W_A), row),
            pl.BlockSpec((tm, W_A), row),
            pl.BlockSpec((tm, 3 * W_B), row),
            pl.BlockSpec((tm, W_B), row),
            pl.BlockSpec((tm, 3 * C_WIDTH), row),
        ],
        out_shape=[
            jax.ShapeDtypeStruct((t_tok, 2 * W_A), BF16),
            jax.ShapeDtypeStruct((t_tok, W_A), BF16),
            jax.ShapeDtypeStruct((t_tok, 3 * W_B), BF16),
            jax.ShapeDtypeStruct((t_tok, W_B), F32),
            jax.ShapeDtypeStruct((t_tok, 3 * C_WIDTH), BF16),
        ],
        compiler_params=_cparams(("parallel",)),
        name="inproj",
    )(x2, g, wqk, wv, whb, wf, wc, gsum, gqk, cos_t, s1_t, s2_t)


def _attn_kernel(lam_ref, q_ref, k_ref, v_ref, g_ref, o_ref, qc_ref, sa_ref, sb_ref, acc_ref, m_ref, l_ref, *,
                 lam_init):
    qi = pl.program_id(1)
    q = q_ref[...]
    lane = lax.broadcasted_iota(jnp.int32, (TQ, DV_A), 1)
    zero = jnp.zeros((TQ, DV_A), BF16)
    for h in range(H_A):
        qh = q[:, h * DV_A:(h + 1) * DV_A]
        qc_ref[h, 0] = jnp.where(lane < DK_A, qh, zero)
        qc_ref[h, 1] = jnp.where(lane >= DK_A, qh, zero)

    acc_ref[...] = jnp.zeros_like(acc_ref)
    m_ref[...] = jnp.full_like(m_ref, -jnp.inf)
    l_ref[...] = jnp.zeros_like(l_ref)

    kpos = lax.broadcasted_iota(jnp.int32, (TK, TQ), 0) // CHUNK
    qpos = lax.broadcasted_iota(jnp.int32, (TK, TQ), 1) // CHUNK
    diag_mask = kpos <= qpos

    def scores(j, s_ref):
        start = pl.multiple_of(j * TK, TK)
        kb = k_ref[pl.ds(start, TK), :]
        for h in range(H_A):
            kh = kb[:, h * DV_A:(h + 1) * DV_A]
            for c in range(2):
                s_ref[h, c] = lax.dot_general(kh, qc_ref[h, c], (((1,), (1,)), ((), ())),
                                              preferred_element_type=F32)

    def consume(j, s_ref, masked):
        start = pl.multiple_of(j * TK, TK)
        vb = v_ref[pl.ds(start, TK), :]
        for h in range(H_A):
            vh = vb[:, h * DV_A:(h + 1) * DV_A]
            for c in range(2):
                s = s_ref[h, c]
                if masked:
                    s = jnp.where(diag_mask, s, MASK_VALUE)
                m_old = m_ref[h, c]
                m_new = jnp.maximum(m_old, jnp.max(s, axis=0, keepdims=True))
                alpha = jnp.exp2(m_old - m_new)
                p = jnp.exp2(s - m_new)
                l_ref[h, c] = alpha * l_ref[h, c] + jnp.sum(p, axis=0, keepdims=True)
                pv = lax.dot_general(vh, p.astype(BF16), (((0,), (0,)), ((), ())),
                                     preferred_element_type=F32)
                acc_ref[h, c] = alpha * acc_ref[h, c] + pv
                m_ref[h, c] = m_new

    scores(0, sa_ref)

    def body(j, carry):
        @pl.when(j % 2 == 0)
        def _():
            scores(j + 1, sb_ref)
            consume(j, sa_ref, False)

        @pl.when(j % 2 == 1)
        def _():
            scores(j + 1, sa_ref)
            consume(j, sb_ref, False)

        return carry

    lax.fori_loop(0, qi, body, 0)

    @pl.when(qi % 2 == 0)
    def _():
        consume(qi, sa_ref, True)

    @pl.when(qi % 2 == 1)
    def _():
        consume(qi, sb_ref, True)

    lam = lam_ref[0]
    for h in range(H_A):
        o = acc_ref[h, 0] / l_ref[h, 0] - lam * (acc_ref[h, 1] / l_ref[h, 1])
        ms = jnp.mean(o * o, axis=0, keepdims=True)
        y = o * lax.rsqrt(ms + EPS) * g_ref[...] * (1.0 - lam_init)
        o_ref[:, h * DV_A:(h + 1) * DV_A] = y.T.astype(BF16)


def _attention(qk, v, lam, subln_col, batch, seq, lam_init):
    assert TQ == TK and TQ % CHUNK == 0 and seq % TQ == 0
    nq = seq // TQ
    t_tok = batch * seq
    kernel = functools.partial(_attn_kernel, lam_init=lam_init)
    return pl.pallas_call(
        kernel,
        grid=(batch, nq),
        in_specs=[
            pl.BlockSpec(memory_space=pltpu.SMEM),
            pl.BlockSpec((TQ, W_A), lambda b, i: (b * nq + i, 0)),
            pl.BlockSpec((seq, W_A), lambda b, i: (b, 1)),
            pl.BlockSpec((seq, W_A), lambda b, i: (b, 0)),
            pl.BlockSpec((DV_A, 1), lambda b, i: (0, 0)),
        ],
        out_specs=pl.BlockSpec((TQ, W_A), lambda b, i: (b * nq + i, 0)),
        out_shape=jax.ShapeDtypeStruct((t_tok, W_A), BF16),
        scratch_shapes=[
            pltpu.VMEM((H_A, 2, TQ, DV_A), BF16),
            pltpu.VMEM((H_A, 2, TK, TQ), F32),
            pltpu.VMEM((H_A, 2, TK, TQ), F32),
            pltpu.VMEM((H_A, 2, DV_A, TQ), F32),
            pltpu.VMEM((H_A, 2, 1, TQ), F32),
            pltpu.VMEM((H_A, 2, 1, TQ), F32),
        ],
        compiler_params=_cparams(("parallel", "arbitrary")),
        name="diff_attn",
    )(lam, qk, qk, v, subln_col)


_LEVELS = (32, 16, 8, 4, 2, 1)


def _ref_rows(g, h):
    n = g.shape[1]
    if 2 * h >= SUBLANES:
        pieces = []
        for m in range(CHUNK // (2 * h)):
            r = 2 * h * m + h - 1
            pieces.append(jnp.broadcast_to(g[r:r + 1, :], (2 * h, n)))
        return pieces[0] if len(pieces) == 1 else jnp.concatenate(pieces, axis=0)
    g3 = g.reshape(CHUNK // SUBLANES, SUBLANES, n)
    sub = lax.broadcasted_iota(jnp.int32, g3.shape, 1)
    out = None
    for m in reversed(range(SUBLANES // (2 * h))):
        r = 2 * h * m + h - 1
        piece = jnp.broadcast_to(g3[:, r:r + 1, :], g3.shape)
        out = piece if out is None else jnp.where(sub < 2 * h * (m + 1), piece, out)
    return out.reshape(CHUNK, n)


def _stack_heads(a, lane_head):
    zero = jnp.zeros_like(a)
    return jnp.concatenate([jnp.where(lane_head == hh, a, zero) for hh in range(H_B)], axis=0)


def _hgrn_kernel(hb_ref, f_ref, c_ref, lb_ref, ng_ref, cw_ref, gavg_ref, o_ref, state_ref, hbuf_ref, *, seq):
    n_chunks = seq // CHUNK
    nb = hb_ref.shape[0] // seq
    lb = lb_ref[...]
    ng = ng_ref[...]
    gavg = gavg_ref[...]

    row = lax.broadcasted_iota(jnp.int32, (CHUNK, W_B), 0)
    lane = lax.broadcasted_iota(jnp.int32, (CHUNK, W_B), 1)
    lane_head = lane // DK_B
    lane_s = lane % CHUNK
    tri = (lax.broadcasted_iota(jnp.int32, (CHUNK, CHUNK), 1)
           <= lax.broadcasted_iota(jnp.int32, (CHUNK, CHUNK), 0)).astype(BF16)
    lvl_mask = [((row // (2 * h)) == (lane_s // (2 * h))) & ((row % (2 * h)) >= h) & ((lane_s % (2 * h)) < h)
                for h in _LEVELS]
    eye_mask = row == lane_s
    r4 = lax.broadcasted_iota(jnp.int32, (H_B * DK_B, W_B), 0) // DK_B
    c4 = lax.broadcasted_iota(jnp.int32, (H_B * DK_B, W_B), 1) // DK_B
    bd_mask = r4 == c4

    state_ref[...] = jnp.zeros_like(state_ref)
    nt = (((1,), (1,)), ((), ()))
    tn = (((0,), (0,)), ((), ()))
    each = range(nb)

    def chunk(i, carry):
        rows = [pl.ds(pl.multiple_of(j * seq + i * CHUNK, CHUNK), CHUNK) for j in each]
        hb = [hb_ref[rows[j], :].astype(F32) for j in each]
        zq = [t[:, 0:W_B] for t in hb]
        v16 = [t[:, W_B:2 * W_B].astype(BF16) for t in hb]
        zg = [t[:, 2 * W_B:3 * W_B] for t in hb]
        zf = [f_ref[rows[j], :] for j in each]

        logf = [jnp.log(jnp.maximum(lb + (1.0 - lb) * jax.nn.sigmoid(z), TINY)) for z in zf]
        key = [(1.0 - lb) * jax.nn.sigmoid(-z) for z in zf]
        q = [z * jax.nn.sigmoid(z) for z in zq]

        g = []
        for x in logf:
            hi = x.astype(BF16)
            r1 = x - hi.astype(F32)
            mid = r1.astype(BF16)
            lo = (r1 - mid.astype(F32)).astype(BF16)
            g.append(jnp.dot(tri, hi, preferred_element_type=F32) + jnp.dot(tri, mid, preferred_element_type=F32)
                     + jnp.dot(tri, lo, preferred_element_type=F32))

        a_all = [jnp.where(eye_mask, lax.dot_general(q[j].astype(BF16), _stack_heads(key[j].astype(BF16), lane_head),
                                                     nt, preferred_element_type=F32), 0.0) for j in each]
        for h, mask in zip(_LEVELS, lvl_mask):
            w = [jnp.exp(-jnp.abs(g[j] - _ref_rows(g[j], h))) for j in each]
            ah = [lax.dot_general((q[j] * w[j]).astype(BF16), _stack_heads((key[j] * w[j]).astype(BF16), lane_head),
                                  nt, preferred_element_type=F32) for j in each]
            a_all = [a_all[j] + jnp.where(mask, ah[j], 0.0) for j in each]
        o = [jnp.dot(a_all[j].astype(BF16), _stack_heads(v16[j], lane_head), preferred_element_type=F32) for j in each]

        st = [state_ref[j] for j in each]
        o = [o[j] + lax.dot_general((q[j] * jnp.exp(g[j])).astype(BF16), st[j].astype(BF16), nt,
                                    preferred_element_type=F32) for j in each]
        g_last = [t[CHUNK - 1:CHUNK, :] for t in g]
        upd = [lax.dot_general(v16[j], (key[j] * jnp.exp(g_last[j] - g[j])).astype(BF16), tn,
                               preferred_element_type=F32) for j in each]
        for j in each:
            state_ref[j] = jnp.exp(g_last[j]) * st[j] + jnp.where(bd_mask, upd[j], 0.0)

        ms = [jnp.dot((t * t).astype(BF16), gavg, preferred_element_type=F32) for t in o]
        for j in each:
            y = o[j] * lax.rsqrt(ms[j] + EPS) * ng * (zg[j] * jax.nn.sigmoid(zg[j]))
            o_ref[rows[j], 0:W_B] = y.astype(BF16)
        return carry

    lax.fori_loop(0, n_chunks, chunk, 0)

    piece = min(512, seq)
    hbuf_ref[0:SUBLANES, :] = jnp.zeros((SUBLANES, C_WIDTH), F32)
    for j in each:
        for p in range(seq // piece):
            cblk = c_ref[j * seq + p * piece:j * seq + (p + 1) * piece, :].astype(F32)
            hbuf_ref[SUBLANES + p * piece:SUBLANES + (p + 1) * piece, :] = (cblk[:, C_WIDTH:2 * C_WIDTH]
                                                                             * cblk[:, 2 * C_WIDTH:])
        for p in range(seq // piece):
            base = SUBLANES + p * piece
            y = (cw_ref[0:1, :] * hbuf_ref[base - 2:base - 2 + piece, :]
                 + cw_ref[1:2, :] * hbuf_ref[base - 1:base - 1 + piece, :]
                 + cw_ref[2:3, :] * hbuf_ref[base:base + piece, :])
            lo_r = j * seq + p * piece
            bgate = c_ref[lo_r:lo_r + piece, 0:C_WIDTH].astype(F32)
            o_ref[lo_r:lo_r + piece, W_B:W_B + C_WIDTH] = (bgate * y).astype(BF16)


def _hgrn_conv(hb, f, cv, lb, ng, cw, gavg, batch, seq):
    t_tok = batch * seq
    nb = HGRN_BATCH if batch % HGRN_BATCH == 0 else 1
    fixed = lambda b: (0, 0)
    return pl.pallas_call(
        functools.partial(_hgrn_kernel, seq=seq),
        grid=(batch // nb,),
        in_specs=[
            pl.BlockSpec((nb * seq, 3 * W_B), lambda b: (b, 0)),
            pl.BlockSpec((nb * seq, W_B), lambda b: (b, 0)),
            pl.BlockSpec((nb * seq, 3 * C_WIDTH), lambda b: (b, 0)),
            pl.BlockSpec((1, W_B), fixed),
            pl.BlockSpec((1, W_B), fixed),
            pl.BlockSpec((CONV_WIDTH, C_WIDTH), fixed),
            pl.BlockSpec((W_B, W_B), fixed),
        ],
        out_specs=pl.BlockSpec((nb * seq, W_B + C_WIDTH), lambda b: (b, 0)),
        out_shape=jax.ShapeDtypeStruct((t_tok, W_B + C_WIDTH), BF16),
        scratch_shapes=[
            pltpu.VMEM((nb, H_B * DV_B, W_B), F32),
            pltpu.VMEM((seq + SUBLANES, C_WIDTH), F32),
        ],
        compiler_params=_cparams(("parallel",)),
        name="hgrn_conv",
    )(hb, f, cv, lb, ng, cw, gavg)


def _outproj_router_kernel(oa_ref, obc_ref, x_ref, wo_ref, g_ref, wr_ref, br_ref,
                           xo_ref, route_ref, tab_ref, cnt_ref, carry_ref):
    i = pl.program_id(0)

    @pl.when(i == 0)
    def _():
        carry_ref[...] = jnp.zeros_like(carry_ref)

    mixed = (jnp.dot(oa_ref[...], wo_ref[0:W_A, :], preferred_element_type=F32)
             + jnp.dot(obc_ref[...], wo_ref[W_A:, :], preferred_element_type=F32))
    xn = x_ref[...] + mixed
    xo_ref[...] = xn

    ms = jnp.mean(xn * xn, axis=-1, keepdims=True)
    hn = (xn * lax.rsqrt(ms + EPS) * g_ref[...]).astype(BF16)
    logits = jnp.dot(hn, wr_ref[...], preferred_element_type=F32) + br_ref[...]

    tm = logits.shape[0]
    lane = lax.broadcasted_iota(jnp.int32, (tm, ROUTE_LANES), 1)
    neg = -jnp.inf
    big = ROUTE_LANES

    def top(mask):
        val = jnp.max(jnp.where(mask, logits, neg), axis=-1, keepdims=True)
        idx = jnp.min(jnp.where(mask & (logits == val), lane, big), axis=-1, keepdims=True)
        return val, idx

    gmask = lane < N_GROUPS
    gmax, gidx = top(gmask)
    p_g = 1.0 / jnp.sum(jnp.where(gmask, jnp.exp(logits - gmax), 0.0), axis=-1, keepdims=True)
    e_lo = N_GROUPS + EXPERTS_PER_GROUP * gidx
    emask = (lane >= e_lo) & (lane < e_lo + EXPERTS_PER_GROUP)
    v1, i1 = top(emask)
    v2, i2 = top(emask & (lane != i1))
    t2 = jnp.exp(v2 - v1)
    w1 = p_g / (1.0 + t2)
    w2 = p_g * t2 / (1.0 + t2)

    sel1 = lane == i1
    sel2 = lane == i2
    onehot = jnp.where(sel1, 1.0, jnp.where(sel2, 1.0, 0.0))
    strict = (lax.broadcasted_iota(jnp.int32, (tm, tm), 1)
              < lax.broadcasted_iota(jnp.int32, (tm, tm), 0)).astype(BF16)
    local = jnp.dot(strict, onehot.astype(BF16), preferred_element_type=F32)
    cnt = jnp.sum(onehot, axis=0, keepdims=True)
    wrows = jnp.floor((cnt + (MOE_WINDOW - 1)) * (1.0 / MOE_WINDOW)) * MOE_WINDOW
    upper = (lax.broadcasted_iota(jnp.int32, (ROUTE_LANES, ROUTE_LANES), 0)
             < lax.broadcasted_iota(jnp.int32, (ROUTE_LANES, ROUTE_LANES), 1)).astype(BF16)
    soff = jnp.dot(jnp.broadcast_to(wrows, (SUBLANES, ROUTE_LANES)).astype(BF16), upper,
                   preferred_element_type=F32)[0:1, :]
    slot = local + soff
    row1 = jnp.sum(jnp.where(sel1, slot, 0.0), axis=-1, keepdims=True)
    row2 = jnp.sum(jnp.where(sel2, slot, 0.0), axis=-1, keepdims=True)

    rec = jnp.where(lane == 0, (i1 - N_GROUPS).astype(F32),
          jnp.where(lane == 1, (i2 - N_GROUPS).astype(F32),
          jnp.where(lane == 2, w1,
          jnp.where(lane == 3, w2,
          jnp.where(lane == 4, row1,
          jnp.where(lane == 5, row2, 0.0))))))
    route_ref[...] = rec
    tab_ref[0] = jnp.zeros((SUBLANES, ROUTE_LANES), F32)
    tab_ref[0, 0:1, :] = cnt
    tab_ref[0, 1:2, :] = carry_ref[0:1, :]
    carry_ref[0:1, :] = carry_ref[0:1, :] + jnp.floor((cnt + (SUBLANES - 1)) * (1.0 / SUBLANES)) * SUBLANES
    cnt_ref[...] = carry_ref[...]


def _outproj_router(o_a, o_bc, x2, wo, g, wr, br):
    t_tok = x2.shape[0]
    tm = min(TM_ROUTE, t_tok)
    row = lambda i: (i, 0)
    fixed = lambda i: (0, 0)
    return pl.pallas_call(
        _outproj_router_kernel,
        grid=(t_tok // tm,),
        in_specs=[
            pl.BlockSpec((tm, W_A), row),
            pl.BlockSpec((tm, W_B + C_WIDTH), row),
            pl.BlockSpec((tm, D_MODEL), row),
            pl.BlockSpec(wo.shape, fixed),
            pl.BlockSpec((1, D_MODEL), fixed),
            pl.BlockSpec(wr.shape, fixed),
            pl.BlockSpec((1, ROUTE_LANES), fixed),
        ],
        out_specs=[
            pl.BlockSpec((tm, D_MODEL), row),
            pl.BlockSpec((tm, ROUTE_LANES), row),
            pl.BlockSpec((1, SUBLANES, ROUTE_LANES), lambda i: (i, 0, 0)),
            pl.BlockSpec((SUBLANES, ROUTE_LANES), fixed),
        ],
        out_shape=[
            jax.ShapeDtypeStruct((t_tok, D_MODEL), F32),
            jax.ShapeDtypeStruct((t_tok, ROUTE_LANES), F32),
            jax.ShapeDtypeStruct((t_tok // tm, SUBLANES, ROUTE_LANES), F32),
            jax.ShapeDtypeStruct((SUBLANES, ROUTE_LANES), F32),
        ],
        scratch_shapes=[pltpu.VMEM((SUBLANES, ROUTE_LANES), F32)],
        compiler_params=_cparams(("arbitrary",)),
        name="outproj_router",
    )(o_a, o_bc, x2, wo, g, wr, br)


def _start_copies(n, make_copy):
    def start(k, carry):
        make_copy(k).start()
        return carry

    lax.fori_loop(0, n, start, 0)


def _wait_copies(n, make_copy):
    def wait(k, carry):
        make_copy(k).wait()
        return carry

    lax.fori_loop(0, n, wait, 0)


def _window_copy_loops(n, make_copy):
    _start_copies(n, make_copy)
    _wait_copies(n, make_copy)


def _dispatch_kernel(pend_ref, tab_ref, tabp_ref, route_ref, x_ref, buf_ref, xp_ref, xq_ref, sem, semq):
    tm = x_ref.shape[0]
    r_st = xp_ref.shape[0]
    step = pl.program_id(0)

    @pl.when(pl.program_id(0) == 0)
    def _():
        xp_ref[...] = jnp.zeros_like(xp_ref)
        copies = []
        for e in range(N_EXPERTS):
            end = pend_ref[e]
            copies.append((None, pltpu.make_async_copy(
                xp_ref.at[0:MOE_BLOCK, :],
                buf_ref.at[pl.ds(pl.multiple_of(end - MOE_BLOCK, SUBLANES), MOE_BLOCK), :], sem)))
            copies.append((end >= MOE_BLOCK + MOE_WINDOW, pltpu.make_async_copy(
                xp_ref.at[0:MOE_WINDOW, :],
                buf_ref.at[pl.ds(pl.multiple_of(jnp.maximum(end - MOE_BLOCK - MOE_WINDOW, 0), SUBLANES), MOE_WINDOW), :],
                sem)))
        for cond, cp in copies:
            if cond is None:
                cp.start()
            else:
                pl.when(cond)(cp.start)
        for cond, cp in copies:
            if cond is None:
                cp.wait()
            else:
                pl.when(cond)(cp.wait)

        def tail_copy(k):
            return pltpu.make_async_copy(
                xp_ref.at[0:MOE_BLOCK, :],
                buf_ref.at[pl.ds(pl.multiple_of(k * MOE_BLOCK, MOE_BLOCK), MOE_BLOCK), :], sem)

        first_unused = pend_ref[N_EXPERTS - 1] // MOE_BLOCK
        n_unused = buf_ref.shape[0] // MOE_BLOCK - first_unused
        _window_copy_loops(n_unused, lambda k: tail_copy(first_unused + k))

    route = route_ref[...]
    r0 = route[:, 4:5].astype(jnp.int32)
    r1 = route[:, 5:6].astype(jnp.int32)
    col = lax.broadcasted_iota(jnp.int32, (tm, r_st), 1)
    sel = jnp.where(col == r0, 1.0, jnp.where(col == r1, 1.0, 0.0)).astype(BF16)

    def window(tabs, stage_ref, s):
        def make_copy(k):
            src = pl.multiple_of(k * MOE_WINDOW, MOE_WINDOW)
            return pltpu.make_async_copy(
                stage_ref.at[pl.ds(src, MOE_WINDOW), :],
                buf_ref.at[pl.ds(pl.multiple_of(tabs[0, 0, k], SUBLANES), MOE_WINDOW), :], s)
        return tabs[0, 0, MAX_WINDOWS], make_copy

    def run(cur_ref, cur_sem, prev_ref, prev_sem):
        cur_ref[...] = _pack_rows(lax.dot_general(sel, x_ref[...].astype(BF16), (((0,), (0,)), ((), ())),
                                                  preferred_element_type=F32))

        @pl.when(step > 0)
        def _():
            _wait_copies(*window(tabp_ref, prev_ref, prev_sem))

        _start_copies(*window(tab_ref, cur_ref, cur_sem))

        @pl.when(step == pl.num_programs(0) - 1)
        def _():
            _wait_copies(*window(tab_ref, cur_ref, cur_sem))

    @pl.when(step % 2 == 0)
    def _():
        run(xp_ref, sem, xq_ref, semq)

    @pl.when(step % 2 == 1)
    def _():
        run(xq_ref, semq, xp_ref, sem)


def _dispatch(pad_end, tabs, route, x2, cap):
    t_tok = x2.shape[0]
    tm = min(TM_ROUTE, t_tok)
    n_tiles = t_tok // tm
    return pl.pallas_call(
        _dispatch_kernel,
        grid=(n_tiles,),
        in_specs=[
            pl.BlockSpec(memory_space=pltpu.SMEM),
            pl.BlockSpec((1, 1, ROUTE_LANES), lambda i: (i, 0, 0), memory_space=pltpu.SMEM),
            pl.BlockSpec((1, 1, ROUTE_LANES), lambda i: (jnp.maximum(i - 1, 0), 0, 0), memory_space=pltpu.SMEM),
            pl.BlockSpec((tm, ROUTE_LANES), lambda i: (i, 0)),
            pl.BlockSpec((tm, D_MODEL), lambda i: (i, 0)),
        ],
        out_specs=pl.BlockSpec(memory_space=pl.ANY),
        out_shape=jax.ShapeDtypeStruct((cap, D_PACKED), jnp.int32),
        scratch_shapes=[pltpu.VMEM((_staging_rows(tm), D_PACKED), jnp.int32),
                        pltpu.VMEM((_staging_rows(tm), D_PACKED), jnp.int32),
                        pltpu.SemaphoreType.DMA(()), pltpu.SemaphoreType.DMA(())],
        compiler_params=_cparams(("arbitrary",)),
        name="moe_dispatch",
    )(pad_end, tabs, tabs, route, x2)


def _experts_kernel(blk_e_ref, n_used_ref, x_ref, g_ref, wg_ref, wu_ref, wd_ref, y_ref):
    del blk_e_ref
    i = pl.program_id(0)

    @pl.when(i < n_used_ref[0])
    def _():
        wg = wg_ref[0, 0].astype(BF16)
        wu = wu_ref[0, 0].astype(BF16)
        wd = wd_ref[0, 0].astype(BF16)
        parts = _row_parts(x_ref.shape[0])
        hn = []
        for rows in parts:
            x = _unpack_rows(x_ref[rows, :])
            ms = jnp.mean(x * x, axis=-1, keepdims=True)
            hn.append((x * lax.rsqrt(ms + EPS) * g_ref[...]).astype(BF16))
        a = [jnp.dot(t, wg, preferred_element_type=F32) for t in hn]
        u = [jnp.dot(t, wu, preferred_element_type=F32) for t in hn]
        act = [(a[r] * jax.nn.sigmoid(a[r]) * u[r]).astype(BF16) for r in range(len(parts))]
        for r, rows in enumerate(parts):
            y = jnp.dot(act[r], wd, preferred_element_type=F32)
            y_ref[rows, :] = _pack_rows(y.astype(BF16).astype(F32))

    @pl.when(i >= n_used_ref[0])
    def _():
        y_ref[...] = jnp.zeros_like(y_ref)


def _experts(blk_e, n_used, buf, g, wg, wu, wd, layer):
    n_blocks = buf.shape[0] // MOE_BLOCK
    grid_spec = pltpu.PrefetchScalarGridSpec(
        num_scalar_prefetch=2,
        grid=(n_blocks,),
        in_specs=[
            pl.BlockSpec((MOE_BLOCK, D_PACKED), lambda i, be, nu: (i, 0)),
            pl.BlockSpec((1, D_MODEL), lambda i, be, nu: (0, 0)),
            pl.BlockSpec((1, 1, D_MODEL, D_EXPERT), lambda i, be, nu: (layer, be[i], 0, 0)),
            pl.BlockSpec((1, 1, D_MODEL, D_EXPERT), lambda i, be, nu: (layer, be[i], 0, 0)),
            pl.BlockSpec((1, 1, D_EXPERT, D_MODEL), lambda i, be, nu: (layer, be[i], 0, 0)),
        ],
        out_specs=pl.BlockSpec((MOE_BLOCK, D_PACKED), lambda i, be, nu: (i, 0)),
    )
    return pl.pallas_call(
        _experts_kernel,
        grid_spec=grid_spec,
        out_shape=jax.ShapeDtypeStruct(buf.shape, jnp.int32),
        compiler_params=_cparams(("arbitrary",)),
        name="moe_experts",
    )(blk_e, n_used, buf, g, wg, wu, wd)


def _combine_kernel(tab_ref, tabn_ref, route_ref, x_ref, y_ref, o_ref, ya_ref, yb_ref, sema, semb):
    tm = x_ref.shape[0]
    r_st = ya_ref.shape[0]
    step = pl.program_id(0)

    def window(tabs, stage_ref, s):
        def make_copy(k):
            dst = pl.multiple_of(k * MOE_WINDOW, MOE_WINDOW)
            return pltpu.make_async_copy(y_ref.at[pl.ds(pl.multiple_of(tabs[0, 0, k], SUBLANES), MOE_WINDOW), :],
                                         stage_ref.at[pl.ds(dst, MOE_WINDOW), :], s)
        return tabs[0, 0, MAX_WINDOWS], make_copy

    @pl.when(step == 0)
    def _():
        ya_ref[...] = jnp.zeros_like(ya_ref)
        yb_ref[...] = jnp.zeros_like(yb_ref)
        _start_copies(*window(tab_ref, ya_ref, sema))

    route = route_ref[...]
    r0 = route[:, 4:5].astype(jnp.int32)
    r1 = route[:, 5:6].astype(jnp.int32)
    col = lax.broadcasted_iota(jnp.int32, (tm, r_st), 1)
    gate = jnp.where(col == r0, route[:, 2:3], jnp.where(col == r1, route[:, 3:4], 0.0)).astype(BF16)

    def run(cur_ref, cur_sem, nxt_ref, nxt_sem):
        @pl.when(step + 1 < pl.num_programs(0))
        def _():
            _start_copies(*window(tabn_ref, nxt_ref, nxt_sem))

        _wait_copies(*window(tab_ref, cur_ref, cur_sem))
        o_ref[...] = x_ref[...] + jnp.dot(gate, _unpack_rows(cur_ref[...]).astype(BF16), preferred_element_type=F32)

    @pl.when(step % 2 == 0)
    def _():
        run(ya_ref, sema, yb_ref, semb)

    @pl.when(step % 2 == 1)
    def _():
        run(yb_ref, semb, ya_ref, sema)


def _combine(tabs, route, x2, y_buf):
    t_tok = x2.shape[0]
    tm = min(TM_ROUTE, t_tok)
    n_tiles = t_tok // tm
    return pl.pallas_call(
        _combine_kernel,
        grid=(n_tiles,),
        in_specs=[
            pl.BlockSpec((1, 1, ROUTE_LANES), lambda i: (i, 0, 0), memory_space=pltpu.SMEM),
            pl.BlockSpec((1, 1, ROUTE_LANES), lambda i: (jnp.minimum(i + 1, n_tiles - 1), 0, 0),
                         memory_space=pltpu.SMEM),
            pl.BlockSpec((tm, ROUTE_LANES), lambda i: (i, 0)),
            pl.BlockSpec((tm, D_MODEL), lambda i: (i, 0)),
            pl.BlockSpec(memory_space=pl.ANY),
        ],
        out_specs=pl.BlockSpec((tm, D_MODEL), lambda i: (i, 0)),
        out_shape=jax.ShapeDtypeStruct((t_tok, D_MODEL), F32),
        scratch_shapes=[pltpu.VMEM((_staging_rows(tm), D_PACKED), jnp.int32),
                        pltpu.VMEM((_staging_rows(tm), D_PACKED), jnp.int32),
                        pltpu.SemaphoreType.DMA(()), pltpu.SemaphoreType.DMA(())],
        compiler_params=_cparams(("arbitrary",)),
        name="moe_combine",
    )(tabs, tabs, route, x2, y_buf)


def _rope_tables(positions):
    half = ROT_DIM // 2
    inv_freq = ROPE_THETA ** (-jnp.arange(0, ROT_DIM, 2, dtype=F32) / ROT_DIM)
    d = jnp.arange(LANES) % DK_A
    freq_lane = jnp.where(d < ROT_DIM, inv_freq[d % half], 0.0)
    ang = positions.reshape(-1, 1).astype(F32) * freq_lane[None, :]
    cos, sin = jnp.cos(ang), jnp.sin(ang)
    s1 = jnp.where(d < half, -sin, 0.0)
    s2 = jnp.where((d >= half) & (d < ROT_DIM), sin, 0.0)
    return cos, s1, s2


def _block_avg(width, group):
    idx = jnp.arange(width) // group
    return jnp.where(idx[:, None] == idx[None, :], 1.0 / group, 0.0).astype(BF16)


def kernel(x, positions, norm_mix_g, w_in, attn_q_norm_g, attn_k_norm_g, lambda_q1, lambda_k1, lambda_q2,
           lambda_k2, attn_subln_g, hgrn_lower_bounds, hgrn_norm_g, conv_w, w_out, norm_ffn_g, router_group_w,
           router_group_b, router_expert_w, router_expert_b, expert_w_gate, expert_w_up, expert_w_down):
    batch, seq, d = x.shape
    assert d == D_MODEL
    depth = w_in.shape[0]
    t_tok = batch * seq
    n_assign = 2 * t_tok
    n_route_tiles = t_tok // min(TM_ROUTE, t_tok)
    n_blocks = (n_assign + (SUBLANES - 1) * n_route_tiles * N_EXPERTS) // MOE_BLOCK + 2 * N_EXPERTS + 1
    cap = n_blocks * MOE_BLOCK

    lb_probs = jax.nn.softmax(hgrn_lower_bounds.astype(F32), axis=0)
    lower_bounds = jnp.cumsum(lb_probs, axis=0) - lb_probs[0:1]
    cos_t, s1_t, s2_t = _rope_tables(positions)
    gsum_a = _block_avg(W_A, DK_A)
    gavg_b = _block_avg(W_B, DV_B)

    o_q, o_k, o_v = 0, W_A, 2 * W_A
    o_bq = 3 * W_A
    o_bf, o_bi, o_bg = o_bq + W_B, o_bq + 2 * W_B, o_bq + 3 * W_B
    o_c = o_bq + 4 * W_B

    x2 = x.reshape(t_tok, d)
    for l in range(depth):
        w = w_in[l]
        wqk = w[:, o_q:o_v].astype(BF16)
        wv = w[:, o_v:o_bq].astype(BF16)
        whb = jnp.concatenate([w[:, o_bq:o_bf], w[:, o_bi:o_c]], axis=1).astype(BF16)
        wf = w[:, o_bf:o_bi].astype(BF16)
        wc = w[:, o_c:].astype(BF16)
        gqk = jnp.concatenate([jnp.tile(attn_q_norm_g[l], 2 * H_A) * (DK_A ** -0.5 * math.log2(math.e)),
                               jnp.tile(attn_k_norm_g[l], 2 * H_A)]).reshape(1, 2 * W_A).astype(F32)
        qk, v, hb, f, cv = _inproj(x2, norm_mix_g[l].reshape(1, d), wqk, wv, whb, wf, wc, gsum_a, gqk,
                                   cos_t, s1_t, s2_t)

        lam_init = 0.8 - 0.6 * math.exp(-0.3 * l)
        lam = (jnp.exp(jnp.sum(lambda_q1[l].astype(F32) * lambda_k1[l].astype(F32)))
               - jnp.exp(jnp.sum(lambda_q2[l].astype(F32) * lambda_k2[l].astype(F32))) + lam_init).reshape(1)
        o_a = _attention(qk, v, lam, attn_subln_g[l].reshape(DV_A, 1).astype(F32), batch, seq, lam_init)

        o_bc = _hgrn_conv(hb, f, cv, lower_bounds[l].reshape(1, W_B), jnp.tile(hgrn_norm_g[l], H_B).reshape(1, W_B),
                          conv_w[l], gavg_b, batch, seq)

        wr = jnp.concatenate([router_group_w[l], router_expert_w[l],
                              jnp.zeros((d, ROUTE_LANES - N_LOGITS), F32)], axis=1).astype(BF16)
        br = jnp.concatenate([router_group_b[l], router_expert_b[l],
                              jnp.zeros((ROUTE_LANES - N_LOGITS,), F32)]).reshape(1, ROUTE_LANES)
        x2, route, tab, cnt = _outproj_router(o_a, o_bc, x2, w_out[l].astype(BF16), norm_ffn_g[l].reshape(1, d), wr, br)

        counts = cnt[0, N_GROUPS:N_GROUPS + N_EXPERTS].astype(jnp.int32)
        padded = (counts + MOE_WINDOW + MOE_BLOCK - 1) // MOE_BLOCK * MOE_BLOCK
        pad_end = jnp.cumsum(padded)
        pad_start = pad_end - padded
        blk_pos = jnp.arange(n_blocks, dtype=jnp.int32) * MOE_BLOCK
        blk_e = jnp.minimum(jnp.sum((pad_end[None, :] <= blk_pos[:, None]).astype(jnp.int32), axis=1),
                            N_EXPERTS - 1)
        n_used = (pad_end[-1] // MOE_BLOCK).astype(jnp.int32).reshape(1)
        tile_cnt = tab[:, 0, N_GROUPS:N_GROUPS + N_EXPERTS].astype(jnp.int32)
        tile_base = tab[:, 1, N_GROUPS:N_GROUPS + N_EXPERTS].astype(jnp.int32) + pad_start[None, :]
        n_win = (tile_cnt + MOE_WINDOW - 1) // MOE_WINDOW
        w_end = jnp.cumsum(n_win, axis=1)
        w_start = w_end - n_win
        k_idx = jnp.arange(MAX_WINDOWS, dtype=jnp.int32)[None, :, None]
        owner = (w_start[:, None, :] <= k_idx) & (k_idx < w_end[:, None, :])
        win_dst = jnp.sum(jnp.where(owner, tile_base[:, None, :] + (k_idx - w_start[:, None, :]) * MOE_WINDOW, 0),
                          axis=-1)
        tabs = jnp.concatenate([win_dst, w_end[:, -1:],
                                jnp.zeros((win_dst.shape[0], ROUTE_LANES - MAX_WINDOWS - 1), jnp.int32)],
                               axis=1).reshape(-1, 1, ROUTE_LANES)

        buf = _dispatch(pad_end, tabs, route, x2, cap)
        y_buf = _experts(blk_e, n_used, buf, norm_ffn_g[l].reshape(1, d), expert_w_gate, expert_w_up,
                         expert_w_down, l)
        x2 = _combine(tabs, route, x2, y_buf)
    return x2.reshape(batch, seq, d)
```

```python
import functools
import math

import jax
import jax.numpy as jnp
from jax import lax
from jax.experimental import pallas as pl
from jax.experimental.pallas import tpu as pltpu

F32 = jnp.float32
BF16 = jnp.bfloat16

LANES = 128
SUBLANES = 8

D_MODEL = 1024
CHUNK = 64
EPS = 1e-6
MASK_VALUE = -1e30
TINY = 1e-30
H_A = 4
DK_A = D_MODEL // 16
DV_A = 2 * DK_A
ROT_DIM = DK_A // 4
ROPE_THETA = 500000.0
H_B = 4
DK_B = D_MODEL // 16
DV_B = D_MODEL // 16
C_WIDTH = D_MODEL // 4
CONV_WIDTH = 3
N_GROUPS = 4
EXPERTS_PER_GROUP = 8
N_EXPERTS = N_GROUPS * EXPERTS_PER_GROUP
D_EXPERT = D_MODEL // 2
D_PACKED = D_MODEL // 2

W_A = H_A * DV_A
W_B = H_B * DK_B
ROUTE_LANES = LANES
N_LOGITS = N_GROUPS + N_EXPERTS

TM_PROJ = 512
TK = 256
KV_PER_Q = 2
TQ = KV_PER_Q * TK
TM_ROUTE = 512
MOE_BLOCK = 512
HGRN_BATCH = 2
MOE_WINDOW = 16
MAX_WINDOWS = 2 * TM_ROUTE // MOE_WINDOW + N_EXPERTS
VMEM_LIMIT = 56 * 1024 * 1024


def _staging_rows(tm):
    return 2 * tm + N_EXPERTS * MOE_WINDOW


def _row_parts(rows, n=2):
    step = rows // n
    return [slice(r * step, (r + 1) * step) for r in range(n)]


def _pack_rows(x):
    half = x.shape[1] // 2
    lo = lax.bitcast_convert_type(x[:, :half], jnp.int32)
    hi = lax.bitcast_convert_type(x[:, half:], jnp.int32)
    return hi | lax.shift_right_logical(lo, 16)


def _unpack_rows(p):
    lo = lax.bitcast_convert_type(lax.shift_left(p, 16), F32)
    hi = lax.bitcast_convert_type(p & jnp.int32(-65536), F32)
    return jnp.concatenate([lo, hi], axis=1)


def _cparams(sem, vmem=VMEM_LIMIT):
    return pltpu.CompilerParams(dimension_semantics=sem, vmem_limit_bytes=vmem)


def _inproj_kernel(x_ref, g_ref, wqk_ref, wv_ref, whb_ref, wf_ref, wc_ref, gsum_ref, gqk_ref,
                   cos_ref, s1_ref, s2_ref, qk_out, v_out, hb_out, f_out, c_out):
    parts = _row_parts(x_ref.shape[0])
    xn = []
    for rows in parts:
        x = x_ref[rows, :]
        ms = jnp.mean(x * x, axis=-1, keepdims=True)
        xn.append((x * lax.rsqrt(ms + EPS) * g_ref[...]).astype(BF16))

    gsum = gsum_ref[...]
    for part in range(2):
        lo = part * W_A
        t = [jnp.dot(xr, wqk_ref[:, lo:lo + W_A], preferred_element_type=F32) for xr in xn]
        msq = [jnp.dot((tr * tr).astype(BF16), gsum, preferred_element_type=F32) for tr in t]
        for r, rows in enumerate(parts):
            tn = t[r] * lax.rsqrt(msq[r] + EPS) * gqk_ref[:, lo:lo + W_A]
            cos = cos_ref[rows, :]
            s1 = s1_ref[rows, :]
            s2 = s2_ref[rows, :]
            for h in range(H_A):
                slab = tn[:, h * LANES:(h + 1) * LANES]
                up = pltpu.roll(slab, LANES - ROT_DIM // 2, axis=1)
                dn = pltpu.roll(slab, ROT_DIM // 2, axis=1)
                qk_out[rows, lo + h * LANES:lo + (h + 1) * LANES] = (slab * cos + up * s1 + dn * s2).astype(BF16)

    for r, rows in enumerate(parts):
        v_out[rows, :] = jnp.dot(xn[r], wv_ref[...], preferred_element_type=F32).astype(BF16)
    for r, rows in enumerate(parts):
        hb_out[rows, :] = jnp.dot(xn[r], whb_ref[...], preferred_element_type=F32).astype(BF16)
    for r, rows in enumerate(parts):
        f_out[rows, :] = jnp.dot(xn[r], wf_ref[...], preferred_element_type=F32)
    for r, rows in enumerate(parts):
        c_out[rows, :] = jnp.dot(xn[r], wc_ref[...], preferred_element_type=F32).astype(BF16)


def _inproj(x2, g, wqk, wv, whb, wf, wc, gsum, gqk, cos_t, s1_t, s2_t):
    t_tok = x2.shape[0]
    tm = min(TM_PROJ, t_tok)
    grid = (t_tok // tm,)
    row = lambda i: (i, 0)
    fixed = lambda i: (0, 0)
    return pl.pallas_call(
        _inproj_kernel,
        grid=grid,
        in_specs=[
            pl.BlockSpec((tm, D_MODEL), row),
            pl.BlockSpec((1, D_MODEL), fixed),
            pl.BlockSpec(wqk.shape, fixed),
            pl.BlockSpec(wv.shape, fixed),
            pl.BlockSpec(whb.shape, fixed),
            pl.BlockSpec(wf.shape, fixed),
            pl.BlockSpec(wc.shape, fixed),
            pl.BlockSpec(gsum.shape, fixed),
            pl.BlockSpec(gqk.shape, fixed),
            pl.BlockSpec((tm, LANES), row),
            pl.BlockSpec((tm, LANES), row),
            pl.BlockSpec((tm, LANES), row),
        ],
        out_specs=[
            pl.BlockSpec((tm, 2 * W_A), row),
            pl.BlockSpec((tm, W_A), row),
            pl.BlockSpec((tm, 3 * W_B), row),
            pl.BlockSpec((tm, W_B), row),
            pl.BlockSpec((tm, 3 * C_WIDTH), row),
        ],
        out_shape=[
            jax.ShapeDtypeStruct((t_tok, 2 * W_A), BF16),
            jax.ShapeDtypeStruct((t_tok, W_A), BF16),
            jax.ShapeDtypeStruct((t_tok, 3 * W_B), BF16),
            jax.ShapeDtypeStruct((t_tok, W_B), F32),
            jax.ShapeDtypeStruct((t_tok, 3 * C_WIDTH), BF16),
        ],
        compiler_params=_cparams(("parallel",)),
        name="inproj",
    )(x2, g, wqk, wv, whb, wf, wc, gsum, gqk, cos_t, s1_t, s2_t)


def _attn_kernel(lam_ref, q_ref, k_ref, v_ref, g_ref, o_ref, qc_ref, sa_ref, sb_ref, acc_ref, m_ref, l_ref, *,
                 lam_init):
    qi = pl.program_id(1)
    q = q_ref[...]
    lane = lax.broadcasted_iota(jnp.int32, (TQ, DV_A), 1)
    zero = jnp.zeros((TQ, DV_A), BF16)
    for h in range(H_A):
        qh = q[:, h * DV_A:(h + 1) * DV_A]
        qc_ref[h, 0] = jnp.where(lane < DK_A, qh, zero)
        qc_ref[h, 1] = jnp.where(lane >= DK_A, qh, zero)

    acc_ref[...] = jnp.zeros_like(acc_ref)
    m_ref[...] = jnp.full_like(m_ref, -jnp.inf)
    l_ref[...] = jnp.zeros_like(l_ref)

    kchunk = lax.broadcasted_iota(jnp.int32, (TK, TQ), 0) // CHUNK
    qchunk = lax.broadcasted_iota(jnp.int32, (TK, TQ), 1) // CHUNK

    def scores(j, s_ref):
        start = pl.multiple_of(j * TK, TK)
        kb = k_ref[pl.ds(start, TK), :]
        for h in range(H_A):
            kh = kb[:, h * DV_A:(h + 1) * DV_A]
            for c in range(2):
                s_ref[h, c] = lax.dot_general(kh, qc_ref[h, c], (((1,), (1,)), ((), ())),
                                              preferred_element_type=F32)

    def consume(j, s_ref, mask):
        start = pl.multiple_of(j * TK, TK)
        vb = v_ref[pl.ds(start, TK), :]
        for h in range(H_A):
            vh = vb[:, h * DV_A:(h + 1) * DV_A]
            for c in range(2):
                s = s_ref[h, c]
                if mask is not None:
                    s = jnp.where(mask, s, MASK_VALUE)
                m_old = m_ref[h, c]
                m_new = jnp.maximum(m_old, jnp.max(s, axis=0, keepdims=True))
                alpha = jnp.exp2(m_old - m_new)
                p = jnp.exp2(s - m_new)
                l_ref[h, c] = alpha * l_ref[h, c] + jnp.sum(p, axis=0, keepdims=True)
                pv = lax.dot_general(vh, p.astype(BF16), (((0,), (0,)), ((), ())),
                                     preferred_element_type=F32)
                acc_ref[h, c] = alpha * acc_ref[h, c] + pv
                m_ref[h, c] = m_new

    first_diag = KV_PER_Q * qi
    scores(0, sa_ref)

    def body(jj, carry):
        j = 2 * jj
        scores(j + 1, sb_ref)
        consume(j, sa_ref, None)
        scores(j + 2, sa_ref)
        consume(j + 1, sb_ref, None)
        return carry

    lax.fori_loop(0, first_diag // 2, body, 0)
    for d in range(KV_PER_Q):
        cur, nxt = (sa_ref, sb_ref) if d % 2 == 0 else (sb_ref, sa_ref)
        if d + 1 < KV_PER_Q:
            scores(first_diag + d + 1, nxt)
        consume(first_diag + d, cur, (kchunk + d * (TK // CHUNK)) <= qchunk)

    lam = lam_ref[0]
    for h in range(H_A):
        o = acc_ref[h, 0] / l_ref[h, 0] - lam * (acc_ref[h, 1] / l_ref[h, 1])
        ms = jnp.mean(o * o, axis=0, keepdims=True)
        y = o * lax.rsqrt(ms + EPS) * g_ref[...] * (1.0 - lam_init)
        o_ref[:, h * DV_A:(h + 1) * DV_A] = y.T.astype(BF16)


def _attention(qk, v, lam, subln_col, batch, seq, lam_init):
    assert TQ == KV_PER_Q * TK and KV_PER_Q % 2 == 0 and TK % CHUNK == 0 and seq % TQ == 0
    nq = seq // TQ
    t_tok = batch * seq
    kernel = functools.partial(_attn_kernel, lam_init=lam_init)
    return pl.pallas_call(
        kernel,
        grid=(batch, nq),
        in_specs=[
            pl.BlockSpec(memory_space=pltpu.SMEM),
            pl.BlockSpec((TQ, W_A), lambda b, i: (b * nq + i, 0)),
            pl.BlockSpec((seq, W_A), lambda b, i: (b, 1)),
            pl.BlockSpec((seq, W_A), lambda b, i: (b, 0)),
            pl.BlockSpec((DV_A, 1), lambda b, i: (0, 0)),
        ],
        out_specs=pl.BlockSpec((TQ, W_A), lambda b, i: (b * nq + i, 0)),
        out_shape=jax.ShapeDtypeStruct((t_tok, W_A), BF16),
        scratch_shapes=[
            pltpu.VMEM((H_A, 2, TQ, DV_A), BF16),
            pltpu.VMEM((H_A, 2, TK, TQ), F32),
            pltpu.VMEM((H_A, 2, TK, TQ), F32),
            pltpu.VMEM((H_A, 2, DV_A, TQ), F32),
            pltpu.VMEM((H_A, 2, 1, TQ), F32),
            pltpu.VMEM((H_A, 2, 1, TQ), F32),
        ],
        compiler_params=_cparams(("parallel", "arbitrary")),
        name="diff_attn",
    )(lam, qk, qk, v, subln_col)


_LEVELS = (32, 16, 8, 4, 2, 1)


def _ref_rows(g, h):
    n = g.shape[1]
    if 2 * h >= SUBLANES:
        pieces = []
        for m in range(CHUNK // (2 * h)):
            r = 2 * h * m + h - 1
            pieces.append(jnp.broadcast_to(g[r:r + 1, :], (2 * h, n)))
        return pieces[0] if len(pieces) == 1 else jnp.concatenate(pieces, axis=0)
    g3 = g.reshape(CHUNK // SUBLANES, SUBLANES, n)
    sub = lax.broadcasted_iota(jnp.int32, g3.shape, 1)
    out = None
    for m in reversed(range(SUBLANES // (2 * h))):
        r = 2 * h * m + h - 1
        piece = jnp.broadcast_to(g3[:, r:r + 1, :], g3.shape)
        out = piece if out is None else jnp.where(sub < 2 * h * (m + 1), piece, out)
    return out.reshape(CHUNK, n)


def _stack_heads(a, lane_head):
    zero = jnp.zeros_like(a)
    return jnp.concatenate([jnp.where(lane_head == hh, a, zero) for hh in range(H_B)], axis=0)


def _hgrn_kernel(hb_ref, f_ref, c_ref, lb_ref, ng_ref, cw_ref, gavg_ref, o_ref, state_ref, hbuf_ref, *, seq):
    n_chunks = seq // CHUNK
    nb = hb_ref.shape[0] // seq
    lb = lb_ref[...]
    ng = ng_ref[...]
    gavg = gavg_ref[...]

    row = lax.broadcasted_iota(jnp.int32, (CHUNK, W_B), 0)
    lane = lax.broadcasted_iota(jnp.int32, (CHUNK, W_B), 1)
    lane_head = lane // DK_B
    lane_s = lane % CHUNK
    tri = (lax.broadcasted_iota(jnp.int32, (CHUNK, CHUNK), 1)
           <= lax.broadcasted_iota(jnp.int32, (CHUNK, CHUNK), 0)).astype(BF16)
    lvl_mask = [((row // (2 * h)) == (lane_s // (2 * h))) & ((row % (2 * h)) >= h) & ((lane_s % (2 * h)) < h)
                for h in _LEVELS]
    eye_mask = row == lane_s
    r4 = lax.broadcasted_iota(jnp.int32, (H_B * DK_B, W_B), 0) // DK_B
    c4 = lax.broadcasted_iota(jnp.int32, (H_B * DK_B, W_B), 1) // DK_B
    bd_mask = r4 == c4

    state_ref[...] = jnp.zeros_like(state_ref)
    nt = (((1,), (1,)), ((), ()))
    tn = (((0,), (0,)), ((), ()))
    each = range(nb)

    def chunk(i, carry):
        rows = [pl.ds(pl.multiple_of(j * seq + i * CHUNK, CHUNK), CHUNK) for j in each]
        hb = [hb_ref[rows[j], :].astype(F32) for j in each]
        zq = [t[:, 0:W_B] for t in hb]
        v16 = [t[:, W_B:2 * W_B].astype(BF16) for t in hb]
        zg = [t[:, 2 * W_B:3 * W_B] for t in hb]
        zf = [f_ref[rows[j], :] for j in each]

        logf = [jnp.log(jnp.maximum(lb + (1.0 - lb) * jax.nn.sigmoid(z), TINY)) for z in zf]
        key = [(1.0 - lb) * jax.nn.sigmoid(-z) for z in zf]
        q = [z * jax.nn.sigmoid(z) for z in zq]

        g = []
        for x in logf:
            hi = x.astype(BF16)
            r1 = x - hi.astype(F32)
            mid = r1.astype(BF16)
            lo = (r1 - mid.astype(F32)).astype(BF16)
            g.append(jnp.dot(tri, hi, preferred_element_type=F32) + jnp.dot(tri, mid, preferred_element_type=F32)
                     + jnp.dot(tri, lo, preferred_element_type=F32))

        a_all = [jnp.where(eye_mask, lax.dot_general(q[j].astype(BF16), _stack_heads(key[j].astype(BF16), lane_head),
                                                     nt, preferred_element_type=F32), 0.0) for j in each]
        for h, mask in zip(_LEVELS, lvl_mask):
            w = [jnp.exp(-jnp.abs(g[j] - _ref_rows(g[j], h))) for j in each]
            ah = [lax.dot_general((q[j] * w[j]).astype(BF16), _stack_heads((key[j] * w[j]).astype(BF16), lane_head),
                                  nt, preferred_element_type=F32) for j in each]
            a_all = [a_all[j] + jnp.where(mask, ah[j], 0.0) for j in each]
        o = [jnp.dot(a_all[j].astype(BF16), _stack_heads(v16[j], lane_head), preferred_element_type=F32) for j in each]

        st = [state_ref[j] for j in each]
        o = [o[j] + lax.dot_general((q[j] * jnp.exp(g[j])).astype(BF16), st[j].astype(BF16), nt,
                                    preferred_element_type=F32) for j in each]
        g_last = [t[CHUNK - 1:CHUNK, :] for t in g]
        upd = [lax.dot_general(v16[j], (key[j] * jnp.exp(g_last[j] - g[j])).astype(BF16), tn,
                               preferred_element_type=F32) for j in each]
        for j in each:
            state_ref[j] = jnp.exp(g_last[j]) * st[j] + jnp.where(bd_mask, upd[j], 0.0)

        ms = [jnp.dot((t * t).astype(BF16), gavg, preferred_element_type=F32) for t in o]
        for j in each:
            y = o[j] * lax.rsqrt(ms[j] + EPS) * ng * (zg[j] * jax.nn.sigmoid(zg[j]))
            o_ref[rows[j], 0:W_B] = y.astype(BF16)
        return carry

    lax.fori_loop(0, n_chunks, chunk, 0)

    piece = min(512, seq)
    hbuf_ref[0:SUBLANES, :] = jnp.zeros((SUBLANES, C_WIDTH), F32)
    for j in each:
        for p in range(seq // piece):
            cblk = c_ref[j * seq + p * piece:j * seq + (p + 1) * piece, :].astype(F32)
            hbuf_ref[SUBLANES + p * piece:SUBLANES + (p + 1) * piece, :] = (cblk[:, C_WIDTH:2 * C_WIDTH]
                                                                             * cblk[:, 2 * C_WIDTH:])
        for p in range(seq // piece):
            base = SUBLANES + p * piece
            y = (cw_ref[0:1, :] * hbuf_ref[base - 2:base - 2 + piece, :]
                 + cw_ref[1:2, :] * hbuf_ref[base - 1:base - 1 + piece, :]
                 + cw_ref[2:3, :] * hbuf_ref[base:base + piece, :])
            lo_r = j * seq + p * piece
            bgate = c_ref[lo_r:lo_r + piece, 0:C_WIDTH].astype(F32)
            o_ref[lo_r:lo_r + piece, W_B:W_B + C_WIDTH] = (bgate * y).astype(BF16)


def _hgrn_conv(hb, f, cv, lb, ng, cw, gavg, batch, seq):
    t_tok = batch * seq
    nb = HGRN_BATCH if batch % HGRN_BATCH == 0 else 1
    fixed = lambda b: (0, 0)
    return pl.pallas_call(
        functools.partial(_hgrn_kernel, seq=seq),
        grid=(batch // nb,),
        in_specs=[
            pl.BlockSpec((nb * seq, 3 * W_B), lambda b: (b, 0)),
            pl.BlockSpec((nb * seq, W_B), lambda b: (b, 0)),
            pl.BlockSpec((nb * seq, 3 * C_WIDTH), lambda b: (b, 0)),
            pl.BlockSpec((1, W_B), fixed),
            pl.BlockSpec((1, W_B), fixed),
            pl.BlockSpec((CONV_WIDTH, C_WIDTH), fixed),
            pl.BlockSpec((W_B, W_B), fixed),
        ],
        out_specs=pl.BlockSpec((nb * seq, W_B + C_WIDTH), lambda b: (b, 0)),
        out_shape=jax.ShapeDtypeStruct((t_tok, W_B + C_WIDTH), BF16),
        scratch_shapes=[
            pltpu.VMEM((nb, H_B * DV_B, W_B), F32),
            pltpu.VMEM((seq + SUBLANES, C_WIDTH), F32),
        ],
        compiler_params=_cparams(("parallel",)),
        name="hgrn_conv",
    )(hb, f, cv, lb, ng, cw, gavg)


def _outproj_router_kernel(oa_ref, obc_ref, x_ref, wo_ref, g_ref, wr_ref, br_ref,
                           xo_ref, route_ref, tab_ref, cnt_ref, carry_ref):
    i = pl.program_id(0)

    @pl.when(i == 0)
    def _():
        carry_ref[...] = jnp.zeros_like(carry_ref)

    mixed = (jnp.dot(oa_ref[...], wo_ref[0:W_A, :], preferred_element_type=F32)
             + jnp.dot(obc_ref[...], wo_ref[W_A:, :], preferred_element_type=F32))
    xn = x_ref[...] + mixed
    xo_ref[...] = xn

    ms = jnp.mean(xn * xn, axis=-1, keepdims=True)
    hn = (xn * lax.rsqrt(ms + EPS) * g_ref[...]).astype(BF16)
    logits = jnp.dot(hn, wr_ref[...], preferred_element_type=F32) + br_ref[...]

    tm = logits.shape[0]
    lane = lax.broadcasted_iota(jnp.int32, (tm, ROUTE_LANES), 1)
    neg = -jnp.inf
    big = ROUTE_LANES

    def top(mask):
        val = jnp.max(jnp.where(mask, logits, neg), axis=-1, keepdims=True)
        idx = jnp.min(jnp.where(mask & (logits == val), lane, big), axis=-1, keepdims=True)
        return val, idx

    gmask = lane < N_GROUPS
    gmax, gidx = top(gmask)
    p_g = 1.0 / jnp.sum(jnp.where(gmask, jnp.exp(logits - gmax), 0.0), axis=-1, keepdims=True)
    e_lo = N_GROUPS + EXPERTS_PER_GROUP * gidx
    emask = (lane >= e_lo) & (lane < e_lo + EXPERTS_PER_GROUP)
    v1, i1 = top(emask)
    v2, i2 = top(emask & (lane != i1))
    t2 = jnp.exp(v2 - v1)
    w1 = p_g / (1.0 + t2)
    w2 = p_g * t2 / (1.0 + t2)

    sel1 = lane == i1
    sel2 = lane == i2
    onehot = jnp.where(sel1, 1.0, jnp.where(sel2, 1.0, 0.0))
    strict = (lax.broadcasted_iota(jnp.int32, (tm, tm), 1)
              < lax.broadcasted_iota(jnp.int32, (tm, tm), 0)).astype(BF16)
    local = jnp.dot(strict, onehot.astype(BF16), preferred_element_type=F32)
    cnt = jnp.sum(onehot, axis=0, keepdims=True)
    wrows = jnp.floor((cnt + (MOE_WINDOW - 1)) * (1.0 / MOE_WINDOW)) * MOE_WINDOW
    upper = (lax.broadcasted_iota(jnp.int32, (ROUTE_LANES, ROUTE_LANES), 0)
             < lax.broadcasted_iota(jnp.int32, (ROUTE_LANES, ROUTE_LANES), 1)).astype(BF16)
    soff = jnp.dot(jnp.broadcast_to(wrows, (SUBLANES, ROUTE_LANES)).astype(BF16), upper,
                   preferred_element_type=F32)[0:1, :]
    slot = local + soff
    row1 = jnp.sum(jnp.where(sel1, slot, 0.0), axis=-1, keepdims=True)
    row2 = jnp.sum(jnp.where(sel2, slot, 0.0), axis=-1, keepdims=True)

    rec = jnp.where(lane == 0, (i1 - N_GROUPS).astype(F32),
          jnp.where(lane == 1, (i2 - N_GROUPS).astype(F32),
          jnp.where(lane == 2, w1,
          jnp.where(lane == 3, w2,
          jnp.where(lane == 4, row1,
          jnp.where(lane == 5, row2, 0.0))))))
    route_ref[...] = rec
    tab_ref[0] = jnp.zeros((SUBLANES, ROUTE_LANES), F32)
    tab_ref[0, 0:1, :] = cnt
    tab_ref[0, 1:2, :] = carry_ref[0:1, :]
    carry_ref[0:1, :] = carry_ref[0:1, :] + jnp.floor((cnt + (SUBLANES - 1)) * (1.0 / SUBLANES)) * SUBLANES
    cnt_ref[...] = carry_ref[...]


def _outproj_router(o_a, o_bc, x2, wo, g, wr, br):
    t_tok = x2.shape[0]
    tm = min(TM_ROUTE, t_tok)
    row = lambda i: (i, 0)
    fixed = lambda i: (0, 0)
    return pl.pallas_call(
        _outproj_router_kernel,
        grid=(t_tok // tm,),
        in_specs=[
            pl.BlockSpec((tm, W_A), row),
            pl.BlockSpec((tm, W_B + C_WIDTH), row),
            pl.BlockSpec((tm, D_MODEL), row),
            pl.BlockSpec(wo.shape, fixed),
            pl.BlockSpec((1, D_MODEL), fixed),
            pl.BlockSpec(wr.shape, fixed),
            pl.BlockSpec((1, ROUTE_LANES), fixed),
        ],
        out_specs=[
            pl.BlockSpec((tm, D_MODEL), row),
            pl.BlockSpec((tm, ROUTE_LANES), row),
            pl.BlockSpec((1, SUBLANES, ROUTE_LANES), lambda i: (i, 0, 0)),
            pl.BlockSpec((SUBLANES, ROUTE_LANES), fixed),
        ],
        out_shape=[
            jax.ShapeDtypeStruct((t_tok, D_MODEL), F32),
            jax.ShapeDtypeStruct((t_tok, ROUTE_LANES), F32),
            jax.ShapeDtypeStruct((t_tok // tm, SUBLANES, ROUTE_LANES), F32),
            jax.ShapeDtypeStruct((SUBLANES, ROUTE_LANES), F32),
        ],
        scratch_shapes=[pltpu.VMEM((SUBLANES, ROUTE_LANES), F32)],
        compiler_params=_cparams(("arbitrary",)),
        name="outproj_router",
    )(o_a, o_bc, x2, wo, g, wr, br)


def _start_copies(n, make_copy):
    def start(k, carry):
        make_copy(k).start()
        return carry

    lax.fori_loop(0, n, start, 0)


def _wait_copies(n, make_copy):
    def wait(k, carry):
        make_copy(k).wait()
        return carry

    lax.fori_loop(0, n, wait, 0)


def _window_copy_loops(n, make_copy):
    _start_copies(n, make_copy)
    _wait_copies(n, make_copy)


def _dispatch_kernel(pend_ref, tab_ref, tabp_ref, route_ref, x_ref, buf_ref, xp_ref, xq_ref, sem, semq):
    tm = x_ref.shape[0]
    r_st = xp_ref.shape[0]
    step = pl.program_id(0)

    @pl.when(pl.program_id(0) == 0)
    def _():
        xp_ref[...] = jnp.zeros_like(xp_ref)
        copies = []
        for e in range(N_EXPERTS):
            end = pend_ref[e]
            copies.append((None, pltpu.make_async_copy(
                xp_ref.at[0:MOE_BLOCK, :],
                buf_ref.at[pl.ds(pl.multiple_of(end - MOE_BLOCK, SUBLANES), MOE_BLOCK), :], sem)))
            copies.append((end >= MOE_BLOCK + MOE_WINDOW, pltpu.make_async_copy(
                xp_ref.at[0:MOE_WINDOW, :],
                buf_ref.at[pl.ds(pl.multiple_of(jnp.maximum(end - MOE_BLOCK - MOE_WINDOW, 0), SUBLANES), MOE_WINDOW), :],
                sem)))
        for cond, cp in copies:
            if cond is None:
                cp.start()
            else:
                pl.when(cond)(cp.start)
        for cond, cp in copies:
            if cond is None:
                cp.wait()
            else:
                pl.when(cond)(cp.wait)

        def tail_copy(k):
            return pltpu.make_async_copy(
                xp_ref.at[0:MOE_BLOCK, :],
                buf_ref.at[pl.ds(pl.multiple_of(k * MOE_BLOCK, MOE_BLOCK), MOE_BLOCK), :], sem)

        first_unused = pend_ref[N_EXPERTS - 1] // MOE_BLOCK
        n_unused = buf_ref.shape[0] // MOE_BLOCK - first_unused
        _window_copy_loops(n_unused, lambda k: tail_copy(first_unused + k))

    route = route_ref[...]
    r0 = route[:, 4:5].astype(jnp.int32)
    r1 = route[:, 5:6].astype(jnp.int32)
    col = lax.broadcasted_iota(jnp.int32, (tm, r_st), 1)
    sel = jnp.where(col == r0, 1.0, jnp.where(col == r1, 1.0, 0.0)).astype(BF16)

    def window(tabs, stage_ref, s):
        def make_copy(k):
            src = pl.multiple_of(k * MOE_WINDOW, MOE_WINDOW)
            return pltpu.make_async_copy(
                stage_ref.at[pl.ds(src, MOE_WINDOW), :],
                buf_ref.at[pl.ds(pl.multiple_of(tabs[0, 0, k], SUBLANES), MOE_WINDOW), :], s)
        return tabs[0, 0, MAX_WINDOWS], make_copy

    def run(cur_ref, cur_sem, prev_ref, prev_sem):
        cur_ref[...] = _pack_rows(lax.dot_general(sel, x_ref[...].astype(BF16), (((0,), (0,)), ((), ())),
                                                  preferred_element_type=F32))

        @pl.when(step > 0)
        def _():
            _wait_copies(*window(tabp_ref, prev_ref, prev_sem))

        _start_copies(*window(tab_ref, cur_ref, cur_sem))

        @pl.when(step == pl.num_programs(0) - 1)
        def _():
            _wait_copies(*window(tab_ref, cur_ref, cur_sem))

    @pl.when(step % 2 == 0)
    def _():
        run(xp_ref, sem, xq_ref, semq)

    @pl.when(step % 2 == 1)
    def _():
        run(xq_ref, semq, xp_ref, sem)


def _dispatch(pad_end, tabs, route, x2, cap):
    t_tok = x2.shape[0]
    tm = min(TM_ROUTE, t_tok)
    n_tiles = t_tok // tm
    return pl.pallas_call(
        _dispatch_kernel,
        grid=(n_tiles,),
        in_specs=[
            pl.BlockSpec(memory_space=pltpu.SMEM),
            pl.BlockSpec((1, 1, ROUTE_LANES), lambda i: (i, 0, 0), memory_space=pltpu.SMEM),
            pl.BlockSpec((1, 1, ROUTE_LANES), lambda i: (jnp.maximum(i - 1, 0), 0, 0), memory_space=pltpu.SMEM),
            pl.BlockSpec((tm, ROUTE_LANES), lambda i: (i, 0)),
            pl.BlockSpec((tm, D_MODEL), lambda i: (i, 0)),
        ],
        out_specs=pl.BlockSpec(memory_space=pl.ANY),
        out_shape=jax.ShapeDtypeStruct((cap, D_PACKED), jnp.int32),
        scratch_shapes=[pltpu.VMEM((_staging_rows(tm), D_PACKED), jnp.int32),
                        pltpu.VMEM((_staging_rows(tm), D_PACKED), jnp.int32),
                        pltpu.SemaphoreType.DMA(()), pltpu.SemaphoreType.DMA(())],
        compiler_params=_cparams(("arbitrary",)),
        name="moe_dispatch",
    )(pad_end, tabs, tabs, route, x2)


def _experts_kernel(blk_e_ref, n_used_ref, x_ref, g_ref, wg_ref, wu_ref, wd_ref, y_ref):
    del blk_e_ref
    i = pl.program_id(0)

    @pl.when(i < n_used_ref[0])
    def _():
        wg = wg_ref[0, 0].astype(BF16)
        wu = wu_ref[0, 0].astype(BF16)
        wd = wd_ref[0, 0].astype(BF16)
        parts = _row_parts(x_ref.shape[0])
        hn = []
        for rows in parts:
            x = _unpack_rows(x_ref[rows, :])
            ms = jnp.mean(x * x, axis=-1, keepdims=True)
            hn.append((x * lax.rsqrt(ms + EPS) * g_ref[...]).astype(BF16))
        a = [jnp.dot(t, wg, preferred_element_type=F32) for t in hn]
        u = [jnp.dot(t, wu, preferred_element_type=F32) for t in hn]
        act = [(a[r] * jax.nn.sigmoid(a[r]) * u[r]).astype(BF16) for r in range(len(parts))]
        for r, rows in enumerate(parts):
            y = jnp.dot(act[r], wd, preferred_element_type=F32)
            y_ref[rows, :] = _pack_rows(y.astype(BF16).astype(F32))

    @pl.when(i >= n_used_ref[0])
    def _():
        y_ref[...] = jnp.zeros_like(y_ref)


def _experts(blk_e, n_used, buf, g, wg, wu, wd, layer):
    n_blocks = buf.shape[0] // MOE_BLOCK
    grid_spec = pltpu.PrefetchScalarGridSpec(
        num_scalar_prefetch=2,
        grid=(n_blocks,),
        in_specs=[
            pl.BlockSpec((MOE_BLOCK, D_PACKED), lambda i, be, nu: (i, 0)),
            pl.BlockSpec((1, D_MODEL), lambda i, be, nu: (0, 0)),
            pl.BlockSpec((1, 1, D_MODEL, D_EXPERT), lambda i, be, nu: (layer, be[i], 0, 0)),
            pl.BlockSpec((1, 1, D_MODEL, D_EXPERT), lambda i, be, nu: (layer, be[i], 0, 0)),
            pl.BlockSpec((1, 1, D_EXPERT, D_MODEL), lambda i, be, nu: (layer, be[i], 0, 0)),
        ],
        out_specs=pl.BlockSpec((MOE_BLOCK, D_PACKED), lambda i, be, nu: (i, 0)),
    )
    return pl.pallas_call(
        _experts_kernel,
        grid_spec=grid_spec,
        out_shape=jax.ShapeDtypeStruct(buf.shape, jnp.int32),
        compiler_params=_cparams(("arbitrary",)),
        name="moe_experts",
    )(blk_e, n_used, buf, g, wg, wu, wd)


def _combine_kernel(tab_ref, tabn_ref, route_ref, x_ref, y_ref, o_ref, ya_ref, yb_ref, sema, semb):
    tm = x_ref.shape[0]
    r_st = ya_ref.shape[0]
    step = pl.program_id(0)

    def window(tabs, stage_ref, s):
        def make_copy(k):
            dst = pl.multiple_of(k * MOE_WINDOW, MOE_WINDOW)
            return pltpu.make_async_copy(y_ref.at[pl.ds(pl.multiple_of(tabs[0, 0, k], SUBLANES), MOE_WINDOW), :],
                                         stage_ref.at[pl.ds(dst, MOE_WINDOW), :], s)
        return tabs[0, 0, MAX_WINDOWS], make_copy

    @pl.when(step == 0)
    def _():
        ya_ref[...] = jnp.zeros_like(ya_ref)
        yb_ref[...] = jnp.zeros_like(yb_ref)
        _start_copies(*window(tab_ref, ya_ref, sema))

    route = route_ref[...]
    r0 = route[:, 4:5].astype(jnp.int32)
    r1 = route[:, 5:6].astype(jnp.int32)
    col = lax.broadcasted_iota(jnp.int32, (tm, r_st), 1)
    gate = jnp.where(col == r0, route[:, 2:3], jnp.where(col == r1, route[:, 3:4], 0.0)).astype(BF16)

    def run(cur_ref, cur_sem, nxt_ref, nxt_sem):
        @pl.when(step + 1 < pl.num_programs(0))
        def _():
            _start_copies(*window(tabn_ref, nxt_ref, nxt_sem))

        _wait_copies(*window(tab_ref, cur_ref, cur_sem))
        o_ref[...] = x_ref[...] + jnp.dot(gate, _unpack_rows(cur_ref[...]).astype(BF16), preferred_element_type=F32)

    @pl.when(step % 2 == 0)
    def _():
        run(ya_ref, sema, yb_ref, semb)

    @pl.when(step % 2 == 1)
    def _():
        run(yb_ref, semb, ya_ref, sema)


def _combine(tabs, route, x2, y_buf):
    t_tok = x2.shape[0]
    tm = min(TM_ROUTE, t_tok)
    n_tiles = t_tok // tm
    return pl.pallas_call(
        _combine_kernel,
        grid=(n_tiles,),
        in_specs=[
            pl.BlockSpec((1, 1, ROUTE_LANES), lambda i: (i, 0, 0), memory_space=pltpu.SMEM),
            pl.BlockSpec((1, 1, ROUTE_LANES), lambda i: (jnp.minimum(i + 1, n_tiles - 1), 0, 0),
                         memory_space=pltpu.SMEM),
            pl.BlockSpec((tm, ROUTE_LANES), lambda i: (i, 0)),
            pl.BlockSpec((tm, D_MODEL), lambda i: (i, 0)),
            pl.BlockSpec(memory_space=pl.ANY),
        ],
        out_specs=pl.BlockSpec((tm, D_MODEL), lambda i: (i, 0)),
        out_shape=jax.ShapeDtypeStruct((t_tok, D_MODEL), F32),
        scratch_shapes=[pltpu.VMEM((_staging_rows(tm), D_PACKED), jnp.int32),
                        pltpu.VMEM((_staging_rows(tm), D_PACKED), jnp.int32),
                        pltpu.SemaphoreType.DMA(()), pltpu.SemaphoreType.DMA(())],
        compiler_params=_cparams(("arbitrary",)),
        name="moe_combine",
    )(tabs, tabs, route, x2, y_buf)


def _rope_tables(positions):
    half = ROT_DIM // 2
    inv_freq = ROPE_THETA ** (-jnp.arange(0, ROT_DIM, 2, dtype=F32) / ROT_DIM)
    d = jnp.arange(LANES) % DK_A
    freq_lane = jnp.where(d < ROT_DIM, inv_freq[d % half], 0.0)
    ang = positions.reshape(-1, 1).astype(F32) * freq_lane[None, :]
    cos, sin = jnp.cos(ang), jnp.sin(ang)
    s1 = jnp.where(d < half, -sin, 0.0)
    s2 = jnp.where((d >= half) & (d < ROT_DIM), sin, 0.0)
    return cos, s1, s2


def _block_avg(width, group):
    idx = jnp.arange(width) // group
    return jnp.where(idx[:, None] == idx[None, :], 1.0 / group, 0.0).astype(BF16)


def kernel(x, positions, norm_mix_g, w_in, attn_q_norm_g, attn_k_norm_g, lambda_q1, lambda_k1, lambda_q2,
           lambda_k2, attn_subln_g, hgrn_lower_bounds, hgrn_norm_g, conv_w, w_out, norm_ffn_g, router_group_w,
           router_group_b, router_expert_w, router_expert_b, expert_w_gate, expert_w_up, expert_w_down):
    batch, seq, d = x.shape
    assert d == D_MODEL
    depth = w_in.shape[0]
    t_tok = batch * seq
    n_assign = 2 * t_tok
    n_route_tiles = t_tok // min(TM_ROUTE, t_tok)
    n_blocks = (n_assign + (SUBLANES - 1) * n_route_tiles * N_EXPERTS) // MOE_BLOCK + 2 * N_EXPERTS + 1
    cap = n_blocks * MOE_BLOCK

    lb_probs = jax.nn.softmax(hgrn_lower_bounds.astype(F32), axis=0)
    lower_bounds = jnp.cumsum(lb_probs, axis=0) - lb_probs[0:1]
    cos_t, s1_t, s2_t = _rope_tables(positions)
    gsum_a = _block_avg(W_A, DK_A)
    gavg_b = _block_avg(W_B, DV_B)

    o_q, o_k, o_v = 0, W_A, 2 * W_A
    o_bq = 3 * W_A
    o_bf, o_bi, o_bg = o_bq + W_B, o_bq + 2 * W_B, o_bq + 3 * W_B
    o_c = o_bq + 4 * W_B

    x2 = x.reshape(t_tok, d)
    for l in range(depth):
        w = w_in[l]
        wqk = w[:, o_q:o_v].astype(BF16)
        wv = w[:, o_v:o_bq].astype(BF16)
        whb = jnp.concatenate([w[:, o_bq:o_bf], w[:, o_bi:o_c]], axis=1).astype(BF16)
        wf = w[:, o_bf:o_bi].astype(BF16)
        wc = w[:, o_c:].astype(BF16)
        gqk = jnp.concatenate([jnp.tile(attn_q_norm_g[l], 2 * H_A) * (DK_A ** -0.5 * math.log2(math.e)),
                               jnp.tile(attn_k_norm_g[l], 2 * H_A)]).reshape(1, 2 * W_A).astype(F32)
        qk, v, hb, f, cv = _inproj(x2, norm_mix_g[l].reshape(1, d), wqk, wv, whb, wf, wc, gsum_a, gqk,
                                   cos_t, s1_t, s2_t)

        lam_init = 0.8 - 0.6 * math.exp(-0.3 * l)
        lam = (jnp.exp(jnp.sum(lambda_q1[l].astype(F32) * lambda_k1[l].astype(F32)))
               - jnp.exp(jnp.sum(lambda_q2[l].astype(F32) * lambda_k2[l].astype(F32))) + lam_init).reshape(1)
        o_a = _attention(qk, v, lam, attn_subln_g[l].reshape(DV_A, 1).astype(F32), batch, seq, lam_init)

        o_bc = _hgrn_conv(hb, f, cv, lower_bounds[l].reshape(1, W_B), jnp.tile(hgrn_norm_g[l], H_B).reshape(1, W_B),
                          conv_w[l], gavg_b, batch, seq)

        wr = jnp.concatenate([router_group_w[l], router_expert_w[l],
                              jnp.zeros((d, ROUTE_LANES - N_LOGITS), F32)], axis=1).astype(BF16)
        br = jnp.concatenate([router_group_b[l], router_expert_b[l],
                              jnp.zeros((ROUTE_LANES - N_LOGITS,), F32)]).reshape(1, ROUTE_LANES)
        x2, route, tab, cnt = _outproj_router(o_a, o_bc, x2, w_out[l].astype(BF16), norm_ffn_g[l].reshape(1, d), wr, br)

        counts = cnt[0, N_GROUPS:N_GROUPS + N_EXPERTS].astype(jnp.int32)
        padded = (counts + MOE_WINDOW + MOE_BLOCK - 1) // MOE_BLOCK * MOE_BLOCK
        pad_end = jnp.cumsum(padded)
        pad_start = pad_end - padded
        blk_pos = jnp.arange(n_blocks, dtype=jnp.int32) * MOE_BLOCK
        blk_e = jnp.minimum(jnp.sum((pad_end[None, :] <= blk_pos[:, None]).astype(jnp.int32), axis=1),
                            N_EXPERTS - 1)
        n_used = (pad_end[-1] // MOE_BLOCK).astype(jnp.int32).reshape(1)
        tile_cnt = tab[:, 0, N_GROUPS:N_GROUPS + N_EXPERTS].astype(jnp.int32)
        tile_base = tab[:, 1, N_GROUPS:N_GROUPS + N_EXPERTS].astype(jnp.int32) + pad_start[None, :]
        n_win = (tile_cnt + MOE_WINDOW - 1) // MOE_WINDOW
        w_end = jnp.cumsum(n_win, axis=1)
        w_start = w_end - n_win
        k_idx = jnp.arange(MAX_WINDOWS, dtype=jnp.int32)[None, :, None]
        owner = (w_start[:, None, :] <= k_idx) & (k_idx < w_end[:, None, :])
        win_dst = jnp.sum(jnp.where(owner, tile_base[:, None, :] + (k_idx - w_start[:, None, :]) * MOE_WINDOW, 0),
                          axis=-1)
        tabs = jnp.concatenate([win_dst, w_end[:, -1:],
                                jnp.zeros((win_dst.shape[0], ROUTE_LANES - MAX_WINDOWS - 1), jnp.int32)],
                               axis=1).reshape(-1, 1, ROUTE_LANES)

        buf = _dispatch(pad_end, tabs, route, x2, cap)
        y_buf = _experts(blk_e, n_used, buf, norm_ffn_g[l].reshape(1, d), expert_w_gate, expert_w_up,
                         expert_w_down, l)
        x2 = _combine(tabs, route, x2, y_buf)
    return x2.reshape(batch, seq, d)
```

```python
import functools
import math

import jax
import jax.numpy as jnp
from jax import lax
from jax.experimental import pallas as pl
from jax.experimental.pallas import tpu as pltpu

F32 = jnp.float32
BF16 = jnp.bfloat16

LANES = 128
SUBLANES = 8

D_MODEL = 1024
CHUNK = 64
EPS = 1e-6
MASK_VALUE = -1e30
TINY = 1e-30
LOG2_E = math.log2(math.e)
H_A = 4
DK_A = D_MODEL // 16
DV_A = 2 * DK_A
ROT_DIM = DK_A // 4
ROPE_THETA = 500000.0
H_B = 4
DK_B = D_MODEL // 16
DV_B = D_MODEL // 16
C_WIDTH = D_MODEL // 4
CONV_WIDTH = 3
N_GROUPS = 4
EXPERTS_PER_GROUP = 8
N_EXPERTS = N_GROUPS * EXPERTS_PER_GROUP
D_EXPERT = D_MODEL // 2
D_PACKED = D_MODEL // 2

W_A = H_A * DV_A
W_B = H_B * DK_B
ROUTE_LANES = LANES
N_LOGITS = N_GROUPS + N_EXPERTS

TM_PROJ = 512
TK = 256
KV_PER_Q = 2
TQ = KV_PER_Q * TK
TM_ROUTE = 512
MOE_BLOCK = 512
HGRN_BATCH = 2
HGRN_UNROLL = 2
MOE_WINDOW = 16
MAX_WINDOWS = 2 * TM_ROUTE // MOE_WINDOW + N_EXPERTS
VMEM_LIMIT = 56 * 1024 * 1024


def _staging_rows(tm):
    return 2 * tm + N_EXPERTS * MOE_WINDOW


def _row_parts(rows, n=2):
    step = rows // n
    return [slice(r * step, (r + 1) * step) for r in range(n)]


def _pack_rows(x):
    half = x.shape[1] // 2
    lo = lax.bitcast_convert_type(x[:, :half], jnp.int32)
    hi = lax.bitcast_convert_type(x[:, half:], jnp.int32)
    return hi | lax.shift_right_logical(lo, 16)


def _unpack_rows(p):
    lo = lax.bitcast_convert_type(lax.shift_left(p, 16), F32)
    hi = lax.bitcast_convert_type(p & jnp.int32(-65536), F32)
    return jnp.concatenate([lo, hi], axis=1)


def _cparams(sem, vmem=VMEM_LIMIT):
    return pltpu.CompilerParams(dimension_semantics=sem, vmem_limit_bytes=vmem)


def _inproj_kernel(x_ref, g_ref, wqk_ref, wv_ref, whb_ref, wf_ref, wc_ref, gsum_ref, gqk_ref,
                   cos_ref, s1_ref, s2_ref, qk_out, v_out, hb_out, f_out, c_out):
    parts = _row_parts(x_ref.shape[0])
    xn = []
    for rows in parts:
        x = x_ref[rows, :]
        ms = jnp.mean(x * x, axis=-1, keepdims=True)
        xn.append((x * lax.rsqrt(ms + EPS) * g_ref[...]).astype(BF16))

    gsum = gsum_ref[...]
    for part in range(2):
        lo = part * W_A
        t = [jnp.dot(xr, wqk_ref[:, lo:lo + W_A], preferred_element_type=F32) for xr in xn]
        msq = [jnp.dot((tr * tr).astype(BF16), gsum, preferred_element_type=F32) for tr in t]
        for r, rows in enumerate(parts):
            tn = t[r] * lax.rsqrt(msq[r] + EPS) * gqk_ref[:, lo:lo + W_A]
            cos = cos_ref[rows, :]
            s1 = s1_ref[rows, :]
            s2 = s2_ref[rows, :]
            for h in range(H_A):
                slab = tn[:, h * LANES:(h + 1) * LANES]
                up = pltpu.roll(slab, LANES - ROT_DIM // 2, axis=1)
                dn = pltpu.roll(slab, ROT_DIM // 2, axis=1)
                qk_out[rows, lo + h * LANES:lo + (h + 1) * LANES] = (slab * cos + up * s1 + dn * s2).astype(BF16)

    for r, rows in enumerate(parts):
        v_out[rows, :] = jnp.dot(xn[r], wv_ref[...], preferred_element_type=F32).astype(BF16)
    for r, rows in enumerate(parts):
        hb_out[rows, :] = jnp.dot(xn[r], whb_ref[...], preferred_element_type=F32).astype(BF16)
    for r, rows in enumerate(parts):
        f_out[rows, :] = jnp.dot(xn[r], wf_ref[...], preferred_element_type=F32)
    for r, rows in enumerate(parts):
        c_out[rows, :] = jnp.dot(xn[r], wc_ref[...], preferred_element_type=F32).astype(BF16)


def _inproj(x2, g, wqk, wv, whb, wf, wc, gsum, gqk, cos_t, s1_t, s2_t):
    t_tok = x2.shape[0]
    tm = min(TM_PROJ, t_tok)
    grid = (t_tok // tm,)
    row = lambda i: (i, 0)
    fixed = lambda i: (0, 0)
    return pl.pallas_call(
        _inproj_kernel,
        grid=grid,
        in_specs=[
            pl.BlockSpec((tm, D_MODEL), row),
            pl.BlockSpec((1, D_MODEL), fixed),
            pl.BlockSpec(wqk.shape, fixed),
            pl.BlockSpec(wv.shape, fixed),
            pl.BlockSpec(whb.shape, fixed),
            pl.BlockSpec(wf.shape, fixed),
            pl.BlockSpec(wc.shape, fixed),
            pl.BlockSpec(gsum.shape, fixed),
            pl.BlockSpec(gqk.shape, fixed),
            pl.BlockSpec((tm, LANES), row),
            pl.BlockSpec((tm, LANES), row),
            pl.BlockSpec((tm, LANES), row),
        ],
        out_specs=[
            pl.BlockSpec((tm, 2 * W_A), row),
            pl.BlockSpec((tm, W_A), row),
            pl.BlockSpec((tm, 3 * W_B), row),
            pl.BlockSpec((tm, W_B), row),
            pl.BlockSpec((tm, 3 * C_WIDTH), row),
        ],
        out_shape=[
            jax.ShapeDtypeStruct((t_tok, 2 * W_A), BF16),
            jax.ShapeDtypeStruct((t_tok, W_A), BF16),
            jax.ShapeDtypeStruct((t_tok, 3 * W_B), BF16),
            jax.ShapeDtypeStruct((t_tok, W_B), F32),
            jax.ShapeDtypeStruct((t_tok, 3 * C_WIDTH), BF16),
        ],
        compiler_params=_cparams(("parallel",)),
        name="inproj",
    )(x2, g, wqk, wv, whb, wf, wc, gsum, gqk, cos_t, s1_t, s2_t)


def _attn_kernel(lam_ref, q_ref, k_ref, v_ref, g_ref, o_ref, qc_ref, sa_ref, sb_ref, acc_ref, m_ref, l_ref, *,
                 lam_init):
    qi = pl.program_id(1)
    q = q_ref[...]
    lane = lax.broadcasted_iota(jnp.int32, (TQ, DV_A), 1)
    zero = jnp.zeros((TQ, DV_A), BF16)
    for h in range(H_A):
        qh = q[:, h * DV_A:(h + 1) * DV_A]
        qc_ref[h, 0] = jnp.where(lane < DK_A, qh, zero)
        qc_ref[h, 1] = jnp.where(lane >= DK_A, qh, zero)

    acc_ref[...] = jnp.zeros_like(acc_ref)
    m_ref[...] = jnp.full_like(m_ref, -jnp.inf)
    l_ref[...] = jnp.zeros_like(l_ref)

    kchunk = lax.broadcasted_iota(jnp.int32, (TK, TQ), 0) // CHUNK
    qchunk = lax.broadcasted_iota(jnp.int32, (TK, TQ), 1) // CHUNK

    def scores(j, s_ref, q0=0):
        start = pl.multiple_of(j * TK, TK)
        kb = k_ref[pl.ds(start, TK), :]
        for h in range(H_A):
            kh = kb[:, h * DV_A:(h + 1) * DV_A]
            for c in range(2):
                s_ref[h, c, :, q0:] = lax.dot_general(kh, qc_ref[h, c, q0:, :], (((1,), (1,)), ((), ())),
                                                      preferred_element_type=F32)

    def consume(j, s_ref, mask, q0=0):
        start = pl.multiple_of(j * TK, TK)
        vb = v_ref[pl.ds(start, TK), :]
        for h in range(H_A):
            vh = vb[:, h * DV_A:(h + 1) * DV_A]
            for c in range(2):
                s = s_ref[h, c, :, q0:]
                if mask is not None:
                    s = jnp.where(mask[:, q0:], s, MASK_VALUE)
                m_old = m_ref[h, c, :, q0:]
                m_new = jnp.maximum(m_old, jnp.max(s, axis=0, keepdims=True))
                alpha = jnp.exp2(m_old - m_new)
                p = jnp.exp2(s - m_new)
                l_ref[h, c, :, q0:] = alpha * l_ref[h, c, :, q0:] + jnp.sum(p, axis=0, keepdims=True)
                pv = lax.dot_general(vh, p.astype(BF16), (((0,), (0,)), ((), ())),
                                     preferred_element_type=F32)
                acc_ref[h, c, :, q0:] = alpha * acc_ref[h, c, :, q0:] + pv
                m_ref[h, c, :, q0:] = m_new

    first_diag = KV_PER_Q * qi
    scores(0, sa_ref)

    def body(jj, carry):
        j = 2 * jj
        scores(j + 1, sb_ref)
        consume(j, sa_ref, None)
        scores(j + 2, sa_ref)
        consume(j + 1, sb_ref, None)
        return carry

    lax.fori_loop(0, first_diag // 2, body, 0)
    for d in range(KV_PER_Q):
        cur, nxt = (sa_ref, sb_ref) if d % 2 == 0 else (sb_ref, sa_ref)
        if d + 1 < KV_PER_Q:
            scores(first_diag + d + 1, nxt, (d + 1) * TK)
        consume(first_diag + d, cur, (kchunk + d * (TK // CHUNK)) <= qchunk, d * TK)

    lam = lam_ref[0]
    for h in range(H_A):
        o = acc_ref[h, 0] / l_ref[h, 0] - lam * (acc_ref[h, 1] / l_ref[h, 1])
        ms = jnp.mean(o * o, axis=0, keepdims=True)
        y = o * lax.rsqrt(ms + EPS) * g_ref[...] * (1.0 - lam_init)
        o_ref[:, h * DV_A:(h + 1) * DV_A] = y.T.astype(BF16)


def _attention(qk, v, lam, subln_col, batch, seq, lam_init):
    assert TQ == KV_PER_Q * TK and KV_PER_Q % 2 == 0 and TK % CHUNK == 0 and seq % TQ == 0
    nq = seq // TQ
    t_tok = batch * seq
    kernel = functools.partial(_attn_kernel, lam_init=lam_init)
    return pl.pallas_call(
        kernel,
        grid=(batch, nq),
        in_specs=[
            pl.BlockSpec(memory_space=pltpu.SMEM),
            pl.BlockSpec((TQ, W_A), lambda b, i: (b * nq + i, 0)),
            pl.BlockSpec((seq, W_A), lambda b, i: (b, 1)),
            pl.BlockSpec((seq, W_A), lambda b, i: (b, 0)),
            pl.BlockSpec((DV_A, 1), lambda b, i: (0, 0)),
        ],
        out_specs=pl.BlockSpec((TQ, W_A), lambda b, i: (b * nq + i, 0)),
        out_shape=jax.ShapeDtypeStruct((t_tok, W_A), BF16),
        scratch_shapes=[
            pltpu.VMEM((H_A, 2, TQ, DV_A), BF16),
            pltpu.VMEM((H_A, 2, TK, TQ), F32),
            pltpu.VMEM((H_A, 2, TK, TQ), F32),
            pltpu.VMEM((H_A, 2, DV_A, TQ), F32),
            pltpu.VMEM((H_A, 2, 1, TQ), F32),
            pltpu.VMEM((H_A, 2, 1, TQ), F32),
        ],
        compiler_params=_cparams(("parallel", "arbitrary")),
        name="diff_attn",
    )(lam, qk, qk, v, subln_col)


_LEVELS = (32, 16, 8, 4, 2, 1)


def _ref_rows(g, h):
    n = g.shape[1]
    if 2 * h >= SUBLANES:
        pieces = []
        for m in range(CHUNK // (2 * h)):
            r = 2 * h * m + h - 1
            pieces.append(jnp.broadcast_to(g[r:r + 1, :], (2 * h, n)))
        return pieces[0] if len(pieces) == 1 else jnp.concatenate(pieces, axis=0)
    g3 = g.reshape(CHUNK // SUBLANES, SUBLANES, n)
    sub = lax.broadcasted_iota(jnp.int32, g3.shape, 1)
    out = None
    for m in reversed(range(SUBLANES // (2 * h))):
        r = 2 * h * m + h - 1
        piece = jnp.broadcast_to(g3[:, r:r + 1, :], g3.shape)
        out = piece if out is None else jnp.where(sub < 2 * h * (m + 1), piece, out)
    return out.reshape(CHUNK, n)


def _stack_heads(a, lane_head):
    zero = jnp.zeros_like(a)
    return jnp.concatenate([jnp.where(lane_head == hh, a, zero) for hh in range(H_B)], axis=0)


def _hgrn_kernel(hb_ref, f_ref, c_ref, lb_ref, ng_ref, cw_ref, gavg_ref, o_ref, state_ref, hbuf_ref, *, seq):
    n_chunks = seq // CHUNK
    nb = hb_ref.shape[0] // seq
    lb = lb_ref[...]
    ng = ng_ref[...]
    gavg = gavg_ref[...]

    row = lax.broadcasted_iota(jnp.int32, (CHUNK, W_B), 0)
    lane = lax.broadcasted_iota(jnp.int32, (CHUNK, W_B), 1)
    lane_head = lane // DK_B
    lane_s = lane % CHUNK
    tri = (lax.broadcasted_iota(jnp.int32, (CHUNK, CHUNK), 1)
           <= lax.broadcasted_iota(jnp.int32, (CHUNK, CHUNK), 0)).astype(BF16)
    lvl_mask = [((row // (2 * h)) == (lane_s // (2 * h))) & ((row % (2 * h)) >= h) & ((lane_s % (2 * h)) < h)
                for h in _LEVELS]
    eye_mask = row == lane_s
    r4 = lax.broadcasted_iota(jnp.int32, (H_B * DK_B, W_B), 0) // DK_B
    c4 = lax.broadcasted_iota(jnp.int32, (H_B * DK_B, W_B), 1) // DK_B
    bd_mask = r4 == c4

    state_ref[...] = jnp.zeros_like(state_ref)
    nt = (((1,), (1,)), ((), ()))
    tn = (((0,), (0,)), ((), ()))
    unroll = HGRN_UNROLL if n_chunks % HGRN_UNROLL == 0 else 1
    each = range(nb * unroll)

    def chunk(i, carry):
        rows = [pl.ds(pl.multiple_of((e % nb) * seq + (i * unroll + e // nb) * CHUNK, CHUNK), CHUNK) for e in each]
        hb = [hb_ref[rows[j], :].astype(F32) for j in each]
        zq = [t[:, 0:W_B] for t in hb]
        v16 = [t[:, W_B:2 * W_B].astype(BF16) for t in hb]
        zg = [t[:, 2 * W_B:3 * W_B] for t in hb]
        zf = [f_ref[rows[j], :] for j in each]

        logf = [jnp.log(jnp.maximum(lb + (1.0 - lb) * jax.nn.sigmoid(z), TINY)) for z in zf]
        key = [(1.0 - lb) * jax.nn.sigmoid(-z) for z in zf]
        q = [z * jax.nn.sigmoid(z) for z in zq]

        g = []
        for x in logf:
            hi = x.astype(BF16)
            r1 = x - hi.astype(F32)
            mid = r1.astype(BF16)
            lo = (r1 - mid.astype(F32)).astype(BF16)
            g.append((jnp.dot(tri, hi, preferred_element_type=F32) + jnp.dot(tri, mid, preferred_element_type=F32)
                      + jnp.dot(tri, lo, preferred_element_type=F32)) * LOG2_E)

        q16 = [t.astype(BF16) for t in q]
        k16 = [t.astype(BF16) for t in key]
        a_all = [jnp.where(eye_mask, lax.dot_general(q16[j], _stack_heads(k16[j], lane_head),
                                                     nt, preferred_element_type=F32), 0.0) for j in each]
        for h, mask in zip(_LEVELS, lvl_mask):
            w = [jnp.exp2(-jnp.abs(g[j] - _ref_rows(g[j], h))).astype(BF16) for j in each]
            ah = [lax.dot_general(q16[j] * w[j], _stack_heads(k16[j] * w[j], lane_head),
                                  nt, preferred_element_type=F32) for j in each]
            a_all = [a_all[j] + jnp.where(mask, ah[j], 0.0) for j in each]
        o = [jnp.dot(a_all[j].astype(BF16), _stack_heads(v16[j], lane_head), preferred_element_type=F32) for j in each]

        qg = [(q[j] * jnp.exp2(g[j])).astype(BF16) for j in each]
        g_last = [t[CHUNK - 1:CHUNK, :] for t in g]
        upd = [lax.dot_general(v16[j], (key[j] * jnp.exp2(g_last[j] - g[j])).astype(BF16), tn,
                               preferred_element_type=F32) for j in each]
        for j in each:
            st = state_ref[j % nb]
            o[j] = o[j] + lax.dot_general(qg[j], st.astype(BF16), nt, preferred_element_type=F32)
            state_ref[j % nb] = jnp.exp2(g_last[j]) * st + jnp.where(bd_mask, upd[j], 0.0)

        ms = [jnp.dot((t * t).astype(BF16), gavg, preferred_element_type=F32) for t in o]
        for j in each:
            y = o[j] * lax.rsqrt(ms[j] + EPS) * ng * (zg[j] * jax.nn.sigmoid(zg[j]))
            o_ref[rows[j], 0:W_B] = y.astype(BF16)
        return carry

    lax.fori_loop(0, n_chunks // unroll, chunk, 0)

    piece = min(512, seq)
    hbuf_ref[0:SUBLANES, :] = jnp.zeros((SUBLANES, C_WIDTH), F32)
    for j in range(nb):
        for p in range(seq // piece):
            cblk = c_ref[j * seq + p * piece:j * seq + (p + 1) * piece, :].astype(F32)
            hbuf_ref[SUBLANES + p * piece:SUBLANES + (p + 1) * piece, :] = (cblk[:, C_WIDTH:2 * C_WIDTH]
                                                                             * cblk[:, 2 * C_WIDTH:])
        for p in range(seq // piece):
            base = SUBLANES + p * piece
            y = (cw_ref[0:1, :] * hbuf_ref[base - 2:base - 2 + piece, :]
                 + cw_ref[1:2, :] * hbuf_ref[base - 1:base - 1 + piece, :]
                 + cw_ref[2:3, :] * hbuf_ref[base:base + piece, :])
            lo_r = j * seq + p * piece
            bgate = c_ref[lo_r:lo_r + piece, 0:C_WIDTH].astype(F32)
            o_ref[lo_r:lo_r + piece, W_B:W_B + C_WIDTH] = (bgate * y).astype(BF16)


def _hgrn_conv(hb, f, cv, lb, ng, cw, gavg, batch, seq):
    t_tok = batch * seq
    nb = HGRN_BATCH if batch % HGRN_BATCH == 0 else 1
    fixed = lambda b: (0, 0)
    return pl.pallas_call(
        functools.partial(_hgrn_kernel, seq=seq),
        grid=(batch // nb,),
        in_specs=[
            pl.BlockSpec((nb * seq, 3 * W_B), lambda b: (b, 0)),
            pl.BlockSpec((nb * seq, W_B), lambda b: (b, 0)),
            pl.BlockSpec((nb * seq, 3 * C_WIDTH), lambda b: (b, 0)),
            pl.BlockSpec((1, W_B), fixed),
            pl.BlockSpec((1, W_B), fixed),
            pl.BlockSpec((CONV_WIDTH, C_WIDTH), fixed),
            pl.BlockSpec((W_B, W_B), fixed),
        ],
        out_specs=pl.BlockSpec((nb * seq, W_B + C_WIDTH), lambda b: (b, 0)),
        out_shape=jax.ShapeDtypeStruct((t_tok, W_B + C_WIDTH), BF16),
        scratch_shapes=[
            pltpu.VMEM((nb, H_B * DV_B, W_B), F32),
            pltpu.VMEM((seq + SUBLANES, C_WIDTH), F32),
        ],
        compiler_params=_cparams(("parallel",)),
        name="hgrn_conv",
    )(hb, f, cv, lb, ng, cw, gavg)


def _outproj_router_kernel(oa_ref, obc_ref, x_ref, wo_ref, g_ref, wr_ref, br_ref,
                           xo_ref, route_ref, tab_ref, cnt_ref, carry_ref):
    i = pl.program_id(0)

    @pl.when(i == 0)
    def _():
        carry_ref[...] = jnp.zeros_like(carry_ref)

    mixed = (jnp.dot(oa_ref[...], wo_ref[0:W_A, :], preferred_element_type=F32)
             + jnp.dot(obc_ref[...], wo_ref[W_A:, :], preferred_element_type=F32))
    xn = x_ref[...] + mixed
    xo_ref[...] = xn

    ms = jnp.mean(xn * xn, axis=-1, keepdims=True)
    hn = (xn * lax.rsqrt(ms + EPS) * g_ref[...]).astype(BF16)
    logits = jnp.dot(hn, wr_ref[...], preferred_element_type=F32) + br_ref[...]

    tm = logits.shape[0]
    lane = lax.broadcasted_iota(jnp.int32, (tm, ROUTE_LANES), 1)
    neg = -jnp.inf
    big = ROUTE_LANES

    def top(mask):
        val = jnp.max(jnp.where(mask, logits, neg), axis=-1, keepdims=True)
        idx = jnp.min(jnp.where(mask & (logits == val), lane, big), axis=-1, keepdims=True)
        return val, idx

    gmask = lane < N_GROUPS
    gmax, gidx = top(gmask)
    p_g = 1.0 / jnp.sum(jnp.where(gmask, jnp.exp(logits - gmax), 0.0), axis=-1, keepdims=True)
    e_lo = N_GROUPS + EXPERTS_PER_GROUP * gidx
    emask = (lane >= e_lo) & (lane < e_lo + EXPERTS_PER_GROUP)
    v1, i1 = top(emask)
    v2, i2 = top(emask & (lane != i1))
    t2 = jnp.exp(v2 - v1)
    w1 = p_g / (1.0 + t2)
    w2 = p_g * t2 / (1.0 + t2)

    sel1 = lane == i1
    sel2 = lane == i2
    onehot = jnp.where(sel1, 1.0, jnp.where(sel2, 1.0, 0.0))
    strict = (lax.broadcasted_iota(jnp.int32, (tm, tm), 1)
              < lax.broadcasted_iota(jnp.int32, (tm, tm), 0)).astype(BF16)
    local = jnp.dot(strict, onehot.astype(BF16), preferred_element_type=F32)
    cnt = jnp.sum(onehot, axis=0, keepdims=True)
    wrows = jnp.floor((cnt + (MOE_WINDOW - 1)) * (1.0 / MOE_WINDOW)) * MOE_WINDOW
    upper = (lax.broadcasted_iota(jnp.int32, (ROUTE_LANES, ROUTE_LANES), 0)
             < lax.broadcasted_iota(jnp.int32, (ROUTE_LANES, ROUTE_LANES), 1)).astype(BF16)
    soff = jnp.dot(jnp.broadcast_to(wrows, (SUBLANES, ROUTE_LANES)).astype(BF16), upper,
                   preferred_element_type=F32)[0:1, :]
    slot = local + soff
    row1 = jnp.sum(jnp.where(sel1, slot, 0.0), axis=-1, keepdims=True)
    row2 = jnp.sum(jnp.where(sel2, slot, 0.0), axis=-1, keepdims=True)

    rec = jnp.where(lane == 0, (i1 - N_GROUPS).astype(F32),
          jnp.where(lane == 1, (i2 - N_GROUPS).astype(F32),
          jnp.where(lane == 2, w1,
          jnp.where(lane == 3, w2,
          jnp.where(lane == 4, row1,
          jnp.where(lane == 5, row2, 0.0))))))
    route_ref[...] = rec
    tab_ref[0] = jnp.zeros((SUBLANES, ROUTE_LANES), F32)
    tab_ref[0, 0:1, :] = cnt
    tab_ref[0, 1:2, :] = carry_ref[0:1, :]
    carry_ref[0:1, :] = carry_ref[0:1, :] + jnp.floor((cnt + (SUBLANES - 1)) * (1.0 / SUBLANES)) * SUBLANES
    cnt_ref[...] = carry_ref[...]


def _outproj_router(o_a, o_bc, x2, wo, g, wr, br):
    t_tok = x2.shape[0]
    tm = min(TM_ROUTE, t_tok)
    row = lambda i: (i, 0)
    fixed = lambda i: (0, 0)
    return pl.pallas_call(
        _outproj_router_kernel,
        grid=(t_tok // tm,),
        in_specs=[
            pl.BlockSpec((tm, W_A), row),
            pl.BlockSpec((tm, W_B + C_WIDTH), row),
            pl.BlockSpec((tm, D_MODEL), row),
            pl.BlockSpec(wo.shape, fixed),
            pl.BlockSpec((1, D_MODEL), fixed),
            pl.BlockSpec(wr.shape, fixed),
            pl.BlockSpec((1, ROUTE_LANES), fixed),
        ],
        out_specs=[
            pl.BlockSpec((tm, D_MODEL), row),
            pl.BlockSpec((tm, ROUTE_LANES), row),
            pl.BlockSpec((1, SUBLANES, ROUTE_LANES), lambda i: (i, 0, 0)),
            pl.BlockSpec((SUBLANES, ROUTE_LANES), fixed),
        ],
        out_shape=[
            jax.ShapeDtypeStruct((t_tok, D_MODEL), F32),
            jax.ShapeDtypeStruct((t_tok, ROUTE_LANES), F32),
            jax.ShapeDtypeStruct((t_tok // tm, SUBLANES, ROUTE_LANES), F32),
            jax.ShapeDtypeStruct((SUBLANES, ROUTE_LANES), F32),
        ],
        scratch_shapes=[pltpu.VMEM((SUBLANES, ROUTE_LANES), F32)],
        compiler_params=_cparams(("arbitrary",)),
        name="outproj_router",
    )(o_a, o_bc, x2, wo, g, wr, br)


def _start_copies(n, make_copy):
    def start(k, carry):
        make_copy(k).start()
        return carry

    lax.fori_loop(0, n, start, 0)


def _wait_copies(n, make_copy):
    def wait(k, carry):
        make_copy(k).wait()
        return carry

    lax.fori_loop(0, n, wait, 0)


def _window_copy_loops(n, make_copy):
    _start_copies(n, make_copy)
    _wait_copies(n, make_copy)


def _dispatch_kernel(pend_ref, tab_ref, tabp_ref, route_ref, x_ref, buf_ref, xp_ref, xq_ref, sem, semq):
    tm = x_ref.shape[0]
    r_st = xp_ref.shape[0]
    step = pl.program_id(0)

    @pl.when(pl.program_id(0) == 0)
    def _():
        xp_ref[...] = jnp.zeros_like(xp_ref)
        copies = []
        for e in range(N_EXPERTS):
            end = pend_ref[e]
            copies.append((None, pltpu.make_async_copy(
                xp_ref.at[0:MOE_BLOCK, :],
                buf_ref.at[pl.ds(pl.multiple_of(end - MOE_BLOCK, SUBLANES), MOE_BLOCK), :], sem)))
            copies.append((end >= MOE_BLOCK + MOE_WINDOW, pltpu.make_async_copy(
                xp_ref.at[0:MOE_WINDOW, :],
                buf_ref.at[pl.ds(pl.multiple_of(jnp.maximum(end - MOE_BLOCK - MOE_WINDOW, 0), SUBLANES), MOE_WINDOW), :],
                sem)))
        for cond, cp in copies:
            if cond is None:
                cp.start()
            else:
                pl.when(cond)(cp.start)
        for cond, cp in copies:
            if cond is None:
                cp.wait()
            else:
                pl.when(cond)(cp.wait)

        def tail_copy(k):
            return pltpu.make_async_copy(
                xp_ref.at[0:MOE_BLOCK, :],
                buf_ref.at[pl.ds(pl.multiple_of(k * MOE_BLOCK, MOE_BLOCK), MOE_BLOCK), :], sem)

        first_unused = pend_ref[N_EXPERTS - 1] // MOE_BLOCK
        n_unused = buf_ref.shape[0] // MOE_BLOCK - first_unused
        _window_copy_loops(n_unused, lambda k: tail_copy(first_unused + k))

    route = route_ref[...]
    r0 = route[:, 4:5].astype(jnp.int32)
    r1 = route[:, 5:6].astype(jnp.int32)
    col = lax.broadcasted_iota(jnp.int32, (tm, r_st), 1)
    sel = jnp.where(col == r0, 1.0, jnp.where(col == r1, 1.0, 0.0)).astype(BF16)

    def window(tabs, stage_ref, s):
        def make_copy(k):
            src = pl.multiple_of(k * MOE_WINDOW, MOE_WINDOW)
            return pltpu.make_async_copy(
                stage_ref.at[pl.ds(src, MOE_WINDOW), :],
                buf_ref.at[pl.ds(pl.multiple_of(tabs[0, 0, k], SUBLANES), MOE_WINDOW), :], s)
        return tabs[0, 0, MAX_WINDOWS], make_copy

    def run(cur_ref, cur_sem, prev_ref, prev_sem):
        cur_ref[...] = _pack_rows(lax.dot_general(sel, x_ref[...].astype(BF16), (((0,), (0,)), ((), ())),
                                                  preferred_element_type=F32))

        @pl.when(step > 0)
        def _():
            _wait_copies(*window(tabp_ref, prev_ref, prev_sem))

        _start_copies(*window(tab_ref, cur_ref, cur_sem))

        @pl.when(step == pl.num_programs(0) - 1)
        def _():
            _wait_copies(*window(tab_ref, cur_ref, cur_sem))

    @pl.when(step % 2 == 0)
    def _():
        run(xp_ref, sem, xq_ref, semq)

    @pl.when(step % 2 == 1)
    def _():
        run(xq_ref, semq, xp_ref, sem)


def _dispatch(pad_end, tabs, route, x2, cap):
    t_tok = x2.shape[0]
    tm = min(TM_ROUTE, t_tok)
    n_tiles = t_tok // tm
    return pl.pallas_call(
        _dispatch_kernel,
        grid=(n_tiles,),
        in_specs=[
            pl.BlockSpec(memory_space=pltpu.SMEM),
            pl.BlockSpec((1, 1, ROUTE_LANES), lambda i: (i, 0, 0), memory_space=pltpu.SMEM),
            pl.BlockSpec((1, 1, ROUTE_LANES), lambda i: (jnp.maximum(i - 1, 0), 0, 0), memory_space=pltpu.SMEM),
            pl.BlockSpec((tm, ROUTE_LANES), lambda i: (i, 0)),
            pl.BlockSpec((tm, D_MODEL), lambda i: (i, 0)),
        ],
        out_specs=pl.BlockSpec(memory_space=pl.ANY),
        out_shape=jax.ShapeDtypeStruct((cap, D_PACKED), jnp.int32),
        scratch_shapes=[pltpu.VMEM((_staging_rows(tm), D_PACKED), jnp.int32),
                        pltpu.VMEM((_staging_rows(tm), D_PACKED), jnp.int32),
                        pltpu.SemaphoreType.DMA(()), pltpu.SemaphoreType.DMA(())],
        compiler_params=_cparams(("arbitrary",)),
        name="moe_dispatch",
    )(pad_end, tabs, tabs, route, x2)


def _experts_kernel(blk_e_ref, n_used_ref, x_ref, g_ref, wg_ref, wu_ref, wd_ref, y_ref):
    del blk_e_ref
    i = pl.program_id(0)

    @pl.when(i < n_used_ref[0])
    def _():
        wg = wg_ref[0, 0].astype(BF16)
        wu = wu_ref[0, 0].astype(BF16)
        wd = wd_ref[0, 0].astype(BF16)
        parts = _row_parts(x_ref.shape[0])
        hn = []
        for rows in parts:
            x = _unpack_rows(x_ref[rows, :])
            ms = jnp.mean(x * x, axis=-1, keepdims=True)
            hn.append((x * lax.rsqrt(ms + EPS) * g_ref[...]).astype(BF16))
        a = [jnp.dot(t, wg, preferred_element_type=F32) for t in hn]
        u = [jnp.dot(t, wu, preferred_element_type=F32) for t in hn]
        act = [(a[r] * jax.nn.sigmoid(a[r]) * u[r]).astype(BF16) for r in range(len(parts))]
        for r, rows in enumerate(parts):
            y = jnp.dot(act[r], wd, preferred_element_type=F32)
            y_ref[rows, :] = _pack_rows(y.astype(BF16).astype(F32))

    @pl.when(i >= n_used_ref[0])
    def _():
        y_ref[...] = jnp.zeros_like(y_ref)


def _experts(blk_e, n_used, buf, g, wg, wu, wd, layer):
    n_blocks = buf.shape[0] // MOE_BLOCK
    grid_spec = pltpu.PrefetchScalarGridSpec(
        num_scalar_prefetch=2,
        grid=(n_blocks,),
        in_specs=[
            pl.BlockSpec((MOE_BLOCK, D_PACKED), lambda i, be, nu: (i, 0)),
            pl.BlockSpec((1, D_MODEL), lambda i, be, nu: (0, 0)),
            pl.BlockSpec((1, 1, D_MODEL, D_EXPERT), lambda i, be, nu: (layer, be[i], 0, 0)),
            pl.BlockSpec((1, 1, D_MODEL, D_EXPERT), lambda i, be, nu: (layer, be[i], 0, 0)),
            pl.BlockSpec((1, 1, D_EXPERT, D_MODEL), lambda i, be, nu: (layer, be[i], 0, 0)),
        ],
        out_specs=pl.BlockSpec((MOE_BLOCK, D_PACKED), lambda i, be, nu: (i, 0)),
    )
    return pl.pallas_call(
        _experts_kernel,
        grid_spec=grid_spec,
        out_shape=jax.ShapeDtypeStruct(buf.shape, jnp.int32),
        compiler_params=_cparams(("arbitrary",)),
        name="moe_experts",
    )(blk_e, n_used, buf, g, wg, wu, wd)


def _combine_kernel(tab_ref, tabn_ref, route_ref, x_ref, y_ref, o_ref, ya_ref, yb_ref, sema, semb):
    tm = x_ref.shape[0]
    r_st = ya_ref.shape[0]
    step = pl.program_id(0)

    def window(tabs, stage_ref, s):
        def make_copy(k):
            dst = pl.multiple_of(k * MOE_WINDOW, MOE_WINDOW)
            return pltpu.make_async_copy(y_ref.at[pl.ds(pl.multiple_of(tabs[0, 0, k], SUBLANES), MOE_WINDOW), :],
                                         stage_ref.at[pl.ds(dst, MOE_WINDOW), :], s)
        return tabs[0, 0, MAX_WINDOWS], make_copy

    @pl.when(step == 0)
    def _():
        ya_ref[...] = jnp.zeros_like(ya_ref)
        yb_ref[...] = jnp.zeros_like(yb_ref)
        _start_copies(*window(tab_ref, ya_ref, sema))

    route = route_ref[...]
    r0 = route[:, 4:5].astype(jnp.int32)
    r1 = route[:, 5:6].astype(jnp.int32)
    col = lax.broadcasted_iota(jnp.int32, (tm, r_st), 1)
    gate = jnp.where(col == r0, route[:, 2:3], jnp.where(col == r1, route[:, 3:4], 0.0)).astype(BF16)

    def run(cur_ref, cur_sem, nxt_ref, nxt_sem):
        @pl.when(step + 1 < pl.num_programs(0))
        def _():
            _start_copies(*window(tabn_ref, nxt_ref, nxt_sem))

        _wait_copies(*window(tab_ref, cur_ref, cur_sem))
        o_ref[...] = x_ref[...] + jnp.dot(gate, _unpack_rows(cur_ref[...]).astype(BF16), preferred_element_type=F32)

    @pl.when(step % 2 == 0)
    def _():
        run(ya_ref, sema, yb_ref, semb)

    @pl.when(step % 2 == 1)
    def _():
        run(yb_ref, semb, ya_ref, sema)


def _combine(tabs, route, x2, y_buf):
    t_tok = x2.shape[0]
    tm = min(TM_ROUTE, t_tok)
    n_tiles = t_tok // tm
    return pl.pallas_call(
        _combine_kernel,
        grid=(n_tiles,),
        in_specs=[
            pl.BlockSpec((1, 1, ROUTE_LANES), lambda i: (i, 0, 0), memory_space=pltpu.SMEM),
            pl.BlockSpec((1, 1, ROUTE_LANES), lambda i: (jnp.minimum(i + 1, n_tiles - 1), 0, 0),
                         memory_space=pltpu.SMEM),
            pl.BlockSpec((tm, ROUTE_LANES), lambda i: (i, 0)),
            pl.BlockSpec((tm, D_MODEL), lambda i: (i, 0)),
            pl.BlockSpec(memory_space=pl.ANY),
        ],
        out_specs=pl.BlockSpec((tm, D_MODEL), lambda i: (i, 0)),
        out_shape=jax.ShapeDtypeStruct((t_tok, D_MODEL), F32),
        scratch_shapes=[pltpu.VMEM((_staging_rows(tm), D_PACKED), jnp.int32),
                        pltpu.VMEM((_staging_rows(tm), D_PACKED), jnp.int32),
                        pltpu.SemaphoreType.DMA(()), pltpu.SemaphoreType.DMA(())],
        compiler_params=_cparams(("arbitrary",)),
        name="moe_combine",
    )(tabs, tabs, route, x2, y_buf)


def _rope_tables(positions):
    half = ROT_DIM // 2
    inv_freq = ROPE_THETA ** (-jnp.arange(0, ROT_DIM, 2, dtype=F32) / ROT_DIM)
    d = jnp.arange(LANES) % DK_A
    freq_lane = jnp.where(d < ROT_DIM, inv_freq[d % half], 0.0)
    ang = positions.reshape(-1, 1).astype(F32) * freq_lane[None, :]
    cos, sin = jnp.cos(ang), jnp.sin(ang)
    s1 = jnp.where(d < half, -sin, 0.0)
    s2 = jnp.where((d >= half) & (d < ROT_DIM), sin, 0.0)
    return cos, s1, s2


def _block_avg(width, group):
    idx = jnp.arange(width) // group
    return jnp.where(idx[:, None] == idx[None, :], 1.0 / group, 0.0).astype(BF16)


def kernel(x, positions, norm_mix_g, w_in, attn_q_norm_g, attn_k_norm_g, lambda_q1, lambda_k1, lambda_q2,
           lambda_k2, attn_subln_g, hgrn_lower_bounds, hgrn_norm_g, conv_w, w_out, norm_ffn_g, router_group_w,
           router_group_b, router_expert_w, router_expert_b, expert_w_gate, expert_w_up, expert_w_down):
    batch, seq, d = x.shape
    assert d == D_MODEL
    depth = w_in.shape[0]
    t_tok = batch * seq
    n_assign = 2 * t_tok
    n_route_tiles = t_tok // min(TM_ROUTE, t_tok)
    n_blocks = (n_assign + (SUBLANES - 1) * n_route_tiles * N_EXPERTS) // MOE_BLOCK + 2 * N_EXPERTS + 1
    cap = n_blocks * MOE_BLOCK

    lb_probs = jax.nn.softmax(hgrn_lower_bounds.astype(F32), axis=0)
    lower_bounds = jnp.cumsum(lb_probs, axis=0) - lb_probs[0:1]
    cos_t, s1_t, s2_t = _rope_tables(positions)
    gsum_a = _block_avg(W_A, DK_A)
    gavg_b = _block_avg(W_B, DV_B)

    o_q, o_k, o_v = 0, W_A, 2 * W_A
    o_bq = 3 * W_A
    o_bf, o_bi, o_bg = o_bq + W_B, o_bq + 2 * W_B, o_bq + 3 * W_B
    o_c = o_bq + 4 * W_B

    x2 = x.reshape(t_tok, d)
    for l in range(depth):
        w = w_in[l]
        wqk = w[:, o_q:o_v].astype(BF16)
        wv = w[:, o_v:o_bq].astype(BF16)
        whb = jnp.concatenate([w[:, o_bq:o_bf], w[:, o_bi:o_c]], axis=1).astype(BF16)
        wf = w[:, o_bf:o_bi].astype(BF16)
        wc = w[:, o_c:].astype(BF16)
        gqk = jnp.concatenate([jnp.tile(attn_q_norm_g[l], 2 * H_A) * (DK_A ** -0.5 * LOG2_E),
                               jnp.tile(attn_k_norm_g[l], 2 * H_A)]).reshape(1, 2 * W_A).astype(F32)
        qk, v, hb, f, cv = _inproj(x2, norm_mix_g[l].reshape(1, d), wqk, wv, whb, wf, wc, gsum_a, gqk,
                                   cos_t, s1_t, s2_t)

        lam_init = 0.8 - 0.6 * math.exp(-0.3 * l)
        lam = (jnp.exp(jnp.sum(lambda_q1[l].astype(F32) * lambda_k1[l].astype(F32)))
               - jnp.exp(jnp.sum(lambda_q2[l].astype(F32) * lambda_k2[l].astype(F32))) + lam_init).reshape(1)
        o_a = _attention(qk, v, lam, attn_subln_g[l].reshape(DV_A, 1).astype(F32), batch, seq, lam_init)

        o_bc = _hgrn_conv(hb, f, cv, lower_bounds[l].reshape(1, W_B), jnp.tile(hgrn_norm_g[l], H_B).reshape(1, W_B),
                          conv_w[l], gavg_b, batch, seq)

        wr = jnp.concatenate([router_group_w[l], router_expert_w[l],
                              jnp.zeros((d, ROUTE_LANES - N_LOGITS), F32)], axis=1).astype(BF16)
        br = jnp.concatenate([router_group_b[l], router_expert_b[l],
                              jnp.zeros((ROUTE_LANES - N_LOGITS,), F32)]).reshape(1, ROUTE_LANES)
        x2, route, tab, cnt = _outproj_router(o_a, o_bc, x2, w_out[l].astype(BF16), norm_ffn_g[l].reshape(1, d), wr, br)

        counts = cnt[0, N_GROUPS:N_GROUPS + N_EXPERTS].astype(jnp.int32)
        padded = (counts + MOE_WINDOW + MOE_BLOCK - 1) // MOE_BLOCK * MOE_BLOCK
        pad_end = jnp.cumsum(padded)
        pad_start = pad_end - padded
        blk_pos = jnp.arange(n_blocks, dtype=jnp.int32) * MOE_BLOCK
        blk_e = jnp.minimum(jnp.sum((pad_end[None, :] <= blk_pos[:, None]).astype(jnp.int32), axis=1),
                            N_EXPERTS - 1)
        n_used = (pad_end[-1] // MOE_BLOCK).astype(jnp.int32).reshape(1)
        tile_cnt = tab[:, 0, N_GROUPS:N_GROUPS + N_EXPERTS].astype(jnp.int32)
        tile_base = tab[:, 1, N_GROUPS:N_GROUPS + N_EXPERTS].astype(jnp.int32) + pad_start[None, :]
        n_win = (tile_cnt + MOE_WINDOW - 1) // MOE_WINDOW
        w_end = jnp.cumsum(n_win, axis=1)
        w_start = w_end - n_win
        k_idx = jnp.arange(MAX_WINDOWS, dtype=jnp.int32)[None, :, None]
        owner = (w_start[:, None, :] <= k_idx) & (k_idx < w_end[:, None, :])
        win_dst = jnp.sum(jnp.where(owner, tile_base[:, None, :] + (k_idx - w_start[:, None, :]) * MOE_WINDOW, 0),
                          axis=-1)
        tabs = jnp.concatenate([win_dst, w_end[:, -1:],
                                jnp.zeros((win_dst.shape[0], ROUTE_LANES - MAX_WINDOWS - 1), jnp.int32)],
                               axis=1).reshape(-1, 1, ROUTE_LANES)

        buf = _dispatch(pad_end, tabs, route, x2, cap)
        y_buf = _experts(blk_e, n_used, buf, norm_ffn_g[l].reshape(1, d), expert_w_gate, expert_w_up,
                         expert_w_down, l)
        x2 = _combine(tabs, route, x2, y_buf)
    return x2.reshape(batch, seq, d)
```

```python
import functools
import math

import jax
import jax.numpy as jnp
from jax import lax
from jax.experimental import pallas as pl
from jax.experimental.pallas import tpu as pltpu

F32 = jnp.float32
BF16 = jnp.bfloat16

LANES = 128
SUBLANES = 8

D_MODEL = 1024
CHUNK = 64
EPS = 1e-6
MASK_VALUE = -1e30
TINY = 1e-30
LOG2_E = math.log2(math.e)
H_A = 4
DK_A = D_MODEL // 16
DV_A = 2 * DK_A
ROT_DIM = DK_A // 4
ROPE_THETA = 500000.0
H_B = 4
DK_B = D_MODEL // 16
DV_B = D_MODEL // 16
C_WIDTH = D_MODEL // 4
CONV_WIDTH = 3
N_GROUPS = 4
EXPERTS_PER_GROUP = 8
N_EXPERTS = N_GROUPS * EXPERTS_PER_GROUP
D_EXPERT = D_MODEL // 2
D_PACKED = D_MODEL // 2

W_A = H_A * DV_A
W_B = H_B * DK_B
ROUTE_LANES = LANES
N_LOGITS = N_GROUPS + N_EXPERTS

TM_PROJ = 1024
TK = 256
KV_PER_Q = 2
TQ = KV_PER_Q * TK
TM_ROUTE = 512
MOE_BLOCK = 512
HGRN_BATCH = 2
HGRN_UNROLL = 2
MOE_WINDOW = 16
MAX_WINDOWS = 2 * TM_ROUTE // MOE_WINDOW + N_EXPERTS
VMEM_LIMIT = 56 * 1024 * 1024


def _staging_rows(tm):
    return 2 * tm + N_EXPERTS * MOE_WINDOW


def _row_parts(rows, n=2):
    step = rows // n
    return [slice(r * step, (r + 1) * step) for r in range(n)]


def _pack_rows(x):
    half = x.shape[1] // 2
    lo = lax.bitcast_convert_type(x[:, :half], jnp.int32)
    hi = lax.bitcast_convert_type(x[:, half:], jnp.int32)
    return hi | lax.shift_right_logical(lo, 16)


def _unpack_rows(p):
    lo = lax.bitcast_convert_type(lax.shift_left(p, 16), F32)
    hi = lax.bitcast_convert_type(p & jnp.int32(-65536), F32)
    return jnp.concatenate([lo, hi], axis=1)


def _cparams(sem, vmem=VMEM_LIMIT):
    return pltpu.CompilerParams(dimension_semantics=sem, vmem_limit_bytes=vmem)


def _inproj_kernel(x_ref, g_ref, wqk_ref, wv_ref, whb_ref, wf_ref, wc_ref, gsum_ref, gqk_ref,
                   cos_ref, s1_ref, s2_ref, qk_out, v_out, hb_out, f_out, c_out):
    parts = _row_parts(x_ref.shape[0])
    xn = []
    for rows in parts:
        x = x_ref[rows, :]
        ms = jnp.mean(x * x, axis=-1, keepdims=True)
        xn.append((x * lax.rsqrt(ms + EPS) * g_ref[...]).astype(BF16))

    gsum = gsum_ref[...]
    for part in range(2):
        lo = part * W_A
        t = [jnp.dot(xr, wqk_ref[:, lo:lo + W_A], preferred_element_type=F32) for xr in xn]
        msq = [jnp.dot((tr * tr).astype(BF16), gsum, preferred_element_type=F32) for tr in t]
        for r, rows in enumerate(parts):
            tn = t[r] * lax.rsqrt(msq[r] + EPS) * gqk_ref[:, lo:lo + W_A]
            cos = cos_ref[rows, :]
            s1 = s1_ref[rows, :]
            s2 = s2_ref[rows, :]
            for h in range(H_A):
                slab = tn[:, h * LANES:(h + 1) * LANES]
                up = pltpu.roll(slab, LANES - ROT_DIM // 2, axis=1)
                dn = pltpu.roll(slab, ROT_DIM // 2, axis=1)
                qk_out[rows, lo + h * LANES:lo + (h + 1) * LANES] = (slab * cos + up * s1 + dn * s2).astype(BF16)

    for r, rows in enumerate(parts):
        v_out[rows, :] = jnp.dot(xn[r], wv_ref[...], preferred_element_type=F32).astype(BF16)
    for r, rows in enumerate(parts):
        hb_out[rows, :] = jnp.dot(xn[r], whb_ref[...], preferred_element_type=F32).astype(BF16)
    for r, rows in enumerate(parts):
        f_out[rows, :] = jnp.dot(xn[r], wf_ref[...], preferred_element_type=F32)
    for r, rows in enumerate(parts):
        c_out[rows, :] = jnp.dot(xn[r], wc_ref[...], preferred_element_type=F32).astype(BF16)


def _inproj(x2, g, wqk, wv, whb, wf, wc, gsum, gqk, cos_t, s1_t, s2_t):
    t_tok = x2.shape[0]
    tm = min(TM_PROJ, t_tok)
    grid = (t_tok // tm,)
    row = lambda i: (i, 0)
    fixed = lambda i: (0, 0)
    return pl.pallas_call(
        _inproj_kernel,
        grid=grid,
        in_specs=[
            pl.BlockSpec((tm, D_MODEL), row),
            pl.BlockSpec((1, D_MODEL), fixed),
            pl.BlockSpec(wqk.shape, fixed),
            pl.BlockSpec(wv.shape, fixed),
            pl.BlockSpec(whb.shape, fixed),
            pl.BlockSpec(wf.shape, fixed),
            pl.BlockSpec(wc.shape, fixed),
            pl.BlockSpec(gsum.shape, fixed),
            pl.BlockSpec(gqk.shape, fixed),
            pl.BlockSpec((tm, LANES), row),
            pl.BlockSpec((tm, LANES), row),
            pl.BlockSpec((tm, LANES), row),
        ],
        out_specs=[
            pl.BlockSpec((tm, 2 * W_A), row),
            pl.BlockSpec((tm, W_A), row),
            pl.BlockSpec((tm, 3 * W_B), row),
            pl.BlockSpec((tm, W_B), row),
            pl.BlockSpec((tm, 3 * C_WIDTH), row),
        ],
        out_shape=[
            jax.ShapeDtypeStruct((t_tok, 2 * W_A), BF16),
            jax.ShapeDtypeStruct((t_tok, W_A), BF16),
            jax.ShapeDtypeStruct((t_tok, 3 * W_B), BF16),
            jax.ShapeDtypeStruct((t_tok, W_B), F32),
            jax.ShapeDtypeStruct((t_tok, 3 * C_WIDTH), BF16),
        ],
        compiler_params=_cparams(("parallel",)),
        name="inproj",
    )(x2, g, wqk, wv, whb, wf, wc, gsum, gqk, cos_t, s1_t, s2_t)


def _attn_kernel(lam_ref, q_ref, k_ref, v_ref, g_ref, o_ref, qc_ref, sa_ref, sb_ref, acc_ref, m_ref, l_ref, *,
                 lam_init):
    qi = pl.program_id(1)
    q = q_ref[...]
    lane = lax.broadcasted_iota(jnp.int32, (TQ, DV_A), 1)
    zero = jnp.zeros((TQ, DV_A), BF16)
    for h in range(H_A):
        qh = q[:, h * DV_A:(h + 1) * DV_A]
        qc_ref[h, 0] = jnp.where(lane < DK_A, qh, zero)
        qc_ref[h, 1] = jnp.where(lane >= DK_A, qh, zero)

    acc_ref[...] = jnp.zeros_like(acc_ref)
    m_ref[...] = jnp.full_like(m_ref, -jnp.inf)
    l_ref[...] = jnp.zeros_like(l_ref)

    kchunk = lax.broadcasted_iota(jnp.int32, (TK, TQ), 0) // CHUNK
    qchunk = lax.broadcasted_iota(jnp.int32, (TK, TQ), 1) // CHUNK

    def scores(j, s_ref, q0=0):
        start = pl.multiple_of(j * TK, TK)
        kb = k_ref[pl.ds(start, TK), :]
        for h in range(H_A):
            kh = kb[:, h * DV_A:(h + 1) * DV_A]
            for c in range(2):
                s_ref[h, c, :, q0:] = lax.dot_general(kh, qc_ref[h, c, q0:, :], (((1,), (1,)), ((), ())),
                                                      preferred_element_type=F32)

    def consume(j, s_ref, mask, q0=0):
        start = pl.multiple_of(j * TK, TK)
        vb = v_ref[pl.ds(start, TK), :]
        for h in range(H_A):
            vh = vb[:, h * DV_A:(h + 1) * DV_A]
            for c in range(2):
                s = s_ref[h, c, :, q0:]
                if mask is not None:
                    s = jnp.where(mask[:, q0:], s, MASK_VALUE)
                m_old = m_ref[h, c, :, q0:]
                m_new = jnp.maximum(m_old, jnp.max(s, axis=0, keepdims=True))
                alpha = jnp.exp2(m_old - m_new)
                p = jnp.exp2(s - m_new)
                l_ref[h, c, :, q0:] = alpha * l_ref[h, c, :, q0:] + jnp.sum(p, axis=0, keepdims=True)
                pv = lax.dot_general(vh, p.astype(BF16), (((0,), (0,)), ((), ())),
                                     preferred_element_type=F32)
                acc_ref[h, c, :, q0:] = alpha * acc_ref[h, c, :, q0:] + pv
                m_ref[h, c, :, q0:] = m_new

    first_diag = KV_PER_Q * qi
    scores(0, sa_ref)

    def body(jj, carry):
        j = 2 * jj
        scores(j + 1, sb_ref)
        consume(j, sa_ref, None)
        scores(j + 2, sa_ref)
        consume(j + 1, sb_ref, None)
        return carry

    lax.fori_loop(0, first_diag // 2, body, 0)
    for d in range(KV_PER_Q):
        cur, nxt = (sa_ref, sb_ref) if d % 2 == 0 else (sb_ref, sa_ref)
        if d + 1 < KV_PER_Q:
            scores(first_diag + d + 1, nxt, (d + 1) * TK)
        consume(first_diag + d, cur, (kchunk + d * (TK // CHUNK)) <= qchunk, d * TK)

    lam = lam_ref[0]
    for h in range(H_A):
        o = acc_ref[h, 0] / l_ref[h, 0] - lam * (acc_ref[h, 1] / l_ref[h, 1])
        ms = jnp.mean(o * o, axis=0, keepdims=True)
        y = o * lax.rsqrt(ms + EPS) * g_ref[...] * (1.0 - lam_init)
        o_ref[:, h * DV_A:(h + 1) * DV_A] = y.T.astype(BF16)


def _attention(qk, v, lam, subln_col, batch, seq, lam_init):
    assert TQ == KV_PER_Q * TK and KV_PER_Q % 2 == 0 and TK % CHUNK == 0 and seq % TQ == 0
    nq = seq // TQ
    t_tok = batch * seq
    kernel = functools.partial(_attn_kernel, lam_init=lam_init)
    return pl.pallas_call(
        kernel,
        grid=(batch, nq),
        in_specs=[
            pl.BlockSpec(memory_space=pltpu.SMEM),
            pl.BlockSpec((TQ, W_A), lambda b, i: (b * nq + i, 0)),
            pl.BlockSpec((seq, W_A), lambda b, i: (b, 1)),
            pl.BlockSpec((seq, W_A), lambda b, i: (b, 0)),
            pl.BlockSpec((DV_A, 1), lambda b, i: (0, 0)),
        ],
        out_specs=pl.BlockSpec((TQ, W_A), lambda b, i: (b * nq + i, 0)),
        out_shape=jax.ShapeDtypeStruct((t_tok, W_A), BF16),
        scratch_shapes=[
            pltpu.VMEM((H_A, 2, TQ, DV_A), BF16),
            pltpu.VMEM((H_A, 2, TK, TQ), F32),
            pltpu.VMEM((H_A, 2, TK, TQ), F32),
            pltpu.VMEM((H_A, 2, DV_A, TQ), F32),
            pltpu.VMEM((H_A, 2, 1, TQ), F32),
            pltpu.VMEM((H_A, 2, 1, TQ), F32),
        ],
        compiler_params=_cparams(("parallel", "arbitrary")),
        name="diff_attn",
    )(lam, qk, qk, v, subln_col)


_LEVELS = (32, 16, 8, 4, 2, 1)


def _ref_rows(g, h):
    n = g.shape[1]
    if 2 * h >= SUBLANES:
        pieces = []
        for m in range(CHUNK // (2 * h)):
            r = 2 * h * m + h - 1
            pieces.append(jnp.broadcast_to(g[r:r + 1, :], (2 * h, n)))
        return pieces[0] if len(pieces) == 1 else jnp.concatenate(pieces, axis=0)
    g3 = g.reshape(CHUNK // SUBLANES, SUBLANES, n)
    sub = lax.broadcasted_iota(jnp.int32, g3.shape, 1)
    out = None
    for m in reversed(range(SUBLANES // (2 * h))):
        r = 2 * h * m + h - 1
        piece = jnp.broadcast_to(g3[:, r:r + 1, :], g3.shape)
        out = piece if out is None else jnp.where(sub < 2 * h * (m + 1), piece, out)
    return out.reshape(CHUNK, n)


def _stack_heads(a, lane_head):
    zero = jnp.zeros_like(a)
    return jnp.concatenate([jnp.where(lane_head == hh, a, zero) for hh in range(H_B)], axis=0)


def _hgrn_kernel(hb_ref, f_ref, c_ref, lb_ref, ng_ref, cw_ref, gavg_ref, o_ref, state_ref, hbuf_ref, *, seq):
    n_chunks = seq // CHUNK
    nb = hb_ref.shape[0] // seq
    lb = lb_ref[...]
    ng = ng_ref[...]
    gavg = gavg_ref[...]

    row = lax.broadcasted_iota(jnp.int32, (CHUNK, W_B), 0)
    lane = lax.broadcasted_iota(jnp.int32, (CHUNK, W_B), 1)
    lane_head = lane // DK_B
    lane_s = lane % CHUNK
    tri = (lax.broadcasted_iota(jnp.int32, (CHUNK, CHUNK), 1)
           <= lax.broadcasted_iota(jnp.int32, (CHUNK, CHUNK), 0)).astype(BF16)
    lvl_mask = [((row // (2 * h)) == (lane_s // (2 * h))) & ((row % (2 * h)) >= h) & ((lane_s % (2 * h)) < h)
                for h in _LEVELS]
    eye_mask = row == lane_s
    r4 = lax.broadcasted_iota(jnp.int32, (H_B * DK_B, W_B), 0) // DK_B
    c4 = lax.broadcasted_iota(jnp.int32, (H_B * DK_B, W_B), 1) // DK_B
    bd_mask = r4 == c4

    state_ref[...] = jnp.zeros_like(state_ref)
    nt = (((1,), (1,)), ((), ()))
    tn = (((0,), (0,)), ((), ()))
    unroll = HGRN_UNROLL if n_chunks % HGRN_UNROLL == 0 else 1
    each = range(nb * unroll)

    def chunk(i, carry):
        rows = [pl.ds(pl.multiple_of((e % nb) * seq + (i * unroll + e // nb) * CHUNK, CHUNK), CHUNK) for e in each]
        hb = [hb_ref[rows[j], :].astype(F32) for j in each]
        zq = [t[:, 0:W_B] for t in hb]
        v16 = [t[:, W_B:2 * W_B].astype(BF16) for t in hb]
        zg = [t[:, 2 * W_B:3 * W_B] for t in hb]
        zf = [f_ref[rows[j], :] for j in each]

        logf = [jnp.log(jnp.maximum(lb + (1.0 - lb) * jax.nn.sigmoid(z), TINY)) for z in zf]
        key = [(1.0 - lb) * jax.nn.sigmoid(-z) for z in zf]
        q = [z * jax.nn.sigmoid(z) for z in zq]

        g = []
        for x in logf:
            hi = x.astype(BF16)
            r1 = x - hi.astype(F32)
            mid = r1.astype(BF16)
            lo = (r1 - mid.astype(F32)).astype(BF16)
            g.append((jnp.dot(tri, hi, preferred_element_type=F32) + jnp.dot(tri, mid, preferred_element_type=F32)
                      + jnp.dot(tri, lo, preferred_element_type=F32)) * LOG2_E)

        q16 = [t.astype(BF16) for t in q]
        k16 = [t.astype(BF16) for t in key]
        a_all = [jnp.where(eye_mask, lax.dot_general(q16[j], _stack_heads(k16[j], lane_head),
                                                     nt, preferred_element_type=F32), 0.0) for j in each]
        for h, mask in zip(_LEVELS, lvl_mask):
            w = [jnp.exp2(-jnp.abs(g[j] - _ref_rows(g[j], h))).astype(BF16) for j in each]
            ah = [lax.dot_general(q16[j] * w[j], _stack_heads(k16[j] * w[j], lane_head),
                                  nt, preferred_element_type=F32) for j in each]
            a_all = [a_all[j] + jnp.where(mask, ah[j], 0.0) for j in each]
        o = [jnp.dot(a_all[j].astype(BF16), _stack_heads(v16[j], lane_head), preferred_element_type=F32) for j in each]

        qg = [(q[j] * jnp.exp2(g[j])).astype(BF16) for j in each]
        g_last = [t[CHUNK - 1:CHUNK, :] for t in g]
        upd = [lax.dot_general(v16[j], (key[j] * jnp.exp2(g_last[j] - g[j])).astype(BF16), tn,
                               preferred_element_type=F32) for j in each]
        for j in each:
            st = state_ref[j % nb]
            o[j] = o[j] + lax.dot_general(qg[j], st.astype(BF16), nt, preferred_element_type=F32)
            state_ref[j % nb] = jnp.exp2(g_last[j]) * st + jnp.where(bd_mask, upd[j], 0.0)

        ms = [jnp.dot((t * t).astype(BF16), gavg, preferred_element_type=F32) for t in o]
        for j in each:
            y = o[j] * lax.rsqrt(ms[j] + EPS) * ng * (zg[j] * jax.nn.sigmoid(zg[j]))
            o_ref[rows[j], 0:W_B] = y.astype(BF16)
        return carry

    lax.fori_loop(0, n_chunks // unroll, chunk, 0)

    piece = min(512, seq)
    hbuf_ref[0:SUBLANES, :] = jnp.zeros((SUBLANES, C_WIDTH), F32)
    for j in range(nb):
        for p in range(seq // piece):
            cblk = c_ref[j * seq + p * piece:j * seq + (p + 1) * piece, :].astype(F32)
            hbuf_ref[SUBLANES + p * piece:SUBLANES + (p + 1) * piece, :] = (cblk[:, C_WIDTH:2 * C_WIDTH]
                                                                             * cblk[:, 2 * C_WIDTH:])
        for p in range(seq // piece):
            base = SUBLANES + p * piece
            y = (cw_ref[0:1, :] * hbuf_ref[base - 2:base - 2 + piece, :]
                 + cw_ref[1:2, :] * hbuf_ref[base - 1:base - 1 + piece, :]
                 + cw_ref[2:3, :] * hbuf_ref[base:base + piece, :])
            lo_r = j * seq + p * piece
            bgate = c_ref[lo_r:lo_r + piece, 0:C_WIDTH].astype(F32)
            o_ref[lo_r:lo_r + piece, W_B:W_B + C_WIDTH] = (bgate * y).astype(BF16)


def _hgrn_conv(hb, f, cv, lb, ng, cw, gavg, batch, seq):
    t_tok = batch * seq
    nb = HGRN_BATCH if batch % HGRN_BATCH == 0 else 1
    fixed = lambda b: (0, 0)
    return pl.pallas_call(
        functools.partial(_hgrn_kernel, seq=seq),
        grid=(batch // nb,),
        in_specs=[
            pl.BlockSpec((nb * seq, 3 * W_B), lambda b: (b, 0)),
            pl.BlockSpec((nb * seq, W_B), lambda b: (b, 0)),
            pl.BlockSpec((nb * seq, 3 * C_WIDTH), lambda b: (b, 0)),
            pl.BlockSpec((1, W_B), fixed),
            pl.BlockSpec((1, W_B), fixed),
            pl.BlockSpec((CONV_WIDTH, C_WIDTH), fixed),
            pl.BlockSpec((W_B, W_B), fixed),
        ],
        out_specs=pl.BlockSpec((nb * seq, W_B + C_WIDTH), lambda b: (b, 0)),
        out_shape=jax.ShapeDtypeStruct((t_tok, W_B + C_WIDTH), BF16),
        scratch_shapes=[
            pltpu.VMEM((nb, H_B * DV_B, W_B), F32),
            pltpu.VMEM((seq + SUBLANES, C_WIDTH), F32),
        ],
        compiler_params=_cparams(("parallel",)),
        name="hgrn_conv",
    )(hb, f, cv, lb, ng, cw, gavg)


def _outproj_router_kernel(oa_ref, obc_ref, x_ref, wo_ref, g_ref, wr_ref, br_ref,
                           xo_ref, route_ref, tab_ref, cnt_ref, carry_ref):
    i = pl.program_id(0)

    @pl.when(i == 0)
    def _():
        carry_ref[...] = jnp.zeros_like(carry_ref)

    mixed = (jnp.dot(oa_ref[...], wo_ref[0:W_A, :], preferred_element_type=F32)
             + jnp.dot(obc_ref[...], wo_ref[W_A:, :], preferred_element_type=F32))
    xn = x_ref[...] + mixed
    xo_ref[...] = xn

    ms = jnp.mean(xn * xn, axis=-1, keepdims=True)
    hn = (xn * lax.rsqrt(ms + EPS) * g_ref[...]).astype(BF16)
    logits = jnp.dot(hn, wr_ref[...], preferred_element_type=F32) + br_ref[...]

    tm = logits.shape[0]
    lane = lax.broadcasted_iota(jnp.int32, (tm, ROUTE_LANES), 1)
    neg = -jnp.inf
    big = ROUTE_LANES

    def top(cand):
        val = jnp.max(cand, axis=-1, keepdims=True)
        idx = jnp.min(jnp.where(cand == val, lane, big), axis=-1, keepdims=True)
        return val, idx

    glog = jnp.where(lane < N_GROUPS, logits, neg)
    gmax, gidx = top(glog)
    p_g = 1.0 / jnp.sum(jnp.exp(glog - gmax), axis=-1, keepdims=True)
    off = lane - (N_GROUPS + EXPERTS_PER_GROUP * gidx)
    elog = jnp.where(jnp.abs(2 * off - (EXPERTS_PER_GROUP - 1)) < EXPERTS_PER_GROUP, logits, neg)
    v1, i1 = top(elog)
    v2, i2 = top(jnp.where(lane == i1, neg, elog))
    t2 = jnp.exp(v2 - v1)
    w1 = p_g / (1.0 + t2)
    w2 = p_g * t2 / (1.0 + t2)

    sel1 = lane == i1
    sel2 = lane == i2
    onehot = jnp.where(sel1, 1.0, jnp.where(sel2, 1.0, 0.0))
    strict = (lax.broadcasted_iota(jnp.int32, (tm, tm), 1)
              < lax.broadcasted_iota(jnp.int32, (tm, tm), 0)).astype(BF16)
    local = jnp.dot(strict, onehot.astype(BF16), preferred_element_type=F32)
    cnt = jnp.sum(onehot, axis=0, keepdims=True)
    wrows = jnp.floor((cnt + (MOE_WINDOW - 1)) * (1.0 / MOE_WINDOW)) * MOE_WINDOW
    upper = (lax.broadcasted_iota(jnp.int32, (ROUTE_LANES, ROUTE_LANES), 0)
             < lax.broadcasted_iota(jnp.int32, (ROUTE_LANES, ROUTE_LANES), 1)).astype(BF16)
    soff = jnp.dot(jnp.broadcast_to(wrows, (SUBLANES, ROUTE_LANES)).astype(BF16), upper,
                   preferred_element_type=F32)[0:1, :]
    slot = local + soff
    row1 = jnp.sum(jnp.where(sel1, slot, 0.0), axis=-1, keepdims=True)
    row2 = jnp.sum(jnp.where(sel2, slot, 0.0), axis=-1, keepdims=True)

    rec = jnp.where(lane == 0, (i1 - N_GROUPS).astype(F32),
          jnp.where(lane == 1, (i2 - N_GROUPS).astype(F32),
          jnp.where(lane == 2, w1,
          jnp.where(lane == 3, w2,
          jnp.where(lane == 4, row1,
          jnp.where(lane == 5, row2, 0.0))))))
    route_ref[...] = rec
    tab_ref[0] = jnp.zeros((SUBLANES, ROUTE_LANES), F32)
    tab_ref[0, 0:1, :] = cnt
    tab_ref[0, 1:2, :] = carry_ref[0:1, :]
    carry_ref[0:1, :] = carry_ref[0:1, :] + jnp.floor((cnt + (SUBLANES - 1)) * (1.0 / SUBLANES)) * SUBLANES
    cnt_ref[...] = carry_ref[...]


def _outproj_router(o_a, o_bc, x2, wo, g, wr, br):
    t_tok = x2.shape[0]
    tm = min(TM_ROUTE, t_tok)
    row = lambda i: (i, 0)
    fixed = lambda i: (0, 0)
    return pl.pallas_call(
        _outproj_router_kernel,
        grid=(t_tok // tm,),
        in_specs=[
            pl.BlockSpec((tm, W_A), row),
            pl.BlockSpec((tm, W_B + C_WIDTH), row),
            pl.BlockSpec((tm, D_MODEL), row),
            pl.BlockSpec(wo.shape, fixed),
            pl.BlockSpec((1, D_MODEL), fixed),
            pl.BlockSpec(wr.shape, fixed),
            pl.BlockSpec((1, ROUTE_LANES), fixed),
        ],
        out_specs=[
            pl.BlockSpec((tm, D_MODEL), row),
            pl.BlockSpec((tm, ROUTE_LANES), row),
            pl.BlockSpec((1, SUBLANES, ROUTE_LANES), lambda i: (i, 0, 0)),
            pl.BlockSpec((SUBLANES, ROUTE_LANES), fixed),
        ],
        out_shape=[
            jax.ShapeDtypeStruct((t_tok, D_MODEL), F32),
            jax.ShapeDtypeStruct((t_tok, ROUTE_LANES), F32),
            jax.ShapeDtypeStruct((t_tok // tm, SUBLANES, ROUTE_LANES), F32),
            jax.ShapeDtypeStruct((SUBLANES, ROUTE_LANES), F32),
        ],
        scratch_shapes=[pltpu.VMEM((SUBLANES, ROUTE_LANES), F32)],
        compiler_params=_cparams(("arbitrary",)),
        name="outproj_router",
    )(o_a, o_bc, x2, wo, g, wr, br)


def _start_copies(n, make_copy):
    def start(k, carry):
        make_copy(k).start()
        return carry

    lax.fori_loop(0, n, start, 0)


def _wait_copies(n, make_copy):
    def wait(k, carry):
        make_copy(k).wait()
        return carry

    lax.fori_loop(0, n, wait, 0)


def _window_copy_loops(n, make_copy):
    _start_copies(n, make_copy)
    _wait_copies(n, make_copy)


def _dispatch_kernel(pend_ref, tab_ref, tabp_ref, route_ref, x_ref, buf_ref, xp_ref, xq_ref, sem, semq):
    tm = x_ref.shape[0]
    r_st = xp_ref.shape[0]
    step = pl.program_id(0)

    @pl.when(pl.program_id(0) == 0)
    def _():
        xp_ref[...] = jnp.zeros_like(xp_ref)
        copies = []
        for e in range(N_EXPERTS):
            end = pend_ref[e]
            copies.append((None, pltpu.make_async_copy(
                xp_ref.at[0:MOE_BLOCK, :],
                buf_ref.at[pl.ds(pl.multiple_of(end - MOE_BLOCK, SUBLANES), MOE_BLOCK), :], sem)))
            copies.append((end >= MOE_BLOCK + MOE_WINDOW, pltpu.make_async_copy(
                xp_ref.at[0:MOE_WINDOW, :],
                buf_ref.at[pl.ds(pl.multiple_of(jnp.maximum(end - MOE_BLOCK - MOE_WINDOW, 0), SUBLANES), MOE_WINDOW), :],
                sem)))
        for cond, cp in copies:
            if cond is None:
                cp.start()
            else:
                pl.when(cond)(cp.start)
        for cond, cp in copies:
            if cond is None:
                cp.wait()
            else:
                pl.when(cond)(cp.wait)

        def tail_copy(k):
            return pltpu.make_async_copy(
                xp_ref.at[0:MOE_BLOCK, :],
                buf_ref.at[pl.ds(pl.multiple_of(k * MOE_BLOCK, MOE_BLOCK), MOE_BLOCK), :], sem)

        first_unused = pend_ref[N_EXPERTS - 1] // MOE_BLOCK
        n_unused = buf_ref.shape[0] // MOE_BLOCK - first_unused
        _window_copy_loops(n_unused, lambda k: tail_copy(first_unused + k))

    route = route_ref[...]
    r0 = route[:, 4:5].astype(jnp.int32)
    r1 = route[:, 5:6].astype(jnp.int32)
    col = lax.broadcasted_iota(jnp.int32, (tm, r_st), 1)
    sel = jnp.where(col == r0, 1.0, jnp.where(col == r1, 1.0, 0.0)).astype(BF16)

    def window(tabs, stage_ref, s):
        def make_copy(k):
            src = pl.multiple_of(k * MOE_WINDOW, MOE_WINDOW)
            return pltpu.make_async_copy(
                stage_ref.at[pl.ds(src, MOE_WINDOW), :],
                buf_ref.at[pl.ds(pl.multiple_of(tabs[0, 0, k], SUBLANES), MOE_WINDOW), :], s)
        return tabs[0, 0, MAX_WINDOWS], make_copy

    def run(cur_ref, cur_sem, prev_ref, prev_sem):
        cur_ref[...] = _pack_rows(lax.dot_general(sel, x_ref[...].astype(BF16), (((0,), (0,)), ((), ())),
                                                  preferred_element_type=F32))

        @pl.when(step > 0)
        def _():
            _wait_copies(*window(tabp_ref, prev_ref, prev_sem))

        _start_copies(*window(tab_ref, cur_ref, cur_sem))

        @pl.when(step == pl.num_programs(0) - 1)
        def _():
            _wait_copies(*window(tab_ref, cur_ref, cur_sem))

    @pl.when(step % 2 == 0)
    def _():
        run(xp_ref, sem, xq_ref, semq)

    @pl.when(step % 2 == 1)
    def _():
        run(xq_ref, semq, xp_ref, sem)


def _dispatch(pad_end, tabs, route, x2, cap):
    t_tok = x2.shape[0]
    tm = min(TM_ROUTE, t_tok)
    n_tiles = t_tok // tm
    return pl.pallas_call(
        _dispatch_kernel,
        grid=(n_tiles,),
        in_specs=[
            pl.BlockSpec(memory_space=pltpu.SMEM),
            pl.BlockSpec((1, 1, ROUTE_LANES), lambda i: (i, 0, 0), memory_space=pltpu.SMEM),
            pl.BlockSpec((1, 1, ROUTE_LANES), lambda i: (jnp.maximum(i - 1, 0), 0, 0), memory_space=pltpu.SMEM),
            pl.BlockSpec((tm, ROUTE_LANES), lambda i: (i, 0)),
            pl.BlockSpec((tm, D_MODEL), lambda i: (i, 0)),
        ],
        out_specs=pl.BlockSpec(memory_space=pl.ANY),
        out_shape=jax.ShapeDtypeStruct((cap, D_PACKED), jnp.int32),
        scratch_shapes=[pltpu.VMEM((_staging_rows(tm), D_PACKED), jnp.int32),
                        pltpu.VMEM((_staging_rows(tm), D_PACKED), jnp.int32),
                        pltpu.SemaphoreType.DMA(()), pltpu.SemaphoreType.DMA(())],
        compiler_params=_cparams(("arbitrary",)),
        name="moe_dispatch",
    )(pad_end, tabs, tabs, route, x2)


def _experts_kernel(blk_e_ref, n_used_ref, x_ref, g_ref, wg_ref, wu_ref, wd_ref, y_ref):
    del blk_e_ref
    i = pl.program_id(0)

    @pl.when(i < n_used_ref[0])
    def _():
        wg = wg_ref[0, 0].astype(BF16)
        wu = wu_ref[0, 0].astype(BF16)
        wd = wd_ref[0, 0].astype(BF16)
        parts = _row_parts(x_ref.shape[0])
        hn = []
        for rows in parts:
            x = _unpack_rows(x_ref[rows, :])
            ms = jnp.mean(x * x, axis=-1, keepdims=True)
            hn.append((x * lax.rsqrt(ms + EPS) * g_ref[...]).astype(BF16))
        a = [jnp.dot(t, wg, preferred_element_type=F32) for t in hn]
        u = [jnp.dot(t, wu, preferred_element_type=F32) for t in hn]
        act = [(a[r] * jax.nn.sigmoid(a[r]) * u[r]).astype(BF16) for r in range(len(parts))]
        for r, rows in enumerate(parts):
            y = jnp.dot(act[r], wd, preferred_element_type=F32)
            y_ref[rows, :] = _pack_rows(y.astype(BF16).astype(F32))

    @pl.when(i >= n_used_ref[0])
    def _():
        y_ref[...] = jnp.zeros_like(y_ref)


def _experts(blk_e, n_used, buf, g, wg, wu, wd, layer):
    n_blocks = buf.shape[0] // MOE_BLOCK
    grid_spec = pltpu.PrefetchScalarGridSpec(
        num_scalar_prefetch=2,
        grid=(n_blocks,),
        in_specs=[
            pl.BlockSpec((MOE_BLOCK, D_PACKED), lambda i, be, nu: (i, 0)),
            pl.BlockSpec((1, D_MODEL), lambda i, be, nu: (0, 0)),
            pl.BlockSpec((1, 1, D_MODEL, D_EXPERT), lambda i, be, nu: (layer, be[i], 0, 0)),
            pl.BlockSpec((1, 1, D_MODEL, D_EXPERT), lambda i, be, nu: (layer, be[i], 0, 0)),
            pl.BlockSpec((1, 1, D_EXPERT, D_MODEL), lambda i, be, nu: (layer, be[i], 0, 0)),
        ],
        out_specs=pl.BlockSpec((MOE_BLOCK, D_PACKED), lambda i, be, nu: (i, 0)),
    )
    return pl.pallas_call(
        _experts_kernel,
        grid_spec=grid_spec,
        out_shape=jax.ShapeDtypeStruct(buf.shape, jnp.int32),
        compiler_params=_cparams(("arbitrary",)),
        name="moe_experts",
    )(blk_e, n_used, buf, g, wg, wu, wd)


def _combine_kernel(tab_ref, tabn_ref, route_ref, x_ref, y_ref, o_ref, ya_ref, yb_ref, sema, semb):
    tm = x_ref.shape[0]
    r_st = ya_ref.shape[0]
    step = pl.program_id(0)

    def window(tabs, stage_ref, s):
        def make_copy(k):
            dst = pl.multiple_of(k * MOE_WINDOW, MOE_WINDOW)
            return pltpu.make_async_copy(y_ref.at[pl.ds(pl.multiple_of(tabs[0, 0, k], SUBLANES), MOE_WINDOW), :],
                                         stage_ref.at[pl.ds(dst, MOE_WINDOW), :], s)
        return tabs[0, 0, MAX_WINDOWS], make_copy

    @pl.when(step == 0)
    def _():
        ya_ref[...] = jnp.zeros_like(ya_ref)
        yb_ref[...] = jnp.zeros_like(yb_ref)
        _start_copies(*window(tab_ref, ya_ref, sema))

    route = route_ref[...]
    r0 = route[:, 4:5].astype(jnp.int32)
    r1 = route[:, 5:6].astype(jnp.int32)
    col = lax.broadcasted_iota(jnp.int32, (tm, r_st), 1)
    gate = jnp.where(col == r0, route[:, 2:3], jnp.where(col == r1, route[:, 3:4], 0.0)).astype(BF16)

    def run(cur_ref, cur_sem, nxt_ref, nxt_sem):
        @pl.when(step + 1 < pl.num_programs(0))
        def _():
            _start_copies(*window(tabn_ref, nxt_ref, nxt_sem))

        _wait_copies(*window(tab_ref, cur_ref, cur_sem))
        o_ref[...] = x_ref[...] + jnp.dot(gate, _unpack_rows(cur_ref[...]).astype(BF16), preferred_element_type=F32)

    @pl.when(step % 2 == 0)
    def _():
        run(ya_ref, sema, yb_ref, semb)

    @pl.when(step % 2 == 1)
    def _():
        run(yb_ref, semb, ya_ref, sema)


def _combine(tabs, route, x2, y_buf):
    t_tok = x2.shape[0]
    tm = min(TM_ROUTE, t_tok)
    n_tiles = t_tok // tm
    return pl.pallas_call(
        _combine_kernel,
        grid=(n_tiles,),
        in_specs=[
            pl.BlockSpec((1, 1, ROUTE_LANES), lambda i: (i, 0, 0), memory_space=pltpu.SMEM),
            pl.BlockSpec((1, 1, ROUTE_LANES), lambda i: (jnp.minimum(i + 1, n_tiles - 1), 0, 0),
                         memory_space=pltpu.SMEM),
            pl.BlockSpec((tm, ROUTE_LANES), lambda i: (i, 0)),
            pl.BlockSpec((tm, D_MODEL), lambda i: (i, 0)),
            pl.BlockSpec(memory_space=pl.ANY),
        ],
        out_specs=pl.BlockSpec((tm, D_MODEL), lambda i: (i, 0)),
        out_shape=jax.ShapeDtypeStruct((t_tok, D_MODEL), F32),
        scratch_shapes=[pltpu.VMEM((_staging_rows(tm), D_PACKED), jnp.int32),
                        pltpu.VMEM((_staging_rows(tm), D_PACKED), jnp.int32),
                        pltpu.SemaphoreType.DMA(()), pltpu.SemaphoreType.DMA(())],
        compiler_params=_cparams(("arbitrary",)),
        name="moe_combine",
    )(tabs, tabs, route, x2, y_buf)


def _rope_tables(positions):
    half = ROT_DIM // 2
    inv_freq = ROPE_THETA ** (-jnp.arange(0, ROT_DIM, 2, dtype=F32) / ROT_DIM)
    d = jnp.arange(LANES) % DK_A
    freq_lane = jnp.where(d < ROT_DIM, inv_freq[d % half], 0.0)
    ang = positions.reshape(-1, 1).astype(F32) * freq_lane[None, :]
    cos, sin = jnp.cos(ang), jnp.sin(ang)
    s1 = jnp.where(d < half, -sin, 0.0)
    s2 = jnp.where((d >= half) & (d < ROT_DIM), sin, 0.0)
    return cos, s1, s2


def _block_avg(width, group):
    idx = jnp.arange(width) // group
    return jnp.where(idx[:, None] == idx[None, :], 1.0 / group, 0.0).astype(BF16)


def kernel(x, positions, norm_mix_g, w_in, attn_q_norm_g, attn_k_norm_g, lambda_q1, lambda_k1, lambda_q2,
           lambda_k2, attn_subln_g, hgrn_lower_bounds, hgrn_norm_g, conv_w, w_out, norm_ffn_g, router_group_w,
           router_group_b, router_expert_w, router_expert_b, expert_w_gate, expert_w_up, expert_w_down):
    batch, seq, d = x.shape
    assert d == D_MODEL
    depth = w_in.shape[0]
    t_tok = batch * seq
    n_assign = 2 * t_tok
    n_route_tiles = t_tok // min(TM_ROUTE, t_tok)
    n_blocks = (n_assign + (SUBLANES - 1) * n_route_tiles * N_EXPERTS) // MOE_BLOCK + 2 * N_EXPERTS + 1
    cap = n_blocks * MOE_BLOCK

    lb_probs = jax.nn.softmax(hgrn_lower_bounds.astype(F32), axis=0)
    lower_bounds = jnp.cumsum(lb_probs, axis=0) - lb_probs[0:1]
    cos_t, s1_t, s2_t = _rope_tables(positions)
    gsum_a = _block_avg(W_A, DK_A)
    gavg_b = _block_avg(W_B, DV_B)

    o_q, o_k, o_v = 0, W_A, 2 * W_A
    o_bq = 3 * W_A
    o_bf, o_bi, o_bg = o_bq + W_B, o_bq + 2 * W_B, o_bq + 3 * W_B
    o_c = o_bq + 4 * W_B

    x2 = x.reshape(t_tok, d)
    for l in range(depth):
        w = w_in[l]
        wqk = w[:, o_q:o_v].astype(BF16)
        wv = w[:, o_v:o_bq].astype(BF16)
        whb = jnp.concatenate([w[:, o_bq:o_bf], w[:, o_bi:o_c]], axis=1).astype(BF16)
        wf = w[:, o_bf:o_bi].astype(BF16)
        wc = w[:, o_c:].astype(BF16)
        gqk = jnp.concatenate([jnp.tile(attn_q_norm_g[l], 2 * H_A) * (DK_A ** -0.5 * LOG2_E),
                               jnp.tile(attn_k_norm_g[l], 2 * H_A)]).reshape(1, 2 * W_A).astype(F32)
        qk, v, hb, f, cv = _inproj(x2, norm_mix_g[l].reshape(1, d), wqk, wv, whb, wf, wc, gsum_a, gqk,
                                   cos_t, s1_t, s2_t)

        lam_init = 0.8 - 0.6 * math.exp(-0.3 * l)
        lam = (jnp.exp(jnp.sum(lambda_q1[l].astype(F32) * lambda_k1[l].astype(F32)))
               - jnp.exp(jnp.sum(lambda_q2[l].astype(F32) * lambda_k2[l].astype(F32))) + lam_init).reshape(1)
        o_a = _attention(qk, v, lam, attn_subln_g[l].reshape(DV_A, 1).astype(F32), batch, seq, lam_init)

        o_bc = _hgrn_conv(hb, f, cv, lower_bounds[l].reshape(1, W_B), jnp.tile(hgrn_norm_g[l], H_B).reshape(1, W_B),
                          conv_w[l], gavg_b, batch, seq)

        wr = jnp.concatenate([router_group_w[l], router_expert_w[l],
                              jnp.zeros((d, ROUTE_LANES - N_LOGITS), F32)], axis=1).astype(BF16)
        br = jnp.concatenate([router_group_b[l], router_expert_b[l],
                              jnp.zeros((ROUTE_LANES - N_LOGITS,), F32)]).reshape(1, ROUTE_LANES)
        x2, route, tab, cnt = _outproj_router(o_a, o_bc, x2, w_out[l].astype(BF16), norm_ffn_g[l].reshape(1, d), wr, br)

        counts = cnt[0, N_GROUPS:N_GROUPS + N_EXPERTS].astype(jnp.int32)
        padded = (counts + MOE_WINDOW + MOE_BLOCK - 1) // MOE_BLOCK * MOE_BLOCK
        pad_end = jnp.cumsum(padded)
        pad_start = pad_end - padded
        blk_pos = jnp.arange(n_blocks, dtype=jnp.int32) * MOE_BLOCK
        blk_e = jnp.minimum(jnp.sum((pad_end[None, :] <= blk_pos[:, None]).astype(jnp.int32), axis=1),
                            N_EXPERTS - 1)
        n_used = (pad_end[-1] // MOE_BLOCK).astype(jnp.int32).reshape(1)
        tile_cnt = tab[:, 0, N_GROUPS:N_GROUPS + N_EXPERTS].astype(jnp.int32)
        tile_base = tab[:, 1, N_GROUPS:N_GROUPS + N_EXPERTS].astype(jnp.int32) + pad_start[None, :]
        n_win = (tile_cnt + MOE_WINDOW - 1) // MOE_WINDOW
        w_end = jnp.cumsum(n_win, axis=1)
        w_start = w_end - n_win
        k_idx = jnp.arange(MAX_WINDOWS, dtype=jnp.int32)[None, :, None]
        owner = (w_start[:, None, :] <= k_idx) & (k_idx < w_end[:, None, :])
        win_dst = jnp.sum(jnp.where(owner, tile_base[:, None, :] + (k_idx - w_start[:, None, :]) * MOE_WINDOW, 0),
                          axis=-1)
        tabs = jnp.concatenate([win_dst, w_end[:, -1:],
                                jnp.zeros((win_dst.shape[0], ROUTE_LANES - MAX_WINDOWS - 1), jnp.int32)],
                               axis=1).reshape(-1, 1, ROUTE_LANES)

        buf = _dispatch(pad_end, tabs, route, x2, cap)
        y_buf = _experts(blk_e, n_used, buf, norm_ffn_g[l].reshape(1, d), expert_w_gate, expert_w_up,
                         expert_w_down, l)
        x2 = _combine(tabs, route, x2, y_buf)
    return x2.reshape(batch, seq, d)
```

```python
import functools
import math

import jax
import jax.numpy as jnp
from jax import lax
from jax.experimental import pallas as pl
from jax.experimental.pallas import tpu as pltpu

F32 = jnp.float32
BF16 = jnp.bfloat16

LANES = 128
SUBLANES = 8

D_MODEL = 1024
CHUNK = 64
EPS = 1e-6
MASK_VALUE = -1e30
TINY = 1e-30
LOG2_E = math.log2(math.e)
H_A = 4
DK_A = D_MODEL // 16
DV_A = 2 * DK_A
ROT_DIM = DK_A // 4
ROPE_THETA = 500000.0
H_B = 4
DK_B = D_MODEL // 16
DV_B = D_MODEL // 16
C_WIDTH = D_MODEL // 4
CONV_WIDTH = 3
N_GROUPS = 4
EXPERTS_PER_GROUP = 8
N_EXPERTS = N_GROUPS * EXPERTS_PER_GROUP
D_EXPERT = D_MODEL // 2
D_PACKED = D_MODEL // 2

W_A = H_A * DV_A
W_B = H_B * DK_B
ROUTE_LANES = LANES
N_LOGITS = N_GROUPS + N_EXPERTS

TM_PROJ = 1024
TK = 256
KV_PER_Q = 2
TQ = KV_PER_Q * TK
TM_ROUTE = 512
MOE_BLOCK = 512
HGRN_BATCH = 2
HGRN_UNROLL = 2
MOE_WINDOW = 16
MAX_WINDOWS = 2 * TM_ROUTE // MOE_WINDOW + N_EXPERTS
VMEM_LIMIT = 56 * 1024 * 1024


def _staging_rows(tm):
    return 2 * tm + N_EXPERTS * MOE_WINDOW


def _row_parts(rows, n=2):
    step = rows // n
    return [slice(r * step, (r + 1) * step) for r in range(n)]


def _pack_rows(x):
    half = x.shape[1] // 2
    lo = lax.bitcast_convert_type(x[:, :half], jnp.int32)
    hi = lax.bitcast_convert_type(x[:, half:], jnp.int32)
    return hi | lax.shift_right_logical(lo, 16)


def _unpack_rows(p):
    lo = lax.bitcast_convert_type(lax.shift_left(p, 16), F32)
    hi = lax.bitcast_convert_type(p & jnp.int32(-65536), F32)
    return jnp.concatenate([lo, hi], axis=1)


def _cparams(sem, vmem=VMEM_LIMIT):
    return pltpu.CompilerParams(dimension_semantics=sem, vmem_limit_bytes=vmem)


def _inproj_kernel(x_ref, g_ref, wqk_ref, wv_ref, whb_ref, wf_ref, wc_ref, gsum_ref, gqk_ref,
                   cos_ref, s1_ref, s2_ref, qk_out, v_out, hb_out, f_out, c_out):
    parts = _row_parts(x_ref.shape[0])
    xn = []
    for rows in parts:
        x = x_ref[rows, :]
        ms = jnp.mean(x * x, axis=-1, keepdims=True)
        xn.append((x * lax.rsqrt(ms + EPS) * g_ref[...]).astype(BF16))

    gsum = gsum_ref[...]
    for part in range(2):
        lo = part * W_A
        t = [jnp.dot(xr, wqk_ref[:, lo:lo + W_A], preferred_element_type=F32) for xr in xn]
        msq = [jnp.dot((tr * tr).astype(BF16), gsum, preferred_element_type=F32) for tr in t]
        for r, rows in enumerate(parts):
            tn = t[r] * lax.rsqrt(msq[r] + EPS) * gqk_ref[:, lo:lo + W_A]
            cos = cos_ref[rows, :]
            s1 = s1_ref[rows, :]
            s2 = s2_ref[rows, :]
            for h in range(H_A):
                slab = tn[:, h * LANES:(h + 1) * LANES]
                up = pltpu.roll(slab, LANES - ROT_DIM // 2, axis=1)
                dn = pltpu.roll(slab, ROT_DIM // 2, axis=1)
                qk_out[rows, lo + h * LANES:lo + (h + 1) * LANES] = (slab * cos + up * s1 + dn * s2).astype(BF16)

    for r, rows in enumerate(parts):
        v_out[rows, :] = jnp.dot(xn[r], wv_ref[...], preferred_element_type=F32).astype(BF16)
    for r, rows in enumerate(parts):
        hb_out[rows, :] = jnp.dot(xn[r], whb_ref[...], preferred_element_type=F32).astype(BF16)
    for r, rows in enumerate(parts):
        f_out[rows, :] = jnp.dot(xn[r], wf_ref[...], preferred_element_type=F32)
    for r, rows in enumerate(parts):
        c_out[rows, :] = jnp.dot(xn[r], wc_ref[...], preferred_element_type=F32).astype(BF16)


def _inproj(x2, g, wqk, wv, whb, wf, wc, gsum, gqk, cos_t, s1_t, s2_t):
    t_tok = x2.shape[0]
    tm = min(TM_PROJ, t_tok)
    grid = (t_tok // tm,)
    row = lambda i: (i, 0)
    fixed = lambda i: (0, 0)
    return pl.pallas_call(
        _inproj_kernel,
        grid=grid,
        in_specs=[
            pl.BlockSpec((tm, D_MODEL), row),
            pl.BlockSpec((1, D_MODEL), fixed),
            pl.BlockSpec(wqk.shape, fixed),
            pl.BlockSpec(wv.shape, fixed),
            pl.BlockSpec(whb.shape, fixed),
            pl.BlockSpec(wf.shape, fixed),
            pl.BlockSpec(wc.shape, fixed),
            pl.BlockSpec(gsum.shape, fixed),
            pl.BlockSpec(gqk.shape, fixed),
            pl.BlockSpec((tm, LANES), row),
            pl.BlockSpec((tm, LANES), row),
            pl.BlockSpec((tm, LANES), row),
        ],
        out_specs=[
            pl.BlockSpec((tm, 2 * W_A), row),
            pl.BlockSpec((tm, W_A), row),
            pl.BlockSpec((tm, 3 * W_B), row),
            pl.BlockSpec((tm, W_B), row),
            pl.BlockSpec((tm, 3 * C_WIDTH), row),
        ],
        out_shape=[
            jax.ShapeDtypeStruct((t_tok, 2 * W_A), BF16),
            jax.ShapeDtypeStruct((t_tok, W_A), BF16),
            jax.ShapeDtypeStruct((t_tok, 3 * W_B), BF16),
            jax.ShapeDtypeStruct((t_tok, W_B), F32),
            jax.ShapeDtypeStruct((t_tok, 3 * C_WIDTH), BF16),
        ],
        compiler_params=_cparams(("parallel",)),
        name="inproj",
    )(x2, g, wqk, wv, whb, wf, wc, gsum, gqk, cos_t, s1_t, s2_t)


def _attn_kernel(lam_ref, q_ref, k_ref, v_ref, g_ref, o_ref, qc_ref, sa_ref, sb_ref, acc_ref, m_ref, l_ref, *,
                 lam_init):
    qi = pl.program_id(1)
    q = q_ref[...]
    lane = lax.broadcasted_iota(jnp.int32, (TQ, DV_A), 1)
    zero = jnp.zeros((TQ, DV_A), BF16)
    for h in range(H_A):
        qh = q[:, h * DV_A:(h + 1) * DV_A]
        qc_ref[h, 0] = jnp.where(lane < DK_A, qh, zero)
        qc_ref[h, 1] = jnp.where(lane >= DK_A, qh, zero)

    acc_ref[...] = jnp.zeros_like(acc_ref)
    m_ref[...] = jnp.full_like(m_ref, -jnp.inf)
    l_ref[...] = jnp.zeros_like(l_ref)

    kchunk = lax.broadcasted_iota(jnp.int32, (TK, TQ), 0) // CHUNK
    qchunk = lax.broadcasted_iota(jnp.int32, (TK, TQ), 1) // CHUNK

    def scores(j, s_ref, q0=0):
        start = pl.multiple_of(j * TK, TK)
        kb = k_ref[pl.ds(start, TK), :]
        for h in range(H_A):
            kh = kb[:, h * DV_A:(h + 1) * DV_A]
            for c in range(2):
                s_ref[h, c, :, q0:] = lax.dot_general(kh, qc_ref[h, c, q0:, :], (((1,), (1,)), ((), ())),
                                                      preferred_element_type=F32)

    def consume(j, s_ref, mask, q0=0):
        start = pl.multiple_of(j * TK, TK)
        vb = v_ref[pl.ds(start, TK), :]
        for h in range(H_A):
            vh = vb[:, h * DV_A:(h + 1) * DV_A]
            for c in range(2):
                s = s_ref[h, c, :, q0:]
                if mask is not None:
                    s = jnp.where(mask[:, q0:], s, MASK_VALUE)
                m_old = m_ref[h, c, :, q0:]
                m_new = jnp.maximum(m_old, jnp.max(s, axis=0, keepdims=True))
                alpha = jnp.exp2(m_old - m_new)
                p = jnp.exp2(s - m_new)
                l_ref[h, c, :, q0:] = alpha * l_ref[h, c, :, q0:] + jnp.sum(p, axis=0, keepdims=True)
                pv = lax.dot_general(vh, p.astype(BF16), (((0,), (0,)), ((), ())),
                                     preferred_element_type=F32)
                acc_ref[h, c, :, q0:] = alpha * acc_ref[h, c, :, q0:] + pv
                m_ref[h, c, :, q0:] = m_new

    first_diag = KV_PER_Q * qi
    scores(0, sa_ref)

    def body(jj, carry):
        j = 2 * jj
        scores(j + 1, sb_ref)
        consume(j, sa_ref, None)
        scores(j + 2, sa_ref)
        consume(j + 1, sb_ref, None)
        return carry

    lax.fori_loop(0, first_diag // 2, body, 0)
    for d in range(KV_PER_Q):
        cur, nxt = (sa_ref, sb_ref) if d % 2 == 0 else (sb_ref, sa_ref)
        if d + 1 < KV_PER_Q:
            scores(first_diag + d + 1, nxt, (d + 1) * TK)
        consume(first_diag + d, cur, (kchunk + d * (TK // CHUNK)) <= qchunk, d * TK)

    lam = lam_ref[0]
    for h in range(H_A):
        o = acc_ref[h, 0] / l_ref[h, 0] - lam * (acc_ref[h, 1] / l_ref[h, 1])
        ms = jnp.mean(o * o, axis=0, keepdims=True)
        y = o * lax.rsqrt(ms + EPS) * g_ref[...] * (1.0 - lam_init)
        o_ref[:, h * DV_A:(h + 1) * DV_A] = y.T.astype(BF16)


def _attention(qk, v, lam, subln_col, batch, seq, lam_init):
    assert TQ == KV_PER_Q * TK and KV_PER_Q % 2 == 0 and TK % CHUNK == 0 and seq % TQ == 0
    nq = seq // TQ
    t_tok = batch * seq
    kernel = functools.partial(_attn_kernel, lam_init=lam_init)
    return pl.pallas_call(
        kernel,
        grid=(batch, nq),
        in_specs=[
            pl.BlockSpec(memory_space=pltpu.SMEM),
            pl.BlockSpec((TQ, W_A), lambda b, i: (b * nq + i, 0)),
            pl.BlockSpec((seq, W_A), lambda b, i: (b, 1)),
            pl.BlockSpec((seq, W_A), lambda b, i: (b, 0)),
            pl.BlockSpec((DV_A, 1), lambda b, i: (0, 0)),
        ],
        out_specs=pl.BlockSpec((TQ, W_A), lambda b, i: (b * nq + i, 0)),
        out_shape=jax.ShapeDtypeStruct((t_tok, W_A), BF16),
        scratch_shapes=[
            pltpu.VMEM((H_A, 2, TQ, DV_A), BF16),
            pltpu.VMEM((H_A, 2, TK, TQ), F32),
            pltpu.VMEM((H_A, 2, TK, TQ), F32),
            pltpu.VMEM((H_A, 2, DV_A, TQ), F32),
            pltpu.VMEM((H_A, 2, 1, TQ), F32),
            pltpu.VMEM((H_A, 2, 1, TQ), F32),
        ],
        compiler_params=_cparams(("parallel", "arbitrary")),
        name="diff_attn",
    )(lam, qk, qk, v, subln_col)


_LEVELS = (32, 16, 8, 4, 2, 1)


def _ref_rows(g, h):
    n = g.shape[1]
    if 2 * h >= SUBLANES:
        pieces = []
        for m in range(CHUNK // (2 * h)):
            r = 2 * h * m + h - 1
            pieces.append(jnp.broadcast_to(g[r:r + 1, :], (2 * h, n)))
        return pieces[0] if len(pieces) == 1 else jnp.concatenate(pieces, axis=0)
    g3 = g.reshape(CHUNK // SUBLANES, SUBLANES, n)
    sub = lax.broadcasted_iota(jnp.int32, g3.shape, 1)
    out = None
    for m in reversed(range(SUBLANES // (2 * h))):
        r = 2 * h * m + h - 1
        piece = jnp.broadcast_to(g3[:, r:r + 1, :], g3.shape)
        out = piece if out is None else jnp.where(sub < 2 * h * (m + 1), piece, out)
    return out.reshape(CHUNK, n)


def _stack_heads(a, lane_head):
    zero = jnp.zeros_like(a)
    return jnp.concatenate([jnp.where(lane_head == hh, a, zero) for hh in range(H_B)], axis=0)


def _hgrn_kernel(hb_ref, f_ref, c_ref, lb_ref, ng_ref, cw_ref, gavg_ref, o_ref, state_ref, hbuf_ref, *, seq):
    n_chunks = seq // CHUNK
    nb = hb_ref.shape[0] // seq
    lb = lb_ref[...]
    ng = ng_ref[...]
    gavg = gavg_ref[...]

    row = lax.broadcasted_iota(jnp.int32, (CHUNK, W_B), 0)
    lane = lax.broadcasted_iota(jnp.int32, (CHUNK, W_B), 1)
    lane_head = lane // DK_B
    lane_s = lane % CHUNK
    tri = (lax.broadcasted_iota(jnp.int32, (CHUNK, CHUNK), 1)
           <= lax.broadcasted_iota(jnp.int32, (CHUNK, CHUNK), 0)).astype(BF16)
    lvl_mask = [((row // (2 * h)) == (lane_s // (2 * h))) & ((row % (2 * h)) >= h) & ((lane_s % (2 * h)) < h)
                for h in _LEVELS]
    eye_mask = row == lane_s
    r4 = lax.broadcasted_iota(jnp.int32, (H_B * DK_B, W_B), 0) // DK_B
    c4 = lax.broadcasted_iota(jnp.int32, (H_B * DK_B, W_B), 1) // DK_B
    bd_mask = r4 == c4

    state_ref[...] = jnp.zeros_like(state_ref)
    nt = (((1,), (1,)), ((), ()))
    tn = (((0,), (0,)), ((), ()))
    unroll = HGRN_UNROLL if n_chunks % HGRN_UNROLL == 0 else 1
    each = range(nb * unroll)

    def chunk(i, carry):
        rows = [pl.ds(pl.multiple_of((e % nb) * seq + (i * unroll + e // nb) * CHUNK, CHUNK), CHUNK) for e in each]
        hb = [hb_ref[rows[j], :].astype(F32) for j in each]
        zq = [t[:, 0:W_B] for t in hb]
        v16 = [t[:, W_B:2 * W_B].astype(BF16) for t in hb]
        zg = [t[:, 2 * W_B:3 * W_B] for t in hb]
        zf = [f_ref[rows[j], :] for j in each]

        logf = [jnp.log(jnp.maximum(lb + (1.0 - lb) * jax.nn.sigmoid(z), TINY)) for z in zf]
        key = [(1.0 - lb) * jax.nn.sigmoid(-z) for z in zf]
        q = [z * jax.nn.sigmoid(z) for z in zq]

        g = []
        for x in logf:
            hi = x.astype(BF16)
            r1 = x - hi.astype(F32)
            mid = r1.astype(BF16)
            lo = (r1 - mid.astype(F32)).astype(BF16)
            g.append((jnp.dot(tri, hi, preferred_element_type=F32) + jnp.dot(tri, mid, preferred_element_type=F32)
                      + jnp.dot(tri, lo, preferred_element_type=F32)) * LOG2_E)

        q16 = [t.astype(BF16) for t in q]
        k16 = [t.astype(BF16) for t in key]
        a_all = [jnp.where(eye_mask, lax.dot_general(q16[j], _stack_heads(k16[j], lane_head),
                                                     nt, preferred_element_type=F32), 0.0) for j in each]
        for h, mask in zip(_LEVELS, lvl_mask):
            w = [jnp.exp2(-jnp.abs(g[j] - _ref_rows(g[j], h))).astype(BF16) for j in each]
            ah = [lax.dot_general(q16[j] * w[j], _stack_heads(k16[j] * w[j], lane_head),
                                  nt, preferred_element_type=F32) for j in each]
            a_all = [a_all[j] + jnp.where(mask, ah[j], 0.0) for j in each]
        o = [jnp.dot(a_all[j].astype(BF16), _stack_heads(v16[j], lane_head), preferred_element_type=F32) for j in each]

        qg = [(q[j] * jnp.exp2(g[j])).astype(BF16) for j in each]
        g_last = [t[CHUNK - 1:CHUNK, :] for t in g]
        upd = [lax.dot_general(v16[j], (key[j] * jnp.exp2(g_last[j] - g[j])).astype(BF16), tn,
                               preferred_element_type=F32) for j in each]
        for j in each:
            st = state_ref[j % nb]
            o[j] = o[j] + lax.dot_general(qg[j], st.astype(BF16), nt, preferred_element_type=F32)
            state_ref[j % nb] = jnp.exp2(g_last[j]) * st + jnp.where(bd_mask, upd[j], 0.0)

        ms = [jnp.dot((t * t).astype(BF16), gavg, preferred_element_type=F32) for t in o]
        for j in each:
            y = o[j] * lax.rsqrt(ms[j] + EPS) * ng * (zg[j] * jax.nn.sigmoid(zg[j]))
            o_ref[rows[j], 0:W_B] = y.astype(BF16)
        return carry

    lax.fori_loop(0, n_chunks // unroll, chunk, 0)

    piece = min(512, seq)
    hbuf_ref[0:SUBLANES, :] = jnp.zeros((SUBLANES, C_WIDTH), F32)
    for j in range(nb):
        for p in range(seq // piece):
            cblk = c_ref[j * seq + p * piece:j * seq + (p + 1) * piece, :].astype(F32)
            hbuf_ref[SUBLANES + p * piece:SUBLANES + (p + 1) * piece, :] = (cblk[:, C_WIDTH:2 * C_WIDTH]
                                                                             * cblk[:, 2 * C_WIDTH:])
        for p in range(seq // piece):
            base = SUBLANES + p * piece
            y = (cw_ref[0:1, :] * hbuf_ref[base - 2:base - 2 + piece, :]
                 + cw_ref[1:2, :] * hbuf_ref[base - 1:base - 1 + piece, :]
                 + cw_ref[2:3, :] * hbuf_ref[base:base + piece, :])
            lo_r = j * seq + p * piece
            bgate = c_ref[lo_r:lo_r + piece, 0:C_WIDTH].astype(F32)
            o_ref[lo_r:lo_r + piece, W_B:W_B + C_WIDTH] = (bgate * y).astype(BF16)


def _hgrn_conv(hb, f, cv, lb, ng, cw, gavg, batch, seq):
    t_tok = batch * seq
    nb = HGRN_BATCH if batch % HGRN_BATCH == 0 else 1
    fixed = lambda b: (0, 0)
    return pl.pallas_call(
        functools.partial(_hgrn_kernel, seq=seq),
        grid=(batch // nb,),
        in_specs=[
            pl.BlockSpec((nb * seq, 3 * W_B), lambda b: (b, 0)),
            pl.BlockSpec((nb * seq, W_B), lambda b: (b, 0)),
            pl.BlockSpec((nb * seq, 3 * C_WIDTH), lambda b: (b, 0)),
            pl.BlockSpec((1, W_B), fixed),
            pl.BlockSpec((1, W_B), fixed),
            pl.BlockSpec((CONV_WIDTH, C_WIDTH), fixed),
            pl.BlockSpec((W_B, W_B), fixed),
        ],
        out_specs=pl.BlockSpec((nb * seq, W_B + C_WIDTH), lambda b: (b, 0)),
        out_shape=jax.ShapeDtypeStruct((t_tok, W_B + C_WIDTH), BF16),
        scratch_shapes=[
            pltpu.VMEM((nb, H_B * DV_B, W_B), F32),
            pltpu.VMEM((seq + SUBLANES, C_WIDTH), F32),
        ],
        compiler_params=_cparams(("parallel",)),
        name="hgrn_conv",
    )(hb, f, cv, lb, ng, cw, gavg)


def _outproj_router_kernel(oa_ref, obc_ref, x_ref, wo_ref, g_ref, wr_ref, br_ref,
                           xo_ref, route_ref, tab_ref, cnt_ref, carry_ref):
    i = pl.program_id(0)

    @pl.when(i == 0)
    def _():
        carry_ref[...] = jnp.zeros_like(carry_ref)

    mixed = (jnp.dot(oa_ref[...], wo_ref[0:W_A, :], preferred_element_type=F32)
             + jnp.dot(obc_ref[...], wo_ref[W_A:, :], preferred_element_type=F32))
    xn = x_ref[...] + mixed
    xo_ref[...] = xn

    ms = jnp.mean(xn * xn, axis=-1, keepdims=True)
    hn = (xn * lax.rsqrt(ms + EPS) * g_ref[...]).astype(BF16)
    logits = jnp.dot(hn, wr_ref[...], preferred_element_type=F32) + br_ref[...]

    tm = logits.shape[0]
    lane = lax.broadcasted_iota(jnp.int32, (tm, ROUTE_LANES), 1)
    neg = -jnp.inf
    big = ROUTE_LANES

    def top(cand):
        val = jnp.max(cand, axis=-1, keepdims=True)
        idx = jnp.min(jnp.where(cand == val, lane, big), axis=-1, keepdims=True)
        return val, idx

    glog = jnp.where(lane < N_GROUPS, logits, neg)
    gmax, gidx = top(glog)
    p_g = 1.0 / jnp.sum(jnp.exp(glog - gmax), axis=-1, keepdims=True)
    off = lane - (N_GROUPS + EXPERTS_PER_GROUP * gidx)
    elog = jnp.where(jnp.abs(2 * off - (EXPERTS_PER_GROUP - 1)) < EXPERTS_PER_GROUP, logits, neg)
    v1, i1 = top(elog)
    v2, i2 = top(jnp.where(lane == i1, neg, elog))
    t2 = jnp.exp(v2 - v1)
    w1 = p_g / (1.0 + t2)
    w2 = p_g * t2 / (1.0 + t2)

    sel1 = lane == i1
    sel2 = lane == i2
    onehot = jnp.where(sel1, 1.0, jnp.where(sel2, 1.0, 0.0))
    strict = (lax.broadcasted_iota(jnp.int32, (tm, tm), 1)
              < lax.broadcasted_iota(jnp.int32, (tm, tm), 0)).astype(BF16)
    local = jnp.dot(strict, onehot.astype(BF16), preferred_element_type=F32)
    cnt = jnp.sum(onehot, axis=0, keepdims=True)
    wrows = jnp.floor((cnt + (MOE_WINDOW - 1)) * (1.0 / MOE_WINDOW)) * MOE_WINDOW
    upper = (lax.broadcasted_iota(jnp.int32, (ROUTE_LANES, ROUTE_LANES), 0)
             < lax.broadcasted_iota(jnp.int32, (ROUTE_LANES, ROUTE_LANES), 1)).astype(BF16)
    soff = jnp.dot(jnp.broadcast_to(wrows, (SUBLANES, ROUTE_LANES)).astype(BF16), upper,
                   preferred_element_type=F32)[0:1, :]
    slot = local + soff
    row1 = jnp.sum(jnp.where(sel1, slot, 0.0), axis=-1, keepdims=True)
    row2 = jnp.sum(jnp.where(sel2, slot, 0.0), axis=-1, keepdims=True)

    rec = jnp.where(lane == 0, (i1 - N_GROUPS).astype(F32),
          jnp.where(lane == 1, (i2 - N_GROUPS).astype(F32),
          jnp.where(lane == 2, w1,
          jnp.where(lane == 3, w2,
          jnp.where(lane == 4, row1,
          jnp.where(lane == 5, row2, 0.0))))))
    route_ref[...] = rec
    tab_ref[0] = jnp.zeros((SUBLANES, ROUTE_LANES), F32)
    tab_ref[0, 0:1, :] = cnt
    tab_ref[0, 1:2, :] = carry_ref[0:1, :]
    carry_ref[0:1, :] = carry_ref[0:1, :] + jnp.floor((cnt + (SUBLANES - 1)) * (1.0 / SUBLANES)) * SUBLANES
    cnt_ref[...] = carry_ref[...]


def _outproj_router(o_a, o_bc, x2, wo, g, wr, br):
    t_tok = x2.shape[0]
    tm = min(TM_ROUTE, t_tok)
    row = lambda i: (i, 0)
    fixed = lambda i: (0, 0)
    return pl.pallas_call(
        _outproj_router_kernel,
        grid=(t_tok // tm,),
        in_specs=[
            pl.BlockSpec((tm, W_A), row),
            pl.BlockSpec((tm, W_B + C_WIDTH), row),
            pl.BlockSpec((tm, D_MODEL), row),
            pl.BlockSpec(wo.shape, fixed),
            pl.BlockSpec((1, D_MODEL), fixed),
            pl.BlockSpec(wr.shape, fixed),
            pl.BlockSpec((1, ROUTE_LANES), fixed),
        ],
        out_specs=[
            pl.BlockSpec((tm, D_MODEL), row),
            pl.BlockSpec((tm, ROUTE_LANES), row),
            pl.BlockSpec((1, SUBLANES, ROUTE_LANES), lambda i: (i, 0, 0)),
            pl.BlockSpec((SUBLANES, ROUTE_LANES), fixed),
        ],
        out_shape=[
            jax.ShapeDtypeStruct((t_tok, D_MODEL), F32),
            jax.ShapeDtypeStruct((t_tok, ROUTE_LANES), F32),
            jax.ShapeDtypeStruct((t_tok // tm, SUBLANES, ROUTE_LANES), F32),
            jax.ShapeDtypeStruct((SUBLANES, ROUTE_LANES), F32),
        ],
        scratch_shapes=[pltpu.VMEM((SUBLANES, ROUTE_LANES), F32)],
        compiler_params=_cparams(("arbitrary",)),
        name="outproj_router",
    )(o_a, o_bc, x2, wo, g, wr, br)


def _start_copies(n, make_copy):
    def start(k, carry):
        make_copy(k).start()
        return carry

    lax.fori_loop(0, n, start, 0)


def _wait_copies(n, make_copy):
    def wait(k, carry):
        make_copy(k).wait()
        return carry

    lax.fori_loop(0, n, wait, 0)


def _window_copy_loops(n, make_copy):
    _start_copies(n, make_copy)
    _wait_copies(n, make_copy)


def _dispatch_kernel(pend_ref, tab_ref, tabp_ref, route_ref, x_ref, buf_ref, xp_ref, xq_ref, sem, semq):
    tm = x_ref.shape[0]
    r_st = xp_ref.shape[0]
    step = pl.program_id(0)

    @pl.when(pl.program_id(0) == 0)
    def _():
        xp_ref[...] = jnp.zeros_like(xp_ref)
        copies = []
        for e in range(N_EXPERTS):
            end = pend_ref[e]
            copies.append((None, pltpu.make_async_copy(
                xp_ref.at[0:MOE_BLOCK, :],
                buf_ref.at[pl.ds(pl.multiple_of(end - MOE_BLOCK, SUBLANES), MOE_BLOCK), :], sem)))
            copies.append((end >= MOE_BLOCK + MOE_WINDOW, pltpu.make_async_copy(
                xp_ref.at[0:MOE_WINDOW, :],
                buf_ref.at[pl.ds(pl.multiple_of(jnp.maximum(end - MOE_BLOCK - MOE_WINDOW, 0), SUBLANES), MOE_WINDOW), :],
                sem)))
        for cond, cp in copies:
            if cond is None:
                cp.start()
            else:
                pl.when(cond)(cp.start)
        for cond, cp in copies:
            if cond is None:
                cp.wait()
            else:
                pl.when(cond)(cp.wait)

        def tail_copy(k):
            return pltpu.make_async_copy(
                xp_ref.at[0:MOE_BLOCK, :],
                buf_ref.at[pl.ds(pl.multiple_of(k * MOE_BLOCK, MOE_BLOCK), MOE_BLOCK), :], sem)

        first_unused = pend_ref[N_EXPERTS - 1] // MOE_BLOCK
        n_unused = buf_ref.shape[0] // MOE_BLOCK - first_unused
        _window_copy_loops(n_unused, lambda k: tail_copy(first_unused + k))

    route = route_ref[...]
    r0 = route[:, 4:5].astype(jnp.int32)
    r1 = route[:, 5:6].astype(jnp.int32)
    col = lax.broadcasted_iota(jnp.int32, (tm, r_st), 1)
    sel = jnp.where(col == r0, 1.0, jnp.where(col == r1, 1.0, 0.0)).astype(BF16)

    def window(tabs, stage_ref, s):
        def make_copy(k):
            src = pl.multiple_of(k * MOE_WINDOW, MOE_WINDOW)
            return pltpu.make_async_copy(
                stage_ref.at[pl.ds(src, MOE_WINDOW), :],
                buf_ref.at[pl.ds(pl.multiple_of(tabs[0, 0, k], SUBLANES), MOE_WINDOW), :], s)
        return tabs[0, 0, MAX_WINDOWS], make_copy

    def run(cur_ref, cur_sem, prev_ref, prev_sem):
        cur_ref[...] = _pack_rows(lax.dot_general(sel, x_ref[...].astype(BF16), (((0,), (0,)), ((), ())),
                                                  preferred_element_type=F32))

        @pl.when(step > 0)
        def _():
            _wait_copies(*window(tabp_ref, prev_ref, prev_sem))

        _start_copies(*window(tab_ref, cur_ref, cur_sem))

        @pl.when(step == pl.num_programs(0) - 1)
        def _():
            _wait_copies(*window(tab_ref, cur_ref, cur_sem))

    @pl.when(step % 2 == 0)
    def _():
        run(xp_ref, sem, xq_ref, semq)

    @pl.when(step % 2 == 1)
    def _():
        run(xq_ref, semq, xp_ref, sem)


def _dispatch(pad_end, tabs, route, x2, cap):
    t_tok = x2.shape[0]
    tm = min(TM_ROUTE, t_tok)
    n_tiles = t_tok // tm
    return pl.pallas_call(
        _dispatch_kernel,
        grid=(n_tiles,),
        in_specs=[
            pl.BlockSpec(memory_space=pltpu.SMEM),
            pl.BlockSpec((1, 1, ROUTE_LANES), lambda i: (i, 0, 0), memory_space=pltpu.SMEM),
            pl.BlockSpec((1, 1, ROUTE_LANES), lambda i: (jnp.maximum(i - 1, 0), 0, 0), memory_space=pltpu.SMEM),
            pl.BlockSpec((tm, ROUTE_LANES), lambda i: (i, 0)),
            pl.BlockSpec((tm, D_MODEL), lambda i: (i, 0)),
        ],
        out_specs=pl.BlockSpec(memory_space=pl.ANY),
        out_shape=jax.ShapeDtypeStruct((cap, D_PACKED), jnp.int32),
        scratch_shapes=[pltpu.VMEM((_staging_rows(tm), D_PACKED), jnp.int32),
                        pltpu.VMEM((_staging_rows(tm), D_PACKED), jnp.int32),
                        pltpu.SemaphoreType.DMA(()), pltpu.SemaphoreType.DMA(())],
        compiler_params=_cparams(("arbitrary",)),
        name="moe_dispatch",
    )(pad_end, tabs, tabs, route, x2)


def _experts_kernel(blk_e_ref, n_used_ref, x_ref, g_ref, wg_ref, wu_ref, wd_ref, y_ref, wg16, wu16, wd16):
    i = pl.program_id(0)
    used = i < n_used_ref[0]

    @pl.when(used & ((i == 0) | (blk_e_ref[i] != blk_e_ref[jnp.maximum(i - 1, 0)])))
    def _():
        wg16[...] = wg_ref[0, 0].astype(BF16)
        wu16[...] = wu_ref[0, 0].astype(BF16)
        wd16[...] = wd_ref[0, 0].astype(BF16)

    @pl.when(used)
    def _():
        wg = wg16[...]
        wu = wu16[...]
        wd = wd16[...]
        parts = _row_parts(x_ref.shape[0])
        hn = []
        for rows in parts:
            x = _unpack_rows(x_ref[rows, :])
            ms = jnp.mean(x * x, axis=-1, keepdims=True)
            hn.append((x * lax.rsqrt(ms + EPS) * g_ref[...]).astype(BF16))
        a = [jnp.dot(t, wg, preferred_element_type=F32) for t in hn]
        u = [jnp.dot(t, wu, preferred_element_type=F32) for t in hn]
        act = [(a[r] * jax.nn.sigmoid(a[r]) * u[r]).astype(BF16) for r in range(len(parts))]
        for r, rows in enumerate(parts):
            y = jnp.dot(act[r], wd, preferred_element_type=F32)
            y_ref[rows, :] = _pack_rows(y.astype(BF16).astype(F32))


def _experts(blk_e, n_used, buf, g, wg, wu, wd, layer):
    n_blocks = buf.shape[0] // MOE_BLOCK
    grid_spec = pltpu.PrefetchScalarGridSpec(
        num_scalar_prefetch=2,
        grid=(n_blocks,),
        in_specs=[
            pl.BlockSpec((MOE_BLOCK, D_PACKED), lambda i, be, nu: (jnp.minimum(i, nu[0] - 1), 0)),
            pl.BlockSpec((1, D_MODEL), lambda i, be, nu: (0, 0)),
            pl.BlockSpec((1, 1, D_MODEL, D_EXPERT), lambda i, be, nu: (layer, be[i], 0, 0)),
            pl.BlockSpec((1, 1, D_MODEL, D_EXPERT), lambda i, be, nu: (layer, be[i], 0, 0)),
            pl.BlockSpec((1, 1, D_EXPERT, D_MODEL), lambda i, be, nu: (layer, be[i], 0, 0)),
        ],
        out_specs=pl.BlockSpec((MOE_BLOCK, D_PACKED), lambda i, be, nu: (jnp.minimum(i, nu[0] - 1), 0)),
        scratch_shapes=[pltpu.VMEM((D_MODEL, D_EXPERT), BF16), pltpu.VMEM((D_MODEL, D_EXPERT), BF16),
                        pltpu.VMEM((D_EXPERT, D_MODEL), BF16)],
    )
    return pl.pallas_call(
        _experts_kernel,
        grid_spec=grid_spec,
        out_shape=jax.ShapeDtypeStruct(buf.shape, jnp.int32),
        input_output_aliases={2: 0},
        compiler_params=_cparams(("arbitrary",)),
        name="moe_experts",
    )(blk_e, n_used, buf, g, wg, wu, wd)


def _combine_kernel(tab_ref, tabn_ref, route_ref, x_ref, y_ref, o_ref, ya_ref, yb_ref, sema, semb):
    tm = x_ref.shape[0]
    r_st = ya_ref.shape[0]
    step = pl.program_id(0)

    def window(tabs, stage_ref, s):
        def make_copy(k):
            dst = pl.multiple_of(k * MOE_WINDOW, MOE_WINDOW)
            return pltpu.make_async_copy(y_ref.at[pl.ds(pl.multiple_of(tabs[0, 0, k], SUBLANES), MOE_WINDOW), :],
                                         stage_ref.at[pl.ds(dst, MOE_WINDOW), :], s)
        return tabs[0, 0, MAX_WINDOWS], make_copy

    @pl.when(step == 0)
    def _():
        ya_ref[...] = jnp.zeros_like(ya_ref)
        yb_ref[...] = jnp.zeros_like(yb_ref)
        _start_copies(*window(tab_ref, ya_ref, sema))

    route = route_ref[...]
    r0 = route[:, 4:5].astype(jnp.int32)
    r1 = route[:, 5:6].astype(jnp.int32)
    col = lax.broadcasted_iota(jnp.int32, (tm, r_st), 1)
    gate = jnp.where(col == r0, route[:, 2:3], jnp.where(col == r1, route[:, 3:4], 0.0)).astype(BF16)

    def run(cur_ref, cur_sem, nxt_ref, nxt_sem):
        @pl.when(step + 1 < pl.num_programs(0))
        def _():
            _start_copies(*window(tabn_ref, nxt_ref, nxt_sem))

        _wait_copies(*window(tab_ref, cur_ref, cur_sem))
        o_ref[...] = x_ref[...] + jnp.dot(gate, _unpack_rows(cur_ref[...]).astype(BF16), preferred_element_type=F32)

    @pl.when(step % 2 == 0)
    def _():
        run(ya_ref, sema, yb_ref, semb)

    @pl.when(step % 2 == 1)
    def _():
        run(yb_ref, semb, ya_ref, sema)


def _combine(tabs, route, x2, y_buf):
    t_tok = x2.shape[0]
    tm = min(TM_ROUTE, t_tok)
    n_tiles = t_tok // tm
    return pl.pallas_call(
        _combine_kernel,
        grid=(n_tiles,),
        in_specs=[
            pl.BlockSpec((1, 1, ROUTE_LANES), lambda i: (i, 0, 0), memory_space=pltpu.SMEM),
            pl.BlockSpec((1, 1, ROUTE_LANES), lambda i: (jnp.minimum(i + 1, n_tiles - 1), 0, 0),
                         memory_space=pltpu.SMEM),
            pl.BlockSpec((tm, ROUTE_LANES), lambda i: (i, 0)),
            pl.BlockSpec((tm, D_MODEL), lambda i: (i, 0)),
            pl.BlockSpec(memory_space=pl.ANY),
        ],
        out_specs=pl.BlockSpec((tm, D_MODEL), lambda i: (i, 0)),
        out_shape=jax.ShapeDtypeStruct((t_tok, D_MODEL), F32),
        scratch_shapes=[pltpu.VMEM((_staging_rows(tm), D_PACKED), jnp.int32),
                        pltpu.VMEM((_staging_rows(tm), D_PACKED), jnp.int32),
                        pltpu.SemaphoreType.DMA(()), pltpu.SemaphoreType.DMA(())],
        compiler_params=_cparams(("arbitrary",)),
        name="moe_combine",
    )(tabs, tabs, route, x2, y_buf)


def _rope_tables(positions):
    half = ROT_DIM // 2
    inv_freq = ROPE_THETA ** (-jnp.arange(0, ROT_DIM, 2, dtype=F32) / ROT_DIM)
    d = jnp.arange(LANES) % DK_A
    freq_lane = jnp.where(d < ROT_DIM, inv_freq[d % half], 0.0)
    ang = positions.reshape(-1, 1).astype(F32) * freq_lane[None, :]
    cos, sin = jnp.cos(ang), jnp.sin(ang)
    s1 = jnp.where(d < half, -sin, 0.0)
    s2 = jnp.where((d >= half) & (d < ROT_DIM), sin, 0.0)
    return cos, s1, s2


def _block_avg(width, group):
    idx = jnp.arange(width) // group
    return jnp.where(idx[:, None] == idx[None, :], 1.0 / group, 0.0).astype(BF16)


def kernel(x, positions, norm_mix_g, w_in, attn_q_norm_g, attn_k_norm_g, lambda_q1, lambda_k1, lambda_q2,
           lambda_k2, attn_subln_g, hgrn_lower_bounds, hgrn_norm_g, conv_w, w_out, norm_ffn_g, router_group_w,
           router_group_b, router_expert_w, router_expert_b, expert_w_gate, expert_w_up, expert_w_down):
    batch, seq, d = x.shape
    assert d == D_MODEL
    depth = w_in.shape[0]
    t_tok = batch * seq
    n_assign = 2 * t_tok
    n_route_tiles = t_tok // min(TM_ROUTE, t_tok)
    n_blocks = (n_assign + (SUBLANES - 1) * n_route_tiles * N_EXPERTS) // MOE_BLOCK + 2 * N_EXPERTS + 1
    cap = n_blocks * MOE_BLOCK

    lb_probs = jax.nn.softmax(hgrn_lower_bounds.astype(F32), axis=0)
    lower_bounds = jnp.cumsum(lb_probs, axis=0) - lb_probs[0:1]
    cos_t, s1_t, s2_t = _rope_tables(positions)
    gsum_a = _block_avg(W_A, DK_A)
    gavg_b = _block_avg(W_B, DV_B)

    o_q, o_k, o_v = 0, W_A, 2 * W_A
    o_bq = 3 * W_A
    o_bf, o_bi, o_bg = o_bq + W_B, o_bq + 2 * W_B, o_bq + 3 * W_B
    o_c = o_bq + 4 * W_B

    x2 = x.reshape(t_tok, d)
    for l in range(depth):
        w = w_in[l]
        wqk = w[:, o_q:o_v].astype(BF16)
        wv = w[:, o_v:o_bq].astype(BF16)
        whb = jnp.concatenate([w[:, o_bq:o_bf], w[:, o_bi:o_c]], axis=1).astype(BF16)
        wf = w[:, o_bf:o_bi].astype(BF16)
        wc = w[:, o_c:].astype(BF16)
        gqk = jnp.concatenate([jnp.tile(attn_q_norm_g[l], 2 * H_A) * (DK_A ** -0.5 * LOG2_E),
                               jnp.tile(attn_k_norm_g[l], 2 * H_A)]).reshape(1, 2 * W_A).astype(F32)
        qk, v, hb, f, cv = _inproj(x2, norm_mix_g[l].reshape(1, d), wqk, wv, whb, wf, wc, gsum_a, gqk,
                                   cos_t, s1_t, s2_t)

        lam_init = 0.8 - 0.6 * math.exp(-0.3 * l)
        lam = (jnp.exp(jnp.sum(lambda_q1[l].astype(F32) * lambda_k1[l].astype(F32)))
               - jnp.exp(jnp.sum(lambda_q2[l].astype(F32) * lambda_k2[l].astype(F32))) + lam_init).reshape(1)
        o_a = _attention(qk, v, lam, attn_subln_g[l].reshape(DV_A, 1).astype(F32), batch, seq, lam_init)

        o_bc = _hgrn_conv(hb, f, cv, lower_bounds[l].reshape(1, W_B), jnp.tile(hgrn_norm_g[l], H_B).reshape(1, W_B),
                          conv_w[l], gavg_b, batch, seq)

        wr = jnp.concatenate([router_group_w[l], router_expert_w[l],
                              jnp.zeros((d, ROUTE_LANES - N_LOGITS), F32)], axis=1).astype(BF16)
        br = jnp.concatenate([router_group_b[l], router_expert_b[l],
                              jnp.zeros((ROUTE_LANES - N_LOGITS,), F32)]).reshape(1, ROUTE_LANES)
        x2, route, tab, cnt = _outproj_router(o_a, o_bc, x2, w_out[l].astype(BF16), norm_ffn_g[l].reshape(1, d), wr, br)

        counts = cnt[0, N_GROUPS:N_GROUPS + N_EXPERTS].astype(jnp.int32)
        padded = (counts + MOE_WINDOW + MOE_BLOCK - 1) // MOE_BLOCK * MOE_BLOCK
        pad_end = jnp.cumsum(padded)
        pad_start = pad_end - padded
        blk_pos = jnp.arange(n_blocks, dtype=jnp.int32) * MOE_BLOCK
        blk_e = jnp.minimum(jnp.sum((pad_end[None, :] <= blk_pos[:, None]).astype(jnp.int32), axis=1),
                            N_EXPERTS - 1)
        n_used = (pad_end[-1] // MOE_BLOCK).astype(jnp.int32).reshape(1)
        tile_cnt = tab[:, 0, N_GROUPS:N_GROUPS + N_EXPERTS].astype(jnp.int32)
        tile_base = tab[:, 1, N_GROUPS:N_GROUPS + N_EXPERTS].astype(jnp.int32) + pad_start[None, :]
        n_win = (tile_cnt + MOE_WINDOW - 1) // MOE_WINDOW
        w_end = jnp.cumsum(n_win, axis=1)
        w_start = w_end - n_win
        k_idx = jnp.arange(MAX_WINDOWS, dtype=jnp.int32)[None, :, None]
        owner = (w_start[:, None, :] <= k_idx) & (k_idx < w_end[:, None, :])
        win_dst = jnp.sum(jnp.where(owner, tile_base[:, None, :] + (k_idx - w_start[:, None, :]) * MOE_WINDOW, 0),
                          axis=-1)
        tabs = jnp.concatenate([win_dst, w_end[:, -1:],
                                jnp.zeros((win_dst.shape[0], ROUTE_LANES - MAX_WINDOWS - 1), jnp.int32)],
                               axis=1).reshape(-1, 1, ROUTE_LANES)

        buf = _dispatch(pad_end, tabs, route, x2, cap)
        y_buf = _experts(blk_e, n_used, buf, norm_ffn_g[l].reshape(1, d), expert_w_gate, expert_w_up,
                         expert_w_down, l)
        x2 = _combine(tabs, route, x2, y_buf)
    return x2.reshape(batch, seq, d)
```

```python
import functools
import math

import jax
import jax.numpy as jnp
from jax import lax
from jax.experimental import pallas as pl
from jax.experimental.pallas import tpu as pltpu

F32 = jnp.float32
BF16 = jnp.bfloat16

LANES = 128
SUBLANES = 8

D_MODEL = 1024
CHUNK = 64
EPS = 1e-6
MASK_VALUE = -1e30
TINY = 1e-30
LOG2_E = math.log2(math.e)
H_A = 4
DK_A = D_MODEL // 16
DV_A = 2 * DK_A
ROT_DIM = DK_A // 4
ROPE_THETA = 500000.0
H_B = 4
DK_B = D_MODEL // 16
DV_B = D_MODEL // 16
C_WIDTH = D_MODEL // 4
CONV_WIDTH = 3
N_GROUPS = 4
EXPERTS_PER_GROUP = 8
N_EXPERTS = N_GROUPS * EXPERTS_PER_GROUP
D_EXPERT = D_MODEL // 2
D_PACKED = D_MODEL // 2

W_A = H_A * DV_A
W_B = H_B * DK_B
ROUTE_LANES = LANES
N_LOGITS = N_GROUPS + N_EXPERTS
ROUTE_ROWS = -(-N_LOGITS // SUBLANES) * SUBLANES

TM_PROJ = 1024
TK = 256
KV_PER_Q = 2
TQ = KV_PER_Q * TK
TM_ROUTE = 512
MOE_BLOCK = 512
HGRN_BATCH = 2
HGRN_UNROLL = 2
MOE_WINDOW = 16
MAX_WINDOWS = 2 * TM_ROUTE // MOE_WINDOW + N_EXPERTS
VMEM_LIMIT = 56 * 1024 * 1024


def _staging_rows(tm):
    return 2 * tm + N_EXPERTS * MOE_WINDOW


def _row_parts(rows, n=2):
    step = rows // n
    return [slice(r * step, (r + 1) * step) for r in range(n)]


def _pack_rows(x):
    half = x.shape[1] // 2
    lo = lax.bitcast_convert_type(x[:, :half], jnp.int32)
    hi = lax.bitcast_convert_type(x[:, half:], jnp.int32)
    return hi | lax.shift_right_logical(lo, 16)


def _unpack_rows(p):
    lo = lax.bitcast_convert_type(lax.shift_left(p, 16), F32)
    hi = lax.bitcast_convert_type(p & jnp.int32(-65536), F32)
    return jnp.concatenate([lo, hi], axis=1)


def _cparams(sem, vmem=VMEM_LIMIT):
    return pltpu.CompilerParams(dimension_semantics=sem, vmem_limit_bytes=vmem)


def _inproj_kernel(x_ref, g_ref, wqk_ref, wv_ref, whb_ref, wf_ref, wc_ref, gsum_ref, gqk_ref,
                   cos_ref, s1_ref, s2_ref, qk_out, v_out, hb_out, f_out, c_out):
    parts = _row_parts(x_ref.shape[0])
    xn = []
    for rows in parts:
        x = x_ref[rows, :]
        ms = jnp.mean(x * x, axis=-1, keepdims=True)
        xn.append((x * lax.rsqrt(ms + EPS) * g_ref[...]).astype(BF16))

    gsum = gsum_ref[...]
    for part in range(2):
        lo = part * W_A
        t = [jnp.dot(xr, wqk_ref[:, lo:lo + W_A], preferred_element_type=F32) for xr in xn]
        msq = [jnp.dot((tr * tr).astype(BF16), gsum, preferred_element_type=F32) for tr in t]
        for r, rows in enumerate(parts):
            tn = t[r] * lax.rsqrt(msq[r] + EPS) * gqk_ref[:, lo:lo + W_A]
            cos = cos_ref[rows, :]
            s1 = s1_ref[rows, :]
            s2 = s2_ref[rows, :]
            for h in range(H_A):
                slab = tn[:, h * LANES:(h + 1) * LANES]
                up = pltpu.roll(slab, LANES - ROT_DIM // 2, axis=1)
                dn = pltpu.roll(slab, ROT_DIM // 2, axis=1)
                qk_out[rows, lo + h * LANES:lo + (h + 1) * LANES] = (slab * cos + up * s1 + dn * s2).astype(BF16)

    for r, rows in enumerate(parts):
        v_out[rows, :] = jnp.dot(xn[r], wv_ref[...], preferred_element_type=F32).astype(BF16)
    for r, rows in enumerate(parts):
        hb_out[rows, :] = jnp.dot(xn[r], whb_ref[...], preferred_element_type=F32).astype(BF16)
    for r, rows in enumerate(parts):
        f_out[rows, :] = jnp.dot(xn[r], wf_ref[...], preferred_element_type=F32)
    for r, rows in enumerate(parts):
        c_out[rows, :] = jnp.dot(xn[r], wc_ref[...], preferred_element_type=F32).astype(BF16)


def _inproj(x2, g, wqk, wv, whb, wf, wc, gsum, gqk, cos_t, s1_t, s2_t):
    t_tok = x2.shape[0]
    tm = min(TM_PROJ, t_tok)
    grid = (t_tok // tm,)
    row = lambda i: (i, 0)
    fixed = lambda i: (0, 0)
    return pl.pallas_call(
        _inproj_kernel,
        grid=grid,
        in_specs=[
            pl.BlockSpec((tm, D_MODEL), row),
            pl.BlockSpec((1, D_MODEL), fixed),
            pl.BlockSpec(wqk.shape, fixed),
            pl.BlockSpec(wv.shape, fixed),
            pl.BlockSpec(whb.shape, fixed),
            pl.BlockSpec(wf.shape, fixed),
            pl.BlockSpec(wc.shape, fixed),
            pl.BlockSpec(gsum.shape, fixed),
            pl.BlockSpec(gqk.shape, fixed),
            pl.BlockSpec((tm, LANES), row),
            pl.BlockSpec((tm, LANES), row),
            pl.BlockSpec((tm, LANES), row),
        ],
        out_specs=[
            pl.BlockSpec((tm, 2 * W_A), row),
            pl.BlockSpec((tm, W_A), row),
            pl.BlockSpec((tm, 3 * W_B), row),
            pl.BlockSpec((tm, W_B), row),
            pl.BlockSpec((tm, 3 * C_WIDTH), row),
        ],
        out_shape=[
            jax.ShapeDtypeStruct((t_tok, 2 * W_A), BF16),
            jax.ShapeDtypeStruct((t_tok, W_A), BF16),
            jax.ShapeDtypeStruct((t_tok, 3 * W_B), BF16),
            jax.ShapeDtypeStruct((t_tok, W_B), F32),
            jax.ShapeDtypeStruct((t_tok, 3 * C_WIDTH), BF16),
        ],
        compiler_params=_cparams(("parallel",)),
        name="inproj",
    )(x2, g, wqk, wv, whb, wf, wc, gsum, gqk, cos_t, s1_t, s2_t)


def _attn_kernel(lam_ref, q_ref, k_ref, v_ref, g_ref, o_ref, qc_ref, sa_ref, sb_ref, acc_ref, m_ref, l_ref, *,
                 lam_init):
    qi = pl.program_id(1)
    q = q_ref[...]
    lane = lax.broadcasted_iota(jnp.int32, (TQ, DV_A), 1)
    zero = jnp.zeros((TQ, DV_A), BF16)
    for h in range(H_A):
        qh = q[:, h * DV_A:(h + 1) * DV_A]
        qc_ref[h, 0] = jnp.where(lane < DK_A, qh, zero)
        qc_ref[h, 1] = jnp.where(lane >= DK_A, qh, zero)

    acc_ref[...] = jnp.zeros_like(acc_ref)
    m_ref[...] = jnp.full_like(m_ref, -jnp.inf)
    l_ref[...] = jnp.zeros_like(l_ref)

    kchunk = lax.broadcasted_iota(jnp.int32, (TK, TQ), 0) // CHUNK
    qchunk = lax.broadcasted_iota(jnp.int32, (TK, TQ), 1) // CHUNK

    def scores(j, s_ref, q0=0):
        start = pl.multiple_of(j * TK, TK)
        kb = k_ref[pl.ds(start, TK), :]
        for h in range(H_A):
            kh = kb[:, h * DV_A:(h + 1) * DV_A]
            for c in range(2):
                s_ref[h, c, :, q0:] = lax.dot_general(kh, qc_ref[h, c, q0:, :], (((1,), (1,)), ((), ())),
                                                      preferred_element_type=F32)

    def consume(j, s_ref, mask, q0=0):
        start = pl.multiple_of(j * TK, TK)
        vb = v_ref[pl.ds(start, TK), :]
        for h in range(H_A):
            vh = vb[:, h * DV_A:(h + 1) * DV_A]
            for c in range(2):
                s = s_ref[h, c, :, q0:]
                if mask is not None:
                    s = jnp.where(mask[:, q0:], s, MASK_VALUE)
                m_old = m_ref[h, c, :, q0:]
                m_new = jnp.maximum(m_old, jnp.max(s, axis=0, keepdims=True))
                alpha = jnp.exp2(m_old - m_new)
                p = jnp.exp2(s - m_new)
                l_ref[h, c, :, q0:] = alpha * l_ref[h, c, :, q0:] + jnp.sum(p, axis=0, keepdims=True)
                pv = lax.dot_general(vh, p.astype(BF16), (((0,), (0,)), ((), ())),
                                     preferred_element_type=F32)
                acc_ref[h, c, :, q0:] = alpha * acc_ref[h, c, :, q0:] + pv
                m_ref[h, c, :, q0:] = m_new

    first_diag = KV_PER_Q * qi
    scores(0, sa_ref)

    def body(jj, carry):
        j = 2 * jj
        scores(j + 1, sb_ref)
        consume(j, sa_ref, None)
        scores(j + 2, sa_ref)
        consume(j + 1, sb_ref, None)
        return carry

    lax.fori_loop(0, first_diag // 2, body, 0)
    for d in range(KV_PER_Q):
        cur, nxt = (sa_ref, sb_ref) if d % 2 == 0 else (sb_ref, sa_ref)
        if d + 1 < KV_PER_Q:
            scores(first_diag + d + 1, nxt, (d + 1) * TK)
        consume(first_diag + d, cur, (kchunk + d * (TK // CHUNK)) <= qchunk, d * TK)

    lam = lam_ref[0]
    for h in range(H_A):
        o = acc_ref[h, 0] / l_ref[h, 0] - lam * (acc_ref[h, 1] / l_ref[h, 1])
        ms = jnp.mean(o * o, axis=0, keepdims=True)
        y = o * lax.rsqrt(ms + EPS) * g_ref[...] * (1.0 - lam_init)
        o_ref[:, h * DV_A:(h + 1) * DV_A] = y.T.astype(BF16)


def _attention(qk, v, lam, subln_col, batch, seq, lam_init):
    assert TQ == KV_PER_Q * TK and KV_PER_Q % 2 == 0 and TK % CHUNK == 0 and seq % TQ == 0
    nq = seq // TQ
    t_tok = batch * seq
    kernel = functools.partial(_attn_kernel, lam_init=lam_init)
    return pl.pallas_call(
        kernel,
        grid=(batch, nq),
        in_specs=[
            pl.BlockSpec(memory_space=pltpu.SMEM),
            pl.BlockSpec((TQ, W_A), lambda b, i: (b * nq + i, 0)),
            pl.BlockSpec((seq, W_A), lambda b, i: (b, 1)),
            pl.BlockSpec((seq, W_A), lambda b, i: (b, 0)),
            pl.BlockSpec((DV_A, 1), lambda b, i: (0, 0)),
        ],
        out_specs=pl.BlockSpec((TQ, W_A), lambda b, i: (b * nq + i, 0)),
        out_shape=jax.ShapeDtypeStruct((t_tok, W_A), BF16),
        scratch_shapes=[
            pltpu.VMEM((H_A, 2, TQ, DV_A), BF16),
            pltpu.VMEM((H_A, 2, TK, TQ), F32),
            pltpu.VMEM((H_A, 2, TK, TQ), F32),
            pltpu.VMEM((H_A, 2, DV_A, TQ), F32),
            pltpu.VMEM((H_A, 2, 1, TQ), F32),
            pltpu.VMEM((H_A, 2, 1, TQ), F32),
        ],
        compiler_params=_cparams(("parallel", "arbitrary")),
        name="diff_attn",
    )(lam, qk, qk, v, subln_col)


_LEVELS = (32, 16, 8, 4, 2, 1)


def _ref_rows(g, h):
    n = g.shape[1]
    if 2 * h >= SUBLANES:
        pieces = []
        for m in range(CHUNK // (2 * h)):
            r = 2 * h * m + h - 1
            pieces.append(jnp.broadcast_to(g[r:r + 1, :], (2 * h, n)))
        return pieces[0] if len(pieces) == 1 else jnp.concatenate(pieces, axis=0)
    g3 = g.reshape(CHUNK // SUBLANES, SUBLANES, n)
    sub = lax.broadcasted_iota(jnp.int32, g3.shape, 1)
    out = None
    for m in reversed(range(SUBLANES // (2 * h))):
        r = 2 * h * m + h - 1
        piece = jnp.broadcast_to(g3[:, r:r + 1, :], g3.shape)
        out = piece if out is None else jnp.where(sub < 2 * h * (m + 1), piece, out)
    return out.reshape(CHUNK, n)


def _stack_heads(a, lane_head):
    zero = jnp.zeros_like(a)
    return jnp.concatenate([jnp.where(lane_head == hh, a, zero) for hh in range(H_B)], axis=0)


def _hgrn_kernel(hb_ref, f_ref, c_ref, lb_ref, ng_ref, cw_ref, gavg_ref, o_ref, state_ref, hbuf_ref, *, seq):
    n_chunks = seq // CHUNK
    nb = hb_ref.shape[0] // seq
    lb = lb_ref[...]
    ng = ng_ref[...]
    gavg = gavg_ref[...]

    row = lax.broadcasted_iota(jnp.int32, (CHUNK, W_B), 0)
    lane = lax.broadcasted_iota(jnp.int32, (CHUNK, W_B), 1)
    lane_head = lane // DK_B
    lane_s = lane % CHUNK
    tri = (lax.broadcasted_iota(jnp.int32, (CHUNK, CHUNK), 1)
           <= lax.broadcasted_iota(jnp.int32, (CHUNK, CHUNK), 0)).astype(BF16)
    lvl_mask = [((row // (2 * h)) == (lane_s // (2 * h))) & ((row % (2 * h)) >= h) & ((lane_s % (2 * h)) < h)
                for h in _LEVELS]
    eye_mask = row == lane_s
    r4 = lax.broadcasted_iota(jnp.int32, (H_B * DK_B, W_B), 0) // DK_B
    c4 = lax.broadcasted_iota(jnp.int32, (H_B * DK_B, W_B), 1) // DK_B
    bd_mask = r4 == c4

    state_ref[...] = jnp.zeros_like(state_ref)
    nt = (((1,), (1,)), ((), ()))
    tn = (((0,), (0,)), ((), ()))
    unroll = HGRN_UNROLL if n_chunks % HGRN_UNROLL == 0 else 1
    each = range(nb * unroll)

    def chunk(i, carry):
        rows = [pl.ds(pl.multiple_of((e % nb) * seq + (i * unroll + e // nb) * CHUNK, CHUNK), CHUNK) for e in each]
        hb = [hb_ref[rows[j], :].astype(F32) for j in each]
        zq = [t[:, 0:W_B] for t in hb]
        v16 = [t[:, W_B:2 * W_B].astype(BF16) for t in hb]
        zg = [t[:, 2 * W_B:3 * W_B] for t in hb]
        zf = [f_ref[rows[j], :] for j in each]

        logf = [jnp.log(jnp.maximum(lb + (1.0 - lb) * jax.nn.sigmoid(z), TINY)) for z in zf]
        key = [(1.0 - lb) * jax.nn.sigmoid(-z) for z in zf]
        q = [z * jax.nn.sigmoid(z) for z in zq]

        g = []
        for x in logf:
            hi = x.astype(BF16)
            r1 = x - hi.astype(F32)
            mid = r1.astype(BF16)
            lo = (r1 - mid.astype(F32)).astype(BF16)
            g.append((jnp.dot(tri, hi, preferred_element_type=F32) + jnp.dot(tri, mid, preferred_element_type=F32)
                      + jnp.dot(tri, lo, preferred_element_type=F32)) * LOG2_E)

        q16 = [t.astype(BF16) for t in q]
        k16 = [t.astype(BF16) for t in key]
        a_all = [jnp.where(eye_mask, lax.dot_general(q16[j], _stack_heads(k16[j], lane_head),
                                                     nt, preferred_element_type=F32), 0.0) for j in each]
        for h, mask in zip(_LEVELS, lvl_mask):
            w = [jnp.exp2(-jnp.abs(g[j] - _ref_rows(g[j], h))).astype(BF16) for j in each]
            ah = [lax.dot_general(q16[j] * w[j], _stack_heads(k16[j] * w[j], lane_head),
                                  nt, preferred_element_type=F32) for j in each]
            a_all = [a_all[j] + jnp.where(mask, ah[j], 0.0) for j in each]
        o = [jnp.dot(a_all[j].astype(BF16), _stack_heads(v16[j], lane_head), preferred_element_type=F32) for j in each]

        qg = [(q[j] * jnp.exp2(g[j])).astype(BF16) for j in each]
        g_last = [t[CHUNK - 1:CHUNK, :] for t in g]
        upd = [lax.dot_general(v16[j], (key[j] * jnp.exp2(g_last[j] - g[j])).astype(BF16), tn,
                               preferred_element_type=F32) for j in each]
        for j in each:
            st = state_ref[j % nb]
            o[j] = o[j] + lax.dot_general(qg[j], st.astype(BF16), nt, preferred_element_type=F32)
            state_ref[j % nb] = jnp.exp2(g_last[j]) * st + jnp.where(bd_mask, upd[j], 0.0)

        ms = [jnp.dot((t * t).astype(BF16), gavg, preferred_element_type=F32) for t in o]
        for j in each:
            y = o[j] * lax.rsqrt(ms[j] + EPS) * ng * (zg[j] * jax.nn.sigmoid(zg[j]))
            o_ref[rows[j], 0:W_B] = y.astype(BF16)
        return carry

    lax.fori_loop(0, n_chunks // unroll, chunk, 0)

    piece = min(512, seq)
    hbuf_ref[0:SUBLANES, :] = jnp.zeros((SUBLANES, C_WIDTH), F32)
    for j in range(nb):
        for p in range(seq // piece):
            cblk = c_ref[j * seq + p * piece:j * seq + (p + 1) * piece, :].astype(F32)
            hbuf_ref[SUBLANES + p * piece:SUBLANES + (p + 1) * piece, :] = (cblk[:, C_WIDTH:2 * C_WIDTH]
                                                                             * cblk[:, 2 * C_WIDTH:])
        for p in range(seq // piece):
            base = SUBLANES + p * piece
            y = (cw_ref[0:1, :] * hbuf_ref[base - 2:base - 2 + piece, :]
                 + cw_ref[1:2, :] * hbuf_ref[base - 1:base - 1 + piece, :]
                 + cw_ref[2:3, :] * hbuf_ref[base:base + piece, :])
            lo_r = j * seq + p * piece
            bgate = c_ref[lo_r:lo_r + piece, 0:C_WIDTH].astype(F32)
            o_ref[lo_r:lo_r + piece, W_B:W_B + C_WIDTH] = (bgate * y).astype(BF16)


def _hgrn_conv(hb, f, cv, lb, ng, cw, gavg, batch, seq):
    t_tok = batch * seq
    nb = HGRN_BATCH if batch % HGRN_BATCH == 0 else 1
    fixed = lambda b: (0, 0)
    return pl.pallas_call(
        functools.partial(_hgrn_kernel, seq=seq),
        grid=(batch // nb,),
        in_specs=[
            pl.BlockSpec((nb * seq, 3 * W_B), lambda b: (b, 0)),
            pl.BlockSpec((nb * seq, W_B), lambda b: (b, 0)),
            pl.BlockSpec((nb * seq, 3 * C_WIDTH), lambda b: (b, 0)),
            pl.BlockSpec((1, W_B), fixed),
            pl.BlockSpec((1, W_B), fixed),
            pl.BlockSpec((CONV_WIDTH, C_WIDTH), fixed),
            pl.BlockSpec((W_B, W_B), fixed),
        ],
        out_specs=pl.BlockSpec((nb * seq, W_B + C_WIDTH), lambda b: (b, 0)),
        out_shape=jax.ShapeDtypeStruct((t_tok, W_B + C_WIDTH), BF16),
        scratch_shapes=[
            pltpu.VMEM((nb, H_B * DV_B, W_B), F32),
            pltpu.VMEM((seq + SUBLANES, C_WIDTH), F32),
        ],
        compiler_params=_cparams(("parallel",)),
        name="hgrn_conv",
    )(hb, f, cv, lb, ng, cw, gavg)


def _outproj_router_kernel(oa_ref, obc_ref, x_ref, wo_ref, g_ref, wr_ref, br_ref,
                           xo_ref, route_ref, tab_ref, cnt_ref, carry_ref):
    i = pl.program_id(0)

    @pl.when(i == 0)
    def _():
        carry_ref[...] = jnp.zeros_like(carry_ref)

    mixed = (jnp.dot(oa_ref[...], wo_ref[0:W_A, :], preferred_element_type=F32)
             + jnp.dot(obc_ref[...], wo_ref[W_A:, :], preferred_element_type=F32))
    xn = x_ref[...] + mixed
    xo_ref[...] = xn

    ms = jnp.mean(xn * xn, axis=-1, keepdims=True)
    hn = (xn * lax.rsqrt(ms + EPS) * g_ref[...]).astype(BF16)
    logits = jnp.dot(hn, wr_ref[...], preferred_element_type=F32) + br_ref[...]

    tm = logits.shape[0]
    lt = logits.T[0:ROUTE_ROWS, :]
    r = lax.broadcasted_iota(jnp.int32, (ROUTE_ROWS, tm), 0)
    neg = -jnp.inf
    big = ROUTE_ROWS

    def top(cand):
        val = jnp.max(cand, axis=0, keepdims=True)
        idx = jnp.min(jnp.where(cand == val, r, big), axis=0, keepdims=True)
        return val, idx

    glog = jnp.where(r < N_GROUPS, lt, neg)
    gmax, gidx = top(glog)
    p_g = 1.0 / jnp.sum(jnp.exp(glog - gmax), axis=0, keepdims=True)
    off = r - (N_GROUPS + EXPERTS_PER_GROUP * gidx)
    elog = jnp.where(jnp.abs(2 * off - (EXPERTS_PER_GROUP - 1)) < EXPERTS_PER_GROUP, lt, neg)
    v1, i1 = top(elog)
    v2, i2 = top(jnp.where(r == i1, neg, elog))
    t2 = jnp.exp(v2 - v1)
    w1 = p_g / (1.0 + t2)
    w2 = p_g * t2 / (1.0 + t2)

    sel1 = r == i1
    sel2 = r == i2
    onehot = jnp.where(sel1, 1.0, jnp.where(sel2, 1.0, 0.0))
    earlier = (lax.broadcasted_iota(jnp.int32, (tm, tm), 0)
               < lax.broadcasted_iota(jnp.int32, (tm, tm), 1)).astype(BF16)
    local = jnp.dot(onehot.astype(BF16), earlier, preferred_element_type=F32)
    cnt = jnp.sum(onehot, axis=1, keepdims=True)
    wrows = jnp.floor((cnt + (MOE_WINDOW - 1)) * (1.0 / MOE_WINDOW)) * MOE_WINDOW
    below = (lax.broadcasted_iota(jnp.int32, (ROUTE_ROWS, ROUTE_LANES), 1)
             < lax.broadcasted_iota(jnp.int32, (ROUTE_ROWS, ROUTE_LANES), 0)).astype(BF16)
    wrows_rows = jnp.concatenate([jnp.broadcast_to(wrows, (ROUTE_ROWS, ROUTE_LANES)),
                                  jnp.zeros((ROUTE_LANES - ROUTE_ROWS, ROUTE_LANES), F32)], axis=0).astype(BF16)
    soff = jnp.dot(below, wrows_rows, preferred_element_type=F32)[:, 0:1]
    slot = local + soff
    row1 = jnp.sum(jnp.where(sel1, slot, 0.0), axis=0, keepdims=True)
    row2 = jnp.sum(jnp.where(sel2, slot, 0.0), axis=0, keepdims=True)

    k = lax.broadcasted_iota(jnp.int32, (SUBLANES, tm), 0)
    rec = jnp.where(k == 0, (i1 - N_GROUPS).astype(F32),
          jnp.where(k == 1, (i2 - N_GROUPS).astype(F32),
          jnp.where(k == 2, w1,
          jnp.where(k == 3, w2,
          jnp.where(k == 4, row1,
          jnp.where(k == 5, row2, 0.0))))))
    route_ref[...] = jnp.concatenate([rec, jnp.zeros((ROUTE_LANES - SUBLANES, tm), F32)], axis=0).T

    lane = lax.broadcasted_iota(jnp.int32, (ROUTE_ROWS, ROUTE_LANES), 1)
    tab_ref[0] = jnp.where(lane == 0, cnt, jnp.where(lane == 1, carry_ref[...], 0.0))
    carry_ref[...] = carry_ref[...] + jnp.floor((cnt + (SUBLANES - 1)) * (1.0 / SUBLANES)) * SUBLANES
    cnt_ref[...] = carry_ref[...]


def _outproj_router(o_a, o_bc, x2, wo, g, wr, br):
    t_tok = x2.shape[0]
    tm = min(TM_ROUTE, t_tok)
    row = lambda i: (i, 0)
    fixed = lambda i: (0, 0)
    return pl.pallas_call(
        _outproj_router_kernel,
        grid=(t_tok // tm,),
        in_specs=[
            pl.BlockSpec((tm, W_A), row),
            pl.BlockSpec((tm, W_B + C_WIDTH), row),
            pl.BlockSpec((tm, D_MODEL), row),
            pl.BlockSpec(wo.shape, fixed),
            pl.BlockSpec((1, D_MODEL), fixed),
            pl.BlockSpec(wr.shape, fixed),
            pl.BlockSpec((1, ROUTE_LANES), fixed),
        ],
        out_specs=[
            pl.BlockSpec((tm, D_MODEL), row),
            pl.BlockSpec((tm, ROUTE_LANES), row),
            pl.BlockSpec((1, ROUTE_ROWS, ROUTE_LANES), lambda i: (i, 0, 0)),
            pl.BlockSpec((ROUTE_ROWS, ROUTE_LANES), fixed),
        ],
        out_shape=[
            jax.ShapeDtypeStruct((t_tok, D_MODEL), F32),
            jax.ShapeDtypeStruct((t_tok, ROUTE_LANES), F32),
            jax.ShapeDtypeStruct((t_tok // tm, ROUTE_ROWS, ROUTE_LANES), F32),
            jax.ShapeDtypeStruct((ROUTE_ROWS, ROUTE_LANES), F32),
        ],
        scratch_shapes=[pltpu.VMEM((ROUTE_ROWS, ROUTE_LANES), F32)],
        compiler_params=_cparams(("arbitrary",)),
        name="outproj_router",
    )(o_a, o_bc, x2, wo, g, wr, br)


def _start_copies(n, make_copy):
    def start(k, carry):
        make_copy(k).start()
        return carry

    lax.fori_loop(0, n, start, 0)


def _wait_copies(n, make_copy):
    def wait(k, carry):
        make_copy(k).wait()
        return carry

    lax.fori_loop(0, n, wait, 0)


def _window_copy_loops(n, make_copy):
    _start_copies(n, make_copy)
    _wait_copies(n, make_copy)


def _dispatch_kernel(pend_ref, tab_ref, tabp_ref, route_ref, x_ref, buf_ref, xp_ref, xq_ref, sem, semq):
    tm = x_ref.shape[0]
    r_st = xp_ref.shape[0]
    step = pl.program_id(0)

    @pl.when(pl.program_id(0) == 0)
    def _():
        xp_ref[...] = jnp.zeros_like(xp_ref)
        copies = []
        for e in range(N_EXPERTS):
            end = pend_ref[e]
            copies.append((None, pltpu.make_async_copy(
                xp_ref.at[0:MOE_BLOCK, :],
                buf_ref.at[pl.ds(pl.multiple_of(end - MOE_BLOCK, SUBLANES), MOE_BLOCK), :], sem)))
            copies.append((end >= MOE_BLOCK + MOE_WINDOW, pltpu.make_async_copy(
                xp_ref.at[0:MOE_WINDOW, :],
                buf_ref.at[pl.ds(pl.multiple_of(jnp.maximum(end - MOE_BLOCK - MOE_WINDOW, 0), SUBLANES), MOE_WINDOW), :],
                sem)))
        for cond, cp in copies:
            if cond is None:
                cp.start()
            else:
                pl.when(cond)(cp.start)
        for cond, cp in copies:
            if cond is None:
                cp.wait()
            else:
                pl.when(cond)(cp.wait)

        def tail_copy(k):
            return pltpu.make_async_copy(
                xp_ref.at[0:MOE_BLOCK, :],
                buf_ref.at[pl.ds(pl.multiple_of(k * MOE_BLOCK, MOE_BLOCK), MOE_BLOCK), :], sem)

        first_unused = pend_ref[N_EXPERTS - 1] // MOE_BLOCK
        n_unused = buf_ref.shape[0] // MOE_BLOCK - first_unused
        _window_copy_loops(n_unused, lambda k: tail_copy(first_unused + k))

    route = route_ref[...]
    r0 = route[:, 4:5].astype(jnp.int32)
    r1 = route[:, 5:6].astype(jnp.int32)
    col = lax.broadcasted_iota(jnp.int32, (tm, r_st), 1)
    sel = jnp.where(col == r0, 1.0, jnp.where(col == r1, 1.0, 0.0)).astype(BF16)

    def window(tabs, stage_ref, s):
        def make_copy(k):
            src = pl.multiple_of(k * MOE_WINDOW, MOE_WINDOW)
            return pltpu.make_async_copy(
                stage_ref.at[pl.ds(src, MOE_WINDOW), :],
                buf_ref.at[pl.ds(pl.multiple_of(tabs[0, 0, k], SUBLANES), MOE_WINDOW), :], s)
        return tabs[0, 0, MAX_WINDOWS], make_copy

    def run(cur_ref, cur_sem, prev_ref, prev_sem):
        cur_ref[...] = _pack_rows(lax.dot_general(sel, x_ref[...].astype(BF16), (((0,), (0,)), ((), ())),
                                                  preferred_element_type=F32))

        @pl.when(step > 0)
        def _():
            _wait_copies(*window(tabp_ref, prev_ref, prev_sem))

        _start_copies(*window(tab_ref, cur_ref, cur_sem))

        @pl.when(step == pl.num_programs(0) - 1)
        def _():
            _wait_copies(*window(tab_ref, cur_ref, cur_sem))

    @pl.when(step % 2 == 0)
    def _():
        run(xp_ref, sem, xq_ref, semq)

    @pl.when(step % 2 == 1)
    def _():
        run(xq_ref, semq, xp_ref, sem)


def _dispatch(pad_end, tabs, route, x2, cap):
    t_tok = x2.shape[0]
    tm = min(TM_ROUTE, t_tok)
    n_tiles = t_tok // tm
    return pl.pallas_call(
        _dispatch_kernel,
        grid=(n_tiles,),
        in_specs=[
            pl.BlockSpec(memory_space=pltpu.SMEM),
            pl.BlockSpec((1, 1, ROUTE_LANES), lambda i: (i, 0, 0), memory_space=pltpu.SMEM),
            pl.BlockSpec((1, 1, ROUTE_LANES), lambda i: (jnp.maximum(i - 1, 0), 0, 0), memory_space=pltpu.SMEM),
            pl.BlockSpec((tm, ROUTE_LANES), lambda i: (i, 0)),
            pl.BlockSpec((tm, D_MODEL), lambda i: (i, 0)),
        ],
        out_specs=pl.BlockSpec(memory_space=pl.ANY),
        out_shape=jax.ShapeDtypeStruct((cap, D_PACKED), jnp.int32),
        scratch_shapes=[pltpu.VMEM((_staging_rows(tm), D_PACKED), jnp.int32),
                        pltpu.VMEM((_staging_rows(tm), D_PACKED), jnp.int32),
                        pltpu.SemaphoreType.DMA(()), pltpu.SemaphoreType.DMA(())],
        compiler_params=_cparams(("arbitrary",)),
        name="moe_dispatch",
    )(pad_end, tabs, tabs, route, x2)


def _experts_kernel(blk_e_ref, n_used_ref, x_ref, g_ref, wg_ref, wu_ref, wd_ref, y_ref, wg16, wu16, wd16):
    i = pl.program_id(0)
    used = i < n_used_ref[0]

    @pl.when(used & ((i == 0) | (blk_e_ref[i] != blk_e_ref[jnp.maximum(i - 1, 0)])))
    def _():
        wg16[...] = wg_ref[0, 0].astype(BF16)
        wu16[...] = wu_ref[0, 0].astype(BF16)
        wd16[...] = wd_ref[0, 0].astype(BF16)

    @pl.when(used)
    def _():
        wg = wg16[...]
        wu = wu16[...]
        wd = wd16[...]
        parts = _row_parts(x_ref.shape[0])
        hn = []
        for rows in parts:
            x = _unpack_rows(x_ref[rows, :])
            ms = jnp.mean(x * x, axis=-1, keepdims=True)
            hn.append((x * lax.rsqrt(ms + EPS) * g_ref[...]).astype(BF16))
        a = [jnp.dot(t, wg, preferred_element_type=F32) for t in hn]
        u = [jnp.dot(t, wu, preferred_element_type=F32) for t in hn]
        act = [(a[r] * jax.nn.sigmoid(a[r]) * u[r]).astype(BF16) for r in range(len(parts))]
        for r, rows in enumerate(parts):
            y = jnp.dot(act[r], wd, preferred_element_type=F32)
            y_ref[rows, :] = _pack_rows(y.astype(BF16).astype(F32))


def _experts(blk_e, n_used, buf, g, wg, wu, wd, layer):
    n_blocks = buf.shape[0] // MOE_BLOCK
    grid_spec = pltpu.PrefetchScalarGridSpec(
        num_scalar_prefetch=2,
        grid=(n_blocks,),
        in_specs=[
            pl.BlockSpec((MOE_BLOCK, D_PACKED), lambda i, be, nu: (jnp.minimum(i, nu[0] - 1), 0)),
            pl.BlockSpec((1, D_MODEL), lambda i, be, nu: (0, 0)),
            pl.BlockSpec((1, 1, D_MODEL, D_EXPERT), lambda i, be, nu: (layer, be[i], 0, 0)),
            pl.BlockSpec((1, 1, D_MODEL, D_EXPERT), lambda i, be, nu: (layer, be[i], 0, 0)),
            pl.BlockSpec((1, 1, D_EXPERT, D_MODEL), lambda i, be, nu: (layer, be[i], 0, 0)),
        ],
        out_specs=pl.BlockSpec((MOE_BLOCK, D_PACKED), lambda i, be, nu: (jnp.minimum(i, nu[0] - 1), 0)),
        scratch_shapes=[pltpu.VMEM((D_MODEL, D_EXPERT), BF16), pltpu.VMEM((D_MODEL, D_EXPERT), BF16),
                        pltpu.VMEM((D_EXPERT, D_MODEL), BF16)],
    )
    return pl.pallas_call(
        _experts_kernel,
        grid_spec=grid_spec,
        out_shape=jax.ShapeDtypeStruct(buf.shape, jnp.int32),
        input_output_aliases={2: 0},
        compiler_params=_cparams(("arbitrary",)),
        name="moe_experts",
    )(blk_e, n_used, buf, g, wg, wu, wd)


def _combine_kernel(tab_ref, tabn_ref, route_ref, x_ref, y_ref, o_ref, ya_ref, yb_ref, sema, semb):
    tm = x_ref.shape[0]
    r_st = ya_ref.shape[0]
    step = pl.program_id(0)

    def window(tabs, stage_ref, s):
        def make_copy(k):
            dst = pl.multiple_of(k * MOE_WINDOW, MOE_WINDOW)
            return pltpu.make_async_copy(y_ref.at[pl.ds(pl.multiple_of(tabs[0, 0, k], SUBLANES), MOE_WINDOW), :],
                                         stage_ref.at[pl.ds(dst, MOE_WINDOW), :], s)
        return tabs[0, 0, MAX_WINDOWS], make_copy

    @pl.when(step == 0)
    def _():
        ya_ref[...] = jnp.zeros_like(ya_ref)
        yb_ref[...] = jnp.zeros_like(yb_ref)
        _start_copies(*window(tab_ref, ya_ref, sema))

    route = route_ref[...]
    r0 = route[:, 4:5].astype(jnp.int32)
    r1 = route[:, 5:6].astype(jnp.int32)
    col = lax.broadcasted_iota(jnp.int32, (tm, r_st), 1)
    gate = jnp.where(col == r0, route[:, 2:3], jnp.where(col == r1, route[:, 3:4], 0.0)).astype(BF16)

    def run(cur_ref, cur_sem, nxt_ref, nxt_sem):
        @pl.when(step + 1 < pl.num_programs(0))
        def _():
            _start_copies(*window(tabn_ref, nxt_ref, nxt_sem))

        _wait_copies(*window(tab_ref, cur_ref, cur_sem))
        o_ref[...] = x_ref[...] + jnp.dot(gate, _unpack_rows(cur_ref[...]).astype(BF16), preferred_element_type=F32)

    @pl.when(step % 2 == 0)
    def _():
        run(ya_ref, sema, yb_ref, semb)

    @pl.when(step % 2 == 1)
    def _():
        run(yb_ref, semb, ya_ref, sema)


def _combine(tabs, route, x2, y_buf):
    t_tok = x2.shape[0]
    tm = min(TM_ROUTE, t_tok)
    n_tiles = t_tok // tm
    return pl.pallas_call(
        _combine_kernel,
        grid=(n_tiles,),
        in_specs=[
            pl.BlockSpec((1, 1, ROUTE_LANES), lambda i: (i, 0, 0), memory_space=pltpu.SMEM),
            pl.BlockSpec((1, 1, ROUTE_LANES), lambda i: (jnp.minimum(i + 1, n_tiles - 1), 0, 0),
                         memory_space=pltpu.SMEM),
            pl.BlockSpec((tm, ROUTE_LANES), lambda i: (i, 0)),
            pl.BlockSpec((tm, D_MODEL), lambda i: (i, 0)),
            pl.BlockSpec(memory_space=pl.ANY),
        ],
        out_specs=pl.BlockSpec((tm, D_MODEL), lambda i: (i, 0)),
        out_shape=jax.ShapeDtypeStruct((t_tok, D_MODEL), F32),
        scratch_shapes=[pltpu.VMEM((_staging_rows(tm), D_PACKED), jnp.int32),
                        pltpu.VMEM((_staging_rows(tm), D_PACKED), jnp.int32),
                        pltpu.SemaphoreType.DMA(()), pltpu.SemaphoreType.DMA(())],
        compiler_params=_cparams(("arbitrary",)),
        name="moe_combine",
    )(tabs, tabs, route, x2, y_buf)


def _rope_tables(positions):
    half = ROT_DIM // 2
    inv_freq = ROPE_THETA ** (-jnp.arange(0, ROT_DIM, 2, dtype=F32) / ROT_DIM)
    d = jnp.arange(LANES) % DK_A
    freq_lane = jnp.where(d < ROT_DIM, inv_freq[d % half], 0.0)
    ang = positions.reshape(-1, 1).astype(F32) * freq_lane[None, :]
    cos, sin = jnp.cos(ang), jnp.sin(ang)
    s1 = jnp.where(d < half, -sin, 0.0)
    s2 = jnp.where((d >= half) & (d < ROT_DIM), sin, 0.0)
    return cos, s1, s2


def _block_avg(width, group):
    idx = jnp.arange(width) // group
    return jnp.where(idx[:, None] == idx[None, :], 1.0 / group, 0.0).astype(BF16)


def kernel(x, positions, norm_mix_g, w_in, attn_q_norm_g, attn_k_norm_g, lambda_q1, lambda_k1, lambda_q2,
           lambda_k2, attn_subln_g, hgrn_lower_bounds, hgrn_norm_g, conv_w, w_out, norm_ffn_g, router_group_w,
           router_group_b, router_expert_w, router_expert_b, expert_w_gate, expert_w_up, expert_w_down):
    batch, seq, d = x.shape
    assert d == D_MODEL
    depth = w_in.shape[0]
    t_tok = batch * seq
    n_assign = 2 * t_tok
    n_route_tiles = t_tok // min(TM_ROUTE, t_tok)
    n_blocks = (n_assign + (SUBLANES - 1) * n_route_tiles * N_EXPERTS) // MOE_BLOCK + 2 * N_EXPERTS + 1
    cap = n_blocks * MOE_BLOCK

    lb_probs = jax.nn.softmax(hgrn_lower_bounds.astype(F32), axis=0)
    lower_bounds = jnp.cumsum(lb_probs, axis=0) - lb_probs[0:1]
    cos_t, s1_t, s2_t = _rope_tables(positions)
    gsum_a = _block_avg(W_A, DK_A)
    gavg_b = _block_avg(W_B, DV_B)

    o_q, o_k, o_v = 0, W_A, 2 * W_A
    o_bq = 3 * W_A
    o_bf, o_bi, o_bg = o_bq + W_B, o_bq + 2 * W_B, o_bq + 3 * W_B
    o_c = o_bq + 4 * W_B

    x2 = x.reshape(t_tok, d)
    for l in range(depth):
        w = w_in[l]
        wqk = w[:, o_q:o_v].astype(BF16)
        wv = w[:, o_v:o_bq].astype(BF16)
        whb = jnp.concatenate([w[:, o_bq:o_bf], w[:, o_bi:o_c]], axis=1).astype(BF16)
        wf = w[:, o_bf:o_bi].astype(BF16)
        wc = w[:, o_c:].astype(BF16)
        gqk = jnp.concatenate([jnp.tile(attn_q_norm_g[l], 2 * H_A) * (DK_A ** -0.5 * LOG2_E),
                               jnp.tile(attn_k_norm_g[l], 2 * H_A)]).reshape(1, 2 * W_A).astype(F32)
        qk, v, hb, f, cv = _inproj(x2, norm_mix_g[l].reshape(1, d), wqk, wv, whb, wf, wc, gsum_a, gqk,
                                   cos_t, s1_t, s2_t)

        lam_init = 0.8 - 0.6 * math.exp(-0.3 * l)
        lam = (jnp.exp(jnp.sum(lambda_q1[l].astype(F32) * lambda_k1[l].astype(F32)))
               - jnp.exp(jnp.sum(lambda_q2[l].astype(F32) * lambda_k2[l].astype(F32))) + lam_init).reshape(1)
        o_a = _attention(qk, v, lam, attn_subln_g[l].reshape(DV_A, 1).astype(F32), batch, seq, lam_init)

        o_bc = _hgrn_conv(hb, f, cv, lower_bounds[l].reshape(1, W_B), jnp.tile(hgrn_norm_g[l], H_B).reshape(1, W_B),
                          conv_w[l], gavg_b, batch, seq)

        wr = jnp.concatenate([router_group_w[l], router_expert_w[l],
                              jnp.zeros((d, ROUTE_LANES - N_LOGITS), F32)], axis=1).astype(BF16)
        br = jnp.concatenate([router_group_b[l], router_expert_b[l],
                              jnp.zeros((ROUTE_LANES - N_LOGITS,), F32)]).reshape(1, ROUTE_LANES)
        x2, route, tab, cnt = _outproj_router(o_a, o_bc, x2, w_out[l].astype(BF16), norm_ffn_g[l].reshape(1, d), wr, br)

        counts = cnt[N_GROUPS:N_GROUPS + N_EXPERTS, 0].astype(jnp.int32)
        padded = (counts + MOE_WINDOW + MOE_BLOCK - 1) // MOE_BLOCK * MOE_BLOCK
        pad_end = jnp.cumsum(padded)
        pad_start = pad_end - padded
        blk_pos = jnp.arange(n_blocks, dtype=jnp.int32) * MOE_BLOCK
        blk_e = jnp.minimum(jnp.sum((pad_end[None, :] <= blk_pos[:, None]).astype(jnp.int32), axis=1),
                            N_EXPERTS - 1)
        n_used = (pad_end[-1] // MOE_BLOCK).astype(jnp.int32).reshape(1)
        tile_cnt = tab[:, N_GROUPS:N_GROUPS + N_EXPERTS, 0].astype(jnp.int32)
        tile_base = tab[:, N_GROUPS:N_GROUPS + N_EXPERTS, 1].astype(jnp.int32) + pad_start[None, :]
        n_win = (tile_cnt + MOE_WINDOW - 1) // MOE_WINDOW
        w_end = jnp.cumsum(n_win, axis=1)
        w_start = w_end - n_win
        k_idx = jnp.arange(MAX_WINDOWS, dtype=jnp.int32)[None, :, None]
        owner = (w_start[:, None, :] <= k_idx) & (k_idx < w_end[:, None, :])
        win_dst = jnp.sum(jnp.where(owner, tile_base[:, None, :] + (k_idx - w_start[:, None, :]) * MOE_WINDOW, 0),
                          axis=-1)
        tabs = jnp.concatenate([win_dst, w_end[:, -1:],
                                jnp.zeros((win_dst.shape[0], ROUTE_LANES - MAX_WINDOWS - 1), jnp.int32)],
                               axis=1).reshape(-1, 1, ROUTE_LANES)

        buf = _dispatch(pad_end, tabs, route, x2, cap)
        y_buf = _experts(blk_e, n_used, buf, norm_ffn_g[l].reshape(1, d), expert_w_gate, expert_w_up,
                         expert_w_down, l)
        x2 = _combine(tabs, route, x2, y_buf)
    return x2.reshape(batch, seq, d)
```

```python
import functools
import math

import jax
import jax.numpy as jnp
from jax import lax
from jax.experimental import pallas as pl
from jax.experimental.pallas import tpu as pltpu

F32 = jnp.float32
BF16 = jnp.bfloat16

LANES = 128
SUBLANES = 8

D_MODEL = 1024
CHUNK = 64
EPS = 1e-6
MASK_VALUE = -1e30
TINY = 1e-30
LOG2_E = math.log2(math.e)
H_A = 4
DK_A = D_MODEL // 16
DV_A = 2 * DK_A
ROT_DIM = DK_A // 4
ROPE_THETA = 500000.0
H_B = 4
DK_B = D_MODEL // 16
DV_B = D_MODEL // 16
C_WIDTH = D_MODEL // 4
CONV_WIDTH = 3
N_GROUPS = 4
EXPERTS_PER_GROUP = 8
N_EXPERTS = N_GROUPS * EXPERTS_PER_GROUP
D_EXPERT = D_MODEL // 2
D_PACKED = D_MODEL // 2

W_A = H_A * DV_A
W_B = H_B * DK_B
ROUTE_LANES = LANES
N_LOGITS = N_GROUPS + N_EXPERTS
ROUTE_ROWS = -(-N_LOGITS // SUBLANES) * SUBLANES

TM_PROJ = 1024
TK = 256
KV_PER_Q = 2
TQ = KV_PER_Q * TK
TM_ROUTE = 512
MOE_BLOCK = 512
HGRN_BATCH = 2
HGRN_UNROLL = 2
MOE_WINDOW = 16
STAGE_PARTS = 3
MAX_WINDOWS = 2 * TM_ROUTE // MOE_WINDOW + N_EXPERTS
VMEM_LIMIT = 56 * 1024 * 1024


def _staging_rows(tm):
    return 2 * tm + N_EXPERTS * MOE_WINDOW


def _row_parts(rows, n=2):
    step = rows // n
    return [slice(r * step, (r + 1) * step) for r in range(n)]


def _pack_rows(x):
    half = x.shape[1] // 2
    lo = lax.bitcast_convert_type(x[:, :half], jnp.int32)
    hi = lax.bitcast_convert_type(x[:, half:], jnp.int32)
    return hi | lax.shift_right_logical(lo, 16)


def _unpack_rows(p):
    lo = lax.bitcast_convert_type(lax.shift_left(p, 16), F32)
    hi = lax.bitcast_convert_type(p & jnp.int32(-65536), F32)
    return jnp.concatenate([lo, hi], axis=1)


def _cparams(sem, vmem=VMEM_LIMIT):
    return pltpu.CompilerParams(dimension_semantics=sem, vmem_limit_bytes=vmem)


def _inproj_kernel(x_ref, g_ref, wqk_ref, wv_ref, whb_ref, wf_ref, wc_ref, gsum_ref, gqk_ref,
                   cos_ref, s1_ref, s2_ref, qk_out, v_out, hb_out, f_out, c_out):
    parts = _row_parts(x_ref.shape[0])
    xn = []
    for rows in parts:
        x = x_ref[rows, :]
        ms = jnp.mean(x * x, axis=-1, keepdims=True)
        xn.append((x * lax.rsqrt(ms + EPS) * g_ref[...]).astype(BF16))

    gsum = gsum_ref[...]
    for part in range(2):
        lo = part * W_A
        t = [jnp.dot(xr, wqk_ref[:, lo:lo + W_A], preferred_element_type=F32) for xr in xn]
        msq = [jnp.dot((tr * tr).astype(BF16), gsum, preferred_element_type=F32) for tr in t]
        for r, rows in enumerate(parts):
            tn = t[r] * lax.rsqrt(msq[r] + EPS) * gqk_ref[:, lo:lo + W_A]
            cos = cos_ref[rows, :]
            s1 = s1_ref[rows, :]
            s2 = s2_ref[rows, :]
            for h in range(H_A):
                slab = tn[:, h * LANES:(h + 1) * LANES]
                up = pltpu.roll(slab, LANES - ROT_DIM // 2, axis=1)
                dn = pltpu.roll(slab, ROT_DIM // 2, axis=1)
                qk_out[rows, lo + h * LANES:lo + (h + 1) * LANES] = (slab * cos + up * s1 + dn * s2).astype(BF16)

    for r, rows in enumerate(parts):
        v_out[rows, :] = jnp.dot(xn[r], wv_ref[...], preferred_element_type=F32).astype(BF16)
    for r, rows in enumerate(parts):
        hb_out[rows, :] = jnp.dot(xn[r], whb_ref[...], preferred_element_type=F32).astype(BF16)
    for r, rows in enumerate(parts):
        f_out[rows, :] = jnp.dot(xn[r], wf_ref[...], preferred_element_type=F32)
    for r, rows in enumerate(parts):
        c_out[rows, :] = jnp.dot(xn[r], wc_ref[...], preferred_element_type=F32).astype(BF16)


def _inproj(x2, g, wqk, wv, whb, wf, wc, gsum, gqk, cos_t, s1_t, s2_t):
    t_tok = x2.shape[0]
    tm = min(TM_PROJ, t_tok)
    assert t_tok % tm == 0, "token count must be a multiple of the projection tile"
    grid = (t_tok // tm,)
    row = lambda i: (i, 0)
    fixed = lambda i: (0, 0)
    return pl.pallas_call(
        _inproj_kernel,
        grid=grid,
        in_specs=[
            pl.BlockSpec((tm, D_MODEL), row),
            pl.BlockSpec((1, D_MODEL), fixed),
            pl.BlockSpec(wqk.shape, fixed),
            pl.BlockSpec(wv.shape, fixed),
            pl.BlockSpec(whb.shape, fixed),
            pl.BlockSpec(wf.shape, fixed),
            pl.BlockSpec(wc.shape, fixed),
            pl.BlockSpec(gsum.shape, fixed),
            pl.BlockSpec(gqk.shape, fixed),
            pl.BlockSpec((tm, LANES), row),
            pl.BlockSpec((tm, LANES), row),
            pl.BlockSpec((tm, LANES), row),
        ],
        out_specs=[
            pl.BlockSpec((tm, 2 * W_A), row),
            pl.BlockSpec((tm, W_A), row),
            pl.BlockSpec((tm, 3 * W_B), row),
            pl.BlockSpec((tm, W_B), row),
            pl.BlockSpec((tm, 3 * C_WIDTH), row),
        ],
        out_shape=[
            jax.ShapeDtypeStruct((t_tok, 2 * W_A), BF16),
            jax.ShapeDtypeStruct((t_tok, W_A), BF16),
            jax.ShapeDtypeStruct((t_tok, 3 * W_B), BF16),
            jax.ShapeDtypeStruct((t_tok, W_B), F32),
            jax.ShapeDtypeStruct((t_tok, 3 * C_WIDTH), BF16),
        ],
        compiler_params=_cparams(("parallel",)),
        name="inproj",
    )(x2, g, wqk, wv, whb, wf, wc, gsum, gqk, cos_t, s1_t, s2_t)


def _attn_kernel(lam_ref, q_ref, k_ref, v_ref, g_ref, o_ref, qc_ref, sa_ref, sb_ref, acc_ref, m_ref, l_ref, *,
                 lam_init):
    qi = pl.program_id(1)
    q = q_ref[...]
    lane = lax.broadcasted_iota(jnp.int32, (TQ, DV_A), 1)
    zero = jnp.zeros((TQ, DV_A), BF16)
    for h in range(H_A):
        qh = q[:, h * DV_A:(h + 1) * DV_A]
        qc_ref[h, 0] = jnp.where(lane < DK_A, qh, zero)
        qc_ref[h, 1] = jnp.where(lane >= DK_A, qh, zero)

    acc_ref[...] = jnp.zeros_like(acc_ref)
    m_ref[...] = jnp.full_like(m_ref, -jnp.inf)
    l_ref[...] = jnp.zeros_like(l_ref)

    kchunk = lax.broadcasted_iota(jnp.int32, (TK, TQ), 0) // CHUNK
    qchunk = lax.broadcasted_iota(jnp.int32, (TK, TQ), 1) // CHUNK

    def scores(j, s_ref, q0=0):
        start = pl.multiple_of(j * TK, TK)
        kb = k_ref[pl.ds(start, TK), :]
        for h in range(H_A):
            kh = kb[:, h * DV_A:(h + 1) * DV_A]
            for c in range(2):
                s_ref[h, c, :, q0:] = lax.dot_general(kh, qc_ref[h, c, q0:, :], (((1,), (1,)), ((), ())),
                                                      preferred_element_type=F32)

    def consume(j, s_ref, mask, q0=0):
        start = pl.multiple_of(j * TK, TK)
        vb = v_ref[pl.ds(start, TK), :]
        for h in range(H_A):
            vh = vb[:, h * DV_A:(h + 1) * DV_A]
            for c in range(2):
                s = s_ref[h, c, :, q0:]
                if mask is not None:
                    s = jnp.where(mask[:, q0:], s, MASK_VALUE)
                m_old = m_ref[h, c, :, q0:]
                m_new = jnp.maximum(m_old, jnp.max(s, axis=0, keepdims=True))
                alpha = jnp.exp2(m_old - m_new)
                p = jnp.exp2(s - m_new)
                l_ref[h, c, :, q0:] = alpha * l_ref[h, c, :, q0:] + jnp.sum(p, axis=0, keepdims=True)
                pv = lax.dot_general(vh, p.astype(BF16), (((0,), (0,)), ((), ())),
                                     preferred_element_type=F32)
                acc_ref[h, c, :, q0:] = alpha * acc_ref[h, c, :, q0:] + pv
                m_ref[h, c, :, q0:] = m_new

    first_diag = KV_PER_Q * qi
    scores(0, sa_ref)

    def body(jj, carry):
        j = 2 * jj
        scores(j + 1, sb_ref)
        consume(j, sa_ref, None)
        scores(j + 2, sa_ref)
        consume(j + 1, sb_ref, None)
        return carry

    lax.fori_loop(0, first_diag // 2, body, 0)
    for d in range(KV_PER_Q):
        cur, nxt = (sa_ref, sb_ref) if d % 2 == 0 else (sb_ref, sa_ref)
        if d + 1 < KV_PER_Q:
            scores(first_diag + d + 1, nxt, (d + 1) * TK)
        consume(first_diag + d, cur, (kchunk + d * (TK // CHUNK)) <= qchunk, d * TK)

    lam = lam_ref[0]
    for h in range(H_A):
        o = acc_ref[h, 0] / l_ref[h, 0] - lam * (acc_ref[h, 1] / l_ref[h, 1])
        ms = jnp.mean(o * o, axis=0, keepdims=True)
        y = o * lax.rsqrt(ms + EPS) * g_ref[...] * (1.0 - lam_init)
        o_ref[:, h * DV_A:(h + 1) * DV_A] = y.T.astype(BF16)


def _attention(qk, v, lam, subln_col, batch, seq, lam_init):
    assert TQ == KV_PER_Q * TK and KV_PER_Q % 2 == 0 and TK % CHUNK == 0 and seq % TQ == 0
    nq = seq // TQ
    t_tok = batch * seq
    kernel = functools.partial(_attn_kernel, lam_init=lam_init)
    return pl.pallas_call(
        kernel,
        grid=(batch, nq),
        in_specs=[
            pl.BlockSpec(memory_space=pltpu.SMEM),
            pl.BlockSpec((TQ, W_A), lambda b, i: (b * nq + i, 0)),
            pl.BlockSpec((seq, W_A), lambda b, i: (b, 1)),
            pl.BlockSpec((seq, W_A), lambda b, i: (b, 0)),
            pl.BlockSpec((DV_A, 1), lambda b, i: (0, 0)),
        ],
        out_specs=pl.BlockSpec((TQ, W_A), lambda b, i: (b * nq + i, 0)),
        out_shape=jax.ShapeDtypeStruct((t_tok, W_A), BF16),
        scratch_shapes=[
            pltpu.VMEM((H_A, 2, TQ, DV_A), BF16),
            pltpu.VMEM((H_A, 2, TK, TQ), F32),
            pltpu.VMEM((H_A, 2, TK, TQ), F32),
            pltpu.VMEM((H_A, 2, DV_A, TQ), F32),
            pltpu.VMEM((H_A, 2, 1, TQ), F32),
            pltpu.VMEM((H_A, 2, 1, TQ), F32),
        ],
        compiler_params=_cparams(("parallel", "arbitrary")),
        name="diff_attn",
    )(lam, qk, qk, v, subln_col)


_LEVELS = (32, 16, 8, 4, 2, 1)


def _ref_rows(g, h):
    n = g.shape[1]
    if 2 * h >= SUBLANES:
        pieces = []
        for m in range(CHUNK // (2 * h)):
            r = 2 * h * m + h - 1
            pieces.append(jnp.broadcast_to(g[r:r + 1, :], (2 * h, n)))
        return pieces[0] if len(pieces) == 1 else jnp.concatenate(pieces, axis=0)
    g3 = g.reshape(CHUNK // SUBLANES, SUBLANES, n)
    sub = lax.broadcasted_iota(jnp.int32, g3.shape, 1)
    out = None
    for m in reversed(range(SUBLANES // (2 * h))):
        r = 2 * h * m + h - 1
        piece = jnp.broadcast_to(g3[:, r:r + 1, :], g3.shape)
        out = piece if out is None else jnp.where(sub < 2 * h * (m + 1), piece, out)
    return out.reshape(CHUNK, n)


def _stack_heads(a, lane_head):
    zero = jnp.zeros_like(a)
    return jnp.concatenate([jnp.where(lane_head == hh, a, zero) for hh in range(H_B)], axis=0)


def _hgrn_kernel(hb_ref, f_ref, c_ref, lb_ref, ng_ref, cw_ref, gavg_ref, o_ref, state_ref, hbuf_ref, *, seq):
    n_chunks = seq // CHUNK
    nb = hb_ref.shape[0] // seq
    lb = lb_ref[...]
    ng = ng_ref[...]
    gavg = gavg_ref[...]

    row = lax.broadcasted_iota(jnp.int32, (CHUNK, W_B), 0)
    lane = lax.broadcasted_iota(jnp.int32, (CHUNK, W_B), 1)
    lane_head = lane // DK_B
    lane_s = lane % CHUNK
    tri = (lax.broadcasted_iota(jnp.int32, (CHUNK, CHUNK), 1)
           <= lax.broadcasted_iota(jnp.int32, (CHUNK, CHUNK), 0)).astype(BF16)
    lvl_mask = [((row // (2 * h)) == (lane_s // (2 * h))) & ((row % (2 * h)) >= h) & ((lane_s % (2 * h)) < h)
                for h in _LEVELS]
    eye_mask = row == lane_s
    r4 = lax.broadcasted_iota(jnp.int32, (H_B * DK_B, W_B), 0) // DK_B
    c4 = lax.broadcasted_iota(jnp.int32, (H_B * DK_B, W_B), 1) // DK_B
    bd_mask = r4 == c4

    state_ref[...] = jnp.zeros_like(state_ref)
    nt = (((1,), (1,)), ((), ()))
    tn = (((0,), (0,)), ((), ()))
    unroll = HGRN_UNROLL if n_chunks % HGRN_UNROLL == 0 else 1
    each = range(nb * unroll)

    def chunk(i, carry):
        rows = [pl.ds(pl.multiple_of((e % nb) * seq + (i * unroll + e // nb) * CHUNK, CHUNK), CHUNK) for e in each]
        hb = [hb_ref[rows[j], :].astype(F32) for j in each]
        zq = [t[:, 0:W_B] for t in hb]
        v16 = [t[:, W_B:2 * W_B].astype(BF16) for t in hb]
        zg = [t[:, 2 * W_B:3 * W_B] for t in hb]
        zf = [f_ref[rows[j], :] for j in each]

        logf = [jnp.log(jnp.maximum(lb + (1.0 - lb) * jax.nn.sigmoid(z), TINY)) for z in zf]
        key = [(1.0 - lb) * jax.nn.sigmoid(-z) for z in zf]
        q = [z * jax.nn.sigmoid(z) for z in zq]

        g = []
        for x in logf:
            hi = x.astype(BF16)
            r1 = x - hi.astype(F32)
            mid = r1.astype(BF16)
            lo = (r1 - mid.astype(F32)).astype(BF16)
            g.append((jnp.dot(tri, hi, preferred_element_type=F32) + jnp.dot(tri, mid, preferred_element_type=F32)
                      + jnp.dot(tri, lo, preferred_element_type=F32)) * LOG2_E)

        q16 = [t.astype(BF16) for t in q]
        k16 = [t.astype(BF16) for t in key]
        a_all = [jnp.where(eye_mask, lax.dot_general(q16[j], _stack_heads(k16[j], lane_head),
                                                     nt, preferred_element_type=F32), 0.0) for j in each]
        for h, mask in zip(_LEVELS, lvl_mask):
            w = [jnp.exp2(-jnp.abs(g[j] - _ref_rows(g[j], h))).astype(BF16) for j in each]
            ah = [lax.dot_general(q16[j] * w[j], _stack_heads(k16[j] * w[j], lane_head),
                                  nt, preferred_element_type=F32) for j in each]
            a_all = [a_all[j] + jnp.where(mask, ah[j], 0.0) for j in each]
        o = [jnp.dot(a_all[j].astype(BF16), _stack_heads(v16[j], lane_head), preferred_element_type=F32) for j in each]

        qg = [(q[j] * jnp.exp2(g[j])).astype(BF16) for j in each]
        g_last = [t[CHUNK - 1:CHUNK, :] for t in g]
        upd = [lax.dot_general(v16[j], (key[j] * jnp.exp2(g_last[j] - g[j])).astype(BF16), tn,
                               preferred_element_type=F32) for j in each]
        for j in each:
            st = state_ref[j % nb]
            o[j] = o[j] + lax.dot_general(qg[j], st.astype(BF16), nt, preferred_element_type=F32)
            state_ref[j % nb] = jnp.exp2(g_last[j]) * st + jnp.where(bd_mask, upd[j], 0.0)

        ms = [jnp.dot((t * t).astype(BF16), gavg, preferred_element_type=F32) for t in o]
        for j in each:
            y = o[j] * lax.rsqrt(ms[j] + EPS) * ng * (zg[j] * jax.nn.sigmoid(zg[j]))
            o_ref[rows[j], 0:W_B] = y.astype(BF16)
        return carry

    lax.fori_loop(0, n_chunks // unroll, chunk, 0)

    piece = min(512, seq)
    hbuf_ref[0:SUBLANES, :] = jnp.zeros((SUBLANES, C_WIDTH), F32)
    for j in range(nb):
        for p in range(seq // piece):
            cblk = c_ref[j * seq + p * piece:j * seq + (p + 1) * piece, :].astype(F32)
            hbuf_ref[SUBLANES + p * piece:SUBLANES + (p + 1) * piece, :] = (cblk[:, C_WIDTH:2 * C_WIDTH]
                                                                             * cblk[:, 2 * C_WIDTH:])
        for p in range(seq // piece):
            base = SUBLANES + p * piece
            y = (cw_ref[0:1, :] * hbuf_ref[base - 2:base - 2 + piece, :]
                 + cw_ref[1:2, :] * hbuf_ref[base - 1:base - 1 + piece, :]
                 + cw_ref[2:3, :] * hbuf_ref[base:base + piece, :])
            lo_r = j * seq + p * piece
            bgate = c_ref[lo_r:lo_r + piece, 0:C_WIDTH].astype(F32)
            o_ref[lo_r:lo_r + piece, W_B:W_B + C_WIDTH] = (bgate * y).astype(BF16)


def _hgrn_conv(hb, f, cv, lb, ng, cw, gavg, batch, seq):
    t_tok = batch * seq
    nb = HGRN_BATCH if batch % HGRN_BATCH == 0 else 1
    fixed = lambda b: (0, 0)
    return pl.pallas_call(
        functools.partial(_hgrn_kernel, seq=seq),
        grid=(batch // nb,),
        in_specs=[
            pl.BlockSpec((nb * seq, 3 * W_B), lambda b: (b, 0)),
            pl.BlockSpec((nb * seq, W_B), lambda b: (b, 0)),
            pl.BlockSpec((nb * seq, 3 * C_WIDTH), lambda b: (b, 0)),
            pl.BlockSpec((1, W_B), fixed),
            pl.BlockSpec((1, W_B), fixed),
            pl.BlockSpec((CONV_WIDTH, C_WIDTH), fixed),
            pl.BlockSpec((W_B, W_B), fixed),
        ],
        out_specs=pl.BlockSpec((nb * seq, W_B + C_WIDTH), lambda b: (b, 0)),
        out_shape=jax.ShapeDtypeStruct((t_tok, W_B + C_WIDTH), BF16),
        scratch_shapes=[
            pltpu.VMEM((nb, H_B * DV_B, W_B), F32),
            pltpu.VMEM((seq + SUBLANES, C_WIDTH), F32),
        ],
        compiler_params=_cparams(("parallel",)),
        name="hgrn_conv",
    )(hb, f, cv, lb, ng, cw, gavg)


def _outproj_router_kernel(oa_ref, obc_ref, x_ref, wo_ref, g_ref, wr_ref, br_ref,
                           xo_ref, route_ref, route_t_ref, tab_ref, cnt_ref, carry_ref):
    i = pl.program_id(0)

    @pl.when(i == 0)
    def _():
        carry_ref[...] = jnp.zeros_like(carry_ref)

    mixed = (jnp.dot(oa_ref[...], wo_ref[0:W_A, :], preferred_element_type=F32)
             + jnp.dot(obc_ref[...], wo_ref[W_A:, :], preferred_element_type=F32))
    xn = x_ref[...] + mixed
    xo_ref[...] = xn

    ms = jnp.mean(xn * xn, axis=-1, keepdims=True)
    hn = (xn * lax.rsqrt(ms + EPS) * g_ref[...]).astype(BF16)
    logits = jnp.dot(hn, wr_ref[...], preferred_element_type=F32) + br_ref[...]

    tm = logits.shape[0]
    lt = logits.T[0:ROUTE_ROWS, :]
    r = lax.broadcasted_iota(jnp.int32, (ROUTE_ROWS, tm), 0)
    neg = -jnp.inf
    big = ROUTE_ROWS

    def top(cand):
        val = jnp.max(cand, axis=0, keepdims=True)
        idx = jnp.min(jnp.where(cand == val, r, big), axis=0, keepdims=True)
        return val, idx

    glog = jnp.where(r < N_GROUPS, lt, neg)
    gmax, gidx = top(glog)
    p_g = 1.0 / jnp.sum(jnp.exp(glog - gmax), axis=0, keepdims=True)
    off = r - (N_GROUPS + EXPERTS_PER_GROUP * gidx)
    elog = jnp.where(jnp.abs(2 * off - (EXPERTS_PER_GROUP - 1)) < EXPERTS_PER_GROUP, lt, neg)
    v1, i1 = top(elog)
    v2, i2 = top(jnp.where(r == i1, neg, elog))
    t2 = jnp.exp(v2 - v1)
    w1 = p_g / (1.0 + t2)
    w2 = p_g * t2 / (1.0 + t2)

    sel1 = r == i1
    sel2 = r == i2
    onehot = jnp.where(sel1, 1.0, jnp.where(sel2, 1.0, 0.0))
    earlier = (lax.broadcasted_iota(jnp.int32, (tm, tm), 0)
               < lax.broadcasted_iota(jnp.int32, (tm, tm), 1)).astype(BF16)
    local = jnp.dot(onehot.astype(BF16), earlier, preferred_element_type=F32)
    cnt = jnp.sum(onehot, axis=1, keepdims=True)
    wrows = jnp.floor((cnt + (MOE_WINDOW - 1)) * (1.0 / MOE_WINDOW)) * MOE_WINDOW
    below = (lax.broadcasted_iota(jnp.int32, (ROUTE_ROWS, ROUTE_LANES), 1)
             < lax.broadcasted_iota(jnp.int32, (ROUTE_ROWS, ROUTE_LANES), 0)).astype(BF16)
    wrows_rows = jnp.concatenate([jnp.broadcast_to(wrows, (ROUTE_ROWS, ROUTE_LANES)),
                                  jnp.zeros((ROUTE_LANES - ROUTE_ROWS, ROUTE_LANES), F32)], axis=0).astype(BF16)
    soff = jnp.dot(below, wrows_rows, preferred_element_type=F32)[:, 0:1]
    slot = local + soff
    row1 = jnp.sum(jnp.where(sel1, slot, 0.0), axis=0, keepdims=True)
    row2 = jnp.sum(jnp.where(sel2, slot, 0.0), axis=0, keepdims=True)

    k = lax.broadcasted_iota(jnp.int32, (SUBLANES, tm), 0)
    rec = jnp.where(k == 0, (i1 - N_GROUPS).astype(F32),
          jnp.where(k == 1, (i2 - N_GROUPS).astype(F32),
          jnp.where(k == 2, w1,
          jnp.where(k == 3, w2,
          jnp.where(k == 4, row1,
          jnp.where(k == 5, row2, 0.0))))))
    route_t_ref[...] = rec
    route_ref[...] = jnp.concatenate([rec, jnp.zeros((ROUTE_LANES - SUBLANES, tm), F32)], axis=0).T

    lane = lax.broadcasted_iota(jnp.int32, (ROUTE_ROWS, ROUTE_LANES), 1)
    tab_ref[0] = jnp.where(lane == 0, cnt, jnp.where(lane == 1, carry_ref[...], 0.0))
    carry_ref[...] = carry_ref[...] + jnp.floor((cnt + (SUBLANES - 1)) * (1.0 / SUBLANES)) * SUBLANES
    cnt_ref[...] = carry_ref[...]


def _outproj_router(o_a, o_bc, x2, wo, g, wr, br):
    t_tok = x2.shape[0]
    tm = min(TM_ROUTE, t_tok)
    assert t_tok % tm == 0, "token count must be a multiple of the routing tile"
    row = lambda i: (i, 0)
    fixed = lambda i: (0, 0)
    return pl.pallas_call(
        _outproj_router_kernel,
        grid=(t_tok // tm,),
        in_specs=[
            pl.BlockSpec((tm, W_A), row),
            pl.BlockSpec((tm, W_B + C_WIDTH), row),
            pl.BlockSpec((tm, D_MODEL), row),
            pl.BlockSpec(wo.shape, fixed),
            pl.BlockSpec((1, D_MODEL), fixed),
            pl.BlockSpec(wr.shape, fixed),
            pl.BlockSpec((1, ROUTE_LANES), fixed),
        ],
        out_specs=[
            pl.BlockSpec((tm, D_MODEL), row),
            pl.BlockSpec((tm, ROUTE_LANES), row),
            pl.BlockSpec((SUBLANES, tm), row),
            pl.BlockSpec((1, ROUTE_ROWS, ROUTE_LANES), lambda i: (i, 0, 0)),
            pl.BlockSpec((ROUTE_ROWS, ROUTE_LANES), fixed),
        ],
        out_shape=[
            jax.ShapeDtypeStruct((t_tok, D_MODEL), F32),
            jax.ShapeDtypeStruct((t_tok, ROUTE_LANES), F32),
            jax.ShapeDtypeStruct((t_tok // tm * SUBLANES, tm), F32),
            jax.ShapeDtypeStruct((t_tok // tm, ROUTE_ROWS, ROUTE_LANES), F32),
            jax.ShapeDtypeStruct((ROUTE_ROWS, ROUTE_LANES), F32),
        ],
        scratch_shapes=[pltpu.VMEM((ROUTE_ROWS, ROUTE_LANES), F32)],
        compiler_params=_cparams(("arbitrary",)),
        name="outproj_router",
    )(o_a, o_bc, x2, wo, g, wr, br)


def _start_copies(n, make_copy):
    def start(k, carry):
        make_copy(k).start()
        return carry

    lax.fori_loop(0, n, start, 0)


def _wait_copies(n, make_copy):
    def wait(k, carry):
        make_copy(k).wait()
        return carry

    lax.fori_loop(0, n, wait, 0)


def _window_copy_loops(n, make_copy):
    _start_copies(n, make_copy)
    _wait_copies(n, make_copy)


def _dispatch_kernel(pend_ref, tab_ref, tabp_ref, route_ref, x_ref, buf_ref, xp_ref, xq_ref, sem, semq):
    tm = x_ref.shape[0]
    r_st = xp_ref.shape[0]
    step = pl.program_id(0)

    @pl.when(pl.program_id(0) == 0)
    def _():
        xp_ref[...] = jnp.zeros_like(xp_ref)
        copies = []
        for e in range(N_EXPERTS):
            end = pend_ref[e]
            copies.append((None, pltpu.make_async_copy(
                xp_ref.at[0:MOE_BLOCK, :],
                buf_ref.at[pl.ds(pl.multiple_of(end - MOE_BLOCK, SUBLANES), MOE_BLOCK), :], sem)))
            copies.append((end >= MOE_BLOCK + MOE_WINDOW, pltpu.make_async_copy(
                xp_ref.at[0:MOE_WINDOW, :],
                buf_ref.at[pl.ds(pl.multiple_of(jnp.maximum(end - MOE_BLOCK - MOE_WINDOW, 0), SUBLANES), MOE_WINDOW), :],
                sem)))
        for cond, cp in copies:
            if cond is None:
                cp.start()
            else:
                pl.when(cond)(cp.start)
        for cond, cp in copies:
            if cond is None:
                cp.wait()
            else:
                pl.when(cond)(cp.wait)

        def tail_copy(k):
            return pltpu.make_async_copy(
                xp_ref.at[0:MOE_BLOCK, :],
                buf_ref.at[pl.ds(pl.multiple_of(k * MOE_BLOCK, MOE_BLOCK), MOE_BLOCK), :], sem)

        first_unused = pend_ref[N_EXPERTS - 1] // MOE_BLOCK
        n_unused = buf_ref.shape[0] // MOE_BLOCK - first_unused
        _window_copy_loops(n_unused, lambda k: tail_copy(first_unused + k))

    r0 = route_ref[4:5, :].astype(jnp.int32)
    r1 = route_ref[5:6, :].astype(jnp.int32)
    part = r_st // STAGE_PARTS
    srow = lax.broadcasted_iota(jnp.int32, (part, tm), 0)

    def window(tabs, stage_ref, s):
        def make_copy(k):
            src = pl.multiple_of(k * MOE_WINDOW, MOE_WINDOW)
            return pltpu.make_async_copy(
                stage_ref.at[pl.ds(src, MOE_WINDOW), :],
                buf_ref.at[pl.ds(pl.multiple_of(tabs[0, 0, k], SUBLANES), MOE_WINDOW), :], s)
        return tabs[0, 0, MAX_WINDOWS], make_copy

    def run(cur_ref, cur_sem, prev_ref, prev_sem):
        x16 = x_ref[...].astype(BF16)
        for p in range(STAGE_PARTS):
            sel = jnp.where(srow + p * part == r0, 1.0, jnp.where(srow + p * part == r1, 1.0, 0.0)).astype(BF16)
            cur_ref[p * part:(p + 1) * part, :] = _pack_rows(jnp.dot(sel, x16, preferred_element_type=F32))

        @pl.when(step > 0)
        def _():
            _wait_copies(*window(tabp_ref, prev_ref, prev_sem))

        _start_copies(*window(tab_ref, cur_ref, cur_sem))

        @pl.when(step == pl.num_programs(0) - 1)
        def _():
            _wait_copies(*window(tab_ref, cur_ref, cur_sem))

    @pl.when(step % 2 == 0)
    def _():
        run(xp_ref, sem, xq_ref, semq)

    @pl.when(step % 2 == 1)
    def _():
        run(xq_ref, semq, xp_ref, sem)


def _dispatch(pad_end, tabs, route_t, x2, cap):
    t_tok = x2.shape[0]
    tm = min(TM_ROUTE, t_tok)
    assert t_tok % tm == 0, "token count must be a multiple of the routing tile"
    n_tiles = t_tok // tm
    return pl.pallas_call(
        _dispatch_kernel,
        grid=(n_tiles,),
        in_specs=[
            pl.BlockSpec(memory_space=pltpu.SMEM),
            pl.BlockSpec((1, 1, ROUTE_LANES), lambda i: (i, 0, 0), memory_space=pltpu.SMEM),
            pl.BlockSpec((1, 1, ROUTE_LANES), lambda i: (jnp.maximum(i - 1, 0), 0, 0), memory_space=pltpu.SMEM),
            pl.BlockSpec((SUBLANES, tm), lambda i: (i, 0)),
            pl.BlockSpec((tm, D_MODEL), lambda i: (i, 0)),
        ],
        out_specs=pl.BlockSpec(memory_space=pl.ANY),
        out_shape=jax.ShapeDtypeStruct((cap, D_PACKED), jnp.int32),
        scratch_shapes=[pltpu.VMEM((_staging_rows(tm), D_PACKED), jnp.int32),
                        pltpu.VMEM((_staging_rows(tm), D_PACKED), jnp.int32),
                        pltpu.SemaphoreType.DMA(()), pltpu.SemaphoreType.DMA(())],
        compiler_params=_cparams(("arbitrary",)),
        name="moe_dispatch",
    )(pad_end, tabs, tabs, route_t, x2)


def _experts_kernel(blk_e_ref, n_used_ref, x_ref, g_ref, wg_ref, wu_ref, wd_ref, y_ref, wg16, wu16, wd16):
    i = pl.program_id(0)
    used = i < n_used_ref[0]

    @pl.when(used & ((i == 0) | (blk_e_ref[i] != blk_e_ref[jnp.maximum(i - 1, 0)])))
    def _():
        wg16[...] = wg_ref[0, 0].astype(BF16)
        wu16[...] = wu_ref[0, 0].astype(BF16)
        wd16[...] = wd_ref[0, 0].astype(BF16)

    @pl.when(used)
    def _():
        wg = wg16[...]
        wu = wu16[...]
        wd = wd16[...]
        parts = _row_parts(x_ref.shape[0])
        hn = []
        for rows in parts:
            x = _unpack_rows(x_ref[rows, :])
            ms = jnp.mean(x * x, axis=-1, keepdims=True)
            hn.append((x * lax.rsqrt(ms + EPS) * g_ref[...]).astype(BF16))
        a = [jnp.dot(t, wg, preferred_element_type=F32) for t in hn]
        u = [jnp.dot(t, wu, preferred_element_type=F32) for t in hn]
        act = [(a[r] * jax.nn.sigmoid(a[r]) * u[r]).astype(BF16) for r in range(len(parts))]
        for r, rows in enumerate(parts):
            y = jnp.dot(act[r], wd, preferred_element_type=F32)
            y_ref[rows, :] = _pack_rows(y.astype(BF16).astype(F32))


def _experts(blk_e, n_used, buf, g, wg, wu, wd, layer):
    n_blocks = buf.shape[0] // MOE_BLOCK
    grid_spec = pltpu.PrefetchScalarGridSpec(
        num_scalar_prefetch=2,
        grid=(n_blocks,),
        in_specs=[
            pl.BlockSpec((MOE_BLOCK, D_PACKED), lambda i, be, nu: (jnp.minimum(i, nu[0] - 1), 0)),
            pl.BlockSpec((1, D_MODEL), lambda i, be, nu: (0, 0)),
            pl.BlockSpec((1, 1, D_MODEL, D_EXPERT), lambda i, be, nu: (layer, be[i], 0, 0)),
            pl.BlockSpec((1, 1, D_MODEL, D_EXPERT), lambda i, be, nu: (layer, be[i], 0, 0)),
            pl.BlockSpec((1, 1, D_EXPERT, D_MODEL), lambda i, be, nu: (layer, be[i], 0, 0)),
        ],
        out_specs=pl.BlockSpec((MOE_BLOCK, D_PACKED), lambda i, be, nu: (jnp.minimum(i, nu[0] - 1), 0)),
        scratch_shapes=[pltpu.VMEM((D_MODEL, D_EXPERT), BF16), pltpu.VMEM((D_MODEL, D_EXPERT), BF16),
                        pltpu.VMEM((D_EXPERT, D_MODEL), BF16)],
    )
    return pl.pallas_call(
        _experts_kernel,
        grid_spec=grid_spec,
        out_shape=jax.ShapeDtypeStruct(buf.shape, jnp.int32),
        input_output_aliases={2: 0},
        compiler_params=_cparams(("arbitrary",)),
        name="moe_experts",
    )(blk_e, n_used, buf, g, wg, wu, wd)


def _combine_kernel(tab_ref, tabn_ref, route_ref, x_ref, y_ref, o_ref, ya_ref, yb_ref, sema, semb):
    tm = x_ref.shape[0]
    r_st = ya_ref.shape[0]
    step = pl.program_id(0)

    def window(tabs, stage_ref, s):
        def make_copy(k):
            dst = pl.multiple_of(k * MOE_WINDOW, MOE_WINDOW)
            return pltpu.make_async_copy(y_ref.at[pl.ds(pl.multiple_of(tabs[0, 0, k], SUBLANES), MOE_WINDOW), :],
                                         stage_ref.at[pl.ds(dst, MOE_WINDOW), :], s)
        return tabs[0, 0, MAX_WINDOWS], make_copy

    @pl.when(step == 0)
    def _():
        ya_ref[...] = jnp.zeros_like(ya_ref)
        yb_ref[...] = jnp.zeros_like(yb_ref)
        _start_copies(*window(tab_ref, ya_ref, sema))

    route = route_ref[...]
    r0 = route[:, 4:5].astype(jnp.int32)
    r1 = route[:, 5:6].astype(jnp.int32)
    part = r_st // STAGE_PARTS
    col = lax.broadcasted_iota(jnp.int32, (tm, part), 1)

    def run(cur_ref, cur_sem, nxt_ref, nxt_sem):
        @pl.when(step + 1 < pl.num_programs(0))
        def _():
            _start_copies(*window(tabn_ref, nxt_ref, nxt_sem))

        _wait_copies(*window(tab_ref, cur_ref, cur_sem))
        acc = x_ref[...]
        for p in range(STAGE_PARTS):
            gate = jnp.where(col + p * part == r0, route[:, 2:3],
                             jnp.where(col + p * part == r1, route[:, 3:4], 0.0)).astype(BF16)
            rows = _unpack_rows(cur_ref[p * part:(p + 1) * part, :]).astype(BF16)
            acc = acc + jnp.dot(gate, rows, preferred_element_type=F32)
        o_ref[...] = acc

    @pl.when(step % 2 == 0)
    def _():
        run(ya_ref, sema, yb_ref, semb)

    @pl.when(step % 2 == 1)
    def _():
        run(yb_ref, semb, ya_ref, sema)


def _combine(tabs, route, x2, y_buf):
    t_tok = x2.shape[0]
    tm = min(TM_ROUTE, t_tok)
    assert t_tok % tm == 0, "token count must be a multiple of the routing tile"
    n_tiles = t_tok // tm
    return pl.pallas_call(
        _combine_kernel,
        grid=(n_tiles,),
        in_specs=[
            pl.BlockSpec((1, 1, ROUTE_LANES), lambda i: (i, 0, 0), memory_space=pltpu.SMEM),
            pl.BlockSpec((1, 1, ROUTE_LANES), lambda i: (jnp.minimum(i + 1, n_tiles - 1), 0, 0),
                         memory_space=pltpu.SMEM),
            pl.BlockSpec((tm, ROUTE_LANES), lambda i: (i, 0)),
            pl.BlockSpec((tm, D_MODEL), lambda i: (i, 0)),
            pl.BlockSpec(memory_space=pl.ANY),
        ],
        out_specs=pl.BlockSpec((tm, D_MODEL), lambda i: (i, 0)),
        out_shape=jax.ShapeDtypeStruct((t_tok, D_MODEL), F32),
        scratch_shapes=[pltpu.VMEM((_staging_rows(tm), D_PACKED), jnp.int32),
                        pltpu.VMEM((_staging_rows(tm), D_PACKED), jnp.int32),
                        pltpu.SemaphoreType.DMA(()), pltpu.SemaphoreType.DMA(())],
        compiler_params=_cparams(("arbitrary",)),
        name="moe_combine",
    )(tabs, tabs, route, x2, y_buf)


def _rope_tables(positions):
    half = ROT_DIM // 2
    inv_freq = ROPE_THETA ** (-jnp.arange(0, ROT_DIM, 2, dtype=F32) / ROT_DIM)
    d = jnp.arange(LANES) % DK_A
    freq_lane = jnp.where(d < ROT_DIM, inv_freq[d % half], 0.0)
    ang = positions.reshape(-1, 1).astype(F32) * freq_lane[None, :]
    cos, sin = jnp.cos(ang), jnp.sin(ang)
    s1 = jnp.where(d < half, -sin, 0.0)
    s2 = jnp.where((d >= half) & (d < ROT_DIM), sin, 0.0)
    return cos, s1, s2


def _block_avg(width, group):
    idx = jnp.arange(width) // group
    return jnp.where(idx[:, None] == idx[None, :], 1.0 / group, 0.0).astype(BF16)


def kernel(x, positions, norm_mix_g, w_in, attn_q_norm_g, attn_k_norm_g, lambda_q1, lambda_k1, lambda_q2,
           lambda_k2, attn_subln_g, hgrn_lower_bounds, hgrn_norm_g, conv_w, w_out, norm_ffn_g, router_group_w,
           router_group_b, router_expert_w, router_expert_b, expert_w_gate, expert_w_up, expert_w_down):
    batch, seq, d = x.shape
    assert d == D_MODEL
    depth = w_in.shape[0]
    t_tok = batch * seq
    n_assign = 2 * t_tok
    n_route_tiles = t_tok // min(TM_ROUTE, t_tok)
    n_blocks = (n_assign + (SUBLANES - 1) * n_route_tiles * N_EXPERTS) // MOE_BLOCK + 2 * N_EXPERTS + 1
    cap = n_blocks * MOE_BLOCK

    lb_probs = jax.nn.softmax(hgrn_lower_bounds.astype(F32), axis=0)
    lower_bounds = jnp.cumsum(lb_probs, axis=0) - lb_probs[0:1]
    cos_t, s1_t, s2_t = _rope_tables(positions)
    gsum_a = _block_avg(W_A, DK_A)
    gavg_b = _block_avg(W_B, DV_B)

    o_q, o_k, o_v = 0, W_A, 2 * W_A
    o_bq = 3 * W_A
    o_bf, o_bi, o_bg = o_bq + W_B, o_bq + 2 * W_B, o_bq + 3 * W_B
    o_c = o_bq + 4 * W_B

    x2 = x.reshape(t_tok, d)
    for l in range(depth):
        w = w_in[l]
        wqk = w[:, o_q:o_v].astype(BF16)
        wv = w[:, o_v:o_bq].astype(BF16)
        whb = jnp.concatenate([w[:, o_bq:o_bf], w[:, o_bi:o_c]], axis=1).astype(BF16)
        wf = w[:, o_bf:o_bi].astype(BF16)
        wc = w[:, o_c:].astype(BF16)
        gqk = jnp.concatenate([jnp.tile(attn_q_norm_g[l], 2 * H_A) * (DK_A ** -0.5 * LOG2_E),
                               jnp.tile(attn_k_norm_g[l], 2 * H_A)]).reshape(1, 2 * W_A).astype(F32)
        qk, v, hb, f, cv = _inproj(x2, norm_mix_g[l].reshape(1, d), wqk, wv, whb, wf, wc, gsum_a, gqk,
                                   cos_t, s1_t, s2_t)

        lam_init = 0.8 - 0.6 * math.exp(-0.3 * l)
        lam = (jnp.exp(jnp.sum(lambda_q1[l].astype(F32) * lambda_k1[l].astype(F32)))
               - jnp.exp(jnp.sum(lambda_q2[l].astype(F32) * lambda_k2[l].astype(F32))) + lam_init).reshape(1)
        o_a = _attention(qk, v, lam, attn_subln_g[l].reshape(DV_A, 1).astype(F32), batch, seq, lam_init)

        o_bc = _hgrn_conv(hb, f, cv, lower_bounds[l].reshape(1, W_B), jnp.tile(hgrn_norm_g[l], H_B).reshape(1, W_B),
                          conv_w[l], gavg_b, batch, seq)

        wr = jnp.concatenate([router_group_w[l], router_expert_w[l],
                              jnp.zeros((d, ROUTE_LANES - N_LOGITS), F32)], axis=1).astype(BF16)
        br = jnp.concatenate([router_group_b[l], router_expert_b[l],
                              jnp.zeros((ROUTE_LANES - N_LOGITS,), F32)]).reshape(1, ROUTE_LANES)
        x2, route, route_t, tab, cnt = _outproj_router(o_a, o_bc, x2, w_out[l].astype(BF16), norm_ffn_g[l].reshape(1, d), wr, br)

        counts = cnt[N_GROUPS:N_GROUPS + N_EXPERTS, 0].astype(jnp.int32)
        padded = (counts + MOE_WINDOW + MOE_BLOCK - 1) // MOE_BLOCK * MOE_BLOCK
        pad_end = jnp.cumsum(padded)
        pad_start = pad_end - padded
        blk_pos = jnp.arange(n_blocks, dtype=jnp.int32) * MOE_BLOCK
        blk_e = jnp.minimum(jnp.sum((pad_end[None, :] <= blk_pos[:, None]).astype(jnp.int32), axis=1),
                            N_EXPERTS - 1)
        n_used = (pad_end[-1] // MOE_BLOCK).astype(jnp.int32).reshape(1)
        tile_cnt = tab[:, N_GROUPS:N_GROUPS + N_EXPERTS, 0].astype(jnp.int32)
        tile_base = tab[:, N_GROUPS:N_GROUPS + N_EXPERTS, 1].astype(jnp.int32) + pad_start[None, :]
        n_win = (tile_cnt + MOE_WINDOW - 1) // MOE_WINDOW
        w_end = jnp.cumsum(n_win, axis=1)
        w_start = w_end - n_win
        k_idx = jnp.arange(MAX_WINDOWS, dtype=jnp.int32)[None, :, None]
        owner = (w_start[:, None, :] <= k_idx) & (k_idx < w_end[:, None, :])
        win_dst = jnp.sum(jnp.where(owner, tile_base[:, None, :] + (k_idx - w_start[:, None, :]) * MOE_WINDOW, 0),
                          axis=-1)
        tabs = jnp.concatenate([win_dst, w_end[:, -1:],
                                jnp.zeros((win_dst.shape[0], ROUTE_LANES - MAX_WINDOWS - 1), jnp.int32)],
                               axis=1).reshape(-1, 1, ROUTE_LANES)

        buf = _dispatch(pad_end, tabs, route_t, x2, cap)
        y_buf = _experts(blk_e, n_used, buf, norm_ffn_g[l].reshape(1, d), expert_w_gate, expert_w_up,
                         expert_w_down, l)
        x2 = _combine(tabs, route, x2, y_buf)
    return x2.reshape(batch, seq, d)
```

```python
import functools
import math

import jax
import jax.numpy as jnp
from jax import lax
from jax.experimental import pallas as pl
from jax.experimental.pallas import tpu as pltpu

F32 = jnp.float32
BF16 = jnp.bfloat16

LANES = 128
SUBLANES = 8

D_MODEL = 1024
CHUNK = 64
EPS = 1e-6
MASK_VALUE = -1e30
TINY = 1e-30
LOG2_E = math.log2(math.e)
H_A = 4
DK_A = D_MODEL // 16
DV_A = 2 * DK_A
ROT_DIM = DK_A // 4
ROPE_THETA = 500000.0
H_B = 4
DK_B = D_MODEL // 16
DV_B = D_MODEL // 16
C_WIDTH = D_MODEL // 4
CONV_WIDTH = 3
N_GROUPS = 4
EXPERTS_PER_GROUP = 8
N_EXPERTS = N_GROUPS * EXPERTS_PER_GROUP
D_EXPERT = D_MODEL // 2
D_PACKED = D_MODEL // 2

W_A = H_A * DV_A
W_B = H_B * DK_B
ROUTE_LANES = LANES
N_LOGITS = N_GROUPS + N_EXPERTS
ROUTE_ROWS = -(-N_LOGITS // SUBLANES) * SUBLANES

TM_PROJ = 1024
TK = 256
KV_PER_Q = 2
TQ = KV_PER_Q * TK
TM_ROUTE = 512
MOE_BLOCK = 1024
HGRN_BATCH = 2
HGRN_UNROLL = 2
MOE_WINDOW = 16
STAGE_PARTS = 3
MAX_WINDOWS = 2 * TM_ROUTE // MOE_WINDOW + N_EXPERTS
VMEM_LIMIT = 56 * 1024 * 1024


def _staging_rows(tm):
    return 2 * tm + N_EXPERTS * MOE_WINDOW


def _row_parts(rows, n=2):
    step = rows // n
    return [slice(r * step, (r + 1) * step) for r in range(n)]


def _pack_rows(x):
    half = x.shape[1] // 2
    lo = lax.bitcast_convert_type(x[:, :half], jnp.int32)
    hi = lax.bitcast_convert_type(x[:, half:], jnp.int32)
    return hi | lax.shift_right_logical(lo, 16)


def _unpack_rows(p):
    lo = lax.bitcast_convert_type(lax.shift_left(p, 16), F32)
    hi = lax.bitcast_convert_type(p & jnp.int32(-65536), F32)
    return jnp.concatenate([lo, hi], axis=1)


def _cparams(sem, vmem=VMEM_LIMIT):
    return pltpu.CompilerParams(dimension_semantics=sem, vmem_limit_bytes=vmem)


def _inproj_kernel(x_ref, g_ref, wqk_ref, wv_ref, whb_ref, wf_ref, wc_ref, gsum_ref, gqk_ref,
                   cos_ref, s1_ref, s2_ref, qk_out, v_out, hb_out, f_out, c_out):
    parts = _row_parts(x_ref.shape[0])
    xn = []
    for rows in parts:
        x = x_ref[rows, :]
        ms = jnp.mean(x * x, axis=-1, keepdims=True)
        xn.append((x * lax.rsqrt(ms + EPS) * g_ref[...]).astype(BF16))

    gsum = gsum_ref[...]
    for part in range(2):
        lo = part * W_A
        t = [jnp.dot(xr, wqk_ref[:, lo:lo + W_A], preferred_element_type=F32) for xr in xn]
        msq = [jnp.dot((tr * tr).astype(BF16), gsum, preferred_element_type=F32) for tr in t]
        for r, rows in enumerate(parts):
            tn = t[r] * lax.rsqrt(msq[r] + EPS) * gqk_ref[:, lo:lo + W_A]
            cos = cos_ref[rows, :]
            s1 = s1_ref[rows, :]
            s2 = s2_ref[rows, :]
            for h in range(H_A):
                slab = tn[:, h * LANES:(h + 1) * LANES]
                up = pltpu.roll(slab, LANES - ROT_DIM // 2, axis=1)
                dn = pltpu.roll(slab, ROT_DIM // 2, axis=1)
                qk_out[rows, lo + h * LANES:lo + (h + 1) * LANES] = (slab * cos + up * s1 + dn * s2).astype(BF16)

    for r, rows in enumerate(parts):
        v_out[rows, :] = jnp.dot(xn[r], wv_ref[...], preferred_element_type=F32).astype(BF16)
    for r, rows in enumerate(parts):
        hb_out[rows, :] = jnp.dot(xn[r], whb_ref[...], preferred_element_type=F32).astype(BF16)
    for r, rows in enumerate(parts):
        f_out[rows, :] = jnp.dot(xn[r], wf_ref[...], preferred_element_type=F32)
    for r, rows in enumerate(parts):
        c_out[rows, :] = jnp.dot(xn[r], wc_ref[...], preferred_element_type=F32).astype(BF16)


def _inproj(x2, g, wqk, wv, whb, wf, wc, gsum, gqk, cos_t, s1_t, s2_t):
    t_tok = x2.shape[0]
    tm = min(TM_PROJ, t_tok)
    assert t_tok % tm == 0, "token count must be a multiple of the projection tile"
    grid = (t_tok // tm,)
    row = lambda i: (i, 0)
    fixed = lambda i: (0, 0)
    return pl.pallas_call(
        _inproj_kernel,
        grid=grid,
        in_specs=[
            pl.BlockSpec((tm, D_MODEL), row),
            pl.BlockSpec((1, D_MODEL), fixed),
            pl.BlockSpec(wqk.shape, fixed),
            pl.BlockSpec(wv.shape, fixed),
            pl.BlockSpec(whb.shape, fixed),
            pl.BlockSpec(wf.shape, fixed),
            pl.BlockSpec(wc.shape, fixed),
            pl.BlockSpec(gsum.shape, fixed),
            pl.BlockSpec(gqk.shape, fixed),
            pl.BlockSpec((tm, LANES), row),
            pl.BlockSpec((tm, LANES), row),
            pl.BlockSpec((tm, LANES), row),
        ],
        out_specs=[
            pl.BlockSpec((tm, 2 * W_A), row),
            pl.BlockSpec((tm, W_A), row),
            pl.BlockSpec((tm, 3 * W_B), row),
            pl.BlockSpec((tm, W_B), row),
            pl.BlockSpec((tm, 3 * C_WIDTH), row),
        ],
        out_shape=[
            jax.ShapeDtypeStruct((t_tok, 2 * W_A), BF16),
            jax.ShapeDtypeStruct((t_tok, W_A), BF16),
            jax.ShapeDtypeStruct((t_tok, 3 * W_B), BF16),
            jax.ShapeDtypeStruct((t_tok, W_B), F32),
            jax.ShapeDtypeStruct((t_tok, 3 * C_WIDTH), BF16),
        ],
        compiler_params=_cparams(("parallel",)),
        name="inproj",
    )(x2, g, wqk, wv, whb, wf, wc, gsum, gqk, cos_t, s1_t, s2_t)


def _attn_kernel(lam_ref, q_ref, k_ref, v_ref, g_ref, o_ref, qc_ref, sa_ref, sb_ref, acc_ref, m_ref, l_ref, *,
                 lam_init):
    qi = pl.program_id(1)
    q = q_ref[...]
    lane = lax.broadcasted_iota(jnp.int32, (TQ, DV_A), 1)
    zero = jnp.zeros((TQ, DV_A), BF16)
    for h in range(H_A):
        qh = q[:, h * DV_A:(h + 1) * DV_A]
        qc_ref[h, 0] = jnp.where(lane < DK_A, qh, zero)
        qc_ref[h, 1] = jnp.where(lane >= DK_A, qh, zero)

    acc_ref[...] = jnp.zeros_like(acc_ref)
    m_ref[...] = jnp.full_like(m_ref, -jnp.inf)
    l_ref[...] = jnp.zeros_like(l_ref)

    kchunk = lax.broadcasted_iota(jnp.int32, (TK, TQ), 0) // CHUNK
    qchunk = lax.broadcasted_iota(jnp.int32, (TK, TQ), 1) // CHUNK

    def scores(j, s_ref, q0=0):
        start = pl.multiple_of(j * TK, TK)
        kb = k_ref[pl.ds(start, TK), :]
        for h in range(H_A):
            kh = kb[:, h * DV_A:(h + 1) * DV_A]
            for c in range(2):
                s_ref[h, c, :, q0:] = lax.dot_general(kh, qc_ref[h, c, q0:, :], (((1,), (1,)), ((), ())),
                                                      preferred_element_type=F32)

    def consume(j, s_ref, mask, q0=0):
        start = pl.multiple_of(j * TK, TK)
        vb = v_ref[pl.ds(start, TK), :]
        for h in range(H_A):
            vh = vb[:, h * DV_A:(h + 1) * DV_A]
            for c in range(2):
                s = s_ref[h, c, :, q0:]
                if mask is not None:
                    s = jnp.where(mask[:, q0:], s, MASK_VALUE)
                m_old = m_ref[h, c, :, q0:]
                m_new = jnp.maximum(m_old, jnp.max(s, axis=0, keepdims=True))
                alpha = jnp.exp2(m_old - m_new)
                p = jnp.exp2(s - m_new)
                l_ref[h, c, :, q0:] = alpha * l_ref[h, c, :, q0:] + jnp.sum(p, axis=0, keepdims=True)
                pv = lax.dot_general(vh, p.astype(BF16), (((0,), (0,)), ((), ())),
                                     preferred_element_type=F32)
                acc_ref[h, c, :, q0:] = alpha * acc_ref[h, c, :, q0:] + pv
                m_ref[h, c, :, q0:] = m_new

    first_diag = KV_PER_Q * qi
    scores(0, sa_ref)

    def body(jj, carry):
        j = 2 * jj
        scores(j + 1, sb_ref)
        consume(j, sa_ref, None)
        scores(j + 2, sa_ref)
        consume(j + 1, sb_ref, None)
        return carry

    lax.fori_loop(0, first_diag // 2, body, 0)
    for d in range(KV_PER_Q):
        cur, nxt = (sa_ref, sb_ref) if d % 2 == 0 else (sb_ref, sa_ref)
        if d + 1 < KV_PER_Q:
            scores(first_diag + d + 1, nxt, (d + 1) * TK)
        consume(first_diag + d, cur, (kchunk + d * (TK // CHUNK)) <= qchunk, d * TK)

    lam = lam_ref[0]
    for h in range(H_A):
        o = acc_ref[h, 0] / l_ref[h, 0] - lam * (acc_ref[h, 1] / l_ref[h, 1])
        ms = jnp.mean(o * o, axis=0, keepdims=True)
        y = o * lax.rsqrt(ms + EPS) * g_ref[...] * (1.0 - lam_init)
        o_ref[:, h * DV_A:(h + 1) * DV_A] = y.T.astype(BF16)


def _attention(qk, v, lam, subln_col, batch, seq, lam_init):
    assert TQ == KV_PER_Q * TK and KV_PER_Q % 2 == 0 and TK % CHUNK == 0 and seq % TQ == 0
    nq = seq // TQ
    t_tok = batch * seq
    kernel = functools.partial(_attn_kernel, lam_init=lam_init)
    return pl.pallas_call(
        kernel,
        grid=(batch, nq),
        in_specs=[
            pl.BlockSpec(memory_space=pltpu.SMEM),
            pl.BlockSpec((TQ, W_A), lambda b, i: (b * nq + i, 0)),
            pl.BlockSpec((seq, W_A), lambda b, i: (b, 1)),
            pl.BlockSpec((seq, W_A), lambda b, i: (b, 0)),
            pl.BlockSpec((DV_A, 1), lambda b, i: (0, 0)),
        ],
        out_specs=pl.BlockSpec((TQ, W_A), lambda b, i: (b * nq + i, 0)),
        out_shape=jax.ShapeDtypeStruct((t_tok, W_A), BF16),
        scratch_shapes=[
            pltpu.VMEM((H_A, 2, TQ, DV_A), BF16),
            pltpu.VMEM((H_A, 2, TK, TQ), F32),
            pltpu.VMEM((H_A, 2, TK, TQ), F32),
            pltpu.VMEM((H_A, 2, DV_A, TQ), F32),
            pltpu.VMEM((H_A, 2, 1, TQ), F32),
            pltpu.VMEM((H_A, 2, 1, TQ), F32),
        ],
        compiler_params=_cparams(("parallel", "arbitrary")),
        name="diff_attn",
    )(lam, qk, qk, v, subln_col)


_LEVELS = (32, 16, 8, 4, 2, 1)


def _ref_rows(g, h):
    n = g.shape[1]
    if 2 * h >= SUBLANES:
        pieces = []
        for m in range(CHUNK // (2 * h)):
            r = 2 * h * m + h - 1
            pieces.append(jnp.broadcast_to(g[r:r + 1, :], (2 * h, n)))
        return pieces[0] if len(pieces) == 1 else jnp.concatenate(pieces, axis=0)
    g3 = g.reshape(CHUNK // SUBLANES, SUBLANES, n)
    sub = lax.broadcasted_iota(jnp.int32, g3.shape, 1)
    out = None
    for m in reversed(range(SUBLANES // (2 * h))):
        r = 2 * h * m + h - 1
        piece = jnp.broadcast_to(g3[:, r:r + 1, :], g3.shape)
        out = piece if out is None else jnp.where(sub < 2 * h * (m + 1), piece, out)
    return out.reshape(CHUNK, n)


def _stack_heads(a, lane_head):
    zero = jnp.zeros_like(a)
    return jnp.concatenate([jnp.where(lane_head == hh, a, zero) for hh in range(H_B)], axis=0)


def _hgrn_kernel(hb_ref, f_ref, c_ref, lb_ref, ng_ref, cw_ref, gavg_ref, o_ref, state_ref, hbuf_ref, *, seq):
    n_chunks = seq // CHUNK
    nb = hb_ref.shape[0] // seq
    lb = lb_ref[...]
    ng = ng_ref[...]
    gavg = gavg_ref[...]

    row = lax.broadcasted_iota(jnp.int32, (CHUNK, W_B), 0)
    lane = lax.broadcasted_iota(jnp.int32, (CHUNK, W_B), 1)
    lane_head = lane // DK_B
    lane_s = lane % CHUNK
    tri = (lax.broadcasted_iota(jnp.int32, (CHUNK, CHUNK), 1)
           <= lax.broadcasted_iota(jnp.int32, (CHUNK, CHUNK), 0)).astype(BF16)
    lvl_mask = [((row // (2 * h)) == (lane_s // (2 * h))) & ((row % (2 * h)) >= h) & ((lane_s % (2 * h)) < h)
                for h in _LEVELS]
    eye_mask = row == lane_s
    r4 = lax.broadcasted_iota(jnp.int32, (H_B * DK_B, W_B), 0) // DK_B
    c4 = lax.broadcasted_iota(jnp.int32, (H_B * DK_B, W_B), 1) // DK_B
    bd_mask = r4 == c4

    state_ref[...] = jnp.zeros_like(state_ref)
    nt = (((1,), (1,)), ((), ()))
    tn = (((0,), (0,)), ((), ()))
    unroll = HGRN_UNROLL if n_chunks % HGRN_UNROLL == 0 else 1
    each = range(nb * unroll)

    def chunk(i, carry):
        rows = [pl.ds(pl.multiple_of((e % nb) * seq + (i * unroll + e // nb) * CHUNK, CHUNK), CHUNK) for e in each]
        hb = [hb_ref[rows[j], :].astype(F32) for j in each]
        zq = [t[:, 0:W_B] for t in hb]
        v16 = [t[:, W_B:2 * W_B].astype(BF16) for t in hb]
        zg = [t[:, 2 * W_B:3 * W_B] for t in hb]
        zf = [f_ref[rows[j], :] for j in each]

        logf = [jnp.log(jnp.maximum(lb + (1.0 - lb) * jax.nn.sigmoid(z), TINY)) for z in zf]
        key = [(1.0 - lb) * jax.nn.sigmoid(-z) for z in zf]
        q = [z * jax.nn.sigmoid(z) for z in zq]

        g = []
        for x in logf:
            hi = x.astype(BF16)
            r1 = x - hi.astype(F32)
            mid = r1.astype(BF16)
            lo = (r1 - mid.astype(F32)).astype(BF16)
            g.append((jnp.dot(tri, hi, preferred_element_type=F32) + jnp.dot(tri, mid, preferred_element_type=F32)
                      + jnp.dot(tri, lo, preferred_element_type=F32)) * LOG2_E)

        q16 = [t.astype(BF16) for t in q]
        k16 = [t.astype(BF16) for t in key]
        a_all = [jnp.where(eye_mask, lax.dot_general(q16[j], _stack_heads(k16[j], lane_head),
                                                     nt, preferred_element_type=F32), 0.0) for j in each]
        for h, mask in zip(_LEVELS, lvl_mask):
            w = [jnp.exp2(-jnp.abs(g[j] - _ref_rows(g[j], h))).astype(BF16) for j in each]
            ah = [lax.dot_general(q16[j] * w[j], _stack_heads(k16[j] * w[j], lane_head),
                                  nt, preferred_element_type=F32) for j in each]
            a_all = [a_all[j] + jnp.where(mask, ah[j], 0.0) for j in each]
        o = [jnp.dot(a_all[j].astype(BF16), _stack_heads(v16[j], lane_head), preferred_element_type=F32) for j in each]

        qg = [(q[j] * jnp.exp2(g[j])).astype(BF16) for j in each]
        g_last = [t[CHUNK - 1:CHUNK, :] for t in g]
        upd = [lax.dot_general(v16[j], (key[j] * jnp.exp2(g_last[j] - g[j])).astype(BF16), tn,
                               preferred_element_type=F32) for j in each]
        for j in each:
            st = state_ref[j % nb]
            o[j] = o[j] + lax.dot_general(qg[j], st.astype(BF16), nt, preferred_element_type=F32)
            state_ref[j % nb] = jnp.exp2(g_last[j]) * st + jnp.where(bd_mask, upd[j], 0.0)

        ms = [jnp.dot((t * t).astype(BF16), gavg, preferred_element_type=F32) for t in o]
        for j in each:
            y = o[j] * lax.rsqrt(ms[j] + EPS) * ng * (zg[j] * jax.nn.sigmoid(zg[j]))
            o_ref[rows[j], 0:W_B] = y.astype(BF16)
        return carry

    lax.fori_loop(0, n_chunks // unroll, chunk, 0)

    piece = min(512, seq)
    hbuf_ref[0:SUBLANES, :] = jnp.zeros((SUBLANES, C_WIDTH), F32)
    for j in range(nb):
        for p in range(seq // piece):
            cblk = c_ref[j * seq + p * piece:j * seq + (p + 1) * piece, :].astype(F32)
            hbuf_ref[SUBLANES + p * piece:SUBLANES + (p + 1) * piece, :] = (cblk[:, C_WIDTH:2 * C_WIDTH]
                                                                             * cblk[:, 2 * C_WIDTH:])
        for p in range(seq // piece):
            base = SUBLANES + p * piece
            y = (cw_ref[0:1, :] * hbuf_ref[base - 2:base - 2 + piece, :]
                 + cw_ref[1:2, :] * hbuf_ref[base - 1:base - 1 + piece, :]
                 + cw_ref[2:3, :] * hbuf_ref[base:base + piece, :])
            lo_r = j * seq + p * piece
            bgate = c_ref[lo_r:lo_r + piece, 0:C_WIDTH].astype(F32)
            o_ref[lo_r:lo_r + piece, W_B:W_B + C_WIDTH] = (bgate * y).astype(BF16)


def _hgrn_conv(hb, f, cv, lb, ng, cw, gavg, batch, seq):
    t_tok = batch * seq
    nb = HGRN_BATCH if batch % HGRN_BATCH == 0 else 1
    fixed = lambda b: (0, 0)
    return pl.pallas_call(
        functools.partial(_hgrn_kernel, seq=seq),
        grid=(batch // nb,),
        in_specs=[
            pl.BlockSpec((nb * seq, 3 * W_B), lambda b: (b, 0)),
            pl.BlockSpec((nb * seq, W_B), lambda b: (b, 0)),
            pl.BlockSpec((nb * seq, 3 * C_WIDTH), lambda b: (b, 0)),
            pl.BlockSpec((1, W_B), fixed),
            pl.BlockSpec((1, W_B), fixed),
            pl.BlockSpec((CONV_WIDTH, C_WIDTH), fixed),
            pl.BlockSpec((W_B, W_B), fixed),
        ],
        out_specs=pl.BlockSpec((nb * seq, W_B + C_WIDTH), lambda b: (b, 0)),
        out_shape=jax.ShapeDtypeStruct((t_tok, W_B + C_WIDTH), BF16),
        scratch_shapes=[
            pltpu.VMEM((nb, H_B * DV_B, W_B), F32),
            pltpu.VMEM((seq + SUBLANES, C_WIDTH), F32),
        ],
        compiler_params=_cparams(("parallel",)),
        name="hgrn_conv",
    )(hb, f, cv, lb, ng, cw, gavg)


def _outproj_router_kernel(oa_ref, obc_ref, x_ref, wo_ref, g_ref, wr_ref, br_ref,
                           xo_ref, route_ref, route_t_ref, tab_ref, cnt_ref, carry_ref):
    i = pl.program_id(0)

    @pl.when(i == 0)
    def _():
        carry_ref[...] = jnp.zeros_like(carry_ref)

    mixed = (jnp.dot(oa_ref[...], wo_ref[0:W_A, :], preferred_element_type=F32)
             + jnp.dot(obc_ref[...], wo_ref[W_A:, :], preferred_element_type=F32))
    xn = x_ref[...] + mixed
    xo_ref[...] = xn

    ms = jnp.mean(xn * xn, axis=-1, keepdims=True)
    hn = (xn * lax.rsqrt(ms + EPS) * g_ref[...]).astype(BF16)
    logits = jnp.dot(hn, wr_ref[...], preferred_element_type=F32) + br_ref[...]

    tm = logits.shape[0]
    lt = logits.T[0:ROUTE_ROWS, :]
    r = lax.broadcasted_iota(jnp.int32, (ROUTE_ROWS, tm), 0)
    neg = -jnp.inf
    big = ROUTE_ROWS

    def top(cand):
        val = jnp.max(cand, axis=0, keepdims=True)
        idx = jnp.min(jnp.where(cand == val, r, big), axis=0, keepdims=True)
        return val, idx

    glog = jnp.where(r < N_GROUPS, lt, neg)
    gmax, gidx = top(glog)
    p_g = 1.0 / jnp.sum(jnp.exp(glog - gmax), axis=0, keepdims=True)
    off = r - (N_GROUPS + EXPERTS_PER_GROUP * gidx)
    elog = jnp.where(jnp.abs(2 * off - (EXPERTS_PER_GROUP - 1)) < EXPERTS_PER_GROUP, lt, neg)
    v1, i1 = top(elog)
    v2, i2 = top(jnp.where(r == i1, neg, elog))
    t2 = jnp.exp(v2 - v1)
    w1 = p_g / (1.0 + t2)
    w2 = p_g * t2 / (1.0 + t2)

    sel1 = r == i1
    sel2 = r == i2
    onehot = jnp.where(sel1, 1.0, jnp.where(sel2, 1.0, 0.0))
    earlier = (lax.broadcasted_iota(jnp.int32, (tm, tm), 0)
               < lax.broadcasted_iota(jnp.int32, (tm, tm), 1)).astype(BF16)
    local = jnp.dot(onehot.astype(BF16), earlier, preferred_element_type=F32)
    cnt = jnp.sum(onehot, axis=1, keepdims=True)
    wrows = jnp.floor((cnt + (MOE_WINDOW - 1)) * (1.0 / MOE_WINDOW)) * MOE_WINDOW
    below = (lax.broadcasted_iota(jnp.int32, (ROUTE_ROWS, ROUTE_LANES), 1)
             < lax.broadcasted_iota(jnp.int32, (ROUTE_ROWS, ROUTE_LANES), 0)).astype(BF16)
    wrows_rows = jnp.concatenate([jnp.broadcast_to(wrows, (ROUTE_ROWS, ROUTE_LANES)),
                                  jnp.zeros((ROUTE_LANES - ROUTE_ROWS, ROUTE_LANES), F32)], axis=0).astype(BF16)
    soff = jnp.dot(below, wrows_rows, preferred_element_type=F32)[:, 0:1]
    slot = local + soff
    row1 = jnp.sum(jnp.where(sel1, slot, 0.0), axis=0, keepdims=True)
    row2 = jnp.sum(jnp.where(sel2, slot, 0.0), axis=0, keepdims=True)

    k = lax.broadcasted_iota(jnp.int32, (SUBLANES, tm), 0)
    rec = jnp.where(k == 0, (i1 - N_GROUPS).astype(F32),
          jnp.where(k == 1, (i2 - N_GROUPS).astype(F32),
          jnp.where(k == 2, w1,
          jnp.where(k == 3, w2,
          jnp.where(k == 4, row1,
          jnp.where(k == 5, row2, 0.0))))))
    route_t_ref[...] = rec
    route_ref[...] = jnp.concatenate([rec, jnp.zeros((ROUTE_LANES - SUBLANES, tm), F32)], axis=0).T

    lane = lax.broadcasted_iota(jnp.int32, (ROUTE_ROWS, ROUTE_LANES), 1)
    tab_ref[0] = jnp.where(lane == 0, cnt, jnp.where(lane == 1, carry_ref[...], 0.0))
    carry_ref[...] = carry_ref[...] + jnp.floor((cnt + (SUBLANES - 1)) * (1.0 / SUBLANES)) * SUBLANES
    cnt_ref[...] = carry_ref[...]


def _outproj_router(o_a, o_bc, x2, wo, g, wr, br):
    t_tok = x2.shape[0]
    tm = min(TM_ROUTE, t_tok)
    assert t_tok % tm == 0, "token count must be a multiple of the routing tile"
    row = lambda i: (i, 0)
    fixed = lambda i: (0, 0)
    return pl.pallas_call(
        _outproj_router_kernel,
        grid=(t_tok // tm,),
        in_specs=[
            pl.BlockSpec((tm, W_A), row),
            pl.BlockSpec((tm, W_B + C_WIDTH), row),
            pl.BlockSpec((tm, D_MODEL), row),
            pl.BlockSpec(wo.shape, fixed),
            pl.BlockSpec((1, D_MODEL), fixed),
            pl.BlockSpec(wr.shape, fixed),
            pl.BlockSpec((1, ROUTE_LANES), fixed),
        ],
        out_specs=[
            pl.BlockSpec((tm, D_MODEL), row),
            pl.BlockSpec((tm, ROUTE_LANES), row),
            pl.BlockSpec((SUBLANES, tm), row),
            pl.BlockSpec((1, ROUTE_ROWS, ROUTE_LANES), lambda i: (i, 0, 0)),
            pl.BlockSpec((ROUTE_ROWS, ROUTE_LANES), fixed),
        ],
        out_shape=[
            jax.ShapeDtypeStruct((t_tok, D_MODEL), F32),
            jax.ShapeDtypeStruct((t_tok, ROUTE_LANES), F32),
            jax.ShapeDtypeStruct((t_tok // tm * SUBLANES, tm), F32),
            jax.ShapeDtypeStruct((t_tok // tm, ROUTE_ROWS, ROUTE_LANES), F32),
            jax.ShapeDtypeStruct((ROUTE_ROWS, ROUTE_LANES), F32),
        ],
        scratch_shapes=[pltpu.VMEM((ROUTE_ROWS, ROUTE_LANES), F32)],
        compiler_params=_cparams(("arbitrary",)),
        name="outproj_router",
    )(o_a, o_bc, x2, wo, g, wr, br)


def _start_copies(n, make_copy):
    def start(k, carry):
        make_copy(k).start()
        return carry

    lax.fori_loop(0, n, start, 0)


def _wait_copies(n, make_copy):
    def wait(k, carry):
        make_copy(k).wait()
        return carry

    lax.fori_loop(0, n, wait, 0)


def _window_copy_loops(n, make_copy):
    _start_copies(n, make_copy)
    _wait_copies(n, make_copy)


def _dispatch_kernel(pend_ref, tab_ref, tabp_ref, route_ref, x_ref, buf_ref, xp_ref, xq_ref, sem, semq):
    tm = x_ref.shape[0]
    r_st = xp_ref.shape[0]
    step = pl.program_id(0)

    @pl.when(pl.program_id(0) == 0)
    def _():
        xp_ref[...] = jnp.zeros_like(xp_ref)
        copies = []
        for e in range(N_EXPERTS):
            end = pend_ref[e]
            copies.append((None, pltpu.make_async_copy(
                xp_ref.at[0:MOE_BLOCK, :],
                buf_ref.at[pl.ds(pl.multiple_of(end - MOE_BLOCK, SUBLANES), MOE_BLOCK), :], sem)))
            copies.append((end >= MOE_BLOCK + MOE_WINDOW, pltpu.make_async_copy(
                xp_ref.at[0:MOE_WINDOW, :],
                buf_ref.at[pl.ds(pl.multiple_of(jnp.maximum(end - MOE_BLOCK - MOE_WINDOW, 0), SUBLANES), MOE_WINDOW), :],
                sem)))
        for cond, cp in copies:
            if cond is None:
                cp.start()
            else:
                pl.when(cond)(cp.start)
        for cond, cp in copies:
            if cond is None:
                cp.wait()
            else:
                pl.when(cond)(cp.wait)

        def tail_copy(k):
            return pltpu.make_async_copy(
                xp_ref.at[0:MOE_BLOCK, :],
                buf_ref.at[pl.ds(pl.multiple_of(k * MOE_BLOCK, MOE_BLOCK), MOE_BLOCK), :], sem)

        first_unused = pend_ref[N_EXPERTS - 1] // MOE_BLOCK
        n_unused = buf_ref.shape[0] // MOE_BLOCK - first_unused
        _window_copy_loops(n_unused, lambda k: tail_copy(first_unused + k))

    r0 = route_ref[4:5, :].astype(jnp.int32)
    r1 = route_ref[5:6, :].astype(jnp.int32)
    part = r_st // STAGE_PARTS
    srow = lax.broadcasted_iota(jnp.int32, (part, tm), 0)

    def window(tabs, stage_ref, s):
        def make_copy(k):
            src = pl.multiple_of(k * MOE_WINDOW, MOE_WINDOW)
            return pltpu.make_async_copy(
                stage_ref.at[pl.ds(src, MOE_WINDOW), :],
                buf_ref.at[pl.ds(pl.multiple_of(tabs[0, 0, k], SUBLANES), MOE_WINDOW), :], s)
        return tabs[0, 0, MAX_WINDOWS], make_copy

    def run(cur_ref, cur_sem, prev_ref, prev_sem):
        x16 = x_ref[...].astype(BF16)
        for p in range(STAGE_PARTS):
            sel = jnp.where(srow + p * part == r0, 1.0, jnp.where(srow + p * part == r1, 1.0, 0.0)).astype(BF16)
            cur_ref[p * part:(p + 1) * part, :] = _pack_rows(jnp.dot(sel, x16, preferred_element_type=F32))

        @pl.when(step > 0)
        def _():
            _wait_copies(*window(tabp_ref, prev_ref, prev_sem))

        _start_copies(*window(tab_ref, cur_ref, cur_sem))

        @pl.when(step == pl.num_programs(0) - 1)
        def _():
            _wait_copies(*window(tab_ref, cur_ref, cur_sem))

    @pl.when(step % 2 == 0)
    def _():
        run(xp_ref, sem, xq_ref, semq)

    @pl.when(step % 2 == 1)
    def _():
        run(xq_ref, semq, xp_ref, sem)


def _dispatch(pad_end, tabs, route_t, x2, cap):
    t_tok = x2.shape[0]
    tm = min(TM_ROUTE, t_tok)
    assert t_tok % tm == 0, "token count must be a multiple of the routing tile"
    n_tiles = t_tok // tm
    return pl.pallas_call(
        _dispatch_kernel,
        grid=(n_tiles,),
        in_specs=[
            pl.BlockSpec(memory_space=pltpu.SMEM),
            pl.BlockSpec((1, 1, ROUTE_LANES), lambda i: (i, 0, 0), memory_space=pltpu.SMEM),
            pl.BlockSpec((1, 1, ROUTE_LANES), lambda i: (jnp.maximum(i - 1, 0), 0, 0), memory_space=pltpu.SMEM),
            pl.BlockSpec((SUBLANES, tm), lambda i: (i, 0)),
            pl.BlockSpec((tm, D_MODEL), lambda i: (i, 0)),
        ],
        out_specs=pl.BlockSpec(memory_space=pl.ANY),
        out_shape=jax.ShapeDtypeStruct((cap, D_PACKED), jnp.int32),
        scratch_shapes=[pltpu.VMEM((_staging_rows(tm), D_PACKED), jnp.int32),
                        pltpu.VMEM((_staging_rows(tm), D_PACKED), jnp.int32),
                        pltpu.SemaphoreType.DMA(()), pltpu.SemaphoreType.DMA(())],
        compiler_params=_cparams(("arbitrary",)),
        name="moe_dispatch",
    )(pad_end, tabs, tabs, route_t, x2)


def _experts_kernel(blk_e_ref, n_used_ref, x_ref, g_ref, wg_ref, wu_ref, wd_ref, y_ref, wg16, wu16, wd16):
    i = pl.program_id(0)
    used = i < n_used_ref[0]

    @pl.when(used & ((i == 0) | (blk_e_ref[i] != blk_e_ref[jnp.maximum(i - 1, 0)])))
    def _():
        wg16[...] = wg_ref[0, 0].astype(BF16)
        wu16[...] = wu_ref[0, 0].astype(BF16)
        wd16[...] = wd_ref[0, 0].astype(BF16)

    @pl.when(used)
    def _():
        wg = wg16[...]
        wu = wu16[...]
        wd = wd16[...]
        parts = _row_parts(x_ref.shape[0])
        hn = []
        for rows in parts:
            x = _unpack_rows(x_ref[rows, :])
            ms = jnp.mean(x * x, axis=-1, keepdims=True)
            hn.append((x * lax.rsqrt(ms + EPS) * g_ref[...]).astype(BF16))
        a = [jnp.dot(t, wg, preferred_element_type=F32) for t in hn]
        u = [jnp.dot(t, wu, preferred_element_type=F32) for t in hn]
        act = [(a[r] * jax.nn.sigmoid(a[r]) * u[r]).astype(BF16) for r in range(len(parts))]
        for r, rows in enumerate(parts):
            y = jnp.dot(act[r], wd, preferred_element_type=F32)
            y_ref[rows, :] = _pack_rows(y.astype(BF16).astype(F32))


def _experts(blk_e, n_used, buf, g, wg, wu, wd, layer):
    n_blocks = buf.shape[0] // MOE_BLOCK
    grid_spec = pltpu.PrefetchScalarGridSpec(
        num_scalar_prefetch=2,
        grid=(n_blocks,),
        in_specs=[
            pl.BlockSpec((MOE_BLOCK, D_PACKED), lambda i, be, nu: (jnp.minimum(i, nu[0] - 1), 0)),
            pl.BlockSpec((1, D_MODEL), lambda i, be, nu: (0, 0)),
            pl.BlockSpec((1, 1, D_MODEL, D_EXPERT), lambda i, be, nu: (layer, be[i], 0, 0)),
            pl.BlockSpec((1, 1, D_MODEL, D_EXPERT), lambda i, be, nu: (layer, be[i], 0, 0)),
            pl.BlockSpec((1, 1, D_EXPERT, D_MODEL), lambda i, be, nu: (layer, be[i], 0, 0)),
        ],
        out_specs=pl.BlockSpec((MOE_BLOCK, D_PACKED), lambda i, be, nu: (jnp.minimum(i, nu[0] - 1), 0)),
        scratch_shapes=[pltpu.VMEM((D_MODEL, D_EXPERT), BF16), pltpu.VMEM((D_MODEL, D_EXPERT), BF16),
                        pltpu.VMEM((D_EXPERT, D_MODEL), BF16)],
    )
    return pl.pallas_call(
        _experts_kernel,
        grid_spec=grid_spec,
        out_shape=jax.ShapeDtypeStruct(buf.shape, jnp.int32),
        input_output_aliases={2: 0},
        compiler_params=_cparams(("arbitrary",)),
        name="moe_experts",
    )(blk_e, n_used, buf, g, wg, wu, wd)


def _combine_kernel(tab_ref, tabn_ref, route_ref, x_ref, y_ref, o_ref, ya_ref, yb_ref, sema, semb):
    tm = x_ref.shape[0]
    r_st = ya_ref.shape[0]
    step = pl.program_id(0)

    def window(tabs, stage_ref, s):
        def make_copy(k):
            dst = pl.multiple_of(k * MOE_WINDOW, MOE_WINDOW)
            return pltpu.make_async_copy(y_ref.at[pl.ds(pl.multiple_of(tabs[0, 0, k], SUBLANES), MOE_WINDOW), :],
                                         stage_ref.at[pl.ds(dst, MOE_WINDOW), :], s)
        return tabs[0, 0, MAX_WINDOWS], make_copy

    @pl.when(step == 0)
    def _():
        ya_ref[...] = jnp.zeros_like(ya_ref)
        yb_ref[...] = jnp.zeros_like(yb_ref)
        _start_copies(*window(tab_ref, ya_ref, sema))

    route = route_ref[...]
    r0 = route[:, 4:5].astype(jnp.int32)
    r1 = route[:, 5:6].astype(jnp.int32)
    part = r_st // STAGE_PARTS
    col = lax.broadcasted_iota(jnp.int32, (tm, part), 1)

    def run(cur_ref, cur_sem, nxt_ref, nxt_sem):
        @pl.when(step + 1 < pl.num_programs(0))
        def _():
            _start_copies(*window(tabn_ref, nxt_ref, nxt_sem))

        _wait_copies(*window(tab_ref, cur_ref, cur_sem))
        acc = x_ref[...]
        for p in range(STAGE_PARTS):
            gate = jnp.where(col + p * part == r0, route[:, 2:3],
                             jnp.where(col + p * part == r1, route[:, 3:4], 0.0)).astype(BF16)
            rows = _unpack_rows(cur_ref[p * part:(p + 1) * part, :]).astype(BF16)
            acc = acc + jnp.dot(gate, rows, preferred_element_type=F32)
        o_ref[...] = acc

    @pl.when(step % 2 == 0)
    def _():
        run(ya_ref, sema, yb_ref, semb)

    @pl.when(step % 2 == 1)
    def _():
        run(yb_ref, semb, ya_ref, sema)


def _combine(tabs, route, x2, y_buf):
    t_tok = x2.shape[0]
    tm = min(TM_ROUTE, t_tok)
    assert t_tok % tm == 0, "token count must be a multiple of the routing tile"
    n_tiles = t_tok // tm
    return pl.pallas_call(
        _combine_kernel,
        grid=(n_tiles,),
        in_specs=[
            pl.BlockSpec((1, 1, ROUTE_LANES), lambda i: (i, 0, 0), memory_space=pltpu.SMEM),
            pl.BlockSpec((1, 1, ROUTE_LANES), lambda i: (jnp.minimum(i + 1, n_tiles - 1), 0, 0),
                         memory_space=pltpu.SMEM),
            pl.BlockSpec((tm, ROUTE_LANES), lambda i: (i, 0)),
            pl.BlockSpec((tm, D_MODEL), lambda i: (i, 0)),
            pl.BlockSpec(memory_space=pl.ANY),
        ],
        out_specs=pl.BlockSpec((tm, D_MODEL), lambda i: (i, 0)),
        out_shape=jax.ShapeDtypeStruct((t_tok, D_MODEL), F32),
        scratch_shapes=[pltpu.VMEM((_staging_rows(tm), D_PACKED), jnp.int32),
                        pltpu.VMEM((_staging_rows(tm), D_PACKED), jnp.int32),
                        pltpu.SemaphoreType.DMA(()), pltpu.SemaphoreType.DMA(())],
        compiler_params=_cparams(("arbitrary",)),
        name="moe_combine",
    )(tabs, tabs, route, x2, y_buf)


def _rope_tables(positions):
    half = ROT_DIM // 2
    inv_freq = ROPE_THETA ** (-jnp.arange(0, ROT_DIM, 2, dtype=F32) / ROT_DIM)
    d = jnp.arange(LANES) % DK_A
    freq_lane = jnp.where(d < ROT_DIM, inv_freq[d % half], 0.0)
    ang = positions.reshape(-1, 1).astype(F32) * freq_lane[None, :]
    cos, sin = jnp.cos(ang), jnp.sin(ang)
    s1 = jnp.where(d < half, -sin, 0.0)
    s2 = jnp.where((d >= half) & (d < ROT_DIM), sin, 0.0)
    return cos, s1, s2


def _block_avg(width, group):
    idx = jnp.arange(width) // group
    return jnp.where(idx[:, None] == idx[None, :], 1.0 / group, 0.0).astype(BF16)


def kernel(x, positions, norm_mix_g, w_in, attn_q_norm_g, attn_k_norm_g, lambda_q1, lambda_k1, lambda_q2,
           lambda_k2, attn_subln_g, hgrn_lower_bounds, hgrn_norm_g, conv_w, w_out, norm_ffn_g, router_group_w,
           router_group_b, router_expert_w, router_expert_b, expert_w_gate, expert_w_up, expert_w_down):
    batch, seq, d = x.shape
    assert d == D_MODEL
    depth = w_in.shape[0]
    t_tok = batch * seq
    n_assign = 2 * t_tok
    n_route_tiles = t_tok // min(TM_ROUTE, t_tok)
    n_blocks = (n_assign + (SUBLANES - 1) * n_route_tiles * N_EXPERTS) // MOE_BLOCK + 2 * N_EXPERTS + 1
    cap = n_blocks * MOE_BLOCK

    lb_probs = jax.nn.softmax(hgrn_lower_bounds.astype(F32), axis=0)
    lower_bounds = jnp.cumsum(lb_probs, axis=0) - lb_probs[0:1]
    cos_t, s1_t, s2_t = _rope_tables(positions)
    gsum_a = _block_avg(W_A, DK_A)
    gavg_b = _block_avg(W_B, DV_B)

    o_q, o_k, o_v = 0, W_A, 2 * W_A
    o_bq = 3 * W_A
    o_bf, o_bi, o_bg = o_bq + W_B, o_bq + 2 * W_B, o_bq + 3 * W_B
    o_c = o_bq + 4 * W_B

    x2 = x.reshape(t_tok, d)
    for l in range(depth):
        w = w_in[l]
        wqk = w[:, o_q:o_v].astype(BF16)
        wv = w[:, o_v:o_bq].astype(BF16)
        whb = jnp.concatenate([w[:, o_bq:o_bf], w[:, o_bi:o_c]], axis=1).astype(BF16)
        wf = w[:, o_bf:o_bi].astype(BF16)
        wc = w[:, o_c:].astype(BF16)
        gqk = jnp.concatenate([jnp.tile(attn_q_norm_g[l], 2 * H_A) * (DK_A ** -0.5 * LOG2_E),
                               jnp.tile(attn_k_norm_g[l], 2 * H_A)]).reshape(1, 2 * W_A).astype(F32)
        qk, v, hb, f, cv = _inproj(x2, norm_mix_g[l].reshape(1, d), wqk, wv, whb, wf, wc, gsum_a, gqk,
                                   cos_t, s1_t, s2_t)

        lam_init = 0.8 - 0.6 * math.exp(-0.3 * l)
        lam = (jnp.exp(jnp.sum(lambda_q1[l].astype(F32) * lambda_k1[l].astype(F32)))
               - jnp.exp(jnp.sum(lambda_q2[l].astype(F32) * lambda_k2[l].astype(F32))) + lam_init).reshape(1)
        o_a = _attention(qk, v, lam, attn_subln_g[l].reshape(DV_A, 1).astype(F32), batch, seq, lam_init)

        o_bc = _hgrn_conv(hb, f, cv, lower_bounds[l].reshape(1, W_B), jnp.tile(hgrn_norm_g[l], H_B).reshape(1, W_B),
                          conv_w[l], gavg_b, batch, seq)

        wr = jnp.concatenate([router_group_w[l], router_expert_w[l],
                              jnp.zeros((d, ROUTE_LANES - N_LOGITS), F32)], axis=1).astype(BF16)
        br = jnp.concatenate([router_group_b[l], router_expert_b[l],
                              jnp.zeros((ROUTE_LANES - N_LOGITS,), F32)]).reshape(1, ROUTE_LANES)
        x2, route, route_t, tab, cnt = _outproj_router(o_a, o_bc, x2, w_out[l].astype(BF16), norm_ffn_g[l].reshape(1, d), wr, br)

        counts = cnt[N_GROUPS:N_GROUPS + N_EXPERTS, 0].astype(jnp.int32)
        padded = (counts + MOE_WINDOW + MOE_BLOCK - 1) // MOE_BLOCK * MOE_BLOCK
        pad_end = jnp.cumsum(padded)
        pad_start = pad_end - padded
        blk_pos = jnp.arange(n_blocks, dtype=jnp.int32) * MOE_BLOCK
        blk_e = jnp.minimum(jnp.sum((pad_end[None, :] <= blk_pos[:, None]).astype(jnp.int32), axis=1),
                            N_EXPERTS - 1)
        n_used = (pad_end[-1] // MOE_BLOCK).astype(jnp.int32).reshape(1)
        tile_cnt = tab[:, N_GROUPS:N_GROUPS + N_EXPERTS, 0].astype(jnp.int32)
        tile_base = tab[:, N_GROUPS:N_GROUPS + N_EXPERTS, 1].astype(jnp.int32) + pad_start[None, :]
        n_win = (tile_cnt + MOE_WINDOW - 1) // MOE_WINDOW
        w_end = jnp.cumsum(n_win, axis=1)
        w_start = w_end - n_win
        k_idx = jnp.arange(MAX_WINDOWS, dtype=jnp.int32)[None, :, None]
        owner = (w_start[:, None, :] <= k_idx) & (k_idx < w_end[:, None, :])
        win_dst = jnp.sum(jnp.where(owner, tile_base[:, None, :] + (k_idx - w_start[:, None, :]) * MOE_WINDOW, 0),
                          axis=-1)
        tabs = jnp.concatenate([win_dst, w_end[:, -1:],
                                jnp.zeros((win_dst.shape[0], ROUTE_LANES - MAX_WINDOWS - 1), jnp.int32)],
                               axis=1).reshape(-1, 1, ROUTE_LANES)

        buf = _dispatch(pad_end, tabs, route_t, x2, cap)
        y_buf = _experts(blk_e, n_used, buf, norm_ffn_g[l].reshape(1, d), expert_w_gate, expert_w_up,
                         expert_w_down, l)
        x2 = _combine(tabs, route, x2, y_buf)
    return x2.reshape(batch, seq, d)
```

```python
import functools
import math

import jax
import jax.numpy as jnp
from jax import lax
from jax.experimental import pallas as pl
from jax.experimental.pallas import tpu as pltpu

F32 = jnp.float32
BF16 = jnp.bfloat16

LANES = 128
SUBLANES = 8

D_MODEL = 1024
CHUNK = 64
EPS = 1e-6
MASK_VALUE = -1e30
TINY = 1e-30
LOG2_E = math.log2(math.e)
H_A = 4
DK_A = D_MODEL // 16
DV_A = 2 * DK_A
ROT_DIM = DK_A // 4
ROPE_THETA = 500000.0
H_B = 4
DK_B = D_MODEL // 16
DV_B = D_MODEL // 16
C_WIDTH = D_MODEL // 4
CONV_WIDTH = 3
N_GROUPS = 4
EXPERTS_PER_GROUP = 8
N_EXPERTS = N_GROUPS * EXPERTS_PER_GROUP
D_EXPERT = D_MODEL // 2
D_PACKED = D_MODEL // 2

W_A = H_A * DV_A
W_B = H_B * DK_B
ROUTE_LANES = LANES
N_LOGITS = N_GROUPS + N_EXPERTS
ROUTE_ROWS = -(-N_LOGITS // SUBLANES) * SUBLANES

TM_PROJ = 1024
TK = 256
KV_PER_Q = 2
TQ = KV_PER_Q * TK
TM_ROUTE = 512
MOE_BLOCK = 1024
HGRN_BATCH = 2
HGRN_UNROLL = 2
MOE_WINDOW = 16
STAGE_PARTS = 3
MAX_WINDOWS = 2 * TM_ROUTE // MOE_WINDOW + N_EXPERTS
VMEM_LIMIT = 56 * 1024 * 1024


def _staging_rows(tm):
    return 2 * tm + N_EXPERTS * MOE_WINDOW


def _row_parts(rows, n=2):
    step = rows // n
    return [slice(r * step, (r + 1) * step) for r in range(n)]


def _pack_rows(x):
    half = x.shape[1] // 2
    lo = lax.bitcast_convert_type(x[:, :half], jnp.int32)
    hi = lax.bitcast_convert_type(x[:, half:], jnp.int32)
    return hi | lax.shift_right_logical(lo, 16)


def _unpack_rows(p):
    lo = lax.bitcast_convert_type(lax.shift_left(p, 16), F32)
    hi = lax.bitcast_convert_type(p & jnp.int32(-65536), F32)
    return jnp.concatenate([lo, hi], axis=1)


def _cparams(sem, vmem=VMEM_LIMIT):
    return pltpu.CompilerParams(dimension_semantics=sem, vmem_limit_bytes=vmem)


def _inproj_kernel(x_ref, g_ref, wqk_ref, wv_ref, whb_ref, wf_ref, wc_ref, gsum_ref, gqk_ref,
                   cos_ref, s1_ref, s2_ref, qk_out, v_out, hb_out, f_out, c_out):
    parts = _row_parts(x_ref.shape[0])
    xn = []
    for rows in parts:
        x = x_ref[rows, :]
        ms = jnp.mean(x * x, axis=-1, keepdims=True)
        xn.append((x * lax.rsqrt(ms + EPS) * g_ref[...]).astype(BF16))

    gsum = gsum_ref[...]
    for part in range(2):
        lo = part * W_A
        t = [jnp.dot(xr, wqk_ref[:, lo:lo + W_A], preferred_element_type=F32) for xr in xn]
        msq = [jnp.dot((tr * tr).astype(BF16), gsum, preferred_element_type=F32) for tr in t]
        for r, rows in enumerate(parts):
            tn = t[r] * lax.rsqrt(msq[r] + EPS) * gqk_ref[:, lo:lo + W_A]
            cos = cos_ref[rows, :]
            s1 = s1_ref[rows, :]
            s2 = s2_ref[rows, :]
            for h in range(H_A):
                slab = tn[:, h * LANES:(h + 1) * LANES]
                up = pltpu.roll(slab, LANES - ROT_DIM // 2, axis=1)
                dn = pltpu.roll(slab, ROT_DIM // 2, axis=1)
                qk_out[rows, lo + h * LANES:lo + (h + 1) * LANES] = (slab * cos + up * s1 + dn * s2).astype(BF16)

    for r, rows in enumerate(parts):
        v_out[rows, :] = jnp.dot(xn[r], wv_ref[...], preferred_element_type=F32).astype(BF16)
    for r, rows in enumerate(parts):
        hb_out[rows, :] = jnp.dot(xn[r], whb_ref[...], preferred_element_type=F32).astype(BF16)
    for r, rows in enumerate(parts):
        f_out[rows, :] = jnp.dot(xn[r], wf_ref[...], preferred_element_type=F32)
    for r, rows in enumerate(parts):
        c_out[rows, :] = jnp.dot(xn[r], wc_ref[...], preferred_element_type=F32).astype(BF16)


def _inproj(x2, g, wqk, wv, whb, wf, wc, gsum, gqk, cos_t, s1_t, s2_t):
    t_tok = x2.shape[0]
    tm = min(TM_PROJ, t_tok)
    assert t_tok % tm == 0, "token count must be a multiple of the projection tile"
    grid = (t_tok // tm,)
    row = lambda i: (i, 0)
    fixed = lambda i: (0, 0)
    return pl.pallas_call(
        _inproj_kernel,
        grid=grid,
        in_specs=[
            pl.BlockSpec((tm, D_MODEL), row),
            pl.BlockSpec((1, D_MODEL), fixed),
            pl.BlockSpec(wqk.shape, fixed),
            pl.BlockSpec(wv.shape, fixed),
            pl.BlockSpec(whb.shape, fixed),
            pl.BlockSpec(wf.shape, fixed),
            pl.BlockSpec(wc.shape, fixed),
            pl.BlockSpec(gsum.shape, fixed),
            pl.BlockSpec(gqk.shape, fixed),
            pl.BlockSpec((tm, LANES), row),
            pl.BlockSpec((tm, LANES), row),
            pl.BlockSpec((tm, LANES), row),
        ],
        out_specs=[
            pl.BlockSpec((tm, 2 * W_A), row),
            pl.BlockSpec((tm, W_A), row),
            pl.BlockSpec((tm, 3 * W_B), row),
            pl.BlockSpec((tm, W_B), row),
            pl.BlockSpec((tm, 3 * C_WIDTH), row),
        ],
        out_shape=[
            jax.ShapeDtypeStruct((t_tok, 2 * W_A), BF16),
            jax.ShapeDtypeStruct((t_tok, W_A), BF16),
            jax.ShapeDtypeStruct((t_tok, 3 * W_B), BF16),
            jax.ShapeDtypeStruct((t_tok, W_B), F32),
            jax.ShapeDtypeStruct((t_tok, 3 * C_WIDTH), BF16),
        ],
        compiler_params=_cparams(("parallel",)),
        name="inproj",
    )(x2, g, wqk, wv, whb, wf, wc, gsum, gqk, cos_t, s1_t, s2_t)


def _attn_kernel(lam_ref, q_ref, k_ref, v_ref, g_ref, o_ref, qc_ref, sa_ref, sb_ref, acc_ref, m_ref, l_ref, *,
                 lam_init):
    qi = pl.program_id(1)
    q = q_ref[...]
    lane = lax.broadcasted_iota(jnp.int32, (TQ, DV_A), 1)
    zero = jnp.zeros((TQ, DV_A), BF16)
    for h in range(H_A):
        qh = q[:, h * DV_A:(h + 1) * DV_A]
        qc_ref[h, 0] = jnp.where(lane < DK_A, qh, zero)
        qc_ref[h, 1] = jnp.where(lane >= DK_A, qh, zero)

    acc_ref[...] = jnp.zeros_like(acc_ref)
    m_ref[...] = jnp.full_like(m_ref, -jnp.inf)
    l_ref[...] = jnp.zeros_like(l_ref)

    kchunk = lax.broadcasted_iota(jnp.int32, (TK, TQ), 0) // CHUNK
    qchunk = lax.broadcasted_iota(jnp.int32, (TK, TQ), 1) // CHUNK

    def scores(j, s_ref, q0=0):
        start = pl.multiple_of(j * TK, TK)
        kb = k_ref[pl.ds(start, TK), :]
        for h in range(H_A):
            kh = kb[:, h * DV_A:(h + 1) * DV_A]
            for c in range(2):
                s_ref[h, c, :, q0:] = lax.dot_general(kh, qc_ref[h, c, q0:, :], (((1,), (1,)), ((), ())),
                                                      preferred_element_type=F32)

    def consume(j, s_ref, mask, q0=0):
        start = pl.multiple_of(j * TK, TK)
        vb = v_ref[pl.ds(start, TK), :]
        for h in range(H_A):
            vh = vb[:, h * DV_A:(h + 1) * DV_A]
            for c in range(2):
                s = s_ref[h, c, :, q0:]
                if mask is not None:
                    s = jnp.where(mask[:, q0:], s, MASK_VALUE)
                m_old = m_ref[h, c, :, q0:]
                m_new = jnp.maximum(m_old, jnp.max(s, axis=0, keepdims=True))
                alpha = jnp.exp2(m_old - m_new)
                p = jnp.exp2(s - m_new)
                l_ref[h, c, :, q0:] = alpha * l_ref[h, c, :, q0:] + jnp.sum(p, axis=0, keepdims=True)
                pv = lax.dot_general(vh, p.astype(BF16), (((0,), (0,)), ((), ())),
                                     preferred_element_type=F32)
                acc_ref[h, c, :, q0:] = alpha * acc_ref[h, c, :, q0:] + pv
                m_ref[h, c, :, q0:] = m_new

    first_diag = KV_PER_Q * qi
    scores(0, sa_ref)

    def body(jj, carry):
        j = 2 * jj
        scores(j + 1, sb_ref)
        consume(j, sa_ref, None)
        scores(j + 2, sa_ref)
        consume(j + 1, sb_ref, None)
        return carry

    lax.fori_loop(0, first_diag // 2, body, 0)
    for d in range(KV_PER_Q):
        cur, nxt = (sa_ref, sb_ref) if d % 2 == 0 else (sb_ref, sa_ref)
        if d + 1 < KV_PER_Q:
            scores(first_diag + d + 1, nxt, (d + 1) * TK)
        consume(first_diag + d, cur, (kchunk + d * (TK // CHUNK)) <= qchunk, d * TK)

    lam = lam_ref[0]
    for h in range(H_A):
        o = acc_ref[h, 0] / l_ref[h, 0] - lam * (acc_ref[h, 1] / l_ref[h, 1])
        ms = jnp.mean(o * o, axis=0, keepdims=True)
        y = o * lax.rsqrt(ms + EPS) * g_ref[...] * (1.0 - lam_init)
        o_ref[:, h * DV_A:(h + 1) * DV_A] = y.T.astype(BF16)


def _attention(qk, v, lam, subln_col, batch, seq, lam_init):
    assert TQ == KV_PER_Q * TK and KV_PER_Q % 2 == 0 and TK % CHUNK == 0 and seq % TQ == 0
    nq = seq // TQ
    t_tok = batch * seq
    kernel = functools.partial(_attn_kernel, lam_init=lam_init)
    return pl.pallas_call(
        kernel,
        grid=(batch, nq),
        in_specs=[
            pl.BlockSpec(memory_space=pltpu.SMEM),
            pl.BlockSpec((TQ, W_A), lambda b, i: (b * nq + i, 0)),
            pl.BlockSpec((seq, W_A), lambda b, i: (b, 1)),
            pl.BlockSpec((seq, W_A), lambda b, i: (b, 0)),
            pl.BlockSpec((DV_A, 1), lambda b, i: (0, 0)),
        ],
        out_specs=pl.BlockSpec((TQ, W_A), lambda b, i: (b * nq + i, 0)),
        out_shape=jax.ShapeDtypeStruct((t_tok, W_A), BF16),
        scratch_shapes=[
            pltpu.VMEM((H_A, 2, TQ, DV_A), BF16),
            pltpu.VMEM((H_A, 2, TK, TQ), F32),
            pltpu.VMEM((H_A, 2, TK, TQ), F32),
            pltpu.VMEM((H_A, 2, DV_A, TQ), F32),
            pltpu.VMEM((H_A, 2, 1, TQ), F32),
            pltpu.VMEM((H_A, 2, 1, TQ), F32),
        ],
        compiler_params=_cparams(("parallel", "arbitrary")),
        name="diff_attn",
    )(lam, qk, qk, v, subln_col)


_LEVELS = (32, 16, 8, 4, 2, 1)


def _ref_rows(g, h):
    n = g.shape[1]
    if 2 * h >= SUBLANES:
        pieces = []
        for m in range(CHUNK // (2 * h)):
            r = 2 * h * m + h - 1
            pieces.append(jnp.broadcast_to(g[r:r + 1, :], (2 * h, n)))
        return pieces[0] if len(pieces) == 1 else jnp.concatenate(pieces, axis=0)
    g3 = g.reshape(CHUNK // SUBLANES, SUBLANES, n)
    sub = lax.broadcasted_iota(jnp.int32, g3.shape, 1)
    out = None
    for m in reversed(range(SUBLANES // (2 * h))):
        r = 2 * h * m + h - 1
        piece = jnp.broadcast_to(g3[:, r:r + 1, :], g3.shape)
        out = piece if out is None else jnp.where(sub < 2 * h * (m + 1), piece, out)
    return out.reshape(CHUNK, n)


def _stack_heads(a, lane_head):
    zero = jnp.zeros_like(a)
    return jnp.concatenate([jnp.where(lane_head == hh, a, zero) for hh in range(H_B)], axis=0)


def _hgrn_kernel(hb_ref, f_ref, c_ref, lb_ref, ng_ref, cw_ref, gavg_ref, o_ref, state_ref, hbuf_ref, *, seq):
    n_chunks = seq // CHUNK
    nb = hb_ref.shape[0] // seq
    lb = lb_ref[...]
    ng = ng_ref[...]
    gavg = gavg_ref[...]

    row = lax.broadcasted_iota(jnp.int32, (CHUNK, W_B), 0)
    lane = lax.broadcasted_iota(jnp.int32, (CHUNK, W_B), 1)
    lane_head = lane // DK_B
    lane_s = lane % CHUNK
    tri = (lax.broadcasted_iota(jnp.int32, (CHUNK, CHUNK), 1)
           <= lax.broadcasted_iota(jnp.int32, (CHUNK, CHUNK), 0)).astype(BF16)
    lvl_mask = [((row // (2 * h)) == (lane_s // (2 * h))) & ((row % (2 * h)) >= h) & ((lane_s % (2 * h)) < h)
                for h in _LEVELS]
    eye_mask = row == lane_s
    r4 = lax.broadcasted_iota(jnp.int32, (H_B * DK_B, W_B), 0) // DK_B
    c4 = lax.broadcasted_iota(jnp.int32, (H_B * DK_B, W_B), 1) // DK_B
    bd_mask = r4 == c4

    state_ref[...] = jnp.zeros_like(state_ref)
    nt = (((1,), (1,)), ((), ()))
    tn = (((0,), (0,)), ((), ()))
    unroll = HGRN_UNROLL if n_chunks % HGRN_UNROLL == 0 else 1
    each = range(nb * unroll)

    def chunk(i, carry):
        rows = [pl.ds(pl.multiple_of((e % nb) * seq + (i * unroll + e // nb) * CHUNK, CHUNK), CHUNK) for e in each]
        hb = [hb_ref[rows[j], :].astype(F32) for j in each]
        zq = [t[:, 0:W_B] for t in hb]
        v16 = [t[:, W_B:2 * W_B].astype(BF16) for t in hb]
        zg = [t[:, 2 * W_B:3 * W_B] for t in hb]
        zf = [f_ref[rows[j], :] for j in each]

        logf = [jnp.log(jnp.maximum(lb + (1.0 - lb) * jax.nn.sigmoid(z), TINY)) for z in zf]
        key = [(1.0 - lb) * jax.nn.sigmoid(-z) for z in zf]
        q = [z * jax.nn.sigmoid(z) for z in zq]

        g = []
        for x in logf:
            hi = x.astype(BF16)
            r1 = x - hi.astype(F32)
            mid = r1.astype(BF16)
            lo = (r1 - mid.astype(F32)).astype(BF16)
            g.append((jnp.dot(tri, hi, preferred_element_type=F32) + jnp.dot(tri, mid, preferred_element_type=F32)
                      + jnp.dot(tri, lo, preferred_element_type=F32)) * LOG2_E)

        q16 = [t.astype(BF16) for t in q]
        k16 = [t.astype(BF16) for t in key]
        a_all = [jnp.where(eye_mask, lax.dot_general(q16[j], _stack_heads(k16[j], lane_head),
                                                     nt, preferred_element_type=F32), 0.0) for j in each]
        for h, mask in zip(_LEVELS, lvl_mask):
            w = [jnp.exp2(-jnp.abs(g[j] - _ref_rows(g[j], h))).astype(BF16) for j in each]
            ah = [lax.dot_general(q16[j] * w[j], _stack_heads(k16[j] * w[j], lane_head),
                                  nt, preferred_element_type=F32) for j in each]
            a_all = [a_all[j] + jnp.where(mask, ah[j], 0.0) for j in each]
        o = [jnp.dot(a_all[j].astype(BF16), _stack_heads(v16[j], lane_head), preferred_element_type=F32) for j in each]

        qg = [(q[j] * jnp.exp2(g[j])).astype(BF16) for j in each]
        g_last = [t[CHUNK - 1:CHUNK, :] for t in g]
        upd = [lax.dot_general(v16[j], (key[j] * jnp.exp2(g_last[j] - g[j])).astype(BF16), tn,
                               preferred_element_type=F32) for j in each]
        for j in each:
            st = state_ref[j % nb]
            o[j] = o[j] + lax.dot_general(qg[j], st.astype(BF16), nt, preferred_element_type=F32)
            state_ref[j % nb] = jnp.exp2(g_last[j]) * st + jnp.where(bd_mask, upd[j], 0.0)

        ms = [jnp.dot((t * t).astype(BF16), gavg, preferred_element_type=F32) for t in o]
        for j in each:
            y = o[j] * lax.rsqrt(ms[j] + EPS) * ng * (zg[j] * jax.nn.sigmoid(zg[j]))
            o_ref[rows[j], 0:W_B] = y.astype(BF16)
        return carry

    lax.fori_loop(0, n_chunks // unroll, chunk, 0)

    piece = min(512, seq)
    hbuf_ref[0:SUBLANES, :] = jnp.zeros((SUBLANES, C_WIDTH), F32)
    for j in range(nb):
        for p in range(seq // piece):
            cblk = c_ref[j * seq + p * piece:j * seq + (p + 1) * piece, :].astype(F32)
            hbuf_ref[SUBLANES + p * piece:SUBLANES + (p + 1) * piece, :] = (cblk[:, C_WIDTH:2 * C_WIDTH]
                                                                             * cblk[:, 2 * C_WIDTH:])
        for p in range(seq // piece):
            base = SUBLANES + p * piece
            y = (cw_ref[0:1, :] * hbuf_ref[base - 2:base - 2 + piece, :]
                 + cw_ref[1:2, :] * hbuf_ref[base - 1:base - 1 + piece, :]
                 + cw_ref[2:3, :] * hbuf_ref[base:base + piece, :])
            lo_r = j * seq + p * piece
            bgate = c_ref[lo_r:lo_r + piece, 0:C_WIDTH].astype(F32)
            o_ref[lo_r:lo_r + piece, W_B:W_B + C_WIDTH] = (bgate * y).astype(BF16)


def _hgrn_conv(hb, f, cv, lb, ng, cw, gavg, batch, seq):
    t_tok = batch * seq
    nb = HGRN_BATCH if batch % HGRN_BATCH == 0 else 1
    fixed = lambda b: (0, 0)
    return pl.pallas_call(
        functools.partial(_hgrn_kernel, seq=seq),
        grid=(batch // nb,),
        in_specs=[
            pl.BlockSpec((nb * seq, 3 * W_B), lambda b: (b, 0)),
            pl.BlockSpec((nb * seq, W_B), lambda b: (b, 0)),
            pl.BlockSpec((nb * seq, 3 * C_WIDTH), lambda b: (b, 0)),
            pl.BlockSpec((1, W_B), fixed),
            pl.BlockSpec((1, W_B), fixed),
            pl.BlockSpec((CONV_WIDTH, C_WIDTH), fixed),
            pl.BlockSpec((W_B, W_B), fixed),
        ],
        out_specs=pl.BlockSpec((nb * seq, W_B + C_WIDTH), lambda b: (b, 0)),
        out_shape=jax.ShapeDtypeStruct((t_tok, W_B + C_WIDTH), BF16),
        scratch_shapes=[
            pltpu.VMEM((nb, H_B * DV_B, W_B), F32),
            pltpu.VMEM((seq + SUBLANES, C_WIDTH), F32),
        ],
        compiler_params=_cparams(("parallel",)),
        name="hgrn_conv",
    )(hb, f, cv, lb, ng, cw, gavg)


def _outproj_router_kernel(oa_ref, obc_ref, x_ref, wo_ref, g_ref, wr_ref, br_ref,
                           xo_ref, route_ref, route_t_ref, tab_ref, cnt_ref, carry_ref):
    i = pl.program_id(0)

    @pl.when(i == 0)
    def _():
        carry_ref[...] = jnp.zeros_like(carry_ref)

    mixed = (jnp.dot(oa_ref[...], wo_ref[0:W_A, :], preferred_element_type=F32)
             + jnp.dot(obc_ref[...], wo_ref[W_A:, :], preferred_element_type=F32))
    xn = x_ref[...] + mixed
    xo_ref[...] = xn

    ms = jnp.mean(xn * xn, axis=-1, keepdims=True)
    hn = (xn * lax.rsqrt(ms + EPS) * g_ref[...]).astype(BF16)
    logits = jnp.dot(hn, wr_ref[...], preferred_element_type=F32) + br_ref[...]

    tm = logits.shape[0]
    lt = logits.T[0:ROUTE_ROWS, :]
    r = lax.broadcasted_iota(jnp.int32, (ROUTE_ROWS, tm), 0)
    neg = -jnp.inf
    big = ROUTE_ROWS

    def top(cand):
        val = jnp.max(cand, axis=0, keepdims=True)
        idx = jnp.min(jnp.where(cand == val, r, big), axis=0, keepdims=True)
        return val, idx

    glog = jnp.where(r < N_GROUPS, lt, neg)
    gmax, gidx = top(glog)
    p_g = 1.0 / jnp.sum(jnp.exp(glog - gmax), axis=0, keepdims=True)
    off = r - (N_GROUPS + EXPERTS_PER_GROUP * gidx)
    elog = jnp.where(jnp.abs(2 * off - (EXPERTS_PER_GROUP - 1)) < EXPERTS_PER_GROUP, lt, neg)
    v1, i1 = top(elog)
    v2, i2 = top(jnp.where(r == i1, neg, elog))
    t2 = jnp.exp(v2 - v1)
    w1 = p_g / (1.0 + t2)
    w2 = p_g * t2 / (1.0 + t2)

    sel1 = r == i1
    sel2 = r == i2
    onehot = jnp.where(sel1, 1.0, jnp.where(sel2, 1.0, 0.0))
    earlier = (lax.broadcasted_iota(jnp.int32, (tm, tm), 0)
               < lax.broadcasted_iota(jnp.int32, (tm, tm), 1)).astype(BF16)
    local = jnp.dot(onehot.astype(BF16), earlier, preferred_element_type=F32)
    cnt = jnp.sum(onehot, axis=1, keepdims=True)
    wrows = jnp.floor((cnt + (MOE_WINDOW - 1)) * (1.0 / MOE_WINDOW)) * MOE_WINDOW
    below = (lax.broadcasted_iota(jnp.int32, (ROUTE_ROWS, ROUTE_LANES), 1)
             < lax.broadcasted_iota(jnp.int32, (ROUTE_ROWS, ROUTE_LANES), 0)).astype(BF16)
    wrows_rows = jnp.concatenate([jnp.broadcast_to(wrows, (ROUTE_ROWS, ROUTE_LANES)),
                                  jnp.zeros((ROUTE_LANES - ROUTE_ROWS, ROUTE_LANES), F32)], axis=0).astype(BF16)
    soff = jnp.dot(below, wrows_rows, preferred_element_type=F32)[:, 0:1]
    slot = local + soff
    row1 = jnp.sum(jnp.where(sel1, slot, 0.0), axis=0, keepdims=True)
    row2 = jnp.sum(jnp.where(sel2, slot, 0.0), axis=0, keepdims=True)

    k = lax.broadcasted_iota(jnp.int32, (SUBLANES, tm), 0)
    rec = jnp.where(k == 0, (i1 - N_GROUPS).astype(F32),
          jnp.where(k == 1, (i2 - N_GROUPS).astype(F32),
          jnp.where(k == 2, w1,
          jnp.where(k == 3, w2,
          jnp.where(k == 4, row1,
          jnp.where(k == 5, row2, 0.0))))))
    route_t_ref[...] = rec
    route_ref[...] = jnp.concatenate([rec, jnp.zeros((ROUTE_LANES - SUBLANES, tm), F32)], axis=0).T

    lane = lax.broadcasted_iota(jnp.int32, (ROUTE_ROWS, ROUTE_LANES), 1)
    tab_ref[0] = jnp.where(lane == 0, cnt, jnp.where(lane == 1, carry_ref[...], 0.0))
    carry_ref[...] = carry_ref[...] + jnp.floor((cnt + (SUBLANES - 1)) * (1.0 / SUBLANES)) * SUBLANES
    cnt_ref[...] = carry_ref[...]


def _outproj_router(o_a, o_bc, x2, wo, g, wr, br):
    t_tok = x2.shape[0]
    tm = min(TM_ROUTE, t_tok)
    assert t_tok % tm == 0, "token count must be a multiple of the routing tile"
    row = lambda i: (i, 0)
    fixed = lambda i: (0, 0)
    return pl.pallas_call(
        _outproj_router_kernel,
        grid=(t_tok // tm,),
        in_specs=[
            pl.BlockSpec((tm, W_A), row),
            pl.BlockSpec((tm, W_B + C_WIDTH), row),
            pl.BlockSpec((tm, D_MODEL), row),
            pl.BlockSpec(wo.shape, fixed),
            pl.BlockSpec((1, D_MODEL), fixed),
            pl.BlockSpec(wr.shape, fixed),
            pl.BlockSpec((1, ROUTE_LANES), fixed),
        ],
        out_specs=[
            pl.BlockSpec((tm, D_MODEL), row),
            pl.BlockSpec((tm, ROUTE_LANES), row),
            pl.BlockSpec((SUBLANES, tm), row),
            pl.BlockSpec((1, ROUTE_ROWS, ROUTE_LANES), lambda i: (i, 0, 0)),
            pl.BlockSpec((ROUTE_ROWS, ROUTE_LANES), fixed),
        ],
        out_shape=[
            jax.ShapeDtypeStruct((t_tok, D_MODEL), F32),
            jax.ShapeDtypeStruct((t_tok, ROUTE_LANES), F32),
            jax.ShapeDtypeStruct((t_tok // tm * SUBLANES, tm), F32),
            jax.ShapeDtypeStruct((t_tok // tm, ROUTE_ROWS, ROUTE_LANES), F32),
            jax.ShapeDtypeStruct((ROUTE_ROWS, ROUTE_LANES), F32),
        ],
        scratch_shapes=[pltpu.VMEM((ROUTE_ROWS, ROUTE_LANES), F32)],
        compiler_params=_cparams(("arbitrary",)),
        name="outproj_router",
    )(o_a, o_bc, x2, wo, g, wr, br)


def _start_copies(n, make_copy):
    def start(k, carry):
        make_copy(k).start()
        return carry

    lax.fori_loop(0, n, start, 0)


def _wait_copies(n, make_copy):
    def wait(k, carry):
        make_copy(k).wait()
        return carry

    lax.fori_loop(0, n, wait, 0)


def _window_copy_loops(n, make_copy):
    _start_copies(n, make_copy)
    _wait_copies(n, make_copy)


def _dispatch_kernel(pend_ref, tab_ref, tabp_ref, route_ref, x_ref, buf_ref, xp_ref, xq_ref, sem, semq):
    tm = x_ref.shape[0]
    r_st = xp_ref.shape[0]
    step = pl.program_id(0)

    @pl.when(pl.program_id(0) == 0)
    def _():
        xp_ref[...] = jnp.zeros_like(xp_ref)
        copies = []
        for e in range(N_EXPERTS):
            end = pend_ref[e]
            copies.append((None, pltpu.make_async_copy(
                xp_ref.at[0:MOE_BLOCK, :],
                buf_ref.at[pl.ds(pl.multiple_of(end - MOE_BLOCK, SUBLANES), MOE_BLOCK), :], sem)))
            copies.append((end >= MOE_BLOCK + MOE_WINDOW, pltpu.make_async_copy(
                xp_ref.at[0:MOE_WINDOW, :],
                buf_ref.at[pl.ds(pl.multiple_of(jnp.maximum(end - MOE_BLOCK - MOE_WINDOW, 0), SUBLANES), MOE_WINDOW), :],
                sem)))
        for cond, cp in copies:
            if cond is None:
                cp.start()
            else:
                pl.when(cond)(cp.start)
        for cond, cp in copies:
            if cond is None:
                cp.wait()
            else:
                pl.when(cond)(cp.wait)

        def tail_copy(k):
            return pltpu.make_async_copy(
                xp_ref.at[0:MOE_BLOCK, :],
                buf_ref.at[pl.ds(pl.multiple_of(k * MOE_BLOCK, MOE_BLOCK), MOE_BLOCK), :], sem)

        first_unused = pend_ref[N_EXPERTS - 1] // MOE_BLOCK
        n_unused = buf_ref.shape[0] // MOE_BLOCK - first_unused
        _window_copy_loops(n_unused, lambda k: tail_copy(first_unused + k))

    r0 = route_ref[4:5, :].astype(jnp.int32)
    r1 = route_ref[5:6, :].astype(jnp.int32)
    part = r_st // STAGE_PARTS
    srow = lax.broadcasted_iota(jnp.int32, (part, tm), 0)

    def window(tabs, stage_ref, s):
        def make_copy(k):
            src = pl.multiple_of(k * MOE_WINDOW, MOE_WINDOW)
            return pltpu.make_async_copy(
                stage_ref.at[pl.ds(src, MOE_WINDOW), :],
                buf_ref.at[pl.ds(pl.multiple_of(tabs[0, 0, k], SUBLANES), MOE_WINDOW), :], s)
        return tabs[0, 0, MAX_WINDOWS], make_copy

    def run(cur_ref, cur_sem, prev_ref, prev_sem):
        x16 = x_ref[...].astype(BF16)
        for p in range(STAGE_PARTS):
            sel = jnp.where(srow + p * part == r0, 1.0, jnp.where(srow + p * part == r1, 1.0, 0.0)).astype(BF16)
            cur_ref[p * part:(p + 1) * part, :] = _pack_rows(jnp.dot(sel, x16, preferred_element_type=F32))

        @pl.when(step > 0)
        def _():
            _wait_copies(*window(tabp_ref, prev_ref, prev_sem))

        _start_copies(*window(tab_ref, cur_ref, cur_sem))

        @pl.when(step == pl.num_programs(0) - 1)
        def _():
            _wait_copies(*window(tab_ref, cur_ref, cur_sem))

    @pl.when(step % 2 == 0)
    def _():
        run(xp_ref, sem, xq_ref, semq)

    @pl.when(step % 2 == 1)
    def _():
        run(xq_ref, semq, xp_ref, sem)


def _dispatch(pad_end, tabs, route_t, x2, cap):
    t_tok = x2.shape[0]
    tm = min(TM_ROUTE, t_tok)
    assert t_tok % tm == 0, "token count must be a multiple of the routing tile"
    assert _staging_rows(tm) >= MOE_BLOCK, "the staging buffer doubles as the zero source of a whole block"
    n_tiles = t_tok // tm
    return pl.pallas_call(
        _dispatch_kernel,
        grid=(n_tiles,),
        in_specs=[
            pl.BlockSpec(memory_space=pltpu.SMEM),
            pl.BlockSpec((1, 1, ROUTE_LANES), lambda i: (i, 0, 0), memory_space=pltpu.SMEM),
            pl.BlockSpec((1, 1, ROUTE_LANES), lambda i: (jnp.maximum(i - 1, 0), 0, 0), memory_space=pltpu.SMEM),
            pl.BlockSpec((SUBLANES, tm), lambda i: (i, 0)),
            pl.BlockSpec((tm, D_MODEL), lambda i: (i, 0)),
        ],
        out_specs=pl.BlockSpec(memory_space=pl.ANY),
        out_shape=jax.ShapeDtypeStruct((cap, D_PACKED), jnp.int32),
        scratch_shapes=[pltpu.VMEM((_staging_rows(tm), D_PACKED), jnp.int32),
                        pltpu.VMEM((_staging_rows(tm), D_PACKED), jnp.int32),
                        pltpu.SemaphoreType.DMA(()), pltpu.SemaphoreType.DMA(())],
        compiler_params=_cparams(("arbitrary",)),
        name="moe_dispatch",
    )(pad_end, tabs, tabs, route_t, x2)


def _experts_kernel(blk_e_ref, n_used_ref, x_ref, g_ref, wg_ref, wu_ref, wd_ref, y_ref, wg16, wu16, wd16):
    i = pl.program_id(0)
    used = i < n_used_ref[0]

    @pl.when(used & ((i == 0) | (blk_e_ref[i] != blk_e_ref[jnp.maximum(i - 1, 0)])))
    def _():
        wg16[...] = wg_ref[0, 0].astype(BF16)
        wu16[...] = wu_ref[0, 0].astype(BF16)
        wd16[...] = wd_ref[0, 0].astype(BF16)

    @pl.when(used)
    def _():
        wg = wg16[...]
        wu = wu16[...]
        wd = wd16[...]
        parts = _row_parts(x_ref.shape[0], 4)
        hn = []
        for rows in parts:
            x = _unpack_rows(x_ref[rows, :])
            ms = jnp.mean(x * x, axis=-1, keepdims=True)
            hn.append((x * lax.rsqrt(ms + EPS) * g_ref[...]).astype(BF16))
        a = [jnp.dot(t, wg, preferred_element_type=F32) for t in hn]
        u = [jnp.dot(t, wu, preferred_element_type=F32) for t in hn]
        act = [(a[r] * jax.nn.sigmoid(a[r]) * u[r]).astype(BF16) for r in range(len(parts))]
        for r, rows in enumerate(parts):
            y = jnp.dot(act[r], wd, preferred_element_type=F32)
            y_ref[rows, :] = _pack_rows(y.astype(BF16).astype(F32))


def _experts(blk_e, n_used, buf, g, wg, wu, wd, layer):
    n_blocks = buf.shape[0] // MOE_BLOCK
    grid_spec = pltpu.PrefetchScalarGridSpec(
        num_scalar_prefetch=2,
        grid=(n_blocks,),
        in_specs=[
            pl.BlockSpec((MOE_BLOCK, D_PACKED), lambda i, be, nu: (jnp.minimum(i, nu[0] - 1), 0)),
            pl.BlockSpec((1, D_MODEL), lambda i, be, nu: (0, 0)),
            pl.BlockSpec((1, 1, D_MODEL, D_EXPERT), lambda i, be, nu: (layer, be[i], 0, 0)),
            pl.BlockSpec((1, 1, D_MODEL, D_EXPERT), lambda i, be, nu: (layer, be[i], 0, 0)),
            pl.BlockSpec((1, 1, D_EXPERT, D_MODEL), lambda i, be, nu: (layer, be[i], 0, 0)),
        ],
        out_specs=pl.BlockSpec((MOE_BLOCK, D_PACKED), lambda i, be, nu: (jnp.minimum(i, nu[0] - 1), 0)),
        scratch_shapes=[pltpu.VMEM((D_MODEL, D_EXPERT), BF16), pltpu.VMEM((D_MODEL, D_EXPERT), BF16),
                        pltpu.VMEM((D_EXPERT, D_MODEL), BF16)],
    )
    return pl.pallas_call(
        _experts_kernel,
        grid_spec=grid_spec,
        out_shape=jax.ShapeDtypeStruct(buf.shape, jnp.int32),
        input_output_aliases={2: 0},
        compiler_params=_cparams(("arbitrary",)),
        name="moe_experts",
    )(blk_e, n_used, buf, g, wg, wu, wd)


def _combine_kernel(tab_ref, tabn_ref, route_ref, x_ref, y_ref, o_ref, ya_ref, yb_ref, sema, semb):
    tm = x_ref.shape[0]
    r_st = ya_ref.shape[0]
    step = pl.program_id(0)

    def window(tabs, stage_ref, s):
        def make_copy(k):
            dst = pl.multiple_of(k * MOE_WINDOW, MOE_WINDOW)
            return pltpu.make_async_copy(y_ref.at[pl.ds(pl.multiple_of(tabs[0, 0, k], SUBLANES), MOE_WINDOW), :],
                                         stage_ref.at[pl.ds(dst, MOE_WINDOW), :], s)
        return tabs[0, 0, MAX_WINDOWS], make_copy

    @pl.when(step == 0)
    def _():
        ya_ref[...] = jnp.zeros_like(ya_ref)
        yb_ref[...] = jnp.zeros_like(yb_ref)
        _start_copies(*window(tab_ref, ya_ref, sema))

    route = route_ref[...]
    r0 = route[:, 4:5].astype(jnp.int32)
    r1 = route[:, 5:6].astype(jnp.int32)
    part = r_st // STAGE_PARTS
    col = lax.broadcasted_iota(jnp.int32, (tm, part), 1)

    def run(cur_ref, cur_sem, nxt_ref, nxt_sem):
        @pl.when(step + 1 < pl.num_programs(0))
        def _():
            _start_copies(*window(tabn_ref, nxt_ref, nxt_sem))

        _wait_copies(*window(tab_ref, cur_ref, cur_sem))
        acc = x_ref[...]
        for p in range(STAGE_PARTS):
            gate = jnp.where(col + p * part == r0, route[:, 2:3],
                             jnp.where(col + p * part == r1, route[:, 3:4], 0.0)).astype(BF16)
            rows = _unpack_rows(cur_ref[p * part:(p + 1) * part, :]).astype(BF16)
            acc = acc + jnp.dot(gate, rows, preferred_element_type=F32)
        o_ref[...] = acc

    @pl.when(step % 2 == 0)
    def _():
        run(ya_ref, sema, yb_ref, semb)

    @pl.when(step % 2 == 1)
    def _():
        run(yb_ref, semb, ya_ref, sema)


def _combine(tabs, route, x2, y_buf):
    t_tok = x2.shape[0]
    tm = min(TM_ROUTE, t_tok)
    assert t_tok % tm == 0, "token count must be a multiple of the routing tile"
    n_tiles = t_tok // tm
    return pl.pallas_call(
        _combine_kernel,
        grid=(n_tiles,),
        in_specs=[
            pl.BlockSpec((1, 1, ROUTE_LANES), lambda i: (i, 0, 0), memory_space=pltpu.SMEM),
            pl.BlockSpec((1, 1, ROUTE_LANES), lambda i: (jnp.minimum(i + 1, n_tiles - 1), 0, 0),
                         memory_space=pltpu.SMEM),
            pl.BlockSpec((tm, ROUTE_LANES), lambda i: (i, 0)),
            pl.BlockSpec((tm, D_MODEL), lambda i: (i, 0)),
            pl.BlockSpec(memory_space=pl.ANY),
        ],
        out_specs=pl.BlockSpec((tm, D_MODEL), lambda i: (i, 0)),
        out_shape=jax.ShapeDtypeStruct((t_tok, D_MODEL), F32),
        scratch_shapes=[pltpu.VMEM((_staging_rows(tm), D_PACKED), jnp.int32),
                        pltpu.VMEM((_staging_rows(tm), D_PACKED), jnp.int32),
                        pltpu.SemaphoreType.DMA(()), pltpu.SemaphoreType.DMA(())],
        compiler_params=_cparams(("arbitrary",)),
        name="moe_combine",
    )(tabs, tabs, route, x2, y_buf)


def _rope_tables(positions):
    half = ROT_DIM // 2
    inv_freq = ROPE_THETA ** (-jnp.arange(0, ROT_DIM, 2, dtype=F32) / ROT_DIM)
    d = jnp.arange(LANES) % DK_A
    freq_lane = jnp.where(d < ROT_DIM, inv_freq[d % half], 0.0)
    ang = positions.reshape(-1, 1).astype(F32) * freq_lane[None, :]
    cos, sin = jnp.cos(ang), jnp.sin(ang)
    s1 = jnp.where(d < half, -sin, 0.0)
    s2 = jnp.where((d >= half) & (d < ROT_DIM), sin, 0.0)
    return cos, s1, s2


def _block_avg(width, group):
    idx = jnp.arange(width) // group
    return jnp.where(idx[:, None] == idx[None, :], 1.0 / group, 0.0).astype(BF16)


def kernel(x, positions, norm_mix_g, w_in, attn_q_norm_g, attn_k_norm_g, lambda_q1, lambda_k1, lambda_q2,
           lambda_k2, attn_subln_g, hgrn_lower_bounds, hgrn_norm_g, conv_w, w_out, norm_ffn_g, router_group_w,
           router_group_b, router_expert_w, router_expert_b, expert_w_gate, expert_w_up, expert_w_down):
    batch, seq, d = x.shape
    assert d == D_MODEL
    depth = w_in.shape[0]
    t_tok = batch * seq
    n_assign = 2 * t_tok
    n_route_tiles = t_tok // min(TM_ROUTE, t_tok)
    n_blocks = (n_assign + (SUBLANES - 1) * n_route_tiles * N_EXPERTS) // MOE_BLOCK + 2 * N_EXPERTS + 1
    cap = n_blocks * MOE_BLOCK

    lb_probs = jax.nn.softmax(hgrn_lower_bounds.astype(F32), axis=0)
    lower_bounds = jnp.cumsum(lb_probs, axis=0) - lb_probs[0:1]
    cos_t, s1_t, s2_t = _rope_tables(positions)
    gsum_a = _block_avg(W_A, DK_A)
    gavg_b = _block_avg(W_B, DV_B)

    o_q, o_k, o_v = 0, W_A, 2 * W_A
    o_bq = 3 * W_A
    o_bf, o_bi, o_bg = o_bq + W_B, o_bq + 2 * W_B, o_bq + 3 * W_B
    o_c = o_bq + 4 * W_B

    x2 = x.reshape(t_tok, d)
    for l in range(depth):
        w = w_in[l]
        wqk = w[:, o_q:o_v].astype(BF16)
        wv = w[:, o_v:o_bq].astype(BF16)
        whb = jnp.concatenate([w[:, o_bq:o_bf], w[:, o_bi:o_c]], axis=1).astype(BF16)
        wf = w[:, o_bf:o_bi].astype(BF16)
        wc = w[:, o_c:].astype(BF16)
        gqk = jnp.concatenate([jnp.tile(attn_q_norm_g[l], 2 * H_A) * (DK_A ** -0.5 * LOG2_E),
                               jnp.tile(attn_k_norm_g[l], 2 * H_A)]).reshape(1, 2 * W_A).astype(F32)
        qk, v, hb, f, cv = _inproj(x2, norm_mix_g[l].reshape(1, d), wqk, wv, whb, wf, wc, gsum_a, gqk,
                                   cos_t, s1_t, s2_t)

        lam_init = 0.8 - 0.6 * math.exp(-0.3 * l)
        lam = (jnp.exp(jnp.sum(lambda_q1[l].astype(F32) * lambda_k1[l].astype(F32)))
               - jnp.exp(jnp.sum(lambda_q2[l].astype(F32) * lambda_k2[l].astype(F32))) + lam_init).reshape(1)
        o_a = _attention(qk, v, lam, attn_subln_g[l].reshape(DV_A, 1).astype(F32), batch, seq, lam_init)

        o_bc = _hgrn_conv(hb, f, cv, lower_bounds[l].reshape(1, W_B), jnp.tile(hgrn_norm_g[l], H_B).reshape(1, W_B),
                          conv_w[l], gavg_b, batch, seq)

        wr = jnp.concatenate([router_group_w[l], router_expert_w[l],
                              jnp.zeros((d, ROUTE_LANES - N_LOGITS), F32)], axis=1).astype(BF16)
        br = jnp.concatenate([router_group_b[l], router_expert_b[l],
                              jnp.zeros((ROUTE_LANES - N_LOGITS,), F32)]).reshape(1, ROUTE_LANES)
        x2, route, route_t, tab, cnt = _outproj_router(o_a, o_bc, x2, w_out[l].astype(BF16), norm_ffn_g[l].reshape(1, d), wr, br)

        counts = cnt[N_GROUPS:N_GROUPS + N_EXPERTS, 0].astype(jnp.int32)
        padded = (counts + MOE_WINDOW + MOE_BLOCK - 1) // MOE_BLOCK * MOE_BLOCK
        pad_end = jnp.cumsum(padded)
        pad_start = pad_end - padded
        blk_pos = jnp.arange(n_blocks, dtype=jnp.int32) * MOE_BLOCK
        blk_e = jnp.minimum(jnp.sum((pad_end[None, :] <= blk_pos[:, None]).astype(jnp.int32), axis=1),
                            N_EXPERTS - 1)
        n_used = (pad_end[-1] // MOE_BLOCK).astype(jnp.int32).reshape(1)
        tile_cnt = tab[:, N_GROUPS:N_GROUPS + N_EXPERTS, 0].astype(jnp.int32)
        tile_base = tab[:, N_GROUPS:N_GROUPS + N_EXPERTS, 1].astype(jnp.int32) + pad_start[None, :]
        n_win = (tile_cnt + MOE_WINDOW - 1) // MOE_WINDOW
        w_end = jnp.cumsum(n_win, axis=1)
        w_start = w_end - n_win
        k_idx = jnp.arange(MAX_WINDOWS, dtype=jnp.int32)[None, :, None]
        owner = (w_start[:, None, :] <= k_idx) & (k_idx < w_end[:, None, :])
        win_dst = jnp.sum(jnp.where(owner, tile_base[:, None, :] + (k_idx - w_start[:, None, :]) * MOE_WINDOW, 0),
                          axis=-1)
        tabs = jnp.concatenate([win_dst, w_end[:, -1:],
                                jnp.zeros((win_dst.shape[0], ROUTE_LANES - MAX_WINDOWS - 1), jnp.int32)],
                               axis=1).reshape(-1, 1, ROUTE_LANES)

        buf = _dispatch(pad_end, tabs, route_t, x2, cap)
        y_buf = _experts(blk_e, n_used, buf, norm_ffn_g[l].reshape(1, d), expert_w_gate, expert_w_up,
                         expert_w_down, l)
        x2 = _combine(tabs, route, x2, y_buf)
    return x2.reshape(batch, seq, d)
```

```python
import functools
import math

import jax
import jax.numpy as jnp
from jax import lax
from jax.experimental import pallas as pl
from jax.experimental.pallas import tpu as pltpu

F32 = jnp.float32
BF16 = jnp.bfloat16

LANES = 128
SUBLANES = 8

D_MODEL = 1024
CHUNK = 64
EPS = 1e-6
MASK_VALUE = -1e30
TINY = 1e-30
LOG2_E = math.log2(math.e)
H_A = 4
DK_A = D_MODEL // 16
DV_A = 2 * DK_A
ROT_DIM = DK_A // 4
ROPE_THETA = 500000.0
H_B = 4
DK_B = D_MODEL // 16
DV_B = D_MODEL // 16
C_WIDTH = D_MODEL // 4
CONV_WIDTH = 3
N_GROUPS = 4
EXPERTS_PER_GROUP = 8
N_EXPERTS = N_GROUPS * EXPERTS_PER_GROUP
D_EXPERT = D_MODEL // 2
D_PACKED = D_MODEL // 2

W_A = H_A * DV_A
W_B = H_B * DK_B
ROUTE_LANES = LANES
N_LOGITS = N_GROUPS + N_EXPERTS
ROUTE_ROWS = -(-N_LOGITS // SUBLANES) * SUBLANES

TM_PROJ = 1024
TK = 256
KV_PER_Q = 2
TQ = KV_PER_Q * TK
TM_ROUTE = 512
MOE_BLOCK = 1024
HGRN_BATCH = 2
HGRN_UNROLL = 2
MOE_WINDOW = 16
STAGE_PARTS = 3
MAX_WINDOWS = 2 * TM_ROUTE // MOE_WINDOW + N_EXPERTS
VMEM_LIMIT = 56 * 1024 * 1024


def _staging_rows(tm):
    return 2 * tm + N_EXPERTS * MOE_WINDOW


def _row_parts(rows, n=2):
    step = rows // n
    return [slice(r * step, (r + 1) * step) for r in range(n)]


def _pack_rows(x):
    half = x.shape[1] // 2
    lo = lax.bitcast_convert_type(x[:, :half], jnp.int32)
    hi = lax.bitcast_convert_type(x[:, half:], jnp.int32)
    return hi | lax.shift_right_logical(lo, 16)


def _unpack_rows(p):
    lo = lax.bitcast_convert_type(lax.shift_left(p, 16), F32)
    hi = lax.bitcast_convert_type(p & jnp.int32(-65536), F32)
    return jnp.concatenate([lo, hi], axis=1)


def _cparams(sem, vmem=VMEM_LIMIT):
    return pltpu.CompilerParams(dimension_semantics=sem, vmem_limit_bytes=vmem)


def _inproj_kernel(x_ref, g_ref, wqk_ref, wv_ref, whb_ref, wf_ref, wc_ref, gsum_ref, gqk_ref,
                   cos_ref, s1_ref, s2_ref, qk_out, v_out, hb_out, f_out, c_out):
    parts = _row_parts(x_ref.shape[0])
    xn = []
    for rows in parts:
        x = x_ref[rows, :]
        ms = jnp.mean(x * x, axis=-1, keepdims=True)
        xn.append((x * lax.rsqrt(ms + EPS) * g_ref[...]).astype(BF16))

    gsum = gsum_ref[...]
    for part in range(2):
        lo = part * W_A
        t = [jnp.dot(xr, wqk_ref[:, lo:lo + W_A], preferred_element_type=F32) for xr in xn]
        msq = [jnp.dot((tr * tr).astype(BF16), gsum, preferred_element_type=F32) for tr in t]
        for r, rows in enumerate(parts):
            tn = t[r] * lax.rsqrt(msq[r] + EPS) * gqk_ref[:, lo:lo + W_A]
            cos = cos_ref[rows, :]
            s1 = s1_ref[rows, :]
            s2 = s2_ref[rows, :]
            for h in range(H_A):
                slab = tn[:, h * LANES:(h + 1) * LANES]
                up = pltpu.roll(slab, LANES - ROT_DIM // 2, axis=1)
                dn = pltpu.roll(slab, ROT_DIM // 2, axis=1)
                qk_out[rows, lo + h * LANES:lo + (h + 1) * LANES] = (slab * cos + up * s1 + dn * s2).astype(BF16)

    for r, rows in enumerate(parts):
        v_out[rows, :] = jnp.dot(xn[r], wv_ref[...], preferred_element_type=F32).astype(BF16)
    for r, rows in enumerate(parts):
        hb_out[rows, :] = jnp.dot(xn[r], whb_ref[...], preferred_element_type=F32).astype(BF16)
    for r, rows in enumerate(parts):
        f_out[rows, :] = jnp.dot(xn[r], wf_ref[...], preferred_element_type=F32)
    for r, rows in enumerate(parts):
        c_out[rows, :] = jnp.dot(xn[r], wc_ref[...], preferred_element_type=F32).astype(BF16)


def _inproj(x2, g, wqk, wv, whb, wf, wc, gsum, gqk, cos_t, s1_t, s2_t):
    t_tok = x2.shape[0]
    tm = min(TM_PROJ, t_tok)
    assert t_tok % tm == 0, "token count must be a multiple of the projection tile"
    grid = (t_tok // tm,)
    row = lambda i: (i, 0)
    fixed = lambda i: (0, 0)
    return pl.pallas_call(
        _inproj_kernel,
        grid=grid,
        in_specs=[
            pl.BlockSpec((tm, D_MODEL), row),
            pl.BlockSpec((1, D_MODEL), fixed),
            pl.BlockSpec(wqk.shape, fixed),
            pl.BlockSpec(wv.shape, fixed),
            pl.BlockSpec(whb.shape, fixed),
            pl.BlockSpec(wf.shape, fixed),
            pl.BlockSpec(wc.shape, fixed),
            pl.BlockSpec(gsum.shape, fixed),
            pl.BlockSpec(gqk.shape, fixed),
            pl.BlockSpec((tm, LANES), row),
            pl.BlockSpec((tm, LANES), row),
            pl.BlockSpec((tm, LANES), row),
        ],
        out_specs=[
            pl.BlockSpec((tm, 2 * W_A), row),
            pl.BlockSpec((tm, W_A), row),
            pl.BlockSpec((tm, 3 * W_B), row),
            pl.BlockSpec((tm, W_B), row),
            pl.BlockSpec((tm, 3 * C_WIDTH), row),
        ],
        out_shape=[
            jax.ShapeDtypeStruct((t_tok, 2 * W_A), BF16),
            jax.ShapeDtypeStruct((t_tok, W_A), BF16),
            jax.ShapeDtypeStruct((t_tok, 3 * W_B), BF16),
            jax.ShapeDtypeStruct((t_tok, W_B), F32),
            jax.ShapeDtypeStruct((t_tok, 3 * C_WIDTH), BF16),
        ],
        compiler_params=_cparams(("parallel",)),
        name="inproj",
    )(x2, g, wqk, wv, whb, wf, wc, gsum, gqk, cos_t, s1_t, s2_t)


def _attn_kernel(lam_ref, q_ref, k_ref, v_ref, g_ref, o_ref, qc_ref, sa_ref, sb_ref, acc_ref, m_ref, l_ref, *,
                 lam_init):
    qi = pl.program_id(1)
    q = q_ref[...]
    lane = lax.broadcasted_iota(jnp.int32, (TQ, DV_A), 1)
    zero = jnp.zeros((TQ, DV_A), BF16)
    for h in range(H_A):
        qh = q[:, h * DV_A:(h + 1) * DV_A]
        qc_ref[h, 0] = jnp.where(lane < DK_A, qh, zero)
        qc_ref[h, 1] = jnp.where(lane >= DK_A, qh, zero)

    acc_ref[...] = jnp.zeros_like(acc_ref)
    m_ref[...] = jnp.full_like(m_ref, -jnp.inf)
    l_ref[...] = jnp.zeros_like(l_ref)

    kchunk = lax.broadcasted_iota(jnp.int32, (TK, TQ), 0) // CHUNK
    qchunk = lax.broadcasted_iota(jnp.int32, (TK, TQ), 1) // CHUNK

    def scores(j, s_ref, q0=0):
        start = pl.multiple_of(j * TK, TK)
        kb = k_ref[pl.ds(start, TK), :]
        for h in range(H_A):
            kh = kb[:, h * DV_A:(h + 1) * DV_A]
            for c in range(2):
                s_ref[h, c, :, q0:] = lax.dot_general(kh, qc_ref[h, c, q0:, :], (((1,), (1,)), ((), ())),
                                                      preferred_element_type=F32)

    def consume(j, s_ref, mask, q0=0):
        start = pl.multiple_of(j * TK, TK)
        vb = v_ref[pl.ds(start, TK), :]
        for h in range(H_A):
            vh = vb[:, h * DV_A:(h + 1) * DV_A]
            for c in range(2):
                s = s_ref[h, c, :, q0:]
                if mask is not None:
                    s = jnp.where(mask[:, q0:], s, MASK_VALUE)
                m_old = m_ref[h, c, :, q0:]
                m_new = jnp.maximum(m_old, jnp.max(s, axis=0, keepdims=True))
                alpha = jnp.exp2(m_old - m_new)
                p = jnp.exp2(s - m_new)
                l_ref[h, c, :, q0:] = alpha * l_ref[h, c, :, q0:] + jnp.sum(p, axis=0, keepdims=True)
                pv = lax.dot_general(vh, p.astype(BF16), (((0,), (0,)), ((), ())),
                                     preferred_element_type=F32)
                acc_ref[h, c, :, q0:] = alpha * acc_ref[h, c, :, q0:] + pv
                m_ref[h, c, :, q0:] = m_new

    first_diag = KV_PER_Q * qi
    scores(0, sa_ref)

    def body(jj, carry):
        j = 2 * jj
        scores(j + 1, sb_ref)
        consume(j, sa_ref, None)
        scores(j + 2, sa_ref)
        consume(j + 1, sb_ref, None)
        return carry

    lax.fori_loop(0, first_diag // 2, body, 0)
    for d in range(KV_PER_Q):
        cur, nxt = (sa_ref, sb_ref) if d % 2 == 0 else (sb_ref, sa_ref)
        if d + 1 < KV_PER_Q:
            scores(first_diag + d + 1, nxt, (d + 1) * TK)
        consume(first_diag + d, cur, (kchunk + d * (TK // CHUNK)) <= qchunk, d * TK)

    lam = lam_ref[0]
    for h in range(H_A):
        o = acc_ref[h, 0] / l_ref[h, 0] - lam * (acc_ref[h, 1] / l_ref[h, 1])
        ms = jnp.mean(o * o, axis=0, keepdims=True)
        y = o * lax.rsqrt(ms + EPS) * g_ref[...] * (1.0 - lam_init)
        o_ref[:, h * DV_A:(h + 1) * DV_A] = y.T.astype(BF16)


def _attention(qk, v, lam, subln_col, batch, seq, lam_init):
    assert TQ == KV_PER_Q * TK and KV_PER_Q % 2 == 0 and TK % CHUNK == 0 and seq % TQ == 0
    nq = seq // TQ
    t_tok = batch * seq
    kernel = functools.partial(_attn_kernel, lam_init=lam_init)
    return pl.pallas_call(
        kernel,
        grid=(batch, nq),
        in_specs=[
            pl.BlockSpec(memory_space=pltpu.SMEM),
            pl.BlockSpec((TQ, W_A), lambda b, i: (b * nq + i, 0)),
            pl.BlockSpec((seq, W_A), lambda b, i: (b, 1)),
            pl.BlockSpec((seq, W_A), lambda b, i: (b, 0)),
            pl.BlockSpec((DV_A, 1), lambda b, i: (0, 0)),
        ],
        out_specs=pl.BlockSpec((TQ, W_A), lambda b, i: (b * nq + i, 0)),
        out_shape=jax.ShapeDtypeStruct((t_tok, W_A), BF16),
        scratch_shapes=[
            pltpu.VMEM((H_A, 2, TQ, DV_A), BF16),
            pltpu.VMEM((H_A, 2, TK, TQ), F32),
            pltpu.VMEM((H_A, 2, TK, TQ), F32),
            pltpu.VMEM((H_A, 2, DV_A, TQ), F32),
            pltpu.VMEM((H_A, 2, 1, TQ), F32),
            pltpu.VMEM((H_A, 2, 1, TQ), F32),
        ],
        compiler_params=_cparams(("parallel", "arbitrary")),
        name="diff_attn",
    )(lam, qk, qk, v, subln_col)


_LEVELS = (32, 16, 8, 4, 2, 1)


def _ref_rows(g, h):
    n = g.shape[1]
    if 2 * h >= SUBLANES:
        pieces = []
        for m in range(CHUNK // (2 * h)):
            r = 2 * h * m + h - 1
            pieces.append(jnp.broadcast_to(g[r:r + 1, :], (2 * h, n)))
        return pieces[0] if len(pieces) == 1 else jnp.concatenate(pieces, axis=0)
    g3 = g.reshape(CHUNK // SUBLANES, SUBLANES, n)
    sub = lax.broadcasted_iota(jnp.int32, g3.shape, 1)
    out = None
    for m in reversed(range(SUBLANES // (2 * h))):
        r = 2 * h * m + h - 1
        piece = jnp.broadcast_to(g3[:, r:r + 1, :], g3.shape)
        out = piece if out is None else jnp.where(sub < 2 * h * (m + 1), piece, out)
    return out.reshape(CHUNK, n)


def _stack_heads(a, lane_head):
    zero = jnp.zeros_like(a)
    return jnp.concatenate([jnp.where(lane_head == hh, a, zero) for hh in range(H_B)], axis=0)


def _hgrn_kernel(hb_ref, f_ref, c_ref, lb_ref, ng_ref, cw_ref, gavg_ref, o_ref, state_ref, hbuf_ref, *, seq):
    n_chunks = seq // CHUNK
    nb = hb_ref.shape[0] // seq
    lb = lb_ref[...]
    ng = ng_ref[...]
    gavg = gavg_ref[...]

    row = lax.broadcasted_iota(jnp.int32, (CHUNK, W_B), 0)
    lane = lax.broadcasted_iota(jnp.int32, (CHUNK, W_B), 1)
    lane_head = lane // DK_B
    lane_s = lane % CHUNK
    tri = (lax.broadcasted_iota(jnp.int32, (CHUNK, CHUNK), 1)
           <= lax.broadcasted_iota(jnp.int32, (CHUNK, CHUNK), 0)).astype(BF16)
    lvl_mask = [((row // (2 * h)) == (lane_s // (2 * h))) & ((row % (2 * h)) >= h) & ((lane_s % (2 * h)) < h)
                for h in _LEVELS]
    eye_mask = row == lane_s
    r4 = lax.broadcasted_iota(jnp.int32, (H_B * DK_B, W_B), 0) // DK_B
    c4 = lax.broadcasted_iota(jnp.int32, (H_B * DK_B, W_B), 1) // DK_B
    bd_mask = r4 == c4

    state_ref[...] = jnp.zeros_like(state_ref)
    nt = (((1,), (1,)), ((), ()))
    tn = (((0,), (0,)), ((), ()))
    unroll = HGRN_UNROLL if n_chunks % HGRN_UNROLL == 0 else 1
    each = range(nb * unroll)

    def chunk(i, carry):
        rows = [pl.ds(pl.multiple_of((e % nb) * seq + (i * unroll + e // nb) * CHUNK, CHUNK), CHUNK) for e in each]
        hb = [hb_ref[rows[j], :].astype(F32) for j in each]
        zq = [t[:, 0:W_B] for t in hb]
        v16 = [t[:, W_B:2 * W_B].astype(BF16) for t in hb]
        zg = [t[:, 2 * W_B:3 * W_B] for t in hb]
        zf = [f_ref[rows[j], :] for j in each]

        logf = [jnp.log(jnp.maximum(lb + (1.0 - lb) * jax.nn.sigmoid(z), TINY)) for z in zf]
        key = [(1.0 - lb) * jax.nn.sigmoid(-z) for z in zf]
        q = [z * jax.nn.sigmoid(z) for z in zq]

        g = []
        for x in logf:
            hi = x.astype(BF16)
            r1 = x - hi.astype(F32)
            mid = r1.astype(BF16)
            lo = (r1 - mid.astype(F32)).astype(BF16)
            g.append((jnp.dot(tri, hi, preferred_element_type=F32) + jnp.dot(tri, mid, preferred_element_type=F32)
                      + jnp.dot(tri, lo, preferred_element_type=F32)) * LOG2_E)

        q16 = [t.astype(BF16) for t in q]
        k16 = [t.astype(BF16) for t in key]
        a_all = [jnp.where(eye_mask, lax.dot_general(q16[j], _stack_heads(k16[j], lane_head),
                                                     nt, preferred_element_type=F32), 0.0) for j in each]
        for h, mask in zip(_LEVELS, lvl_mask):
            w = [jnp.exp2(-jnp.abs(g[j] - _ref_rows(g[j], h))).astype(BF16) for j in each]
            ah = [lax.dot_general(q16[j] * w[j], _stack_heads(k16[j] * w[j], lane_head),
                                  nt, preferred_element_type=F32) for j in each]
            a_all = [a_all[j] + jnp.where(mask, ah[j], 0.0) for j in each]
        o = [jnp.dot(a_all[j].astype(BF16), _stack_heads(v16[j], lane_head), preferred_element_type=F32) for j in each]

        qg = [(q[j] * jnp.exp2(g[j])).astype(BF16) for j in each]
        g_last = [t[CHUNK - 1:CHUNK, :] for t in g]
        upd = [lax.dot_general(v16[j], (key[j] * jnp.exp2(g_last[j] - g[j])).astype(BF16), tn,
                               preferred_element_type=F32) for j in each]
        for j in each:
            st = state_ref[j % nb]
            o[j] = o[j] + lax.dot_general(qg[j], st.astype(BF16), nt, preferred_element_type=F32)
            state_ref[j % nb] = jnp.exp2(g_last[j]) * st + jnp.where(bd_mask, upd[j], 0.0)

        ms = [jnp.dot((t * t).astype(BF16), gavg, preferred_element_type=F32) for t in o]
        for j in each:
            y = o[j] * lax.rsqrt(ms[j] + EPS) * ng * (zg[j] * jax.nn.sigmoid(zg[j]))
            o_ref[rows[j], 0:W_B] = y.astype(BF16)
        return carry

    lax.fori_loop(0, n_chunks // unroll, chunk, 0)

    piece = min(512, seq)
    hbuf_ref[0:SUBLANES, :] = jnp.zeros((SUBLANES, C_WIDTH), F32)
    for j in range(nb):
        for p in range(seq // piece):
            cblk = c_ref[j * seq + p * piece:j * seq + (p + 1) * piece, :].astype(F32)
            hbuf_ref[SUBLANES + p * piece:SUBLANES + (p + 1) * piece, :] = (cblk[:, C_WIDTH:2 * C_WIDTH]
                                                                             * cblk[:, 2 * C_WIDTH:])
        for p in range(seq // piece):
            base = SUBLANES + p * piece
            y = (cw_ref[0:1, :] * hbuf_ref[base - 2:base - 2 + piece, :]
                 + cw_ref[1:2, :] * hbuf_ref[base - 1:base - 1 + piece, :]
                 + cw_ref[2:3, :] * hbuf_ref[base:base + piece, :])
            lo_r = j * seq + p * piece
            bgate = c_ref[lo_r:lo_r + piece, 0:C_WIDTH].astype(F32)
            o_ref[lo_r:lo_r + piece, W_B:W_B + C_WIDTH] = (bgate * y).astype(BF16)


def _hgrn_conv(hb, f, cv, lb, ng, cw, gavg, batch, seq):
    t_tok = batch * seq
    nb = HGRN_BATCH if batch % HGRN_BATCH == 0 else 1
    fixed = lambda b: (0, 0)
    return pl.pallas_call(
        functools.partial(_hgrn_kernel, seq=seq),
        grid=(batch // nb,),
        in_specs=[
            pl.BlockSpec((nb * seq, 3 * W_B), lambda b: (b, 0)),
            pl.BlockSpec((nb * seq, W_B), lambda b: (b, 0)),
            pl.BlockSpec((nb * seq, 3 * C_WIDTH), lambda b: (b, 0)),
            pl.BlockSpec((1, W_B), fixed),
            pl.BlockSpec((1, W_B), fixed),
            pl.BlockSpec((CONV_WIDTH, C_WIDTH), fixed),
            pl.BlockSpec((W_B, W_B), fixed),
        ],
        out_specs=pl.BlockSpec((nb * seq, W_B + C_WIDTH), lambda b: (b, 0)),
        out_shape=jax.ShapeDtypeStruct((t_tok, W_B + C_WIDTH), BF16),
        scratch_shapes=[
            pltpu.VMEM((nb, H_B * DV_B, W_B), F32),
            pltpu.VMEM((seq + SUBLANES, C_WIDTH), F32),
        ],
        compiler_params=_cparams(("parallel",)),
        name="hgrn_conv",
    )(hb, f, cv, lb, ng, cw, gavg)


def _outproj_router_kernel(oa_ref, obc_ref, x_ref, wo_ref, g_ref, wr_ref, br_ref,
                           xo_ref, route_ref, route_t_ref, tab_ref, cnt_ref, carry_ref):
    i = pl.program_id(0)

    @pl.when(i == 0)
    def _():
        carry_ref[...] = jnp.zeros_like(carry_ref)

    mixed = (jnp.dot(oa_ref[...], wo_ref[0:W_A, :], preferred_element_type=F32)
             + jnp.dot(obc_ref[...], wo_ref[W_A:, :], preferred_element_type=F32))
    xn = x_ref[...] + mixed
    xo_ref[...] = xn

    ms = jnp.mean(xn * xn, axis=-1, keepdims=True)
    hn = (xn * lax.rsqrt(ms + EPS) * g_ref[...]).astype(BF16)
    logits = jnp.dot(hn, wr_ref[...], preferred_element_type=F32) + br_ref[...]

    tm = logits.shape[0]
    lt = logits.T[0:ROUTE_ROWS, :]
    r = lax.broadcasted_iota(jnp.int32, (ROUTE_ROWS, tm), 0)
    neg = -jnp.inf
    big = ROUTE_ROWS

    def top(cand):
        val = jnp.max(cand, axis=0, keepdims=True)
        idx = jnp.min(jnp.where(cand == val, r, big), axis=0, keepdims=True)
        return val, idx

    glog = jnp.where(r < N_GROUPS, lt, neg)
    gmax, gidx = top(glog)
    p_g = 1.0 / jnp.sum(jnp.exp(glog - gmax), axis=0, keepdims=True)
    off = r - (N_GROUPS + EXPERTS_PER_GROUP * gidx)
    elog = jnp.where(jnp.abs(2 * off - (EXPERTS_PER_GROUP - 1)) < EXPERTS_PER_GROUP, lt, neg)
    v1, i1 = top(elog)
    v2, i2 = top(jnp.where(r == i1, neg, elog))
    t2 = jnp.exp(v2 - v1)
    w1 = p_g / (1.0 + t2)
    w2 = p_g * t2 / (1.0 + t2)

    sel1 = r == i1
    sel2 = r == i2
    onehot = jnp.where(sel1, 1.0, jnp.where(sel2, 1.0, 0.0))
    earlier = (lax.broadcasted_iota(jnp.int32, (tm, tm), 0)
               < lax.broadcasted_iota(jnp.int32, (tm, tm), 1)).astype(BF16)
    local = jnp.dot(onehot.astype(BF16), earlier, preferred_element_type=F32)
    cnt = jnp.sum(onehot, axis=1, keepdims=True)
    wrows = jnp.floor((cnt + (MOE_WINDOW - 1)) * (1.0 / MOE_WINDOW)) * MOE_WINDOW
    below = (lax.broadcasted_iota(jnp.int32, (ROUTE_ROWS, ROUTE_LANES), 1)
             < lax.broadcasted_iota(jnp.int32, (ROUTE_ROWS, ROUTE_LANES), 0)).astype(BF16)
    wrows_rows = jnp.concatenate([jnp.broadcast_to(wrows, (ROUTE_ROWS, ROUTE_LANES)),
                                  jnp.zeros((ROUTE_LANES - ROUTE_ROWS, ROUTE_LANES), F32)], axis=0).astype(BF16)
    soff = jnp.dot(below, wrows_rows, preferred_element_type=F32)[:, 0:1]
    slot = local + soff
    row1 = jnp.sum(jnp.where(sel1, slot, 0.0), axis=0, keepdims=True)
    row2 = jnp.sum(jnp.where(sel2, slot, 0.0), axis=0, keepdims=True)

    k = lax.broadcasted_iota(jnp.int32, (SUBLANES, tm), 0)
    rec = jnp.where(k == 0, (i1 - N_GROUPS).astype(F32),
          jnp.where(k == 1, (i2 - N_GROUPS).astype(F32),
          jnp.where(k == 2, w1,
          jnp.where(k == 3, w2,
          jnp.where(k == 4, row1,
          jnp.where(k == 5, row2, 0.0))))))
    route_t_ref[...] = rec
    route_ref[...] = jnp.concatenate([rec, jnp.zeros((ROUTE_LANES - SUBLANES, tm), F32)], axis=0).T

    lane = lax.broadcasted_iota(jnp.int32, (ROUTE_ROWS, ROUTE_LANES), 1)
    tab_ref[0] = jnp.where(lane == 0, cnt, jnp.where(lane == 1, carry_ref[...], 0.0))
    carry_ref[...] = carry_ref[...] + jnp.floor((cnt + (SUBLANES - 1)) * (1.0 / SUBLANES)) * SUBLANES
    cnt_ref[...] = carry_ref[...]


def _outproj_router(o_a, o_bc, x2, wo, g, wr, br):
    t_tok = x2.shape[0]
    tm = min(TM_ROUTE, t_tok)
    assert t_tok % tm == 0, "token count must be a multiple of the routing tile"
    row = lambda i: (i, 0)
    fixed = lambda i: (0, 0)
    return pl.pallas_call(
        _outproj_router_kernel,
        grid=(t_tok // tm,),
        in_specs=[
            pl.BlockSpec((tm, W_A), row),
            pl.BlockSpec((tm, W_B + C_WIDTH), row),
            pl.BlockSpec((tm, D_MODEL), row),
            pl.BlockSpec(wo.shape, fixed),
            pl.BlockSpec((1, D_MODEL), fixed),
            pl.BlockSpec(wr.shape, fixed),
            pl.BlockSpec((1, ROUTE_LANES), fixed),
        ],
        out_specs=[
            pl.BlockSpec((tm, D_MODEL), row),
            pl.BlockSpec((tm, ROUTE_LANES), row),
            pl.BlockSpec((SUBLANES, tm), row),
            pl.BlockSpec((1, ROUTE_ROWS, ROUTE_LANES), lambda i: (i, 0, 0)),
            pl.BlockSpec((ROUTE_ROWS, ROUTE_LANES), fixed),
        ],
        out_shape=[
            jax.ShapeDtypeStruct((t_tok, D_MODEL), F32),
            jax.ShapeDtypeStruct((t_tok, ROUTE_LANES), F32),
            jax.ShapeDtypeStruct((t_tok // tm * SUBLANES, tm), F32),
            jax.ShapeDtypeStruct((t_tok // tm, ROUTE_ROWS, ROUTE_LANES), F32),
            jax.ShapeDtypeStruct((ROUTE_ROWS, ROUTE_LANES), F32),
        ],
        scratch_shapes=[pltpu.VMEM((ROUTE_ROWS, ROUTE_LANES), F32)],
        compiler_params=_cparams(("arbitrary",)),
        name="outproj_router",
    )(o_a, o_bc, x2, wo, g, wr, br)


def _start_copies(n, make_copy):
    def start(k, carry):
        make_copy(k).start()
        return carry

    lax.fori_loop(0, n, start, 0)


def _wait_copies(n, make_copy):
    def wait(k, carry):
        make_copy(k).wait()
        return carry

    lax.fori_loop(0, n, wait, 0)


def _window_copy_loops(n, make_copy):
    _start_copies(n, make_copy)
    _wait_copies(n, make_copy)


def _start_all_windows(make_copy):
    for k in range(MAX_WINDOWS):
        make_copy(k).start()


def _wait_all_windows(make_copy):
    for k in range(MAX_WINDOWS):
        make_copy(k).wait()


def _dispatch_kernel(pend_ref, tab_ref, tabp_ref, route_ref, x_ref, buf_ref, xp_ref, xq_ref, sem, semq):
    tm = x_ref.shape[0]
    r_st = xp_ref.shape[0]
    step = pl.program_id(0)

    @pl.when(pl.program_id(0) == 0)
    def _():
        xp_ref[...] = jnp.zeros_like(xp_ref)
        copies = []
        for e in range(N_EXPERTS):
            end = pend_ref[e]
            copies.append((None, pltpu.make_async_copy(
                xp_ref.at[0:MOE_BLOCK, :],
                buf_ref.at[pl.ds(pl.multiple_of(end - MOE_BLOCK, SUBLANES), MOE_BLOCK), :], sem)))
            copies.append((end >= MOE_BLOCK + MOE_WINDOW, pltpu.make_async_copy(
                xp_ref.at[0:MOE_WINDOW, :],
                buf_ref.at[pl.ds(pl.multiple_of(jnp.maximum(end - MOE_BLOCK - MOE_WINDOW, 0), SUBLANES), MOE_WINDOW), :],
                sem)))
        for cond, cp in copies:
            if cond is None:
                cp.start()
            else:
                pl.when(cond)(cp.start)
        for cond, cp in copies:
            if cond is None:
                cp.wait()
            else:
                pl.when(cond)(cp.wait)

        def tail_copy(k):
            return pltpu.make_async_copy(
                xp_ref.at[0:MOE_BLOCK, :],
                buf_ref.at[pl.ds(pl.multiple_of(k * MOE_BLOCK, MOE_BLOCK), MOE_BLOCK), :], sem)

        first_unused = pend_ref[N_EXPERTS - 1] // MOE_BLOCK
        n_unused = buf_ref.shape[0] // MOE_BLOCK - first_unused
        _window_copy_loops(n_unused, lambda k: tail_copy(first_unused + k))

    r0 = route_ref[4:5, :].astype(jnp.int32)
    r1 = route_ref[5:6, :].astype(jnp.int32)
    part = r_st // STAGE_PARTS
    srow = lax.broadcasted_iota(jnp.int32, (part, tm), 0)

    def window(tabs, stage_ref, s):
        def make_copy(k):
            return pltpu.make_async_copy(
                stage_ref.at[k * MOE_WINDOW:(k + 1) * MOE_WINDOW, :],
                buf_ref.at[pl.ds(pl.multiple_of(tabs[0, 0, k], SUBLANES), MOE_WINDOW), :], s)
        return make_copy

    def run(cur_ref, cur_sem, prev_ref, prev_sem):
        x16 = x_ref[...].astype(BF16)
        for p in range(STAGE_PARTS):
            sel = jnp.where(srow + p * part == r0, 1.0, jnp.where(srow + p * part == r1, 1.0, 0.0)).astype(BF16)
            cur_ref[p * part:(p + 1) * part, :] = _pack_rows(jnp.dot(sel, x16, preferred_element_type=F32))

        @pl.when(step > 0)
        def _():
            _wait_all_windows(window(tabp_ref, prev_ref, prev_sem))

        _start_all_windows(window(tab_ref, cur_ref, cur_sem))

        @pl.when(step == pl.num_programs(0) - 1)
        def _():
            _wait_all_windows(window(tab_ref, cur_ref, cur_sem))

    @pl.when(step % 2 == 0)
    def _():
        run(xp_ref, sem, xq_ref, semq)

    @pl.when(step % 2 == 1)
    def _():
        run(xq_ref, semq, xp_ref, sem)


def _dispatch(pad_end, tabs, route_t, x2, cap):
    t_tok = x2.shape[0]
    tm = min(TM_ROUTE, t_tok)
    assert t_tok % tm == 0, "token count must be a multiple of the routing tile"
    assert _staging_rows(tm) >= MOE_BLOCK, "the staging buffer doubles as the zero source of a whole block"
    n_tiles = t_tok // tm
    return pl.pallas_call(
        _dispatch_kernel,
        grid=(n_tiles,),
        in_specs=[
            pl.BlockSpec(memory_space=pltpu.SMEM),
            pl.BlockSpec((1, 1, ROUTE_LANES), lambda i: (i, 0, 0), memory_space=pltpu.SMEM),
            pl.BlockSpec((1, 1, ROUTE_LANES), lambda i: (jnp.maximum(i - 1, 0), 0, 0), memory_space=pltpu.SMEM),
            pl.BlockSpec((SUBLANES, tm), lambda i: (i, 0)),
            pl.BlockSpec((tm, D_MODEL), lambda i: (i, 0)),
        ],
        out_specs=pl.BlockSpec(memory_space=pl.ANY),
        out_shape=jax.ShapeDtypeStruct((cap, D_PACKED), jnp.int32),
        scratch_shapes=[pltpu.VMEM((_staging_rows(tm), D_PACKED), jnp.int32),
                        pltpu.VMEM((_staging_rows(tm), D_PACKED), jnp.int32),
                        pltpu.SemaphoreType.DMA(()), pltpu.SemaphoreType.DMA(())],
        compiler_params=_cparams(("arbitrary",)),
        name="moe_dispatch",
    )(pad_end, tabs, tabs, route_t, x2)


def _experts_kernel(blk_e_ref, n_used_ref, x_ref, g_ref, wg_ref, wu_ref, wd_ref, y_ref, wg16, wu16, wd16):
    i = pl.program_id(0)
    used = i < n_used_ref[0]

    @pl.when(used & ((i == 0) | (blk_e_ref[i] != blk_e_ref[jnp.maximum(i - 1, 0)])))
    def _():
        wg16[...] = wg_ref[0, 0].astype(BF16)
        wu16[...] = wu_ref[0, 0].astype(BF16)
        wd16[...] = wd_ref[0, 0].astype(BF16)

    @pl.when(used)
    def _():
        wg = wg16[...]
        wu = wu16[...]
        wd = wd16[...]
        parts = _row_parts(x_ref.shape[0], 4)
        hn = []
        for rows in parts:
            x = _unpack_rows(x_ref[rows, :])
            ms = jnp.mean(x * x, axis=-1, keepdims=True)
            hn.append((x * lax.rsqrt(ms + EPS) * g_ref[...]).astype(BF16))
        a = [jnp.dot(t, wg, preferred_element_type=F32) for t in hn]
        u = [jnp.dot(t, wu, preferred_element_type=F32) for t in hn]
        act = [(a[r] * jax.nn.sigmoid(a[r]) * u[r]).astype(BF16) for r in range(len(parts))]
        for r, rows in enumerate(parts):
            y = jnp.dot(act[r], wd, preferred_element_type=F32)
            y_ref[rows, :] = _pack_rows(y.astype(BF16).astype(F32))


def _experts(blk_e, n_used, buf, g, wg, wu, wd, layer):
    n_blocks = buf.shape[0] // MOE_BLOCK
    grid_spec = pltpu.PrefetchScalarGridSpec(
        num_scalar_prefetch=2,
        grid=(n_blocks,),
        in_specs=[
            pl.BlockSpec((MOE_BLOCK, D_PACKED), lambda i, be, nu: (jnp.minimum(i, nu[0] - 1), 0)),
            pl.BlockSpec((1, D_MODEL), lambda i, be, nu: (0, 0)),
            pl.BlockSpec((1, 1, D_MODEL, D_EXPERT), lambda i, be, nu: (layer, be[i], 0, 0)),
            pl.BlockSpec((1, 1, D_MODEL, D_EXPERT), lambda i, be, nu: (layer, be[i], 0, 0)),
            pl.BlockSpec((1, 1, D_EXPERT, D_MODEL), lambda i, be, nu: (layer, be[i], 0, 0)),
        ],
        out_specs=pl.BlockSpec((MOE_BLOCK, D_PACKED), lambda i, be, nu: (jnp.minimum(i, nu[0] - 1), 0)),
        scratch_shapes=[pltpu.VMEM((D_MODEL, D_EXPERT), BF16), pltpu.VMEM((D_MODEL, D_EXPERT), BF16),
                        pltpu.VMEM((D_EXPERT, D_MODEL), BF16)],
    )
    return pl.pallas_call(
        _experts_kernel,
        grid_spec=grid_spec,
        out_shape=jax.ShapeDtypeStruct(buf.shape, jnp.int32),
        input_output_aliases={2: 0},
        compiler_params=_cparams(("arbitrary",)),
        name="moe_experts",
    )(blk_e, n_used, buf, g, wg, wu, wd)


def _combine_kernel(tab_ref, tabn_ref, route_ref, x_ref, y_ref, o_ref, ya_ref, yb_ref, sema, semb):
    tm = x_ref.shape[0]
    r_st = ya_ref.shape[0]
    step = pl.program_id(0)

    def window(tabs, stage_ref, s):
        def make_copy(k):
            return pltpu.make_async_copy(y_ref.at[pl.ds(pl.multiple_of(tabs[0, 0, k], SUBLANES), MOE_WINDOW), :],
                                         stage_ref.at[k * MOE_WINDOW:(k + 1) * MOE_WINDOW, :], s)
        return make_copy

    @pl.when(step == 0)
    def _():
        ya_ref[...] = jnp.zeros_like(ya_ref)
        yb_ref[...] = jnp.zeros_like(yb_ref)
        _start_all_windows(window(tab_ref, ya_ref, sema))

    route = route_ref[...]
    r0 = route[:, 4:5].astype(jnp.int32)
    r1 = route[:, 5:6].astype(jnp.int32)
    part = r_st // STAGE_PARTS
    col = lax.broadcasted_iota(jnp.int32, (tm, part), 1)

    def run(cur_ref, cur_sem, nxt_ref, nxt_sem):
        @pl.when(step + 1 < pl.num_programs(0))
        def _():
            _start_all_windows(window(tabn_ref, nxt_ref, nxt_sem))

        _wait_all_windows(window(tab_ref, cur_ref, cur_sem))
        acc = x_ref[...]
        for p in range(STAGE_PARTS):
            gate = jnp.where(col + p * part == r0, route[:, 2:3],
                             jnp.where(col + p * part == r1, route[:, 3:4], 0.0)).astype(BF16)
            rows = _unpack_rows(cur_ref[p * part:(p + 1) * part, :]).astype(BF16)
            acc = acc + jnp.dot(gate, rows, preferred_element_type=F32)
        o_ref[...] = acc

    @pl.when(step % 2 == 0)
    def _():
        run(ya_ref, sema, yb_ref, semb)

    @pl.when(step % 2 == 1)
    def _():
        run(yb_ref, semb, ya_ref, sema)


def _combine(tabs, route, x2, y_buf):
    t_tok = x2.shape[0]
    tm = min(TM_ROUTE, t_tok)
    assert t_tok % tm == 0, "token count must be a multiple of the routing tile"
    n_tiles = t_tok // tm
    return pl.pallas_call(
        _combine_kernel,
        grid=(n_tiles,),
        in_specs=[
            pl.BlockSpec((1, 1, ROUTE_LANES), lambda i: (i, 0, 0), memory_space=pltpu.SMEM),
            pl.BlockSpec((1, 1, ROUTE_LANES), lambda i: (jnp.minimum(i + 1, n_tiles - 1), 0, 0),
                         memory_space=pltpu.SMEM),
            pl.BlockSpec((tm, ROUTE_LANES), lambda i: (i, 0)),
            pl.BlockSpec((tm, D_MODEL), lambda i: (i, 0)),
            pl.BlockSpec(memory_space=pl.ANY),
        ],
        out_specs=pl.BlockSpec((tm, D_MODEL), lambda i: (i, 0)),
        out_shape=jax.ShapeDtypeStruct((t_tok, D_MODEL), F32),
        scratch_shapes=[pltpu.VMEM((_staging_rows(tm), D_PACKED), jnp.int32),
                        pltpu.VMEM((_staging_rows(tm), D_PACKED), jnp.int32),
                        pltpu.SemaphoreType.DMA(()), pltpu.SemaphoreType.DMA(())],
        compiler_params=_cparams(("arbitrary",)),
        name="moe_combine",
    )(tabs, tabs, route, x2, y_buf)


def _rope_tables(positions):
    half = ROT_DIM // 2
    inv_freq = ROPE_THETA ** (-jnp.arange(0, ROT_DIM, 2, dtype=F32) / ROT_DIM)
    d = jnp.arange(LANES) % DK_A
    freq_lane = jnp.where(d < ROT_DIM, inv_freq[d % half], 0.0)
    ang = positions.reshape(-1, 1).astype(F32) * freq_lane[None, :]
    cos, sin = jnp.cos(ang), jnp.sin(ang)
    s1 = jnp.where(d < half, -sin, 0.0)
    s2 = jnp.where((d >= half) & (d < ROT_DIM), sin, 0.0)
    return cos, s1, s2


def _block_avg(width, group):
    idx = jnp.arange(width) // group
    return jnp.where(idx[:, None] == idx[None, :], 1.0 / group, 0.0).astype(BF16)


def kernel(x, positions, norm_mix_g, w_in, attn_q_norm_g, attn_k_norm_g, lambda_q1, lambda_k1, lambda_q2,
           lambda_k2, attn_subln_g, hgrn_lower_bounds, hgrn_norm_g, conv_w, w_out, norm_ffn_g, router_group_w,
           router_group_b, router_expert_w, router_expert_b, expert_w_gate, expert_w_up, expert_w_down):
    batch, seq, d = x.shape
    assert d == D_MODEL
    depth = w_in.shape[0]
    t_tok = batch * seq
    n_assign = 2 * t_tok
    n_route_tiles = t_tok // min(TM_ROUTE, t_tok)
    spare_blocks = -(-MAX_WINDOWS * MOE_WINDOW // MOE_BLOCK)
    n_blocks = (n_assign + (SUBLANES - 1) * n_route_tiles * N_EXPERTS) // MOE_BLOCK + 2 * N_EXPERTS + 1 + spare_blocks
    cap = n_blocks * MOE_BLOCK

    lb_probs = jax.nn.softmax(hgrn_lower_bounds.astype(F32), axis=0)
    lower_bounds = jnp.cumsum(lb_probs, axis=0) - lb_probs[0:1]
    cos_t, s1_t, s2_t = _rope_tables(positions)
    gsum_a = _block_avg(W_A, DK_A)
    gavg_b = _block_avg(W_B, DV_B)

    o_q, o_k, o_v = 0, W_A, 2 * W_A
    o_bq = 3 * W_A
    o_bf, o_bi, o_bg = o_bq + W_B, o_bq + 2 * W_B, o_bq + 3 * W_B
    o_c = o_bq + 4 * W_B

    x2 = x.reshape(t_tok, d)
    for l in range(depth):
        w = w_in[l]
        wqk = w[:, o_q:o_v].astype(BF16)
        wv = w[:, o_v:o_bq].astype(BF16)
        whb = jnp.concatenate([w[:, o_bq:o_bf], w[:, o_bi:o_c]], axis=1).astype(BF16)
        wf = w[:, o_bf:o_bi].astype(BF16)
        wc = w[:, o_c:].astype(BF16)
        gqk = jnp.concatenate([jnp.tile(attn_q_norm_g[l], 2 * H_A) * (DK_A ** -0.5 * LOG2_E),
                               jnp.tile(attn_k_norm_g[l], 2 * H_A)]).reshape(1, 2 * W_A).astype(F32)
        qk, v, hb, f, cv = _inproj(x2, norm_mix_g[l].reshape(1, d), wqk, wv, whb, wf, wc, gsum_a, gqk,
                                   cos_t, s1_t, s2_t)

        lam_init = 0.8 - 0.6 * math.exp(-0.3 * l)
        lam = (jnp.exp(jnp.sum(lambda_q1[l].astype(F32) * lambda_k1[l].astype(F32)))
               - jnp.exp(jnp.sum(lambda_q2[l].astype(F32) * lambda_k2[l].astype(F32))) + lam_init).reshape(1)
        o_a = _attention(qk, v, lam, attn_subln_g[l].reshape(DV_A, 1).astype(F32), batch, seq, lam_init)

        o_bc = _hgrn_conv(hb, f, cv, lower_bounds[l].reshape(1, W_B), jnp.tile(hgrn_norm_g[l], H_B).reshape(1, W_B),
                          conv_w[l], gavg_b, batch, seq)

        wr = jnp.concatenate([router_group_w[l], router_expert_w[l],
                              jnp.zeros((d, ROUTE_LANES - N_LOGITS), F32)], axis=1).astype(BF16)
        br = jnp.concatenate([router_group_b[l], router_expert_b[l],
                              jnp.zeros((ROUTE_LANES - N_LOGITS,), F32)]).reshape(1, ROUTE_LANES)
        x2, route, route_t, tab, cnt = _outproj_router(o_a, o_bc, x2, w_out[l].astype(BF16), norm_ffn_g[l].reshape(1, d), wr, br)

        counts = cnt[N_GROUPS:N_GROUPS + N_EXPERTS, 0].astype(jnp.int32)
        padded = (counts + MOE_WINDOW + MOE_BLOCK - 1) // MOE_BLOCK * MOE_BLOCK
        pad_end = jnp.cumsum(padded)
        pad_start = pad_end - padded
        blk_pos = jnp.arange(n_blocks, dtype=jnp.int32) * MOE_BLOCK
        blk_e = jnp.minimum(jnp.sum((pad_end[None, :] <= blk_pos[:, None]).astype(jnp.int32), axis=1),
                            N_EXPERTS - 1)
        n_used = (pad_end[-1] // MOE_BLOCK).astype(jnp.int32).reshape(1)
        tile_cnt = tab[:, N_GROUPS:N_GROUPS + N_EXPERTS, 0].astype(jnp.int32)
        tile_base = tab[:, N_GROUPS:N_GROUPS + N_EXPERTS, 1].astype(jnp.int32) + pad_start[None, :]
        n_win = (tile_cnt + MOE_WINDOW - 1) // MOE_WINDOW
        w_end = jnp.cumsum(n_win, axis=1)
        w_start = w_end - n_win
        k_idx = jnp.arange(MAX_WINDOWS, dtype=jnp.int32)[None, :, None]
        owner = (w_start[:, None, :] <= k_idx) & (k_idx < w_end[:, None, :])
        win_dst = jnp.sum(jnp.where(owner, tile_base[:, None, :] + (k_idx - w_start[:, None, :]) * MOE_WINDOW, 0),
                          axis=-1)
        spare = cap - MAX_WINDOWS * MOE_WINDOW + k_idx[:, :, 0] * MOE_WINDOW
        win_dst = jnp.where(jnp.any(owner, axis=-1), win_dst, spare)
        tabs = jnp.concatenate([win_dst, w_end[:, -1:],
                                jnp.zeros((win_dst.shape[0], ROUTE_LANES - MAX_WINDOWS - 1), jnp.int32)],
                               axis=1).reshape(-1, 1, ROUTE_LANES)

        buf = _dispatch(pad_end, tabs, route_t, x2, cap)
        y_buf = _experts(blk_e, n_used, buf, norm_ffn_g[l].reshape(1, d), expert_w_gate, expert_w_up,
                         expert_w_down, l)
        x2 = _combine(tabs, route, x2, y_buf)
    return x2.reshape(batch, seq, d)
```

```python
import functools
import math

import jax
import jax.numpy as jnp
from jax import lax
from jax.experimental import pallas as pl
from jax.experimental.pallas import tpu as pltpu

F32 = jnp.float32
BF16 = jnp.bfloat16

LANES = 128
SUBLANES = 8

D_MODEL = 1024
CHUNK = 64
EPS = 1e-6
MASK_VALUE = -1e30
TINY = 1e-30
LOG2_E = math.log2(math.e)
H_A = 4
DK_A = D_MODEL // 16
DV_A = 2 * DK_A
ROT_DIM = DK_A // 4
ROPE_THETA = 500000.0
H_B = 4
DK_B = D_MODEL // 16
DV_B = D_MODEL // 16
C_WIDTH = D_MODEL // 4
CONV_WIDTH = 3
N_GROUPS = 4
EXPERTS_PER_GROUP = 8
N_EXPERTS = N_GROUPS * EXPERTS_PER_GROUP
D_EXPERT = D_MODEL // 2
D_PACKED = D_MODEL // 2

W_A = H_A * DV_A
W_B = H_B * DK_B
ROUTE_LANES = LANES
N_LOGITS = N_GROUPS + N_EXPERTS
ROUTE_ROWS = -(-N_LOGITS // SUBLANES) * SUBLANES

TM_PROJ = 1024
TK = 256
KV_PER_Q = 2
TQ = KV_PER_Q * TK
TM_ROUTE = 512
MOE_BLOCK = 1024
HGRN_BATCH = 2
HGRN_UNROLL = 2
MOE_WINDOW = 8
STAGE_PARTS = 5
MAX_WINDOWS = 2 * TM_ROUTE // MOE_WINDOW + N_EXPERTS
TAB_LANES = -(-(MAX_WINDOWS + 1) // LANES) * LANES
VMEM_LIMIT = 56 * 1024 * 1024


def _staging_rows(tm):
    return 2 * tm + N_EXPERTS * MOE_WINDOW


def _row_parts(rows, n=2):
    step = rows // n
    return [slice(r * step, (r + 1) * step) for r in range(n)]


def _pack_rows(x):
    half = x.shape[1] // 2
    lo = lax.bitcast_convert_type(x[:, :half], jnp.int32)
    hi = lax.bitcast_convert_type(x[:, half:], jnp.int32)
    return hi | lax.shift_right_logical(lo, 16)


def _unpack_rows(p):
    lo = lax.bitcast_convert_type(lax.shift_left(p, 16), F32)
    hi = lax.bitcast_convert_type(p & jnp.int32(-65536), F32)
    return jnp.concatenate([lo, hi], axis=1)


def _cparams(sem, vmem=VMEM_LIMIT):
    return pltpu.CompilerParams(dimension_semantics=sem, vmem_limit_bytes=vmem)


def _inproj_kernel(x_ref, g_ref, wqk_ref, wv_ref, whb_ref, wf_ref, wc_ref, gsum_ref, gqk_ref,
                   cos_ref, s1_ref, s2_ref, qk_out, v_out, hb_out, f_out, c_out):
    parts = _row_parts(x_ref.shape[0])
    xn = []
    for rows in parts:
        x = x_ref[rows, :]
        ms = jnp.mean(x * x, axis=-1, keepdims=True)
        xn.append((x * lax.rsqrt(ms + EPS) * g_ref[...]).astype(BF16))

    gsum = gsum_ref[...]
    for part in range(2):
        lo = part * W_A
        t = [jnp.dot(xr, wqk_ref[:, lo:lo + W_A], preferred_element_type=F32) for xr in xn]
        msq = [jnp.dot((tr * tr).astype(BF16), gsum, preferred_element_type=F32) for tr in t]
        for r, rows in enumerate(parts):
            tn = t[r] * lax.rsqrt(msq[r] + EPS) * gqk_ref[:, lo:lo + W_A]
            cos = cos_ref[rows, :]
            s1 = s1_ref[rows, :]
            s2 = s2_ref[rows, :]
            for h in range(H_A):
                slab = tn[:, h * LANES:(h + 1) * LANES]
                up = pltpu.roll(slab, LANES - ROT_DIM // 2, axis=1)
                dn = pltpu.roll(slab, ROT_DIM // 2, axis=1)
                qk_out[rows, lo + h * LANES:lo + (h + 1) * LANES] = (slab * cos + up * s1 + dn * s2).astype(BF16)

    for r, rows in enumerate(parts):
        v_out[rows, :] = jnp.dot(xn[r], wv_ref[...], preferred_element_type=F32).astype(BF16)
    for r, rows in enumerate(parts):
        hb_out[rows, :] = jnp.dot(xn[r], whb_ref[...], preferred_element_type=F32).astype(BF16)
    for r, rows in enumerate(parts):
        f_out[rows, :] = jnp.dot(xn[r], wf_ref[...], preferred_element_type=F32)
    for r, rows in enumerate(parts):
        c_out[rows, :] = jnp.dot(xn[r], wc_ref[...], preferred_element_type=F32).astype(BF16)


def _inproj(x2, g, wqk, wv, whb, wf, wc, gsum, gqk, cos_t, s1_t, s2_t):
    t_tok = x2.shape[0]
    tm = min(TM_PROJ, t_tok)
    assert t_tok % tm == 0, "token count must be a multiple of the projection tile"
    grid = (t_tok // tm,)
    row = lambda i: (i, 0)
    fixed = lambda i: (0, 0)
    return pl.pallas_call(
        _inproj_kernel,
        grid=grid,
        in_specs=[
            pl.BlockSpec((tm, D_MODEL), row),
            pl.BlockSpec((1, D_MODEL), fixed),
            pl.BlockSpec(wqk.shape, fixed),
            pl.BlockSpec(wv.shape, fixed),
            pl.BlockSpec(whb.shape, fixed),
            pl.BlockSpec(wf.shape, fixed),
            pl.BlockSpec(wc.shape, fixed),
            pl.BlockSpec(gsum.shape, fixed),
            pl.BlockSpec(gqk.shape, fixed),
            pl.BlockSpec((tm, LANES), row),
            pl.BlockSpec((tm, LANES), row),
            pl.BlockSpec((tm, LANES), row),
        ],
        out_specs=[
            pl.BlockSpec((tm, 2 * W_A), row),
            pl.BlockSpec((tm, W_A), row),
            pl.BlockSpec((tm, 3 * W_B), row),
            pl.BlockSpec((tm, W_B), row),
            pl.BlockSpec((tm, 3 * C_WIDTH), row),
        ],
        out_shape=[
            jax.ShapeDtypeStruct((t_tok, 2 * W_A), BF16),
            jax.ShapeDtypeStruct((t_tok, W_A), BF16),
            jax.ShapeDtypeStruct((t_tok, 3 * W_B), BF16),
            jax.ShapeDtypeStruct((t_tok, W_B), F32),
            jax.ShapeDtypeStruct((t_tok, 3 * C_WIDTH), BF16),
        ],
        compiler_params=_cparams(("parallel",)),
        name="inproj",
    )(x2, g, wqk, wv, whb, wf, wc, gsum, gqk, cos_t, s1_t, s2_t)


def _attn_kernel(lam_ref, q_ref, k_ref, v_ref, g_ref, o_ref, qc_ref, sa_ref, sb_ref, acc_ref, m_ref, l_ref, *,
                 lam_init):
    qi = pl.program_id(1)
    q = q_ref[...]
    lane = lax.broadcasted_iota(jnp.int32, (TQ, DV_A), 1)
    zero = jnp.zeros((TQ, DV_A), BF16)
    for h in range(H_A):
        qh = q[:, h * DV_A:(h + 1) * DV_A]
        qc_ref[h, 0] = jnp.where(lane < DK_A, qh, zero)
        qc_ref[h, 1] = jnp.where(lane >= DK_A, qh, zero)

    acc_ref[...] = jnp.zeros_like(acc_ref)
    m_ref[...] = jnp.full_like(m_ref, -jnp.inf)
    l_ref[...] = jnp.zeros_like(l_ref)

    kchunk = lax.broadcasted_iota(jnp.int32, (TK, TQ), 0) // CHUNK
    qchunk = lax.broadcasted_iota(jnp.int32, (TK, TQ), 1) // CHUNK

    def scores(j, s_ref, q0=0):
        start = pl.multiple_of(j * TK, TK)
        kb = k_ref[pl.ds(start, TK), :]
        for h in range(H_A):
            kh = kb[:, h * DV_A:(h + 1) * DV_A]
            for c in range(2):
                s_ref[h, c, :, q0:] = lax.dot_general(kh, qc_ref[h, c, q0:, :], (((1,), (1,)), ((), ())),
                                                      preferred_element_type=F32)

    def consume(j, s_ref, mask, q0=0):
        start = pl.multiple_of(j * TK, TK)
        vb = v_ref[pl.ds(start, TK), :]
        for h in range(H_A):
            vh = vb[:, h * DV_A:(h + 1) * DV_A]
            for c in range(2):
                s = s_ref[h, c, :, q0:]
                if mask is not None:
                    s = jnp.where(mask[:, q0:], s, MASK_VALUE)
                m_old = m_ref[h, c, :, q0:]
                m_new = jnp.maximum(m_old, jnp.max(s, axis=0, keepdims=True))
                alpha = jnp.exp2(m_old - m_new)
                p = jnp.exp2(s - m_new)
                l_ref[h, c, :, q0:] = alpha * l_ref[h, c, :, q0:] + jnp.sum(p, axis=0, keepdims=True)
                pv = lax.dot_general(vh, p.astype(BF16), (((0,), (0,)), ((), ())),
                                     preferred_element_type=F32)
                acc_ref[h, c, :, q0:] = alpha * acc_ref[h, c, :, q0:] + pv
                m_ref[h, c, :, q0:] = m_new

    first_diag = KV_PER_Q * qi
    scores(0, sa_ref)

    def body(jj, carry):
        j = 2 * jj
        scores(j + 1, sb_ref)
        consume(j, sa_ref, None)
        scores(j + 2, sa_ref)
        consume(j + 1, sb_ref, None)
        return carry

    lax.fori_loop(0, first_diag // 2, body, 0)
    for d in range(KV_PER_Q):
        cur, nxt = (sa_ref, sb_ref) if d % 2 == 0 else (sb_ref, sa_ref)
        if d + 1 < KV_PER_Q:
            scores(first_diag + d + 1, nxt, (d + 1) * TK)
        consume(first_diag + d, cur, (kchunk + d * (TK // CHUNK)) <= qchunk, d * TK)

    lam = lam_ref[0]
    for h in range(H_A):
        o = acc_ref[h, 0] / l_ref[h, 0] - lam * (acc_ref[h, 1] / l_ref[h, 1])
        ms = jnp.mean(o * o, axis=0, keepdims=True)
        y = o * lax.rsqrt(ms + EPS) * g_ref[...] * (1.0 - lam_init)
        o_ref[:, h * DV_A:(h + 1) * DV_A] = y.T.astype(BF16)


def _attention(qk, v, lam, subln_col, batch, seq, lam_init):
    assert TQ == KV_PER_Q * TK and KV_PER_Q % 2 == 0 and TK % CHUNK == 0 and seq % TQ == 0
    nq = seq // TQ
    t_tok = batch * seq
    kernel = functools.partial(_attn_kernel, lam_init=lam_init)
    return pl.pallas_call(
        kernel,
        grid=(batch, nq),
        in_specs=[
            pl.BlockSpec(memory_space=pltpu.SMEM),
            pl.BlockSpec((TQ, W_A), lambda b, i: (b * nq + i, 0)),
            pl.BlockSpec((seq, W_A), lambda b, i: (b, 1)),
            pl.BlockSpec((seq, W_A), lambda b, i: (b, 0)),
            pl.BlockSpec((DV_A, 1), lambda b, i: (0, 0)),
        ],
        out_specs=pl.BlockSpec((TQ, W_A), lambda b, i: (b * nq + i, 0)),
        out_shape=jax.ShapeDtypeStruct((t_tok, W_A), BF16),
        scratch_shapes=[
            pltpu.VMEM((H_A, 2, TQ, DV_A), BF16),
            pltpu.VMEM((H_A, 2, TK, TQ), F32),
            pltpu.VMEM((H_A, 2, TK, TQ), F32),
            pltpu.VMEM((H_A, 2, DV_A, TQ), F32),
            pltpu.VMEM((H_A, 2, 1, TQ), F32),
            pltpu.VMEM((H_A, 2, 1, TQ), F32),
        ],
        compiler_params=_cparams(("parallel", "arbitrary")),
        name="diff_attn",
    )(lam, qk, qk, v, subln_col)


_LEVELS = (32, 16, 8, 4, 2, 1)


def _ref_rows(g, h):
    n = g.shape[1]
    if 2 * h >= SUBLANES:
        pieces = []
        for m in range(CHUNK // (2 * h)):
            r = 2 * h * m + h - 1
            pieces.append(jnp.broadcast_to(g[r:r + 1, :], (2 * h, n)))
        return pieces[0] if len(pieces) == 1 else jnp.concatenate(pieces, axis=0)
    g3 = g.reshape(CHUNK // SUBLANES, SUBLANES, n)
    sub = lax.broadcasted_iota(jnp.int32, g3.shape, 1)
    out = None
    for m in reversed(range(SUBLANES // (2 * h))):
        r = 2 * h * m + h - 1
        piece = jnp.broadcast_to(g3[:, r:r + 1, :], g3.shape)
        out = piece if out is None else jnp.where(sub < 2 * h * (m + 1), piece, out)
    return out.reshape(CHUNK, n)


def _stack_heads(a, lane_head):
    zero = jnp.zeros_like(a)
    return jnp.concatenate([jnp.where(lane_head == hh, a, zero) for hh in range(H_B)], axis=0)


def _hgrn_kernel(hb_ref, f_ref, c_ref, lb_ref, ng_ref, cw_ref, gavg_ref, o_ref, state_ref, hbuf_ref, *, seq):
    n_chunks = seq // CHUNK
    nb = hb_ref.shape[0] // seq
    lb = lb_ref[...]
    ng = ng_ref[...]
    gavg = gavg_ref[...]

    row = lax.broadcasted_iota(jnp.int32, (CHUNK, W_B), 0)
    lane = lax.broadcasted_iota(jnp.int32, (CHUNK, W_B), 1)
    lane_head = lane // DK_B
    lane_s = lane % CHUNK
    tri = (lax.broadcasted_iota(jnp.int32, (CHUNK, CHUNK), 1)
           <= lax.broadcasted_iota(jnp.int32, (CHUNK, CHUNK), 0)).astype(BF16)
    lvl_mask = [((row // (2 * h)) == (lane_s // (2 * h))) & ((row % (2 * h)) >= h) & ((lane_s % (2 * h)) < h)
                for h in _LEVELS]
    eye_mask = row == lane_s
    r4 = lax.broadcasted_iota(jnp.int32, (H_B * DK_B, W_B), 0) // DK_B
    c4 = lax.broadcasted_iota(jnp.int32, (H_B * DK_B, W_B), 1) // DK_B
    bd_mask = r4 == c4

    state_ref[...] = jnp.zeros_like(state_ref)
    nt = (((1,), (1,)), ((), ()))
    tn = (((0,), (0,)), ((), ()))
    unroll = HGRN_UNROLL if n_chunks % HGRN_UNROLL == 0 else 1
    each = range(nb * unroll)

    def chunk(i, carry):
        rows = [pl.ds(pl.multiple_of((e % nb) * seq + (i * unroll + e // nb) * CHUNK, CHUNK), CHUNK) for e in each]
        hb = [hb_ref[rows[j], :].astype(F32) for j in each]
        zq = [t[:, 0:W_B] for t in hb]
        v16 = [t[:, W_B:2 * W_B].astype(BF16) for t in hb]
        zg = [t[:, 2 * W_B:3 * W_B] for t in hb]
        zf = [f_ref[rows[j], :] for j in each]

        logf = [jnp.log(jnp.maximum(lb + (1.0 - lb) * jax.nn.sigmoid(z), TINY)) for z in zf]
        key = [(1.0 - lb) * jax.nn.sigmoid(-z) for z in zf]
        q = [z * jax.nn.sigmoid(z) for z in zq]

        g = []
        for x in logf:
            hi = x.astype(BF16)
            r1 = x - hi.astype(F32)
            mid = r1.astype(BF16)
            lo = (r1 - mid.astype(F32)).astype(BF16)
            g.append((jnp.dot(tri, hi, preferred_element_type=F32) + jnp.dot(tri, mid, preferred_element_type=F32)
                      + jnp.dot(tri, lo, preferred_element_type=F32)) * LOG2_E)

        q16 = [t.astype(BF16) for t in q]
        k16 = [t.astype(BF16) for t in key]
        a_all = [jnp.where(eye_mask, lax.dot_general(q16[j], _stack_heads(k16[j], lane_head),
                                                     nt, preferred_element_type=F32), 0.0) for j in each]
        for h, mask in zip(_LEVELS, lvl_mask):
            w = [jnp.exp2(-jnp.abs(g[j] - _ref_rows(g[j], h))).astype(BF16) for j in each]
            ah = [lax.dot_general(q16[j] * w[j], _stack_heads(k16[j] * w[j], lane_head),
                                  nt, preferred_element_type=F32) for j in each]
            a_all = [a_all[j] + jnp.where(mask, ah[j], 0.0) for j in each]
        o = [jnp.dot(a_all[j].astype(BF16), _stack_heads(v16[j], lane_head), preferred_element_type=F32) for j in each]

        qg = [(q[j] * jnp.exp2(g[j])).astype(BF16) for j in each]
        g_last = [t[CHUNK - 1:CHUNK, :] for t in g]
        upd = [lax.dot_general(v16[j], (key[j] * jnp.exp2(g_last[j] - g[j])).astype(BF16), tn,
                               preferred_element_type=F32) for j in each]
        for j in each:
            st = state_ref[j % nb]
            o[j] = o[j] + lax.dot_general(qg[j], st.astype(BF16), nt, preferred_element_type=F32)
            state_ref[j % nb] = jnp.exp2(g_last[j]) * st + jnp.where(bd_mask, upd[j], 0.0)

        ms = [jnp.dot((t * t).astype(BF16), gavg, preferred_element_type=F32) for t in o]
        for j in each:
            y = o[j] * lax.rsqrt(ms[j] + EPS) * ng * (zg[j] * jax.nn.sigmoid(zg[j]))
            o_ref[rows[j], 0:W_B] = y.astype(BF16)
        return carry

    lax.fori_loop(0, n_chunks // unroll, chunk, 0)

    piece = min(512, seq)
    hbuf_ref[0:SUBLANES, :] = jnp.zeros((SUBLANES, C_WIDTH), F32)
    for j in range(nb):
        for p in range(seq // piece):
            cblk = c_ref[j * seq + p * piece:j * seq + (p + 1) * piece, :].astype(F32)
            hbuf_ref[SUBLANES + p * piece:SUBLANES + (p + 1) * piece, :] = (cblk[:, C_WIDTH:2 * C_WIDTH]
                                                                             * cblk[:, 2 * C_WIDTH:])
        for p in range(seq // piece):
            base = SUBLANES + p * piece
            y = (cw_ref[0:1, :] * hbuf_ref[base - 2:base - 2 + piece, :]
                 + cw_ref[1:2, :] * hbuf_ref[base - 1:base - 1 + piece, :]
                 + cw_ref[2:3, :] * hbuf_ref[base:base + piece, :])
            lo_r = j * seq + p * piece
            bgate = c_ref[lo_r:lo_r + piece, 0:C_WIDTH].astype(F32)
            o_ref[lo_r:lo_r + piece, W_B:W_B + C_WIDTH] = (bgate * y).astype(BF16)


def _hgrn_conv(hb, f, cv, lb, ng, cw, gavg, batch, seq):
    t_tok = batch * seq
    nb = HGRN_BATCH if batch % HGRN_BATCH == 0 else 1
    fixed = lambda b: (0, 0)
    return pl.pallas_call(
        functools.partial(_hgrn_kernel, seq=seq),
        grid=(batch // nb,),
        in_specs=[
            pl.BlockSpec((nb * seq, 3 * W_B), lambda b: (b, 0)),
            pl.BlockSpec((nb * seq, W_B), lambda b: (b, 0)),
            pl.BlockSpec((nb * seq, 3 * C_WIDTH), lambda b: (b, 0)),
            pl.BlockSpec((1, W_B), fixed),
            pl.BlockSpec((1, W_B), fixed),
            pl.BlockSpec((CONV_WIDTH, C_WIDTH), fixed),
            pl.BlockSpec((W_B, W_B), fixed),
        ],
        out_specs=pl.BlockSpec((nb * seq, W_B + C_WIDTH), lambda b: (b, 0)),
        out_shape=jax.ShapeDtypeStruct((t_tok, W_B + C_WIDTH), BF16),
        scratch_shapes=[
            pltpu.VMEM((nb, H_B * DV_B, W_B), F32),
            pltpu.VMEM((seq + SUBLANES, C_WIDTH), F32),
        ],
        compiler_params=_cparams(("parallel",)),
        name="hgrn_conv",
    )(hb, f, cv, lb, ng, cw, gavg)


def _outproj_router_kernel(oa_ref, obc_ref, x_ref, wo_ref, g_ref, wr_ref, br_ref,
                           xo_ref, route_ref, route_t_ref, tab_ref, cnt_ref, carry_ref):
    i = pl.program_id(0)

    @pl.when(i == 0)
    def _():
        carry_ref[...] = jnp.zeros_like(carry_ref)

    mixed = (jnp.dot(oa_ref[...], wo_ref[0:W_A, :], preferred_element_type=F32)
             + jnp.dot(obc_ref[...], wo_ref[W_A:, :], preferred_element_type=F32))
    xn = x_ref[...] + mixed
    xo_ref[...] = xn

    ms = jnp.mean(xn * xn, axis=-1, keepdims=True)
    hn = (xn * lax.rsqrt(ms + EPS) * g_ref[...]).astype(BF16)
    logits = jnp.dot(hn, wr_ref[...], preferred_element_type=F32) + br_ref[...]

    tm = logits.shape[0]
    lt = logits.T[0:ROUTE_ROWS, :]
    r = lax.broadcasted_iota(jnp.int32, (ROUTE_ROWS, tm), 0)
    neg = -jnp.inf
    big = ROUTE_ROWS

    def top(cand):
        val = jnp.max(cand, axis=0, keepdims=True)
        idx = jnp.min(jnp.where(cand == val, r, big), axis=0, keepdims=True)
        return val, idx

    glog = jnp.where(r < N_GROUPS, lt, neg)
    gmax, gidx = top(glog)
    p_g = 1.0 / jnp.sum(jnp.exp(glog - gmax), axis=0, keepdims=True)
    off = r - (N_GROUPS + EXPERTS_PER_GROUP * gidx)
    elog = jnp.where(jnp.abs(2 * off - (EXPERTS_PER_GROUP - 1)) < EXPERTS_PER_GROUP, lt, neg)
    v1, i1 = top(elog)
    v2, i2 = top(jnp.where(r == i1, neg, elog))
    t2 = jnp.exp(v2 - v1)
    w1 = p_g / (1.0 + t2)
    w2 = p_g * t2 / (1.0 + t2)

    sel1 = r == i1
    sel2 = r == i2
    onehot = jnp.where(sel1, 1.0, jnp.where(sel2, 1.0, 0.0))
    earlier = (lax.broadcasted_iota(jnp.int32, (tm, tm), 0)
               < lax.broadcasted_iota(jnp.int32, (tm, tm), 1)).astype(BF16)
    local = jnp.dot(onehot.astype(BF16), earlier, preferred_element_type=F32)
    cnt = jnp.sum(onehot, axis=1, keepdims=True)
    wrows = jnp.floor((cnt + (MOE_WINDOW - 1)) * (1.0 / MOE_WINDOW)) * MOE_WINDOW
    below = (lax.broadcasted_iota(jnp.int32, (ROUTE_ROWS, ROUTE_LANES), 1)
             < lax.broadcasted_iota(jnp.int32, (ROUTE_ROWS, ROUTE_LANES), 0)).astype(BF16)
    wrows_rows = jnp.concatenate([jnp.broadcast_to(wrows, (ROUTE_ROWS, ROUTE_LANES)),
                                  jnp.zeros((ROUTE_LANES - ROUTE_ROWS, ROUTE_LANES), F32)], axis=0).astype(BF16)
    soff = jnp.dot(below, wrows_rows, preferred_element_type=F32)[:, 0:1]
    slot = local + soff
    row1 = jnp.sum(jnp.where(sel1, slot, 0.0), axis=0, keepdims=True)
    row2 = jnp.sum(jnp.where(sel2, slot, 0.0), axis=0, keepdims=True)

    k = lax.broadcasted_iota(jnp.int32, (SUBLANES, tm), 0)
    rec = jnp.where(k == 0, (i1 - N_GROUPS).astype(F32),
          jnp.where(k == 1, (i2 - N_GROUPS).astype(F32),
          jnp.where(k == 2, w1,
          jnp.where(k == 3, w2,
          jnp.where(k == 4, row1,
          jnp.where(k == 5, row2, 0.0))))))
    route_t_ref[...] = rec
    route_ref[...] = jnp.concatenate([rec, jnp.zeros((ROUTE_LANES - SUBLANES, tm), F32)], axis=0).T

    lane = lax.broadcasted_iota(jnp.int32, (ROUTE_ROWS, ROUTE_LANES), 1)
    tab_ref[0] = jnp.where(lane == 0, cnt, jnp.where(lane == 1, carry_ref[...], 0.0))
    carry_ref[...] = carry_ref[...] + jnp.floor((cnt + (SUBLANES - 1)) * (1.0 / SUBLANES)) * SUBLANES
    cnt_ref[...] = carry_ref[...]


def _outproj_router(o_a, o_bc, x2, wo, g, wr, br):
    t_tok = x2.shape[0]
    tm = min(TM_ROUTE, t_tok)
    assert t_tok % tm == 0, "token count must be a multiple of the routing tile"
    row = lambda i: (i, 0)
    fixed = lambda i: (0, 0)
    return pl.pallas_call(
        _outproj_router_kernel,
        grid=(t_tok // tm,),
        in_specs=[
            pl.BlockSpec((tm, W_A), row),
            pl.BlockSpec((tm, W_B + C_WIDTH), row),
            pl.BlockSpec((tm, D_MODEL), row),
            pl.BlockSpec(wo.shape, fixed),
            pl.BlockSpec((1, D_MODEL), fixed),
            pl.BlockSpec(wr.shape, fixed),
            pl.BlockSpec((1, ROUTE_LANES), fixed),
        ],
        out_specs=[
            pl.BlockSpec((tm, D_MODEL), row),
            pl.BlockSpec((tm, ROUTE_LANES), row),
            pl.BlockSpec((SUBLANES, tm), row),
            pl.BlockSpec((1, ROUTE_ROWS, ROUTE_LANES), lambda i: (i, 0, 0)),
            pl.BlockSpec((ROUTE_ROWS, ROUTE_LANES), fixed),
        ],
        out_shape=[
            jax.ShapeDtypeStruct((t_tok, D_MODEL), F32),
            jax.ShapeDtypeStruct((t_tok, ROUTE_LANES), F32),
            jax.ShapeDtypeStruct((t_tok // tm * SUBLANES, tm), F32),
            jax.ShapeDtypeStruct((t_tok // tm, ROUTE_ROWS, ROUTE_LANES), F32),
            jax.ShapeDtypeStruct((ROUTE_ROWS, ROUTE_LANES), F32),
        ],
        scratch_shapes=[pltpu.VMEM((ROUTE_ROWS, ROUTE_LANES), F32)],
        compiler_params=_cparams(("arbitrary",)),
        name="outproj_router",
    )(o_a, o_bc, x2, wo, g, wr, br)


def _start_copies(n, make_copy):
    def start(k, carry):
        make_copy(k).start()
        return carry

    lax.fori_loop(0, n, start, 0)


def _wait_copies(n, make_copy):
    def wait(k, carry):
        make_copy(k).wait()
        return carry

    lax.fori_loop(0, n, wait, 0)


def _window_copy_loops(n, make_copy):
    _start_copies(n, make_copy)
    _wait_copies(n, make_copy)


def _start_all_windows(make_copy):
    for k in range(MAX_WINDOWS):
        make_copy(k).start()


def _wait_all_windows(make_copy):
    for k in range(MAX_WINDOWS):
        make_copy(k).wait()


def _dispatch_kernel(pend_ref, tab_ref, tabp_ref, route_ref, x_ref, buf_ref, xp_ref, xq_ref, sem, semq):
    tm = x_ref.shape[0]
    r_st = xp_ref.shape[0]
    step = pl.program_id(0)

    @pl.when(pl.program_id(0) == 0)
    def _():
        xp_ref[...] = jnp.zeros_like(xp_ref)
        copies = []
        for e in range(N_EXPERTS):
            end = pend_ref[e]
            copies.append((None, pltpu.make_async_copy(
                xp_ref.at[0:MOE_BLOCK, :],
                buf_ref.at[pl.ds(pl.multiple_of(end - MOE_BLOCK, SUBLANES), MOE_BLOCK), :], sem)))
            copies.append((end >= MOE_BLOCK + MOE_WINDOW, pltpu.make_async_copy(
                xp_ref.at[0:MOE_WINDOW, :],
                buf_ref.at[pl.ds(pl.multiple_of(jnp.maximum(end - MOE_BLOCK - MOE_WINDOW, 0), SUBLANES), MOE_WINDOW), :],
                sem)))
        for cond, cp in copies:
            if cond is None:
                cp.start()
            else:
                pl.when(cond)(cp.start)
        for cond, cp in copies:
            if cond is None:
                cp.wait()
            else:
                pl.when(cond)(cp.wait)

        def tail_copy(k):
            return pltpu.make_async_copy(
                xp_ref.at[0:MOE_BLOCK, :],
                buf_ref.at[pl.ds(pl.multiple_of(k * MOE_BLOCK, MOE_BLOCK), MOE_BLOCK), :], sem)

        first_unused = pend_ref[N_EXPERTS - 1] // MOE_BLOCK
        n_unused = buf_ref.shape[0] // MOE_BLOCK - first_unused
        _window_copy_loops(n_unused, lambda k: tail_copy(first_unused + k))

    r0 = route_ref[4:5, :].astype(jnp.int32)
    r1 = route_ref[5:6, :].astype(jnp.int32)
    part = r_st // STAGE_PARTS
    srow = lax.broadcasted_iota(jnp.int32, (part, tm), 0)

    def window(tabs, stage_ref, s):
        def make_copy(k):
            return pltpu.make_async_copy(
                stage_ref.at[k * MOE_WINDOW:(k + 1) * MOE_WINDOW, :],
                buf_ref.at[pl.ds(pl.multiple_of(tabs[0, 0, k], SUBLANES), MOE_WINDOW), :], s)
        return make_copy

    def run(cur_ref, cur_sem, prev_ref, prev_sem):
        x16 = x_ref[...].astype(BF16)
        for p in range(STAGE_PARTS):
            sel = jnp.where(srow + p * part == r0, 1.0, jnp.where(srow + p * part == r1, 1.0, 0.0)).astype(BF16)
            cur_ref[p * part:(p + 1) * part, :] = _pack_rows(jnp.dot(sel, x16, preferred_element_type=F32))

        @pl.when(step > 0)
        def _():
            _wait_all_windows(window(tabp_ref, prev_ref, prev_sem))

        _start_all_windows(window(tab_ref, cur_ref, cur_sem))

        @pl.when(step == pl.num_programs(0) - 1)
        def _():
            _wait_all_windows(window(tab_ref, cur_ref, cur_sem))

    @pl.when(step % 2 == 0)
    def _():
        run(xp_ref, sem, xq_ref, semq)

    @pl.when(step % 2 == 1)
    def _():
        run(xq_ref, semq, xp_ref, sem)


def _dispatch(pad_end, tabs, route_t, x2, cap):
    t_tok = x2.shape[0]
    tm = min(TM_ROUTE, t_tok)
    assert t_tok % tm == 0, "token count must be a multiple of the routing tile"
    assert _staging_rows(tm) >= MOE_BLOCK, "the staging buffer doubles as the zero source of a whole block"
    n_tiles = t_tok // tm
    return pl.pallas_call(
        _dispatch_kernel,
        grid=(n_tiles,),
        in_specs=[
            pl.BlockSpec(memory_space=pltpu.SMEM),
            pl.BlockSpec((1, 1, TAB_LANES), lambda i: (i, 0, 0), memory_space=pltpu.SMEM),
            pl.BlockSpec((1, 1, TAB_LANES), lambda i: (jnp.maximum(i - 1, 0), 0, 0), memory_space=pltpu.SMEM),
            pl.BlockSpec((SUBLANES, tm), lambda i: (i, 0)),
            pl.BlockSpec((tm, D_MODEL), lambda i: (i, 0)),
        ],
        out_specs=pl.BlockSpec(memory_space=pl.ANY),
        out_shape=jax.ShapeDtypeStruct((cap, D_PACKED), jnp.int32),
        scratch_shapes=[pltpu.VMEM((_staging_rows(tm), D_PACKED), jnp.int32),
                        pltpu.VMEM((_staging_rows(tm), D_PACKED), jnp.int32),
                        pltpu.SemaphoreType.DMA(()), pltpu.SemaphoreType.DMA(())],
        compiler_params=_cparams(("arbitrary",)),
        name="moe_dispatch",
    )(pad_end, tabs, tabs, route_t, x2)


def _experts_kernel(blk_e_ref, n_used_ref, x_ref, g_ref, wg_ref, wu_ref, wd_ref, y_ref, wg16, wu16, wd16):
    i = pl.program_id(0)
    used = i < n_used_ref[0]

    @pl.when(used & ((i == 0) | (blk_e_ref[i] != blk_e_ref[jnp.maximum(i - 1, 0)])))
    def _():
        wg16[...] = wg_ref[0, 0].astype(BF16)
        wu16[...] = wu_ref[0, 0].astype(BF16)
        wd16[...] = wd_ref[0, 0].astype(BF16)

    @pl.when(used)
    def _():
        wg = wg16[...]
        wu = wu16[...]
        wd = wd16[...]
        parts = _row_parts(x_ref.shape[0], 4)
        hn = []
        for rows in parts:
            x = _unpack_rows(x_ref[rows, :])
            ms = jnp.mean(x * x, axis=-1, keepdims=True)
            hn.append((x * lax.rsqrt(ms + EPS) * g_ref[...]).astype(BF16))
        a = [jnp.dot(t, wg, preferred_element_type=F32) for t in hn]
        u = [jnp.dot(t, wu, preferred_element_type=F32) for t in hn]
        act = [(a[r] * jax.nn.sigmoid(a[r]) * u[r]).astype(BF16) for r in range(len(parts))]
        for r, rows in enumerate(parts):
            y = jnp.dot(act[r], wd, preferred_element_type=F32)
            y_ref[rows, :] = _pack_rows(y.astype(BF16).astype(F32))


def _experts(blk_e, n_used, buf, g, wg, wu, wd, layer):
    n_blocks = buf.shape[0] // MOE_BLOCK
    grid_spec = pltpu.PrefetchScalarGridSpec(
        num_scalar_prefetch=2,
        grid=(n_blocks,),
        in_specs=[
            pl.BlockSpec((MOE_BLOCK, D_PACKED), lambda i, be, nu: (jnp.minimum(i, nu[0] - 1), 0)),
            pl.BlockSpec((1, D_MODEL), lambda i, be, nu: (0, 0)),
            pl.BlockSpec((1, 1, D_MODEL, D_EXPERT), lambda i, be, nu: (layer, be[i], 0, 0)),
            pl.BlockSpec((1, 1, D_MODEL, D_EXPERT), lambda i, be, nu: (layer, be[i], 0, 0)),
            pl.BlockSpec((1, 1, D_EXPERT, D_MODEL), lambda i, be, nu: (layer, be[i], 0, 0)),
        ],
        out_specs=pl.BlockSpec((MOE_BLOCK, D_PACKED), lambda i, be, nu: (jnp.minimum(i, nu[0] - 1), 0)),
        scratch_shapes=[pltpu.VMEM((D_MODEL, D_EXPERT), BF16), pltpu.VMEM((D_MODEL, D_EXPERT), BF16),
                        pltpu.VMEM((D_EXPERT, D_MODEL), BF16)],
    )
    return pl.pallas_call(
        _experts_kernel,
        grid_spec=grid_spec,
        out_shape=jax.ShapeDtypeStruct(buf.shape, jnp.int32),
        input_output_aliases={2: 0},
        compiler_params=_cparams(("arbitrary",)),
        name="moe_experts",
    )(blk_e, n_used, buf, g, wg, wu, wd)


def _combine_kernel(tab_ref, tabn_ref, route_ref, x_ref, y_ref, o_ref, ya_ref, yb_ref, sema, semb):
    tm = x_ref.shape[0]
    r_st = ya_ref.shape[0]
    step = pl.program_id(0)

    def window(tabs, stage_ref, s):
        def make_copy(k):
            return pltpu.make_async_copy(y_ref.at[pl.ds(pl.multiple_of(tabs[0, 0, k], SUBLANES), MOE_WINDOW), :],
                                         stage_ref.at[k * MOE_WINDOW:(k + 1) * MOE_WINDOW, :], s)
        return make_copy

    @pl.when(step == 0)
    def _():
        ya_ref[...] = jnp.zeros_like(ya_ref)
        yb_ref[...] = jnp.zeros_like(yb_ref)
        _start_all_windows(window(tab_ref, ya_ref, sema))

    route = route_ref[...]
    r0 = route[:, 4:5].astype(jnp.int32)
    r1 = route[:, 5:6].astype(jnp.int32)
    part = r_st // STAGE_PARTS
    col = lax.broadcasted_iota(jnp.int32, (tm, part), 1)

    def run(cur_ref, cur_sem, nxt_ref, nxt_sem):
        @pl.when(step + 1 < pl.num_programs(0))
        def _():
            _start_all_windows(window(tabn_ref, nxt_ref, nxt_sem))

        _wait_all_windows(window(tab_ref, cur_ref, cur_sem))
        acc = x_ref[...]
        for p in range(STAGE_PARTS):
            gate = jnp.where(col + p * part == r0, route[:, 2:3],
                             jnp.where(col + p * part == r1, route[:, 3:4], 0.0)).astype(BF16)
            rows = _unpack_rows(cur_ref[p * part:(p + 1) * part, :]).astype(BF16)
            acc = acc + jnp.dot(gate, rows, preferred_element_type=F32)
        o_ref[...] = acc

    @pl.when(step % 2 == 0)
    def _():
        run(ya_ref, sema, yb_ref, semb)

    @pl.when(step % 2 == 1)
    def _():
        run(yb_ref, semb, ya_ref, sema)


def _combine(tabs, route, x2, y_buf):
    t_tok = x2.shape[0]
    tm = min(TM_ROUTE, t_tok)
    assert t_tok % tm == 0, "token count must be a multiple of the routing tile"
    n_tiles = t_tok // tm
    return pl.pallas_call(
        _combine_kernel,
        grid=(n_tiles,),
        in_specs=[
            pl.BlockSpec((1, 1, TAB_LANES), lambda i: (i, 0, 0), memory_space=pltpu.SMEM),
            pl.BlockSpec((1, 1, TAB_LANES), lambda i: (jnp.minimum(i + 1, n_tiles - 1), 0, 0),
                         memory_space=pltpu.SMEM),
            pl.BlockSpec((tm, ROUTE_LANES), lambda i: (i, 0)),
            pl.BlockSpec((tm, D_MODEL), lambda i: (i, 0)),
            pl.BlockSpec(memory_space=pl.ANY),
        ],
        out_specs=pl.BlockSpec((tm, D_MODEL), lambda i: (i, 0)),
        out_shape=jax.ShapeDtypeStruct((t_tok, D_MODEL), F32),
        scratch_shapes=[pltpu.VMEM((_staging_rows(tm), D_PACKED), jnp.int32),
                        pltpu.VMEM((_staging_rows(tm), D_PACKED), jnp.int32),
                        pltpu.SemaphoreType.DMA(()), pltpu.SemaphoreType.DMA(())],
        compiler_params=_cparams(("arbitrary",)),
        name="moe_combine",
    )(tabs, tabs, route, x2, y_buf)


def _rope_tables(positions):
    half = ROT_DIM // 2
    inv_freq = ROPE_THETA ** (-jnp.arange(0, ROT_DIM, 2, dtype=F32) / ROT_DIM)
    d = jnp.arange(LANES) % DK_A
    freq_lane = jnp.where(d < ROT_DIM, inv_freq[d % half], 0.0)
    ang = positions.reshape(-1, 1).astype(F32) * freq_lane[None, :]
    cos, sin = jnp.cos(ang), jnp.sin(ang)
    s1 = jnp.where(d < half, -sin, 0.0)
    s2 = jnp.where((d >= half) & (d < ROT_DIM), sin, 0.0)
    return cos, s1, s2


def _block_avg(width, group):
    idx = jnp.arange(width) // group
    return jnp.where(idx[:, None] == idx[None, :], 1.0 / group, 0.0).astype(BF16)


def kernel(x, positions, norm_mix_g, w_in, attn_q_norm_g, attn_k_norm_g, lambda_q1, lambda_k1, lambda_q2,
           lambda_k2, attn_subln_g, hgrn_lower_bounds, hgrn_norm_g, conv_w, w_out, norm_ffn_g, router_group_w,
           router_group_b, router_expert_w, router_expert_b, expert_w_gate, expert_w_up, expert_w_down):
    batch, seq, d = x.shape
    assert d == D_MODEL
    depth = w_in.shape[0]
    t_tok = batch * seq
    n_assign = 2 * t_tok
    n_route_tiles = t_tok // min(TM_ROUTE, t_tok)
    spare_blocks = -(-MAX_WINDOWS * MOE_WINDOW // MOE_BLOCK)
    n_blocks = (n_assign + (SUBLANES - 1) * n_route_tiles * N_EXPERTS) // MOE_BLOCK + 2 * N_EXPERTS + 1 + spare_blocks
    cap = n_blocks * MOE_BLOCK

    lb_probs = jax.nn.softmax(hgrn_lower_bounds.astype(F32), axis=0)
    lower_bounds = jnp.cumsum(lb_probs, axis=0) - lb_probs[0:1]
    cos_t, s1_t, s2_t = _rope_tables(positions)
    gsum_a = _block_avg(W_A, DK_A)
    gavg_b = _block_avg(W_B, DV_B)

    o_q, o_k, o_v = 0, W_A, 2 * W_A
    o_bq = 3 * W_A
    o_bf, o_bi, o_bg = o_bq + W_B, o_bq + 2 * W_B, o_bq + 3 * W_B
    o_c = o_bq + 4 * W_B

    x2 = x.reshape(t_tok, d)
    for l in range(depth):
        w = w_in[l]
        wqk = w[:, o_q:o_v].astype(BF16)
        wv = w[:, o_v:o_bq].astype(BF16)
        whb = jnp.concatenate([w[:, o_bq:o_bf], w[:, o_bi:o_c]], axis=1).astype(BF16)
        wf = w[:, o_bf:o_bi].astype(BF16)
        wc = w[:, o_c:].astype(BF16)
        gqk = jnp.concatenate([jnp.tile(attn_q_norm_g[l], 2 * H_A) * (DK_A ** -0.5 * LOG2_E),
                               jnp.tile(attn_k_norm_g[l], 2 * H_A)]).reshape(1, 2 * W_A).astype(F32)
        qk, v, hb, f, cv = _inproj(x2, norm_mix_g[l].reshape(1, d), wqk, wv, whb, wf, wc, gsum_a, gqk,
                                   cos_t, s1_t, s2_t)

        lam_init = 0.8 - 0.6 * math.exp(-0.3 * l)
        lam = (jnp.exp(jnp.sum(lambda_q1[l].astype(F32) * lambda_k1[l].astype(F32)))
               - jnp.exp(jnp.sum(lambda_q2[l].astype(F32) * lambda_k2[l].astype(F32))) + lam_init).reshape(1)
        o_a = _attention(qk, v, lam, attn_subln_g[l].reshape(DV_A, 1).astype(F32), batch, seq, lam_init)

        o_bc = _hgrn_conv(hb, f, cv, lower_bounds[l].reshape(1, W_B), jnp.tile(hgrn_norm_g[l], H_B).reshape(1, W_B),
                          conv_w[l], gavg_b, batch, seq)

        wr = jnp.concatenate([router_group_w[l], router_expert_w[l],
                              jnp.zeros((d, ROUTE_LANES - N_LOGITS), F32)], axis=1).astype(BF16)
        br = jnp.concatenate([router_group_b[l], router_expert_b[l],
                              jnp.zeros((ROUTE_LANES - N_LOGITS,), F32)]).reshape(1, ROUTE_LANES)
        x2, route, route_t, tab, cnt = _outproj_router(o_a, o_bc, x2, w_out[l].astype(BF16), norm_ffn_g[l].reshape(1, d), wr, br)

        counts = cnt[N_GROUPS:N_GROUPS + N_EXPERTS, 0].astype(jnp.int32)
        padded = (counts + MOE_WINDOW + MOE_BLOCK - 1) // MOE_BLOCK * MOE_BLOCK
        pad_end = jnp.cumsum(padded)
        pad_start = pad_end - padded
        blk_pos = jnp.arange(n_blocks, dtype=jnp.int32) * MOE_BLOCK
        blk_e = jnp.minimum(jnp.sum((pad_end[None, :] <= blk_pos[:, None]).astype(jnp.int32), axis=1),
                            N_EXPERTS - 1)
        n_used = (pad_end[-1] // MOE_BLOCK).astype(jnp.int32).reshape(1)
        tile_cnt = tab[:, N_GROUPS:N_GROUPS + N_EXPERTS, 0].astype(jnp.int32)
        tile_base = tab[:, N_GROUPS:N_GROUPS + N_EXPERTS, 1].astype(jnp.int32) + pad_start[None, :]
        n_win = (tile_cnt + MOE_WINDOW - 1) // MOE_WINDOW
        w_end = jnp.cumsum(n_win, axis=1)
        w_start = w_end - n_win
        k_idx = jnp.arange(MAX_WINDOWS, dtype=jnp.int32)[None, :, None]
        owner = (w_start[:, None, :] <= k_idx) & (k_idx < w_end[:, None, :])
        win_dst = jnp.sum(jnp.where(owner, tile_base[:, None, :] + (k_idx - w_start[:, None, :]) * MOE_WINDOW, 0),
                          axis=-1)
        spare = cap - MAX_WINDOWS * MOE_WINDOW + k_idx[:, :, 0] * MOE_WINDOW
        win_dst = jnp.where(jnp.any(owner, axis=-1), win_dst, spare)
        tabs = jnp.concatenate([win_dst, w_end[:, -1:],
                                jnp.zeros((win_dst.shape[0], TAB_LANES - MAX_WINDOWS - 1), jnp.int32)],
                               axis=1).reshape(-1, 1, TAB_LANES)

        buf = _dispatch(pad_end, tabs, route_t, x2, cap)
        y_buf = _experts(blk_e, n_used, buf, norm_ffn_g[l].reshape(1, d), expert_w_gate, expert_w_up,
                         expert_w_down, l)
        x2 = _combine(tabs, route, x2, y_buf)
    return x2.reshape(batch, seq, d)
```

```python
import functools
import math

import jax
import jax.numpy as jnp
from jax import lax
from jax.experimental import pallas as pl
from jax.experimental.pallas import tpu as pltpu

F32 = jnp.float32
BF16 = jnp.bfloat16

LANES = 128
SUBLANES = 8

D_MODEL = 1024
CHUNK = 64
EPS = 1e-6
MASK_VALUE = -1e30
TINY = 1e-30
LOG2_E = math.log2(math.e)
H_A = 4
DK_A = D_MODEL // 16
DV_A = 2 * DK_A
ROT_DIM = DK_A // 4
ROPE_THETA = 500000.0
H_B = 4
DK_B = D_MODEL // 16
DV_B = D_MODEL // 16
C_WIDTH = D_MODEL // 4
CONV_WIDTH = 3
N_GROUPS = 4
EXPERTS_PER_GROUP = 8
N_EXPERTS = N_GROUPS * EXPERTS_PER_GROUP
D_EXPERT = D_MODEL // 2
D_PACKED = D_MODEL // 2

W_A = H_A * DV_A
W_B = H_B * DK_B
ROUTE_LANES = LANES
N_LOGITS = N_GROUPS + N_EXPERTS
ROUTE_ROWS = -(-N_LOGITS // SUBLANES) * SUBLANES

TM_PROJ = 1024
TK = 256
KV_PER_Q = 2
TQ = KV_PER_Q * TK
TM_ROUTE = 512
MOE_BLOCK = 1024
HGRN_BATCH = 2
HGRN_UNROLL = 4
MOE_WINDOW = 8
STAGE_PARTS = 5
MAX_WINDOWS = 2 * TM_ROUTE // MOE_WINDOW + N_EXPERTS
TAB_LANES = -(-(MAX_WINDOWS + 1) // LANES) * LANES
VMEM_LIMIT = 56 * 1024 * 1024


def _staging_rows(tm):
    return 2 * tm + N_EXPERTS * MOE_WINDOW


def _row_parts(rows, n=2):
    step = rows // n
    return [slice(r * step, (r + 1) * step) for r in range(n)]


def _pack_rows(x):
    half = x.shape[1] // 2
    lo = lax.bitcast_convert_type(x[:, :half], jnp.int32)
    hi = lax.bitcast_convert_type(x[:, half:], jnp.int32)
    return hi | lax.shift_right_logical(lo, 16)


def _unpack_rows(p):
    lo = lax.bitcast_convert_type(lax.shift_left(p, 16), F32)
    hi = lax.bitcast_convert_type(p & jnp.int32(-65536), F32)
    return jnp.concatenate([lo, hi], axis=1)


def _cparams(sem, vmem=VMEM_LIMIT):
    return pltpu.CompilerParams(dimension_semantics=sem, vmem_limit_bytes=vmem)


def _inproj_kernel(x_ref, g_ref, wqk_ref, wv_ref, whb_ref, wf_ref, wc_ref, gsum_ref, gqk_ref,
                   cos_ref, s1_ref, s2_ref, qk_out, v_out, hb_out, f_out, c_out):
    parts = _row_parts(x_ref.shape[0])
    xn = []
    for rows in parts:
        x = x_ref[rows, :]
        ms = jnp.mean(x * x, axis=-1, keepdims=True)
        xn.append((x * lax.rsqrt(ms + EPS) * g_ref[...]).astype(BF16))

    gsum = gsum_ref[...]
    for part in range(2):
        lo = part * W_A
        t = [jnp.dot(xr, wqk_ref[:, lo:lo + W_A], preferred_element_type=F32) for xr in xn]
        msq = [jnp.dot((tr * tr).astype(BF16), gsum, preferred_element_type=F32) for tr in t]
        for r, rows in enumerate(parts):
            tn = t[r] * lax.rsqrt(msq[r] + EPS) * gqk_ref[:, lo:lo + W_A]
            cos = cos_ref[rows, :]
            s1 = s1_ref[rows, :]
            s2 = s2_ref[rows, :]
            for h in range(H_A):
                slab = tn[:, h * LANES:(h + 1) * LANES]
                up = pltpu.roll(slab, LANES - ROT_DIM // 2, axis=1)
                dn = pltpu.roll(slab, ROT_DIM // 2, axis=1)
                qk_out[rows, lo + h * LANES:lo + (h + 1) * LANES] = (slab * cos + up * s1 + dn * s2).astype(BF16)

    for r, rows in enumerate(parts):
        v_out[rows, :] = jnp.dot(xn[r], wv_ref[...], preferred_element_type=F32).astype(BF16)
    for r, rows in enumerate(parts):
        hb_out[rows, :] = jnp.dot(xn[r], whb_ref[...], preferred_element_type=F32).astype(BF16)
    for r, rows in enumerate(parts):
        f_out[rows, :] = jnp.dot(xn[r], wf_ref[...], preferred_element_type=F32)
    for r, rows in enumerate(parts):
        c_out[rows, :] = jnp.dot(xn[r], wc_ref[...], preferred_element_type=F32).astype(BF16)


def _inproj(x2, g, wqk, wv, whb, wf, wc, gsum, gqk, cos_t, s1_t, s2_t):
    t_tok = x2.shape[0]
    tm = min(TM_PROJ, t_tok)
    assert t_tok % tm == 0, "token count must be a multiple of the projection tile"
    grid = (t_tok // tm,)
    row = lambda i: (i, 0)
    fixed = lambda i: (0, 0)
    return pl.pallas_call(
        _inproj_kernel,
        grid=grid,
        in_specs=[
            pl.BlockSpec((tm, D_MODEL), row),
            pl.BlockSpec((1, D_MODEL), fixed),
            pl.BlockSpec(wqk.shape, fixed),
            pl.BlockSpec(wv.shape, fixed),
            pl.BlockSpec(whb.shape, fixed),
            pl.BlockSpec(wf.shape, fixed),
            pl.BlockSpec(wc.shape, fixed),
            pl.BlockSpec(gsum.shape, fixed),
            pl.BlockSpec(gqk.shape, fixed),
            pl.BlockSpec((tm, LANES), row),
            pl.BlockSpec((tm, LANES), row),
            pl.BlockSpec((tm, LANES), row),
        ],
        out_specs=[
            pl.BlockSpec((tm, 2 * W_A), row),
            pl.BlockSpec((tm, W_A), row),
            pl.BlockSpec((tm, 3 * W_B), row),
            pl.BlockSpec((tm, W_B), row),
            pl.BlockSpec((tm, 3 * C_WIDTH), row),
        ],
        out_shape=[
            jax.ShapeDtypeStruct((t_tok, 2 * W_A), BF16),
            jax.ShapeDtypeStruct((t_tok, W_A), BF16),
            jax.ShapeDtypeStruct((t_tok, 3 * W_B), BF16),
            jax.ShapeDtypeStruct((t_tok, W_B), F32),
            jax.ShapeDtypeStruct((t_tok, 3 * C_WIDTH), BF16),
        ],
        compiler_params=_cparams(("parallel",)),
        name="inproj",
    )(x2, g, wqk, wv, whb, wf, wc, gsum, gqk, cos_t, s1_t, s2_t)


def _attn_kernel(lam_ref, q_ref, k_ref, v_ref, g_ref, o_ref, qc_ref, sa_ref, sb_ref, acc_ref, m_ref, l_ref, *,
                 lam_init):
    qi = pl.program_id(1)
    q = q_ref[...]
    lane = lax.broadcasted_iota(jnp.int32, (TQ, DV_A), 1)
    zero = jnp.zeros((TQ, DV_A), BF16)
    for h in range(H_A):
        qh = q[:, h * DV_A:(h + 1) * DV_A]
        qc_ref[h, 0] = jnp.where(lane < DK_A, qh, zero)
        qc_ref[h, 1] = jnp.where(lane >= DK_A, qh, zero)

    acc_ref[...] = jnp.zeros_like(acc_ref)
    m_ref[...] = jnp.full_like(m_ref, -jnp.inf)
    l_ref[...] = jnp.zeros_like(l_ref)

    kchunk = lax.broadcasted_iota(jnp.int32, (TK, TQ), 0) // CHUNK
    qchunk = lax.broadcasted_iota(jnp.int32, (TK, TQ), 1) // CHUNK

    def scores(j, s_ref, q0=0):
        start = pl.multiple_of(j * TK, TK)
        kb = k_ref[pl.ds(start, TK), :]
        for h in range(H_A):
            kh = kb[:, h * DV_A:(h + 1) * DV_A]
            for c in range(2):
                s_ref[h, c, :, q0:] = lax.dot_general(kh, qc_ref[h, c, q0:, :], (((1,), (1,)), ((), ())),
                                                      preferred_element_type=F32)

    def consume(j, s_ref, mask, q0=0):
        start = pl.multiple_of(j * TK, TK)
        vb = v_ref[pl.ds(start, TK), :]
        for h in range(H_A):
            vh = vb[:, h * DV_A:(h + 1) * DV_A]
            for c in range(2):
                s = s_ref[h, c, :, q0:]
                if mask is not None:
                    s = jnp.where(mask[:, q0:], s, MASK_VALUE)
                m_old = m_ref[h, c, :, q0:]
                m_new = jnp.maximum(m_old, jnp.max(s, axis=0, keepdims=True))
                alpha = jnp.exp2(m_old - m_new)
                p = jnp.exp2(s - m_new)
                l_ref[h, c, :, q0:] = alpha * l_ref[h, c, :, q0:] + jnp.sum(p, axis=0, keepdims=True)
                pv = lax.dot_general(vh, p.astype(BF16), (((0,), (0,)), ((), ())),
                                     preferred_element_type=F32)
                acc_ref[h, c, :, q0:] = alpha * acc_ref[h, c, :, q0:] + pv
                m_ref[h, c, :, q0:] = m_new

    first_diag = KV_PER_Q * qi
    scores(0, sa_ref)

    def body(jj, carry):
        j = 2 * jj
        scores(j + 1, sb_ref)
        consume(j, sa_ref, None)
        scores(j + 2, sa_ref)
        consume(j + 1, sb_ref, None)
        return carry

    lax.fori_loop(0, first_diag // 2, body, 0)
    for d in range(KV_PER_Q):
        cur, nxt = (sa_ref, sb_ref) if d % 2 == 0 else (sb_ref, sa_ref)
        if d + 1 < KV_PER_Q:
            scores(first_diag + d + 1, nxt, (d + 1) * TK)
        consume(first_diag + d, cur, (kchunk + d * (TK // CHUNK)) <= qchunk, d * TK)

    lam = lam_ref[0]
    for h in range(H_A):
        o = acc_ref[h, 0] / l_ref[h, 0] - lam * (acc_ref[h, 1] / l_ref[h, 1])
        ms = jnp.mean(o * o, axis=0, keepdims=True)
        y = o * lax.rsqrt(ms + EPS) * g_ref[...] * (1.0 - lam_init)
        o_ref[:, h * DV_A:(h + 1) * DV_A] = y.T.astype(BF16)


def _attention(qk, v, lam, subln_col, batch, seq, lam_init):
    assert TQ == KV_PER_Q * TK and KV_PER_Q % 2 == 0 and TK % CHUNK == 0 and seq % TQ == 0
    nq = seq // TQ
    t_tok = batch * seq
    kernel = functools.partial(_attn_kernel, lam_init=lam_init)
    return pl.pallas_call(
        kernel,
        grid=(batch, nq),
        in_specs=[
            pl.BlockSpec(memory_space=pltpu.SMEM),
            pl.BlockSpec((TQ, W_A), lambda b, i: (b * nq + i, 0)),
            pl.BlockSpec((seq, W_A), lambda b, i: (b, 1)),
            pl.BlockSpec((seq, W_A), lambda b, i: (b, 0)),
            pl.BlockSpec((DV_A, 1), lambda b, i: (0, 0)),
        ],
        out_specs=pl.BlockSpec((TQ, W_A), lambda b, i: (b * nq + i, 0)),
        out_shape=jax.ShapeDtypeStruct((t_tok, W_A), BF16),
        scratch_shapes=[
            pltpu.VMEM((H_A, 2, TQ, DV_A), BF16),
            pltpu.VMEM((H_A, 2, TK, TQ), F32),
            pltpu.VMEM((H_A, 2, TK, TQ), F32),
            pltpu.VMEM((H_A, 2, DV_A, TQ), F32),
            pltpu.VMEM((H_A, 2, 1, TQ), F32),
            pltpu.VMEM((H_A, 2, 1, TQ), F32),
        ],
        compiler_params=_cparams(("parallel", "arbitrary")),
        name="diff_attn",
    )(lam, qk, qk, v, subln_col)


_LEVELS = (32, 16, 8, 4, 2, 1)


def _ref_rows(g, h):
    n = g.shape[1]
    if 2 * h >= SUBLANES:
        pieces = []
        for m in range(CHUNK // (2 * h)):
            r = 2 * h * m + h - 1
            pieces.append(jnp.broadcast_to(g[r:r + 1, :], (2 * h, n)))
        return pieces[0] if len(pieces) == 1 else jnp.concatenate(pieces, axis=0)
    g3 = g.reshape(CHUNK // SUBLANES, SUBLANES, n)
    sub = lax.broadcasted_iota(jnp.int32, g3.shape, 1)
    out = None
    for m in reversed(range(SUBLANES // (2 * h))):
        r = 2 * h * m + h - 1
        piece = jnp.broadcast_to(g3[:, r:r + 1, :], g3.shape)
        out = piece if out is None else jnp.where(sub < 2 * h * (m + 1), piece, out)
    return out.reshape(CHUNK, n)


def _stack_heads(a, lane_head):
    zero = jnp.zeros_like(a)
    return jnp.concatenate([jnp.where(lane_head == hh, a, zero) for hh in range(H_B)], axis=0)


def _hgrn_kernel(hb_ref, f_ref, c_ref, lb_ref, ng_ref, cw_ref, gavg_ref, o_ref, state_ref, hbuf_ref, *, seq):
    n_chunks = seq // CHUNK
    nb = hb_ref.shape[0] // seq
    lb = lb_ref[...]
    ng = ng_ref[...]
    gavg = gavg_ref[...]

    row = lax.broadcasted_iota(jnp.int32, (CHUNK, W_B), 0)
    lane = lax.broadcasted_iota(jnp.int32, (CHUNK, W_B), 1)
    lane_head = lane // DK_B
    lane_s = lane % CHUNK
    tri = (lax.broadcasted_iota(jnp.int32, (CHUNK, CHUNK), 1)
           <= lax.broadcasted_iota(jnp.int32, (CHUNK, CHUNK), 0)).astype(BF16)
    lvl_mask = [((row // (2 * h)) == (lane_s // (2 * h))) & ((row % (2 * h)) >= h) & ((lane_s % (2 * h)) < h)
                for h in _LEVELS]
    eye_mask = row == lane_s
    r4 = lax.broadcasted_iota(jnp.int32, (H_B * DK_B, W_B), 0) // DK_B
    c4 = lax.broadcasted_iota(jnp.int32, (H_B * DK_B, W_B), 1) // DK_B
    bd_mask = r4 == c4

    state_ref[...] = jnp.zeros_like(state_ref)
    nt = (((1,), (1,)), ((), ()))
    tn = (((0,), (0,)), ((), ()))
    unroll = HGRN_UNROLL if n_chunks % HGRN_UNROLL == 0 else 1
    each = range(nb * unroll)

    def chunk(i, carry):
        rows = [pl.ds(pl.multiple_of((e % nb) * seq + (i * unroll + e // nb) * CHUNK, CHUNK), CHUNK) for e in each]
        hb = [hb_ref[rows[j], :].astype(F32) for j in each]
        zq = [t[:, 0:W_B] for t in hb]
        v16 = [t[:, W_B:2 * W_B].astype(BF16) for t in hb]
        zg = [t[:, 2 * W_B:3 * W_B] for t in hb]
        zf = [f_ref[rows[j], :] for j in each]

        logf = [jnp.log(jnp.maximum(lb + (1.0 - lb) * jax.nn.sigmoid(z), TINY)) for z in zf]
        key = [(1.0 - lb) * jax.nn.sigmoid(-z) for z in zf]
        q = [z * jax.nn.sigmoid(z) for z in zq]

        g = []
        for x in logf:
            hi = x.astype(BF16)
            r1 = x - hi.astype(F32)
            mid = r1.astype(BF16)
            lo = (r1 - mid.astype(F32)).astype(BF16)
            g.append((jnp.dot(tri, hi, preferred_element_type=F32) + jnp.dot(tri, mid, preferred_element_type=F32)
                      + jnp.dot(tri, lo, preferred_element_type=F32)) * LOG2_E)

        q16 = [t.astype(BF16) for t in q]
        k16 = [t.astype(BF16) for t in key]
        a_all = [jnp.where(eye_mask, lax.dot_general(q16[j], _stack_heads(k16[j], lane_head),
                                                     nt, preferred_element_type=F32), 0.0) for j in each]
        for h, mask in zip(_LEVELS, lvl_mask):
            w = [jnp.exp2(-jnp.abs(g[j] - _ref_rows(g[j], h))).astype(BF16) for j in each]
            ah = [lax.dot_general(q16[j] * w[j], _stack_heads(k16[j] * w[j], lane_head),
                                  nt, preferred_element_type=F32) for j in each]
            a_all = [a_all[j] + jnp.where(mask, ah[j], 0.0) for j in each]
        o = [jnp.dot(a_all[j].astype(BF16), _stack_heads(v16[j], lane_head), preferred_element_type=F32) for j in each]

        qg = [(q[j] * jnp.exp2(g[j])).astype(BF16) for j in each]
        g_last = [t[CHUNK - 1:CHUNK, :] for t in g]
        upd = [lax.dot_general(v16[j], (key[j] * jnp.exp2(g_last[j] - g[j])).astype(BF16), tn,
                               preferred_element_type=F32) for j in each]
        for j in each:
            st = state_ref[j % nb]
            o[j] = o[j] + lax.dot_general(qg[j], st.astype(BF16), nt, preferred_element_type=F32)
            state_ref[j % nb] = jnp.exp2(g_last[j]) * st + jnp.where(bd_mask, upd[j], 0.0)

        ms = [jnp.dot((t * t).astype(BF16), gavg, preferred_element_type=F32) for t in o]
        for j in each:
            y = o[j] * lax.rsqrt(ms[j] + EPS) * ng * (zg[j] * jax.nn.sigmoid(zg[j]))
            o_ref[rows[j], 0:W_B] = y.astype(BF16)
        return carry

    lax.fori_loop(0, n_chunks // unroll, chunk, 0)

    piece = min(512, seq)
    hbuf_ref[0:SUBLANES, :] = jnp.zeros((SUBLANES, C_WIDTH), F32)
    for j in range(nb):
        for p in range(seq // piece):
            cblk = c_ref[j * seq + p * piece:j * seq + (p + 1) * piece, :].astype(F32)
            hbuf_ref[SUBLANES + p * piece:SUBLANES + (p + 1) * piece, :] = (cblk[:, C_WIDTH:2 * C_WIDTH]
                                                                             * cblk[:, 2 * C_WIDTH:])
        for p in range(seq // piece):
            base = SUBLANES + p * piece
            y = (cw_ref[0:1, :] * hbuf_ref[base - 2:base - 2 + piece, :]
                 + cw_ref[1:2, :] * hbuf_ref[base - 1:base - 1 + piece, :]
                 + cw_ref[2:3, :] * hbuf_ref[base:base + piece, :])
            lo_r = j * seq + p * piece
            bgate = c_ref[lo_r:lo_r + piece, 0:C_WIDTH].astype(F32)
            o_ref[lo_r:lo_r + piece, W_B:W_B + C_WIDTH] = (bgate * y).astype(BF16)


def _hgrn_conv(hb, f, cv, lb, ng, cw, gavg, batch, seq):
    t_tok = batch * seq
    nb = HGRN_BATCH if batch % HGRN_BATCH == 0 else 1
    fixed = lambda b: (0, 0)
    return pl.pallas_call(
        functools.partial(_hgrn_kernel, seq=seq),
        grid=(batch // nb,),
        in_specs=[
            pl.BlockSpec((nb * seq, 3 * W_B), lambda b: (b, 0)),
            pl.BlockSpec((nb * seq, W_B), lambda b: (b, 0)),
            pl.BlockSpec((nb * seq, 3 * C_WIDTH), lambda b: (b, 0)),
            pl.BlockSpec((1, W_B), fixed),
            pl.BlockSpec((1, W_B), fixed),
            pl.BlockSpec((CONV_WIDTH, C_WIDTH), fixed),
            pl.BlockSpec((W_B, W_B), fixed),
        ],
        out_specs=pl.BlockSpec((nb * seq, W_B + C_WIDTH), lambda b: (b, 0)),
        out_shape=jax.ShapeDtypeStruct((t_tok, W_B + C_WIDTH), BF16),
        scratch_shapes=[
            pltpu.VMEM((nb, H_B * DV_B, W_B), F32),
            pltpu.VMEM((seq + SUBLANES, C_WIDTH), F32),
        ],
        compiler_params=_cparams(("parallel",)),
        name="hgrn_conv",
    )(hb, f, cv, lb, ng, cw, gavg)


def _outproj_router_kernel(oa_ref, obc_ref, x_ref, wo_ref, g_ref, wr_ref, br_ref,
                           xo_ref, route_ref, route_t_ref, tab_ref, cnt_ref, carry_ref):
    i = pl.program_id(0)

    @pl.when(i == 0)
    def _():
        carry_ref[...] = jnp.zeros_like(carry_ref)

    mixed = (jnp.dot(oa_ref[...], wo_ref[0:W_A, :], preferred_element_type=F32)
             + jnp.dot(obc_ref[...], wo_ref[W_A:, :], preferred_element_type=F32))
    xn = x_ref[...] + mixed
    xo_ref[...] = xn

    ms = jnp.mean(xn * xn, axis=-1, keepdims=True)
    hn = (xn * lax.rsqrt(ms + EPS) * g_ref[...]).astype(BF16)
    logits = jnp.dot(hn, wr_ref[...], preferred_element_type=F32) + br_ref[...]

    tm = logits.shape[0]
    lt = logits.T[0:ROUTE_ROWS, :]
    r = lax.broadcasted_iota(jnp.int32, (ROUTE_ROWS, tm), 0)
    neg = -jnp.inf
    big = ROUTE_ROWS

    def top(cand):
        val = jnp.max(cand, axis=0, keepdims=True)
        idx = jnp.min(jnp.where(cand == val, r, big), axis=0, keepdims=True)
        return val, idx

    glog = jnp.where(r < N_GROUPS, lt, neg)
    gmax, gidx = top(glog)
    p_g = 1.0 / jnp.sum(jnp.exp(glog - gmax), axis=0, keepdims=True)
    off = r - (N_GROUPS + EXPERTS_PER_GROUP * gidx)
    elog = jnp.where(jnp.abs(2 * off - (EXPERTS_PER_GROUP - 1)) < EXPERTS_PER_GROUP, lt, neg)
    v1, i1 = top(elog)
    v2, i2 = top(jnp.where(r == i1, neg, elog))
    t2 = jnp.exp(v2 - v1)
    w1 = p_g / (1.0 + t2)
    w2 = p_g * t2 / (1.0 + t2)

    sel1 = r == i1
    sel2 = r == i2
    onehot = jnp.where(sel1, 1.0, jnp.where(sel2, 1.0, 0.0))
    earlier = (lax.broadcasted_iota(jnp.int32, (tm, tm), 0)
               < lax.broadcasted_iota(jnp.int32, (tm, tm), 1)).astype(BF16)
    local = jnp.dot(onehot.astype(BF16), earlier, preferred_element_type=F32)
    cnt = jnp.sum(onehot, axis=1, keepdims=True)
    wrows = jnp.floor((cnt + (MOE_WINDOW - 1)) * (1.0 / MOE_WINDOW)) * MOE_WINDOW
    below = (lax.broadcasted_iota(jnp.int32, (ROUTE_ROWS, ROUTE_LANES), 1)
             < lax.broadcasted_iota(jnp.int32, (ROUTE_ROWS, ROUTE_LANES), 0)).astype(BF16)
    wrows_rows = jnp.concatenate([jnp.broadcast_to(wrows, (ROUTE_ROWS, ROUTE_LANES)),
                                  jnp.zeros((ROUTE_LANES - ROUTE_ROWS, ROUTE_LANES), F32)], axis=0).astype(BF16)
    soff = jnp.dot(below, wrows_rows, preferred_element_type=F32)[:, 0:1]
    slot = local + soff
    row1 = jnp.sum(jnp.where(sel1, slot, 0.0), axis=0, keepdims=True)
    row2 = jnp.sum(jnp.where(sel2, slot, 0.0), axis=0, keepdims=True)

    k = lax.broadcasted_iota(jnp.int32, (SUBLANES, tm), 0)
    rec = jnp.where(k == 0, (i1 - N_GROUPS).astype(F32),
          jnp.where(k == 1, (i2 - N_GROUPS).astype(F32),
          jnp.where(k == 2, w1,
          jnp.where(k == 3, w2,
          jnp.where(k == 4, row1,
          jnp.where(k == 5, row2, 0.0))))))
    route_t_ref[...] = rec
    route_ref[...] = jnp.concatenate([rec, jnp.zeros((ROUTE_LANES - SUBLANES, tm), F32)], axis=0).T

    lane = lax.broadcasted_iota(jnp.int32, (ROUTE_ROWS, ROUTE_LANES), 1)
    tab_ref[0] = jnp.where(lane == 0, cnt, jnp.where(lane == 1, carry_ref[...], 0.0))
    carry_ref[...] = carry_ref[...] + jnp.floor((cnt + (SUBLANES - 1)) * (1.0 / SUBLANES)) * SUBLANES
    cnt_ref[...] = carry_ref[...]


def _outproj_router(o_a, o_bc, x2, wo, g, wr, br):
    t_tok = x2.shape[0]
    tm = min(TM_ROUTE, t_tok)
    assert t_tok % tm == 0, "token count must be a multiple of the routing tile"
    row = lambda i: (i, 0)
    fixed = lambda i: (0, 0)
    return pl.pallas_call(
        _outproj_router_kernel,
        grid=(t_tok // tm,),
        in_specs=[
            pl.BlockSpec((tm, W_A), row),
            pl.BlockSpec((tm, W_B + C_WIDTH), row),
            pl.BlockSpec((tm, D_MODEL), row),
            pl.BlockSpec(wo.shape, fixed),
            pl.BlockSpec((1, D_MODEL), fixed),
            pl.BlockSpec(wr.shape, fixed),
            pl.BlockSpec((1, ROUTE_LANES), fixed),
        ],
        out_specs=[
            pl.BlockSpec((tm, D_MODEL), row),
            pl.BlockSpec((tm, ROUTE_LANES), row),
            pl.BlockSpec((SUBLANES, tm), row),
            pl.BlockSpec((1, ROUTE_ROWS, ROUTE_LANES), lambda i: (i, 0, 0)),
            pl.BlockSpec((ROUTE_ROWS, ROUTE_LANES), fixed),
        ],
        out_shape=[
            jax.ShapeDtypeStruct((t_tok, D_MODEL), F32),
            jax.ShapeDtypeStruct((t_tok, ROUTE_LANES), F32),
            jax.ShapeDtypeStruct((t_tok // tm * SUBLANES, tm), F32),
            jax.ShapeDtypeStruct((t_tok // tm, ROUTE_ROWS, ROUTE_LANES), F32),
            jax.ShapeDtypeStruct((ROUTE_ROWS, ROUTE_LANES), F32),
        ],
        scratch_shapes=[pltpu.VMEM((ROUTE_ROWS, ROUTE_LANES), F32)],
        compiler_params=_cparams(("arbitrary",)),
        name="outproj_router",
    )(o_a, o_bc, x2, wo, g, wr, br)


def _start_copies(n, make_copy):
    def start(k, carry):
        make_copy(k).start()
        return carry

    lax.fori_loop(0, n, start, 0)


def _wait_copies(n, make_copy):
    def wait(k, carry):
        make_copy(k).wait()
        return carry

    lax.fori_loop(0, n, wait, 0)


def _window_copy_loops(n, make_copy):
    _start_copies(n, make_copy)
    _wait_copies(n, make_copy)


def _start_all_windows(make_copy):
    for k in range(MAX_WINDOWS):
        make_copy(k).start()


def _wait_all_windows(make_copy):
    for k in range(MAX_WINDOWS):
        make_copy(k).wait()


def _dispatch_kernel(pend_ref, tab_ref, tabp_ref, route_ref, x_ref, buf_ref, xp_ref, xq_ref, sem, semq):
    tm = x_ref.shape[0]
    r_st = xp_ref.shape[0]
    step = pl.program_id(0)

    @pl.when(pl.program_id(0) == 0)
    def _():
        xp_ref[...] = jnp.zeros_like(xp_ref)
        copies = []
        for e in range(N_EXPERTS):
            end = pend_ref[e]
            copies.append((None, pltpu.make_async_copy(
                xp_ref.at[0:MOE_BLOCK, :],
                buf_ref.at[pl.ds(pl.multiple_of(end - MOE_BLOCK, SUBLANES), MOE_BLOCK), :], sem)))
            copies.append((end >= MOE_BLOCK + MOE_WINDOW, pltpu.make_async_copy(
                xp_ref.at[0:MOE_WINDOW, :],
                buf_ref.at[pl.ds(pl.multiple_of(jnp.maximum(end - MOE_BLOCK - MOE_WINDOW, 0), SUBLANES), MOE_WINDOW), :],
                sem)))
        for cond, cp in copies:
            if cond is None:
                cp.start()
            else:
                pl.when(cond)(cp.start)
        for cond, cp in copies:
            if cond is None:
                cp.wait()
            else:
                pl.when(cond)(cp.wait)

        def tail_copy(k):
            return pltpu.make_async_copy(
                xp_ref.at[0:MOE_BLOCK, :],
                buf_ref.at[pl.ds(pl.multiple_of(k * MOE_BLOCK, MOE_BLOCK), MOE_BLOCK), :], sem)

        first_unused = pend_ref[N_EXPERTS - 1] // MOE_BLOCK
        n_unused = buf_ref.shape[0] // MOE_BLOCK - first_unused
        _window_copy_loops(n_unused, lambda k: tail_copy(first_unused + k))

    r0 = route_ref[4:5, :].astype(jnp.int32)
    r1 = route_ref[5:6, :].astype(jnp.int32)
    part = r_st // STAGE_PARTS
    srow = lax.broadcasted_iota(jnp.int32, (part, tm), 0)

    def window(tabs, stage_ref, s):
        def make_copy(k):
            return pltpu.make_async_copy(
                stage_ref.at[k * MOE_WINDOW:(k + 1) * MOE_WINDOW, :],
                buf_ref.at[pl.ds(pl.multiple_of(tabs[0, 0, k], SUBLANES), MOE_WINDOW), :], s)
        return make_copy

    def run(cur_ref, cur_sem, prev_ref, prev_sem):
        x16 = x_ref[...].astype(BF16)
        for p in range(STAGE_PARTS):
            sel = jnp.where(srow + p * part == r0, 1.0, jnp.where(srow + p * part == r1, 1.0, 0.0)).astype(BF16)
            cur_ref[p * part:(p + 1) * part, :] = _pack_rows(jnp.dot(sel, x16, preferred_element_type=F32))

        @pl.when(step > 0)
        def _():
            _wait_all_windows(window(tabp_ref, prev_ref, prev_sem))

        _start_all_windows(window(tab_ref, cur_ref, cur_sem))

        @pl.when(step == pl.num_programs(0) - 1)
        def _():
            _wait_all_windows(window(tab_ref, cur_ref, cur_sem))

    @pl.when(step % 2 == 0)
    def _():
        run(xp_ref, sem, xq_ref, semq)

    @pl.when(step % 2 == 1)
    def _():
        run(xq_ref, semq, xp_ref, sem)


def _dispatch(pad_end, tabs, route_t, x2, cap):
    t_tok = x2.shape[0]
    tm = min(TM_ROUTE, t_tok)
    assert t_tok % tm == 0, "token count must be a multiple of the routing tile"
    assert _staging_rows(tm) >= MOE_BLOCK, "the staging buffer doubles as the zero source of a whole block"
    n_tiles = t_tok // tm
    return pl.pallas_call(
        _dispatch_kernel,
        grid=(n_tiles,),
        in_specs=[
            pl.BlockSpec(memory_space=pltpu.SMEM),
            pl.BlockSpec((1, 1, TAB_LANES), lambda i: (i, 0, 0), memory_space=pltpu.SMEM),
            pl.BlockSpec((1, 1, TAB_LANES), lambda i: (jnp.maximum(i - 1, 0), 0, 0), memory_space=pltpu.SMEM),
            pl.BlockSpec((SUBLANES, tm), lambda i: (i, 0)),
            pl.BlockSpec((tm, D_MODEL), lambda i: (i, 0)),
        ],
        out_specs=pl.BlockSpec(memory_space=pl.ANY),
        out_shape=jax.ShapeDtypeStruct((cap, D_PACKED), jnp.int32),
        scratch_shapes=[pltpu.VMEM((_staging_rows(tm), D_PACKED), jnp.int32),
                        pltpu.VMEM((_staging_rows(tm), D_PACKED), jnp.int32),
                        pltpu.SemaphoreType.DMA(()), pltpu.SemaphoreType.DMA(())],
        compiler_params=_cparams(("arbitrary",)),
        name="moe_dispatch",
    )(pad_end, tabs, tabs, route_t, x2)


def _experts_kernel(blk_e_ref, n_used_ref, x_ref, g_ref, wg_ref, wu_ref, wd_ref, y_ref, wg16, wu16, wd16):
    i = pl.program_id(0)
    used = i < n_used_ref[0]

    @pl.when(used & ((i == 0) | (blk_e_ref[i] != blk_e_ref[jnp.maximum(i - 1, 0)])))
    def _():
        wg16[...] = wg_ref[0, 0].astype(BF16)
        wu16[...] = wu_ref[0, 0].astype(BF16)
        wd16[...] = wd_ref[0, 0].astype(BF16)

    @pl.when(used)
    def _():
        wg = wg16[...]
        wu = wu16[...]
        wd = wd16[...]
        parts = _row_parts(x_ref.shape[0], 4)
        hn = []
        for rows in parts:
            x = _unpack_rows(x_ref[rows, :])
            ms = jnp.mean(x * x, axis=-1, keepdims=True)
            hn.append((x * lax.rsqrt(ms + EPS) * g_ref[...]).astype(BF16))
        a = [jnp.dot(t, wg, preferred_element_type=F32) for t in hn]
        u = [jnp.dot(t, wu, preferred_element_type=F32) for t in hn]
        act = [(a[r] * jax.nn.sigmoid(a[r]) * u[r]).astype(BF16) for r in range(len(parts))]
        for r, rows in enumerate(parts):
            y = jnp.dot(act[r], wd, preferred_element_type=F32)
            y_ref[rows, :] = _pack_rows(y.astype(BF16).astype(F32))


def _experts(blk_e, n_used, buf, g, wg, wu, wd, layer):
    n_blocks = buf.shape[0] // MOE_BLOCK
    grid_spec = pltpu.PrefetchScalarGridSpec(
        num_scalar_prefetch=2,
        grid=(n_blocks,),
        in_specs=[
            pl.BlockSpec((MOE_BLOCK, D_PACKED), lambda i, be, nu: (jnp.minimum(i, nu[0] - 1), 0)),
            pl.BlockSpec((1, D_MODEL), lambda i, be, nu: (0, 0)),
            pl.BlockSpec((1, 1, D_MODEL, D_EXPERT), lambda i, be, nu: (layer, be[i], 0, 0)),
            pl.BlockSpec((1, 1, D_MODEL, D_EXPERT), lambda i, be, nu: (layer, be[i], 0, 0)),
            pl.BlockSpec((1, 1, D_EXPERT, D_MODEL), lambda i, be, nu: (layer, be[i], 0, 0)),
        ],
        out_specs=pl.BlockSpec((MOE_BLOCK, D_PACKED), lambda i, be, nu: (jnp.minimum(i, nu[0] - 1), 0)),
        scratch_shapes=[pltpu.VMEM((D_MODEL, D_EXPERT), BF16), pltpu.VMEM((D_MODEL, D_EXPERT), BF16),
                        pltpu.VMEM((D_EXPERT, D_MODEL), BF16)],
    )
    return pl.pallas_call(
        _experts_kernel,
        grid_spec=grid_spec,
        out_shape=jax.ShapeDtypeStruct(buf.shape, jnp.int32),
        input_output_aliases={2: 0},
        compiler_params=_cparams(("arbitrary",)),
        name="moe_experts",
    )(blk_e, n_used, buf, g, wg, wu, wd)


def _combine_kernel(tab_ref, tabn_ref, route_ref, x_ref, y_ref, o_ref, ya_ref, yb_ref, sema, semb):
    tm = x_ref.shape[0]
    r_st = ya_ref.shape[0]
    step = pl.program_id(0)

    def window(tabs, stage_ref, s):
        def make_copy(k):
            return pltpu.make_async_copy(y_ref.at[pl.ds(pl.multiple_of(tabs[0, 0, k], SUBLANES), MOE_WINDOW), :],
                                         stage_ref.at[k * MOE_WINDOW:(k + 1) * MOE_WINDOW, :], s)
        return make_copy

    @pl.when(step == 0)
    def _():
        ya_ref[...] = jnp.zeros_like(ya_ref)
        yb_ref[...] = jnp.zeros_like(yb_ref)
        _start_all_windows(window(tab_ref, ya_ref, sema))

    route = route_ref[...]
    r0 = route[:, 4:5].astype(jnp.int32)
    r1 = route[:, 5:6].astype(jnp.int32)
    part = r_st // STAGE_PARTS
    col = lax.broadcasted_iota(jnp.int32, (tm, part), 1)

    def run(cur_ref, cur_sem, nxt_ref, nxt_sem):
        @pl.when(step + 1 < pl.num_programs(0))
        def _():
            _start_all_windows(window(tabn_ref, nxt_ref, nxt_sem))

        _wait_all_windows(window(tab_ref, cur_ref, cur_sem))
        acc = x_ref[...]
        for p in range(STAGE_PARTS):
            gate = jnp.where(col + p * part == r0, route[:, 2:3],
                             jnp.where(col + p * part == r1, route[:, 3:4], 0.0)).astype(BF16)
            rows = _unpack_rows(cur_ref[p * part:(p + 1) * part, :]).astype(BF16)
            acc = acc + jnp.dot(gate, rows, preferred_element_type=F32)
        o_ref[...] = acc

    @pl.when(step % 2 == 0)
    def _():
        run(ya_ref, sema, yb_ref, semb)

    @pl.when(step % 2 == 1)
    def _():
        run(yb_ref, semb, ya_ref, sema)


def _combine(tabs, route, x2, y_buf):
    t_tok = x2.shape[0]
    tm = min(TM_ROUTE, t_tok)
    assert t_tok % tm == 0, "token count must be a multiple of the routing tile"
    n_tiles = t_tok // tm
    return pl.pallas_call(
        _combine_kernel,
        grid=(n_tiles,),
        in_specs=[
            pl.BlockSpec((1, 1, TAB_LANES), lambda i: (i, 0, 0), memory_space=pltpu.SMEM),
            pl.BlockSpec((1, 1, TAB_LANES), lambda i: (jnp.minimum(i + 1, n_tiles - 1), 0, 0),
                         memory_space=pltpu.SMEM),
            pl.BlockSpec((tm, ROUTE_LANES), lambda i: (i, 0)),
            pl.BlockSpec((tm, D_MODEL), lambda i: (i, 0)),
            pl.BlockSpec(memory_space=pl.ANY),
        ],
        out_specs=pl.BlockSpec((tm, D_MODEL), lambda i: (i, 0)),
        out_shape=jax.ShapeDtypeStruct((t_tok, D_MODEL), F32),
        scratch_shapes=[pltpu.VMEM((_staging_rows(tm), D_PACKED), jnp.int32),
                        pltpu.VMEM((_staging_rows(tm), D_PACKED), jnp.int32),
                        pltpu.SemaphoreType.DMA(()), pltpu.SemaphoreType.DMA(())],
        compiler_params=_cparams(("arbitrary",)),
        name="moe_combine",
    )(tabs, tabs, route, x2, y_buf)


def _rope_tables(positions):
    half = ROT_DIM // 2
    inv_freq = ROPE_THETA ** (-jnp.arange(0, ROT_DIM, 2, dtype=F32) / ROT_DIM)
    d = jnp.arange(LANES) % DK_A
    freq_lane = jnp.where(d < ROT_DIM, inv_freq[d % half], 0.0)
    ang = positions.reshape(-1, 1).astype(F32) * freq_lane[None, :]
    cos, sin = jnp.cos(ang), jnp.sin(ang)
    s1 = jnp.where(d < half, -sin, 0.0)
    s2 = jnp.where((d >= half) & (d < ROT_DIM), sin, 0.0)
    return cos, s1, s2


def _block_avg(width, group):
    idx = jnp.arange(width) // group
    return jnp.where(idx[:, None] == idx[None, :], 1.0 / group, 0.0).astype(BF16)


def kernel(x, positions, norm_mix_g, w_in, attn_q_norm_g, attn_k_norm_g, lambda_q1, lambda_k1, lambda_q2,
           lambda_k2, attn_subln_g, hgrn_lower_bounds, hgrn_norm_g, conv_w, w_out, norm_ffn_g, router_group_w,
           router_group_b, router_expert_w, router_expert_b, expert_w_gate, expert_w_up, expert_w_down):
    batch, seq, d = x.shape
    assert d == D_MODEL
    depth = w_in.shape[0]
    t_tok = batch * seq
    n_assign = 2 * t_tok
    n_route_tiles = t_tok // min(TM_ROUTE, t_tok)
    spare_blocks = -(-MAX_WINDOWS * MOE_WINDOW // MOE_BLOCK)
    n_blocks = (n_assign + (SUBLANES - 1) * n_route_tiles * N_EXPERTS) // MOE_BLOCK + 2 * N_EXPERTS + 1 + spare_blocks
    cap = n_blocks * MOE_BLOCK

    lb_probs = jax.nn.softmax(hgrn_lower_bounds.astype(F32), axis=0)
    lower_bounds = jnp.cumsum(lb_probs, axis=0) - lb_probs[0:1]
    cos_t, s1_t, s2_t = _rope_tables(positions)
    gsum_a = _block_avg(W_A, DK_A)
    gavg_b = _block_avg(W_B, DV_B)

    o_q, o_k, o_v = 0, W_A, 2 * W_A
    o_bq = 3 * W_A
    o_bf, o_bi, o_bg = o_bq + W_B, o_bq + 2 * W_B, o_bq + 3 * W_B
    o_c = o_bq + 4 * W_B

    x2 = x.reshape(t_tok, d)
    for l in range(depth):
        w = w_in[l]
        wqk = w[:, o_q:o_v].astype(BF16)
        wv = w[:, o_v:o_bq].astype(BF16)
        whb = jnp.concatenate([w[:, o_bq:o_bf], w[:, o_bi:o_c]], axis=1).astype(BF16)
        wf = w[:, o_bf:o_bi].astype(BF16)
        wc = w[:, o_c:].astype(BF16)
        gqk = jnp.concatenate([jnp.tile(attn_q_norm_g[l], 2 * H_A) * (DK_A ** -0.5 * LOG2_E),
                               jnp.tile(attn_k_norm_g[l], 2 * H_A)]).reshape(1, 2 * W_A).astype(F32)
        qk, v, hb, f, cv = _inproj(x2, norm_mix_g[l].reshape(1, d), wqk, wv, whb, wf, wc, gsum_a, gqk,
                                   cos_t, s1_t, s2_t)

        lam_init = 0.8 - 0.6 * math.exp(-0.3 * l)
        lam = (jnp.exp(jnp.sum(lambda_q1[l].astype(F32) * lambda_k1[l].astype(F32)))
               - jnp.exp(jnp.sum(lambda_q2[l].astype(F32) * lambda_k2[l].astype(F32))) + lam_init).reshape(1)
        o_a = _attention(qk, v, lam, attn_subln_g[l].reshape(DV_A, 1).astype(F32), batch, seq, lam_init)

        o_bc = _hgrn_conv(hb, f, cv, lower_bounds[l].reshape(1, W_B), jnp.tile(hgrn_norm_g[l], H_B).reshape(1, W_B),
                          conv_w[l], gavg_b, batch, seq)

        wr = jnp.concatenate([router_group_w[l], router_expert_w[l],
                              jnp.zeros((d, ROUTE_LANES - N_LOGITS), F32)], axis=1).astype(BF16)
        br = jnp.concatenate([router_group_b[l], router_expert_b[l],
                              jnp.zeros((ROUTE_LANES - N_LOGITS,), F32)]).reshape(1, ROUTE_LANES)
        x2, route, route_t, tab, cnt = _outproj_router(o_a, o_bc, x2, w_out[l].astype(BF16), norm_ffn_g[l].reshape(1, d), wr, br)

        counts = cnt[N_GROUPS:N_GROUPS + N_EXPERTS, 0].astype(jnp.int32)
        padded = (counts + MOE_WINDOW + MOE_BLOCK - 1) // MOE_BLOCK * MOE_BLOCK
        pad_end = jnp.cumsum(padded)
        pad_start = pad_end - padded
        blk_pos = jnp.arange(n_blocks, dtype=jnp.int32) * MOE_BLOCK
        blk_e = jnp.minimum(jnp.sum((pad_end[None, :] <= blk_pos[:, None]).astype(jnp.int32), axis=1),
                            N_EXPERTS - 1)
        n_used = (pad_end[-1] // MOE_BLOCK).astype(jnp.int32).reshape(1)
        tile_cnt = tab[:, N_GROUPS:N_GROUPS + N_EXPERTS, 0].astype(jnp.int32)
        tile_base = tab[:, N_GROUPS:N_GROUPS + N_EXPERTS, 1].astype(jnp.int32) + pad_start[None, :]
        n_win = (tile_cnt + MOE_WINDOW - 1) // MOE_WINDOW
        w_end = jnp.cumsum(n_win, axis=1)
        w_start = w_end - n_win
        k_idx = jnp.arange(MAX_WINDOWS, dtype=jnp.int32)[None, :, None]
        owner = (w_start[:, None, :] <= k_idx) & (k_idx < w_end[:, None, :])
        win_dst = jnp.sum(jnp.where(owner, tile_base[:, None, :] + (k_idx - w_start[:, None, :]) * MOE_WINDOW, 0),
                          axis=-1)
        spare = cap - MAX_WINDOWS * MOE_WINDOW + k_idx[:, :, 0] * MOE_WINDOW
        win_dst = jnp.where(jnp.any(owner, axis=-1), win_dst, spare)
        tabs = jnp.concatenate([win_dst, w_end[:, -1:],
                                jnp.zeros((win_dst.shape[0], TAB_LANES - MAX_WINDOWS - 1), jnp.int32)],
                               axis=1).reshape(-1, 1, TAB_LANES)

        buf = _dispatch(pad_end, tabs, route_t, x2, cap)
        y_buf = _experts(blk_e, n_used, buf, norm_ffn_g[l].reshape(1, d), expert_w_gate, expert_w_up,
                         expert_w_down, l)
        x2 = _combine(tabs, route, x2, y_buf)
    return x2.reshape(batch, seq, d)
```
